```python
import jax, jax.numpy as jnp
from jax import lax
import numpy as np

D_MODEL = 1024
BATCH = 16
SEQ = 256
DEPTH = 2
DEC_BATCH = 8
DEC_SEQ = 2048
PAST_LEN = 256

GRID_W = 64
BRANCH_W = 256
HEAD_DIM = 64
N_BRANCH = 4
CONV_K = 31
FNET_GROUPS = 4
FNET_GDIM = BRANCH_W // FNET_GROUPS
WIN_Q_HEADS = 4
WIN_KV_HEADS = 2
WINDOW = 128
WIN_BLOCK = 128
NA_HEADS = 4
NA_ROWS = 8
NA_COLS = 16
ROPE_THETA = 10000.0
ATTN_SCALE = HEAD_DIM ** -0.5
N_EXPERTS = 16
EXPERT_FF = 1024
EC_CAPACITY = 2
EPS = 1e-6
NEG = -1e30

COLS_A = 2 * BRANCH_W
COLS_B = BRANCH_W
COLS_CQ = WIN_Q_HEADS * HEAD_DIM
COLS_CKV = WIN_KV_HEADS * HEAD_DIM
COLS_D = NA_HEADS * HEAD_DIM
COLS_GATE = N_BRANCH * D_MODEL
IN_COLS = COLS_A + COLS_B + COLS_CQ + 2 * COLS_CKV + 3 * COLS_D + COLS_GATE
SPLITS = (COLS_A,
          COLS_A + COLS_B,
          COLS_A + COLS_B + COLS_CQ,
          COLS_A + COLS_B + COLS_CQ + COLS_CKV,
          COLS_A + COLS_B + COLS_CQ + 2 * COLS_CKV,
          COLS_A + COLS_B + COLS_CQ + 2 * COLS_CKV + COLS_D,
          COLS_A + COLS_B + COLS_CQ + 2 * COLS_CKV + 2 * COLS_D,
          COLS_A + COLS_B + COLS_CQ + 2 * COLS_CKV + 3 * COLS_D)

kernel_name = 'hybrid_diffusion_parallel_mixers_ec_moe_step'


def rms_norm(x, g):
    xf = x.astype(jnp.float32)
    y = xf * lax.rsqrt(jnp.mean(xf * xf, axis=-1, keepdims=True) + EPS)
    return (y * g.astype(jnp.float32)).astype(x.dtype)


def layer_norm(x, g, b):
    xf = x.astype(jnp.float32)
    mu = jnp.mean(xf, axis=-1, keepdims=True)
    var = jnp.mean(jnp.square(xf - mu), axis=-1, keepdims=True)
    y = (xf - mu) * lax.rsqrt(var + EPS) * g.astype(jnp.float32) + b.astype(jnp.float32)
    return y.astype(x.dtype)


def ada_modulation(cond, w_ada, b_ada):
    m = jax.nn.silu(cond) @ w_ada + b_ada
    return jnp.split(m[..., None, :], 6, axis=-1)


def axial_rope(x):
    L = x.shape[1]
    t = jnp.arange(L)
    row, col = t // GRID_W, t % GRID_W

    def rot(xh, pos):
        half = xh.shape[-1] // 2
        freqs = 1.0 / (ROPE_THETA ** (jnp.arange(half, dtype=jnp.float32) / half))
        ang = pos.astype(jnp.float32)[:, None] * freqs[None, :]
        cos = jnp.cos(ang)[:, None, :].astype(xh.dtype)
        sin = jnp.sin(ang)[:, None, :].astype(xh.dtype)
        x1, x2 = xh[..., :half], xh[..., half:]
        return jnp.concatenate([x1 * cos - x2 * sin, x1 * sin + x2 * cos], axis=-1)

    r = HEAD_DIM // 2
    return jnp.concatenate([rot(x[..., :r], row), rot(x[..., r:], col)], axis=-1)


def depthwise_conv(u, w, b):
    ch = u.shape[-1]
    y = lax.conv_general_dilated(u, w[:, None, :], window_strides=(1,),
                                 padding=((CONV_K // 2, CONV_K // 2),),
                                 dimension_numbers=('NWC', 'WIO', 'NWC'),
                                 feature_group_count=ch)
    return y + b


def fourier_mix(f):
    B, L, _ = f.shape
    fg = f.reshape(B, L, FNET_GROUPS, FNET_GDIM).astype(jnp.float32)
    out = jnp.fft.fft2(fg, axes=(1, 3), norm='ortho').real
    return out.reshape(B, L, BRANCH_W).astype(f.dtype)


def ctx_attention(q, k, v, sink):
    B, L, hq, d = q.shape
    hkv = k.shape[1]
    g = hq // hkv
    qh = q.reshape(B, L, hkv, g, d).transpose(0, 2, 3, 1, 4)
    s = jnp.einsum('bkgqd,bkcd->bkgqc', qh, k).astype(jnp.float32) * ATTN_SCALE
    if sink is not None:
        sk = jnp.broadcast_to(sink.astype(jnp.float32).reshape(1, hkv, g, 1, 1), s.shape[:-1] + (1,))
        p = jax.nn.softmax(jnp.concatenate([s, sk], axis=-1), axis=-1)[..., :-1]
    else:
        p = jax.nn.softmax(s, axis=-1)
    o = jnp.einsum('bkgqc,bkcd->bkgqd', p.astype(v.dtype), v)
    return o.transpose(0, 3, 1, 2, 4).reshape(B, L, hq * d)


def window_attention(q, k, v, ctx_k, ctx_v, sink):
    B, L, hq, d = q.shape
    hkv = k.shape[2]
    g = hq // hkv
    nb = L // WIN_BLOCK
    qb = q.reshape(B, nb, WIN_BLOCK, hkv, g, d).transpose(0, 3, 4, 1, 2, 5)

    def band(t):
        tp = jnp.pad(t, ((0, 0), (WIN_BLOCK, WIN_BLOCK), (0, 0), (0, 0)))
        tp = tp.reshape(B, nb + 2, WIN_BLOCK, hkv, d).transpose(0, 3, 1, 2, 4)
        return jnp.concatenate([tp[:, :, 0:nb], tp[:, :, 1:nb + 1], tp[:, :, 2:nb + 2]], axis=3)

    kb, vb = band(k), band(v)
    n = jnp.arange(nb)[:, None, None]
    a = jnp.arange(WIN_BLOCK)[None, :, None]
    j = jnp.arange(3 * WIN_BLOCK)[None, None, :]
    kpos = (n - 1) * WIN_BLOCK + j
    valid = (jnp.abs(j - WIN_BLOCK - a) <= WINDOW) & (kpos >= 0) & (kpos < L)
    s_loc = jnp.einsum('bkgnqd,bknjd->bkgnqj', qb, kb).astype(jnp.float32) * ATTN_SCALE
    s_loc = jnp.where(valid, s_loc, NEG)
    s_ctx = jnp.einsum('bkgnqd,bkcd->bkgnqc', qb, ctx_k).astype(jnp.float32) * ATTN_SCALE
    s_sink = jnp.broadcast_to(sink.astype(jnp.float32).reshape(1, hkv, g, 1, 1, 1), s_ctx.shape[:-1] + (1,))
    p = jax.nn.softmax(jnp.concatenate([s_loc, s_ctx, s_sink], axis=-1), axis=-1).astype(v.dtype)
    nloc = 3 * WIN_BLOCK
    lc = ctx_k.shape[2]
    o = (jnp.einsum('bkgnqj,bknjd->bkgnqd', p[..., :nloc], vb)
         + jnp.einsum('bkgnqc,bkcd->bkgnqd', p[..., nloc:nloc + lc], ctx_v))
    return o.transpose(0, 3, 4, 1, 2, 5).reshape(B, L, hq * d)


def neighborhood_attention(q, k, v, ctx_k, ctx_v, rel_bias):
    B, L, H, d = q.shape
    rows = L // GRID_W
    kr = min(NA_ROWS, rows)

    def grid(t):
        return t.reshape(B, rows, GRID_W, H, d).transpose(0, 3, 1, 2, 4)

    qg, kg, vg = grid(q), grid(k), grid(v)
    r = jnp.arange(rows)
    row_idx = jnp.clip(r - kr // 2, 0, rows - kr)[:, None] + jnp.arange(kr)[None, :]
    k_rows = kg[:, :, row_idx]
    v_rows = vg[:, :, row_idx]
    cq = jnp.arange(GRID_W)
    col_start = jnp.clip(cq - NA_COLS // 2, 0, GRID_W - NA_COLS)[:, None]
    col_ok = (cq[None, :] >= col_start) & (cq[None, :] < col_start + NA_COLS)
    row_off = row_idx - r[:, None] + (NA_ROWS - 1)
    col_off = jnp.clip(cq[None, :] - cq[:, None] + (NA_COLS - 1), 0, 2 * NA_COLS - 2)
    bias = rel_bias.astype(jnp.float32)[:, row_off[:, None, :, None], col_off[None, :, None, :]]
    s = jnp.einsum('bhrqd,bhrkwd->bhrqkw', qg, k_rows).astype(jnp.float32) * ATTN_SCALE + bias
    s = jnp.where(col_ok[:, None, :], s, NEG).reshape(B, H, rows, GRID_W, kr * GRID_W)
    s_ctx = jnp.einsum('bhrqd,bhcd->bhrqc', qg, ctx_k).astype(jnp.float32) * ATTN_SCALE
    p = jax.nn.softmax(jnp.concatenate([s, s_ctx], axis=-1), axis=-1).astype(v.dtype)
    nloc = kr * GRID_W
    p_loc = p[..., :nloc].reshape(B, H, rows, GRID_W, kr, GRID_W)
    o = (jnp.einsum('bhrqkw,bhrkwd->bhrqd', p_loc, v_rows)
         + jnp.einsum('bhrqc,bhcd->bhrqd', p[..., nloc:], ctx_v))
    return o.transpose(0, 2, 3, 1, 4).reshape(B, L, H * d)


def token_mixers(h, w_in, conv_w, conv_b, conv_ln_g, conv_ln_b, win_sink, na_bias, w_branch, w_out, ctx):
    B, L, _ = h.shape
    z = h @ w_in
    a_in, f_in, cq, ck, cv, dq, dk, dv, gate_logits = jnp.split(z, SPLITS, axis=-1)
    ga, gb = jnp.split(a_in, 2, axis=-1)
    br_a = jax.nn.silu(layer_norm(depthwise_conv(ga * jax.nn.sigmoid(gb), conv_w, conv_b), conv_ln_g, conv_ln_b))
    br_b = fourier_mix(f_in)
    cq = cq.reshape(B, L, WIN_Q_HEADS, HEAD_DIM)
    ck = ck.reshape(B, L, WIN_KV_HEADS, HEAD_DIM)
    cv = cv.reshape(B, L, WIN_KV_HEADS, HEAD_DIM)
    dq = dq.reshape(B, L, NA_HEADS, HEAD_DIM)
    dk = dk.reshape(B, L, NA_HEADS, HEAD_DIM)
    dv = dv.reshape(B, L, NA_HEADS, HEAD_DIM)
    if ctx is None:
        win_k, win_v = ck.transpose(0, 2, 1, 3), cv.transpose(0, 2, 1, 3)
        nat_k, nat_v = dk.transpose(0, 2, 1, 3), dv.transpose(0, 2, 1, 3)
        br_c = ctx_attention(cq, win_k, win_v, win_sink)
        br_d = ctx_attention(dq, nat_k, nat_v, None)
        ctx_out = (win_k, win_v, nat_k, nat_v)
    else:
        win_k, win_v, nat_k, nat_v = ctx
        br_c = window_attention(axial_rope(cq), axial_rope(ck), cv, win_k, win_v, win_sink)
        br_d = neighborhood_attention(dq, dk, dv, nat_k, nat_v, na_bias)
        ctx_out = ()
    branches = jnp.stack([br_a, br_b, br_c, br_d], axis=2)
    proj = jnp.einsum('blnc,ncd->blnd', branches, w_branch)
    gates = jax.nn.sigmoid(gate_logits.reshape(B, L, N_BRANCH, D_MODEL).astype(jnp.float32)).astype(h.dtype)
    merged = jnp.einsum('blnd,blnd->bld', gates, proj)
    return merged @ w_out, ctx_out


def expert_choice_ffn(h, w_router, w_gate, w_up, w_down):
    B, L, D = h.shape
    cap = EC_CAPACITY * L // N_EXPERTS
    aff = jax.nn.softmax((h @ w_router).astype(jnp.float32), axis=-1)
    g, idx = lax.top_k(jnp.swapaxes(aff, 1, 2), cap)
    xs = jax.vmap(lambda hb, ib: hb[ib])(h, idx)
    a = jnp.einsum('becd,edf->becf', xs, w_gate)
    u = jnp.einsum('becd,edf->becf', xs, w_up)
    y = jnp.einsum('becf,efd->becd', jax.nn.silu(a) * u, w_down) * g[..., None].astype(h.dtype)
    return jax.vmap(lambda ib, yb: jnp.zeros((L, D), yb.dtype).at[ib.reshape(-1)].add(yb.reshape(-1, D)))(idx, y)


def setup_inputs(seed: int = 0) -> dict:
    key = jax.random.key(seed)
    ks = jax.random.split(key, 32)

    def nrm(k, shape, s):
        return jax.random.normal(k, shape, jnp.float32) * s

    return {
        'x_prompt': nrm(ks[0], (BATCH, SEQ, D_MODEL), 1.0),
        'x_sample': nrm(ks[1], (DEC_BATCH, DEC_SEQ, D_MODEL), 1.0),
        'cache_win_k': nrm(ks[2], (DEC_BATCH, DEPTH, WIN_KV_HEADS, PAST_LEN, HEAD_DIM), 1.0),
        'cache_win_v': nrm(ks[3], (DEC_BATCH, DEPTH, WIN_KV_HEADS, PAST_LEN, HEAD_DIM), 1.0),
        'cache_nat_k': nrm(ks[4], (DEC_BATCH, DEPTH, NA_HEADS, PAST_LEN, HEAD_DIM), 1.0),
        'cache_nat_v': nrm(ks[5], (DEC_BATCH, DEPTH, NA_HEADS, PAST_LEN, HEAD_DIM), 1.0),
        'c': nrm(ks[6], (DEC_BATCH, D_MODEL), 1.0),
        'c_ctx': nrm(ks[7], (D_MODEL,), 1.0),
        'norm1_g': 1.0 + nrm(ks[8], (DEPTH, D_MODEL), 0.02),
        'norm2_g': 1.0 + nrm(ks[9], (DEPTH, D_MODEL), 0.02),
        'w_ada': nrm(ks[10], (DEPTH, D_MODEL, 6 * D_MODEL), 0.5 * D_MODEL ** -0.5),
        'b_ada': nrm(ks[11], (DEPTH, 6 * D_MODEL), 0.02),
        'w_in': nrm(ks[12], (DEPTH, D_MODEL, IN_COLS), D_MODEL ** -0.5),
        'conv_w': nrm(ks[13], (DEPTH, CONV_K, BRANCH_W), CONV_K ** -0.5),
        'conv_b': nrm(ks[14], (DEPTH, BRANCH_W), 0.02),
        'conv_ln_g': 1.0 + nrm(ks[15], (DEPTH, BRANCH_W), 0.02),
        'conv_ln_b': nrm(ks[16], (DEPTH, BRANCH_W), 0.02),
        'win_sink': nrm(ks[17], (DEPTH, WIN_Q_HEADS), 0.5),
        'na_bias': nrm(ks[18], (DEPTH, NA_HEADS, 2 * NA_ROWS - 1, 2 * NA_COLS - 1), 0.2),
        'w_branch': nrm(ks[19], (DEPTH, N_BRANCH, BRANCH_W, D_MODEL), BRANCH_W ** -0.5),
        'w_out': nrm(ks[20], (DEPTH, D_MODEL, D_MODEL), D_MODEL ** -0.5),
        'w_router': nrm(ks[21], (DEPTH, D_MODEL, N_EXPERTS), D_MODEL ** -0.5),
        'w_e_gate': nrm(ks[22], (DEPTH, N_EXPERTS, D_MODEL, EXPERT_FF), D_MODEL ** -0.5),
        'w_e_up': nrm(ks[23], (DEPTH, N_EXPERTS, D_MODEL, EXPERT_FF), D_MODEL ** -0.5),
        'w_e_down': nrm(ks[24], (DEPTH, N_EXPERTS, EXPERT_FF, D_MODEL), EXPERT_FF ** -0.5),
        'final_norm_g': 1.0 + nrm(ks[25], (D_MODEL,), 0.02),
    }


def reference(x_prompt, x_sample, cache_win_k, cache_win_v, cache_nat_k, cache_nat_v, c, c_ctx,
              norm1_g, norm2_g, w_ada, b_ada, w_in, conv_w, conv_b, conv_ln_g, conv_ln_b,
              win_sink, na_bias, w_branch, w_out, w_router, w_e_gate, w_e_up, w_e_down, final_norm_g):
    xp, xs = x_prompt, x_sample
    wk_list, wv_list, nk_list, nv_list = [], [], [], []
    for l in range(DEPTH):
        mix_w = (w_in[l], conv_w[l], conv_b[l], conv_ln_g[l], conv_ln_b[l], win_sink[l], na_bias[l], w_branch[l], w_out[l])
        ffn_w = (w_router[l], w_e_gate[l], w_e_up[l], w_e_down[l])
        sh1, sc1, g1, sh2, sc2, g2 = ada_modulation(c_ctx, w_ada[l], b_ada[l])
        mix, (wk, wv, nk, nv) = token_mixers(rms_norm(xp, norm1_g[l]) * (1 + sc1) + sh1, *mix_w, None)
        xp = xp + g1 * mix
        xp = xp + g2 * expert_choice_ffn(rms_norm(xp, norm2_g[l]) * (1 + sc2) + sh2, *ffn_w)
        wk_list.append(wk)
        wv_list.append(wv)
        nk_list.append(nk)
        nv_list.append(nv)
        ctx = (cache_win_k[:, l], cache_win_v[:, l], cache_nat_k[:, l], cache_nat_v[:, l])
        sh1, sc1, g1, sh2, sc2, g2 = ada_modulation(c, w_ada[l], b_ada[l])
        mix, _ = token_mixers(rms_norm(xs, norm1_g[l]) * (1 + sc1) + sh1, *mix_w, ctx)
        xs = xs + g1 * mix
        xs = xs + g2 * expert_choice_ffn(rms_norm(xs, norm2_g[l]) * (1 + sc2) + sh2, *ffn_w)
    y_prompt = rms_norm(xp, final_norm_g)
    y_sample = rms_norm(xs, final_norm_g)
    new_win_k = jnp.stack(wk_list, axis=1)
    new_win_v = jnp.stack(wv_list, axis=1)
    new_nat_k = jnp.stack(nk_list, axis=1)
    new_nat_v = jnp.stack(nv_list, axis=1)
    return (y_prompt, y_sample, new_win_k, new_win_v, new_nat_k, new_nat_v)
```

```python
import functools
import math

import ml_dtypes
import numpy as np
import jax
import jax.numpy as jnp
from jax import lax
from jax.experimental import pallas as pl
from jax.experimental.pallas import tpu as pltpu

D_MODEL = 1024
DEPTH = 2
GRID_W = 64
BRANCH_W = 256
HEAD_DIM = 64
N_BRANCH = 4
CONV_K = 31
FNET_GROUPS = 4
FNET_GDIM = BRANCH_W // FNET_GROUPS
WIN_Q_HEADS = 4
WIN_KV_HEADS = 2
WIN_BLOCK = 128
NA_HEADS = 4
NA_ROWS = 8
NA_COLS = 16
ROPE_THETA = 10000.0
ATTN_SCALE = HEAD_DIM ** -0.5
N_EXPERTS = 16
EXPERT_FF = 1024
EC_CAPACITY = 2
EPS = 1e-6
NEG = -1e30

MIX_COLS = 2048
GATE_COLS = N_BRANCH * D_MODEL
IN_COLS = MIX_COLS + GATE_COLS
COL_A, COL_F, COL_CQ, COL_CKV, COL_DQ, COL_DK, COL_DV = 0, 512, 768, 1024, 1280, 1536, 1792
ROPE_LO, ROPE_HI = COL_CQ, COL_CKV + WIN_KV_HEADS * HEAD_DIM

LANES = 128
SUBLANES = 8
VMEM_LIMIT = 56 * 1024 * 1024

BF = jnp.bfloat16
F32 = jnp.float32


def _params(*sem):
    return pltpu.CompilerParams(dimension_semantics=sem, vmem_limit_bytes=VMEM_LIMIT)


def _dot(a, b):
    return jnp.dot(a, b, preferred_element_type=F32)


def _dot_nt(a, b):
    return lax.dot_general(a, b, (((1,), (1,)), ((), ())), preferred_element_type=F32)


def _norm_mod(x, g, sc, sh):
    y = x * lax.rsqrt(jnp.mean(x * x, axis=-1, keepdims=True) + EPS)
    return (y * g) * (1.0 + sc) + sh


def _silu(x):
    return x * jax.nn.sigmoid(x)


CAST_BYTES = 4 * 1024 * 1024


def _cast_kernel(x_ref, o_ref):
    o_ref[...] = x_ref[...].astype(BF)


def _cast_call(x):
    shape = x.shape
    x2 = x.reshape(-1, shape[-1])
    rows, cols = x2.shape
    tr = min(rows, 1 << (max(16, CAST_BYTES // (4 * cols)).bit_length() - 1))
    assert rows % tr == 0
    out = pl.pallas_call(
        _cast_kernel,
        out_shape=jax.ShapeDtypeStruct((rows, cols), BF),
        grid=(rows // tr,),
        in_specs=[pl.BlockSpec((tr, cols), lambda i: (i, 0))],
        out_specs=pl.BlockSpec((tr, cols), lambda i: (i, 0)),
        compiler_params=_params("arbitrary"),
        name="cast_bf16",
    )(x2)
    return out.reshape(shape)


ADA_ROWS = 16
ADA_TN = 1536


def _ada_kernel(c_ref, w_ref, b_ref, o_ref):
    s = _silu(c_ref[...]).astype(BF)
    o_ref[0] = _dot(s, w_ref[0].astype(BF)) + b_ref[0]


def _ada_call(cond, w_ada, b_ada):
    n = 6 * D_MODEL
    return pl.pallas_call(
        _ada_kernel,
        out_shape=jax.ShapeDtypeStruct((DEPTH, ADA_ROWS, n), F32),
        grid=(DEPTH, n // ADA_TN),
        in_specs=[
            pl.BlockSpec((ADA_ROWS, D_MODEL), lambda l, j: (0, 0)),
            pl.BlockSpec((1, D_MODEL, ADA_TN), lambda l, j: (l, 0, j)),
            pl.BlockSpec((1, 1, ADA_TN), lambda l, j: (l, 0, j)),
        ],
        out_specs=pl.BlockSpec((1, ADA_ROWS, ADA_TN), lambda l, j: (l, 0, j)),
        compiler_params=_params("arbitrary", "arbitrary"),
        name="ada_mod",
    )(cond, w_ada, b_ada.reshape(DEPTH, 1, n))


def _mod_spec(bm, chunk, ngrid):
    if ngrid == 1:
        imap = (lambda b: (b, 0, chunk)) if bm > 1 else (lambda b: (0, 0, chunk))
    else:
        imap = (lambda b, i: (b, 0, chunk)) if bm > 1 else (lambda b, i: (0, 0, chunk))
    return pl.BlockSpec((1, 1, D_MODEL), imap)


def _inproj_kernel(*refs, rope, want_kv):
    x_ref, g_ref, sc_ref, sh_ref, w_ref = refs[:5]
    pos = 5
    if rope:
        cos_ref, sin_ref = refs[pos:pos + 2]
        pos += 2
    z_ref = refs[pos]
    h = _norm_mod(x_ref[0], g_ref[...], sc_ref[0], sh_ref[0]).astype(BF)
    z = _dot(h, w_ref[0])
    if want_kv:
        kv_ref = refs[pos + 1]
        kv_ref[0, :, 0:256] = z[:, COL_CKV:COL_DQ]
        kv_ref[0, :, 256:768] = z[:, COL_DK:MIX_COLS]
    if rope:
        z_ref[0, :, 0:ROPE_LO] = z[:, 0:ROPE_LO].astype(BF)
        lane = lax.broadcasted_iota(jnp.int32, (z.shape[0], LANES), 1)
        first_half = (lane % 32) < 16
        cos = cos_ref[...]
        sin = sin_ref[...]
        for c0 in range(ROPE_LO, ROPE_HI, LANES):
            xc = z[:, c0:c0 + LANES]
            partner = jnp.where(first_half, pltpu.roll(xc, LANES - 16, 1), pltpu.roll(xc, 16, 1))
            z_ref[0, :, c0:c0 + LANES] = (xc * cos + partner * sin).astype(BF)
        z_ref[0, :, ROPE_HI:MIX_COLS] = z[:, ROPE_HI:MIX_COLS].astype(BF)
    else:
        z_ref[0] = z.astype(BF)


def _rope_tables(L):
    t = np.arange(L)
    half = HEAD_DIM // 4
    freqs = 1.0 / (ROPE_THETA ** (np.arange(half, dtype=np.float64) / half))
    ang_r = (t // GRID_W).astype(np.float64)[:, None] * freqs[None, :]
    ang_c = (t % GRID_W).astype(np.float64)[:, None] * freqs[None, :]
    cos_h = np.concatenate([np.cos(ang_r), np.cos(ang_r), np.cos(ang_c), np.cos(ang_c)], axis=1)
    sin_h = np.concatenate([-np.sin(ang_r), np.sin(ang_r), -np.sin(ang_c), np.sin(ang_c)], axis=1)
    reps = LANES // HEAD_DIM
    return (jnp.asarray(np.tile(cos_h, (1, reps)), F32), jnp.asarray(np.tile(sin_h, (1, reps)), F32))


def _inproj_call(x, norm_g, mod, w_in_bf, layer, tm, rope, want_kv):
    B, L, _ = x.shape
    bm = mod.shape[0]
    in_specs = [
        pl.BlockSpec((1, tm, D_MODEL), lambda b, i: (b, i, 0)),
        pl.BlockSpec((1, D_MODEL), lambda b, i: (0, 0)),
        _mod_spec(bm, 1, 2),
        _mod_spec(bm, 0, 2),
        pl.BlockSpec((1, D_MODEL, MIX_COLS), lambda b, i: (layer, 0, 0)),
    ]
    args = [x, norm_g, mod, mod, w_in_bf]
    if rope:
        cos, sin = _rope_tables(L)
        in_specs += [pl.BlockSpec((tm, LANES), lambda b, i: (i, 0))] * 2
        args += [cos, sin]
    out_shape = [jax.ShapeDtypeStruct((B, L, MIX_COLS), BF)]
    out_specs = [pl.BlockSpec((1, tm, MIX_COLS), lambda b, i: (b, i, 0))]
    if want_kv:
        out_shape.append(jax.ShapeDtypeStruct((B, L, 768), F32))
        out_specs.append(pl.BlockSpec((1, tm, 768), lambda b, i: (b, i, 0)))
    return pl.pallas_call(
        functools.partial(_inproj_kernel, rope=rope, want_kv=want_kv),
        out_shape=out_shape,
        grid=(B, L // tm),
        in_specs=in_specs,
        out_specs=out_specs,
        compiler_params=_params("arbitrary", "arbitrary"),
        name="in_proj",
    )(*args)


CONV_PAD = 16
CONV_TR = 64
CONV_CHUNK = 256


def _conv_kernel(a_ref, w_ref, cb_ref, lg_ref, lb_ref, o_ref, p_ref, u_ref, *, L):
    zeros = jnp.zeros((CONV_PAD, BRANCH_W), F32)
    p_ref[0:CONV_PAD, :] = zeros
    p_ref[L + CONV_PAD:L + 2 * CONV_PAD, :] = zeros
    for r0 in range(0, L, CONV_CHUNK):
        a = a_ref[0, r0:r0 + CONV_CHUNK, :].astype(F32)
        p_ref[CONV_PAD + r0:CONV_PAD + r0 + CONV_CHUNK, :] = a[:, :BRANCH_W] * jax.nn.sigmoid(a[:, BRANCH_W:])
    n_u = L + 2 * CONV_PAD - SUBLANES
    for s in range(SUBLANES):
        u_ref[s] = p_ref[s:s + n_u, :]
    cb = cb_ref[...]
    lg = lg_ref[...]
    lb = lb_ref[...]

    def body(i, carry):
        base = pl.multiple_of(i * CONV_TR, CONV_TR)
        acc = jnp.zeros((CONV_TR, BRANCH_W), F32) + cb
        for k in range(CONV_K):
            q, s = divmod(k + CONV_PAD - CONV_K // 2, SUBLANES)
            acc = acc + w_ref[k:k + 1, :] * u_ref[s, pl.ds(base + SUBLANES * q, CONV_TR), :]
        mu = jnp.mean(acc, axis=-1, keepdims=True)
        d = acc - mu
        var = jnp.mean(d * d, axis=-1, keepdims=True)
        y = d * lax.rsqrt(var + EPS) * lg + lb
        o_ref[0, pl.ds(base, CONV_TR), :] = _silu(y).astype(BF)
        return carry

    lax.fori_loop(0, L // CONV_TR, body, 0)


def _conv_call(z, conv_w, conv_b, ln_g, ln_b):
    B, L, _ = z.shape
    n_u = L + 2 * CONV_PAD - SUBLANES
    vec = lambda b: (0, 0)
    return pl.pallas_call(
        functools.partial(_conv_kernel, L=L),
        out_shape=jax.ShapeDtypeStruct((B, L, BRANCH_W), BF),
        grid=(B,),
        in_specs=[
            pl.BlockSpec((1, L, 2 * BRANCH_W), lambda b: (b, 0, COL_A // (2 * BRANCH_W))),
            pl.BlockSpec((CONV_K, BRANCH_W), vec),
            pl.BlockSpec((1, BRANCH_W), vec),
            pl.BlockSpec((1, BRANCH_W), vec),
            pl.BlockSpec((1, BRANCH_W), vec),
        ],
        out_specs=pl.BlockSpec((1, L, BRANCH_W), lambda b: (b, 0, 0)),
        scratch_shapes=[
            pltpu.VMEM((L + 2 * CONV_PAD, BRANCH_W), F32),
            pltpu.VMEM((SUBLANES, n_u, BRANCH_W), F32),
        ],
        compiler_params=_params("arbitrary"),
        name="conv_module",
    )(z, conv_w, conv_b.reshape(1, -1), ln_g.reshape(1, -1), ln_b.reshape(1, -1))


def _dft_tables(L):
    c = np.arange(FNET_GDIM)
    ang_c = 2.0 * np.pi * ((c[:, None] * c[None, :]) % FNET_GDIM) / FNET_GDIM
    cc = np.zeros((BRANCH_W, BRANCH_W))
    sc = np.zeros((BRANCH_W, BRANCH_W))
    for g in range(FNET_GROUPS):
        sl = slice(g * FNET_GDIM, (g + 1) * FNET_GDIM)
        cc[sl, sl] = np.cos(ang_c)
        sc[sl, sl] = np.sin(ang_c)
    ccs = np.concatenate([cc, sc], axis=1)
    t = np.arange(L)
    ang_l = 2.0 * np.pi * ((t[:, None] * t[None, :]) % L) / L
    csl = np.concatenate([np.cos(ang_l), -np.sin(ang_l)], axis=1)
    return _cast_call(jnp.asarray(ccs, F32)), _cast_call(jnp.asarray(csl, F32))


FOURIER_TR = 512


def _fourier_kernel(f_ref, ccs_ref, csl_ref, o_ref, pq_ref, *, L):
    scale = float(1.0 / np.sqrt(L * FNET_GDIM))
    tr = min(L, FOURIER_TR)
    for r0 in range(0, L, tr):
        pq = _dot(f_ref[0, r0:r0 + tr, :], ccs_ref[...])
        pq_ref[r0:r0 + tr, :] = pq[:, :BRANCH_W].astype(BF)
        pq_ref[L + r0:L + r0 + tr, :] = pq[:, BRANCH_W:].astype(BF)
    for r0 in range(0, L, tr):
        o = _dot(csl_ref[r0:r0 + tr, :], pq_ref[...])
        o_ref[0, r0:r0 + tr, :] = (o * scale).astype(BF)


def _fourier_call(z, ccs, csl):
    B, L, _ = z.shape
    return pl.pallas_call(
        functools.partial(_fourier_kernel, L=L),
        out_shape=jax.ShapeDtypeStruct((B, L, BRANCH_W), BF),
        grid=(B,),
        in_specs=[
            pl.BlockSpec((1, L, BRANCH_W), lambda b: (b, 0, COL_F // BRANCH_W)),
            pl.BlockSpec((BRANCH_W, 2 * BRANCH_W), lambda b: (0, 0)),
            pl.BlockSpec((L, 2 * L), lambda b: (0, 0), pipeline_mode=pl.Buffered(1)),
        ],
        out_specs=pl.BlockSpec((1, L, BRANCH_W), lambda b: (b, 0, 0)),
        scratch_shapes=[pltpu.VMEM((2 * L, BRANCH_W), BF)],
        compiler_params=_params("arbitrary"),
        name="fourier_mix",
    )(z, ccs, csl)


def _head(ref, rows, h):
    return ref[0, rows, h * HEAD_DIM:(h + 1) * HEAD_DIM]


def _ctx_attn_kernel(sink_ref, qc_ref, kvc_ref, qd_ref, kd_ref, vd_ref, oc_ref, od_ref):
    rows = slice(None)
    group = WIN_Q_HEADS // WIN_KV_HEADS
    for h in range(WIN_Q_HEADS):
        kvh = h // group
        s = _dot_nt(_head(qc_ref, rows, h), _head(kvc_ref, rows, kvh)) * ATTN_SCALE
        sink = sink_ref[h]
        m = jnp.maximum(jnp.max(s, axis=-1, keepdims=True), sink)
        p = jnp.exp(s - m)
        den = jnp.sum(p, axis=-1, keepdims=True) + jnp.exp(sink - m)
        o = _dot(p.astype(BF), _head(kvc_ref, rows, WIN_KV_HEADS + kvh)) / den
        oc_ref[0, :, h * HEAD_DIM:(h + 1) * HEAD_DIM] = o.astype(BF)
    for h in range(NA_HEADS):
        s = _dot_nt(_head(qd_ref, rows, h), _head(kd_ref, rows, h)) * ATTN_SCALE
        m = jnp.max(s, axis=-1, keepdims=True)
        p = jnp.exp(s - m)
        den = jnp.sum(p, axis=-1, keepdims=True)
        o = _dot(p.astype(BF), _head(vd_ref, rows, h)) / den
        od_ref[0, :, h * HEAD_DIM:(h + 1) * HEAD_DIM] = o.astype(BF)


def _zcol_spec(L, col):
    return pl.BlockSpec((1, L, BRANCH_W), lambda b: (b, 0, col // BRANCH_W))


def _ctx_attn_call(z, sink):
    B, L, _ = z.shape
    out = jax.ShapeDtypeStruct((B, L, BRANCH_W), BF)
    ospec = pl.BlockSpec((1, L, BRANCH_W), lambda b: (b, 0, 0))
    return pl.pallas_call(
        _ctx_attn_kernel,
        out_shape=[out, out],
        grid=(B,),
        in_specs=[
            pl.BlockSpec(memory_space=pltpu.SMEM),
            _zcol_spec(L, COL_CQ), _zcol_spec(L, COL_CKV),
            _zcol_spec(L, COL_DQ), _zcol_spec(L, COL_DK), _zcol_spec(L, COL_DV),
        ],
        out_specs=[ospec, ospec],
        compiler_params=_params("arbitrary"),
        name="ctx_attention",
    )(sink, z, z, z, z, z)


def _win_attn_kernel(sink_ref, q_ref, kv_ref, ck_ref, cv_ref, o_ref, *, L):
    nb = L // WIN_BLOCK
    group = WIN_Q_HEADS // WIN_KV_HEADS
    m_rows = group * WIN_BLOCK
    row = lax.broadcasted_iota(jnp.int32, (m_rows, WIN_BLOCK), 0)
    a = row % WIN_BLOCK
    j = lax.broadcasted_iota(jnp.int32, (m_rows, WIN_BLOCK), 1)
    row1 = lax.broadcasted_iota(jnp.int32, (m_rows, 1), 0)
    for kvh in range(WIN_KV_HEADS):
        ctx_k = ck_ref[0, 0, kvh].astype(BF)
        ctx_v = cv_ref[0, 0, kvh].astype(BF)
        sink = jnp.zeros((m_rows, 1), F32)
        for g in range(group):
            sink = jnp.where(row1 // WIN_BLOCK == g, sink_ref[kvh * group + g], sink)

        def body(n, carry, kvh=kvh, ctx_k=ctx_k, ctx_v=ctx_v, sink=sink):
            r0 = pl.multiple_of(n * WIN_BLOCK, WIN_BLOCK)
            q = jnp.concatenate(
                [_head(q_ref, pl.ds(r0, WIN_BLOCK), kvh * group + g) for g in range(group)], axis=0)
            s_ctx = _dot_nt(q, ctx_k) * ATTN_SCALE
            m = jnp.maximum(jnp.max(s_ctx, axis=-1, keepdims=True), sink)
            scores = []
            vals = []
            for dj in (-1, 0, 1):
                nk = jnp.clip(n + dj, 0, nb - 1)
                rk = pl.multiple_of(nk * WIN_BLOCK, WIN_BLOCK)
                s = _dot_nt(q, _head(kv_ref, pl.ds(rk, WIN_BLOCK), kvh)) * ATTN_SCALE
                if dj == -1:
                    s = jnp.where((j >= a) & (n >= 1), s, NEG)
                elif dj == 1:
                    s = jnp.where((j <= a) & (n <= nb - 2), s, NEG)
                m = jnp.maximum(m, jnp.max(s, axis=-1, keepdims=True))
                scores.append(s)
                vals.append(_head(kv_ref, pl.ds(rk, WIN_BLOCK), WIN_KV_HEADS + kvh))
            p_ctx = jnp.exp(s_ctx - m)
            den = jnp.sum(p_ctx, axis=-1, keepdims=True) + jnp.exp(sink - m)
            o = _dot(p_ctx.astype(BF), ctx_v)
            for s, v in zip(scores, vals):
                p = jnp.exp(s - m)
                den = den + jnp.sum(p, axis=-1, keepdims=True)
                o = o + _dot(p.astype(BF), v)
            o = (o / den).astype(BF)
            for g in range(group):
                h = kvh * group + g
                o_ref[0, pl.ds(r0, WIN_BLOCK), h * HEAD_DIM:(h + 1) * HEAD_DIM] = (
                    o[g * WIN_BLOCK:(g + 1) * WIN_BLOCK])
            return carry

        lax.fori_loop(0, nb, body, 0, unroll=2)


def _win_attn_call(z, cache_k, cache_v, sink, layer):
    B, L, _ = z.shape
    P = cache_k.shape[3]
    cspec = pl.BlockSpec((1, 1, WIN_KV_HEADS, P, HEAD_DIM), lambda b: (b, layer, 0, 0, 0))
    return pl.pallas_call(
        functools.partial(_win_attn_kernel, L=L),
        out_shape=jax.ShapeDtypeStruct((B, L, BRANCH_W), BF),
        grid=(B,),
        in_specs=[
            pl.BlockSpec(memory_space=pltpu.SMEM),
            _zcol_spec(L, COL_CQ), _zcol_spec(L, COL_CKV), cspec, cspec,
        ],
        out_specs=pl.BlockSpec((1, L, BRANCH_W), lambda b: (b, 0, 0)),
        compiler_params=_params("arbitrary"),
        name="window_attention",
    )(sink, z, z, cache_k, cache_v)


N_ROW_OFF = 2 * NA_ROWS - 1
N_COL_OFF = 2 * NA_COLS - 1


NA_QROWS = 4
NA_WIN_ROWS = NA_ROWS + NA_QROWS
NA_PATTERNS = 3


def _na_window_start(r0, rows):
    return min(max(r0 - NA_ROWS // 2, 0), rows - NA_WIN_ROWS)


def _na_pattern_blocks(p, rows):
    r0 = (0, 2 * NA_QROWS, rows - NA_QROWS)[p]
    ws = _na_window_start(r0, rows)
    blocks = {}
    for i in range(NA_QROWS):
        r = r0 + i
        start = min(max(r - NA_ROWS // 2, 0), rows - NA_ROWS)
        for kk in range(NA_WIN_ROWS):
            rel = ws + kk - start
            blocks[i, kk] = ws + kk - r + (NA_ROWS - 1) if 0 <= rel < NA_ROWS else None
    return blocks


def _na_bias_kernel(rb_ref, o_ref, *, rows):
    h = pl.program_id(0)
    cq = lax.broadcasted_iota(jnp.int32, (GRID_W, GRID_W), 0)
    ck = lax.broadcasted_iota(jnp.int32, (GRID_W, GRID_W), 1)
    col_start = jnp.clip(cq - NA_COLS // 2, 0, GRID_W - NA_COLS)
    col_ok = (ck >= col_start) & (ck < col_start + NA_COLS)
    col_off = jnp.clip(ck - cq + (NA_COLS - 1), 0, N_COL_OFF - 1)
    tiles = []
    for ro in range(N_ROW_OFF):
        t = jnp.zeros((GRID_W, GRID_W), F32)
        for co in range(N_COL_OFF):
            t = jnp.where(col_off == co, rb_ref[(h * N_ROW_OFF + ro) * N_COL_OFF + co], t)
        tiles.append(jnp.where(col_ok, t, NEG))
    masked = jnp.full((GRID_W, GRID_W), NEG, F32)
    for p in range(NA_PATTERNS):
        for (i, kk), ro in _na_pattern_blocks(p, rows).items():
            o_ref[0, p, i * GRID_W:(i + 1) * GRID_W, kk * GRID_W:(kk + 1) * GRID_W] = (
                masked if ro is None else tiles[ro])


def _na_bias_call(na_bias, rows):
    shape = (NA_HEADS, NA_PATTERNS, NA_QROWS * GRID_W, NA_WIN_ROWS * GRID_W)
    return pl.pallas_call(
        functools.partial(_na_bias_kernel, rows=rows),
        out_shape=jax.ShapeDtypeStruct(shape, F32),
        grid=(NA_HEADS,),
        in_specs=[pl.BlockSpec(memory_space=pltpu.SMEM)],
        out_specs=pl.BlockSpec((1,) + shape[1:], lambda h: (h, 0, 0, 0)),
        compiler_params=_params("arbitrary"),
        name="na_bias_table",
    )(na_bias.reshape(-1))


def _na_attn_kernel(q_ref, k_ref, v_ref, ck_ref, cv_ref, bt_ref, o_ref, *, L):
    rows = L // GRID_W
    groups = rows // NA_QROWS
    for h in range(NA_HEADS):
        ctx_k = ck_ref[0, 0, h].astype(BF)
        ctx_v = cv_ref[0, 0, h].astype(BF)

        def body(qi, carry, h=h, ctx_k=ctx_k, ctx_v=ctx_v):
            r0 = qi * NA_QROWS
            ws = jnp.clip(r0 - NA_ROWS // 2, 0, rows - NA_WIN_ROWS)
            pat = jnp.where(qi == 0, 0, jnp.where(qi == groups - 1, 2, 1))
            rq = pl.ds(pl.multiple_of(r0 * GRID_W, NA_QROWS * GRID_W), NA_QROWS * GRID_W)
            rk = pl.ds(pl.multiple_of(ws * GRID_W, GRID_W), NA_WIN_ROWS * GRID_W)
            q = _head(q_ref, rq, h)
            s = _dot_nt(q, _head(k_ref, rk, h)) * ATTN_SCALE + bt_ref[h, pat]
            s_ctx = _dot_nt(q, ctx_k) * ATTN_SCALE
            m = jnp.maximum(jnp.max(s, axis=-1, keepdims=True), jnp.max(s_ctx, axis=-1, keepdims=True))
            p = jnp.exp(s - m)
            p_ctx = jnp.exp(s_ctx - m)
            den = jnp.sum(p, axis=-1, keepdims=True) + jnp.sum(p_ctx, axis=-1, keepdims=True)
            o = (_dot(p.astype(BF), _head(v_ref, rk, h)) + _dot(p_ctx.astype(BF), ctx_v)) / den
            o_ref[0, rq, h * HEAD_DIM:(h + 1) * HEAD_DIM] = o.astype(BF)
            return carry

        lax.fori_loop(0, groups, body, 0, unroll=2)


def _na_attn_call(z, cache_k, cache_v, bias_table, layer):
    B, L, _ = z.shape
    P = cache_k.shape[3]
    assert L // GRID_W >= NA_WIN_ROWS + NA_QROWS and (L // GRID_W) % NA_QROWS == 0
    cspec = pl.BlockSpec((1, 1, NA_HEADS, P, HEAD_DIM), lambda b: (b, layer, 0, 0, 0))
    return pl.pallas_call(
        functools.partial(_na_attn_kernel, L=L),
        out_shape=jax.ShapeDtypeStruct((B, L, BRANCH_W), BF),
        grid=(B,),
        in_specs=[
            _zcol_spec(L, COL_DQ), _zcol_spec(L, COL_DK), _zcol_spec(L, COL_DV), cspec, cspec,
            pl.BlockSpec(bias_table.shape, lambda b: (0, 0, 0, 0), pipeline_mode=pl.Buffered(1)),
        ],
        out_specs=pl.BlockSpec((1, L, BRANCH_W), lambda b: (b, 0, 0)),
        compiler_params=_params("arbitrary"),
        name="neighbourhood_attention",
    )(z, z, z, cache_k, cache_v, bias_table)


def _merge_kernel(x_ref, g_ref, sc_ref, sh_ref, gate_ref, ba_ref, bb_ref, bc_ref, bd_ref,
                  wg0_ref, wg1_ref, wb_ref, wo_ref, o_ref):
    x = x_ref[0]
    h = _norm_mod(x, g_ref[...], sc_ref[0], sh_ref[0]).astype(BF)
    merged = jnp.zeros(x.shape, F32)
    per_ref = MIX_COLS // D_MODEL
    for n, br_ref in enumerate((ba_ref, bb_ref, bc_ref, bd_ref)):
        wg_ref = (wg0_ref, wg1_ref)[n // per_ref]
        c0 = (n % per_ref) * D_MODEL
        gate = jax.nn.sigmoid(_dot(h, wg_ref[0, :, c0:c0 + D_MODEL]))
        merged = merged + gate * _dot(br_ref[0], wb_ref[0, n])
    o_ref[0] = x + gate_ref[0] * _dot(merged.astype(BF), wo_ref[0])


def _merge_call(x, norm_g, mod, branches, w_in_bf, w_branch, w_out, layer, tm):
    B, L, _ = x.shape
    bm = mod.shape[0]
    xspec = pl.BlockSpec((1, tm, D_MODEL), lambda b, i: (b, i, 0))
    bspec = pl.BlockSpec((1, tm, BRANCH_W), lambda b, i: (b, i, 0))
    return pl.pallas_call(
        _merge_kernel,
        out_shape=jax.ShapeDtypeStruct((B, L, D_MODEL), F32),
        grid=(B, L // tm),
        in_specs=[
            xspec,
            pl.BlockSpec((1, D_MODEL), lambda b, i: (0, 0)),
            _mod_spec(bm, 1, 2), _mod_spec(bm, 0, 2), _mod_spec(bm, 2, 2),
            bspec, bspec, bspec, bspec,
            pl.BlockSpec((1, D_MODEL, MIX_COLS), lambda b, i: (layer, 0, 1)),
            pl.BlockSpec((1, D_MODEL, MIX_COLS), lambda b, i: (layer, 0, 2)),
            pl.BlockSpec((1, N_BRANCH, BRANCH_W, D_MODEL), lambda b, i: (layer, 0, 0, 0)),
            pl.BlockSpec((1, D_MODEL, D_MODEL), lambda b, i: (layer, 0, 0)),
        ],
        out_specs=xspec,
        compiler_params=_params("arbitrary", "arbitrary"),
        name="merge_out_proj",
    )(x, norm_g, mod, mod, mod, *branches, w_in_bf, w_in_bf, w_branch, w_out)


ROUTE_CHUNK = 256
GATHER_ROWS = 1024


def _prefix_excl(mask_ref, out_ref, tri, L):
    run = jnp.zeros((N_EXPERTS, 1), F32)
    for c0 in range(0, L, LANES):
        blk = mask_ref[:, c0:c0 + LANES]
        out_ref[:, c0:c0 + LANES] = _dot(blk.astype(BF), tri) + run
        run = run + jnp.sum(blk, axis=-1, keepdims=True)


def _split3(v):
    hi = v.astype(BF)
    r = v - hi.astype(F32)
    mid = r.astype(BF)
    lo = (r - mid.astype(F32)).astype(BF)
    return hi, mid, lo


def _route_kernel(x_ref, g_ref, sc_ref, sh_ref, wh_ref, wl_ref, tri_ref,
                  xs_ref, gs_ref, slot_t_ref, h_ref, lt_ref, msk_ref, pre_ref, oh_ref, *, L, cap):
    g = g_ref[...]
    sc = sc_ref[0]
    sh = sh_ref[0]
    wh = wh_ref[...]
    wl = wl_ref[...]
    rc = min(L, ROUTE_CHUNK)
    for r0 in range(0, L, rc):
        h = _norm_mod(x_ref[0, r0:r0 + rc, :], g, sc, sh)
        h_hi = h.astype(BF)
        h_ref[r0:r0 + rc, :] = h_hi
        h_lo = (h - h_hi.astype(F32)).astype(BF)
        lt_ref[:, r0:r0 + rc] = _dot_nt(wh, h_hi) + (_dot_nt(wh, h_lo) + _dot_nt(wl, h_hi))
    lt = lt_ref[...]
    e = jnp.exp(lt - jnp.max(lt, axis=0, keepdims=True))
    aff = e / jnp.sum(e, axis=0, keepdims=True)
    thr = jnp.zeros((N_EXPERTS, 1), jnp.int32)
    for bit in range(30, -1, -1):
        cand = thr | (1 << bit)
        cnt = jnp.sum((aff >= pltpu.bitcast(cand, F32)).astype(F32), axis=-1, keepdims=True)
        thr = jnp.where(cnt >= cap, cand, thr)
    thr_f = pltpu.bitcast(thr, F32)
    gt = aff > thr_f
    eq = aff == thr_f
    need = cap - jnp.sum(gt.astype(F32), axis=-1, keepdims=True)
    tri = tri_ref[...]
    msk_ref[...] = eq.astype(F32)
    _prefix_excl(msk_ref, pre_ref, tri, L)
    sel = gt | (eq & (pre_ref[...] < need))
    msk_ref[...] = sel.astype(F32)
    _prefix_excl(msk_ref, pre_ref, tri, L)
    slot = jnp.where(sel, pre_ref[...], -1.0)
    pre_ref[...] = slot
    slot_t_ref[0] = slot.T
    g_hi, g_mid, g_lo = _split3(jnp.where(sel, aff, 0.0).T)

    ge = max(1, min(N_EXPERTS, GATHER_ROWS // cap))
    slot_iota = lax.broadcasted_iota(jnp.int32, (cap, L), 0).astype(F32)
    for e0 in range(0, N_EXPERTS, ge):
        for j in range(ge):
            oh_ref[j * cap:(j + 1) * cap, :] = jnp.where(
                pre_ref[e0 + j:e0 + j + 1, :] == slot_iota, 1.0, 0.0).astype(BF)
        oh = oh_ref[...]
        rows = _dot(oh, h_ref[...])
        gates = _dot(oh, g_hi) + (_dot(oh, g_mid) + _dot(oh, g_lo))
        for j in range(ge):
            xs_ref[e0 + j, 0] = rows[j * cap:(j + 1) * cap].astype(BF)
            gs_ref[e0 + j, 0] = gates[j * cap:(j + 1) * cap]


def _route_call(x, norm_g, mod, wr_hi, wr_lo):
    B, L, _ = x.shape
    bm = mod.shape[0]
    cap = EC_CAPACITY * L // N_EXPERTS
    ge = max(1, min(N_EXPERTS, GATHER_ROWS // cap))
    tri = jnp.asarray(np.triu(np.ones((LANES, LANES)), 1).astype(ml_dtypes.bfloat16))
    wspec = pl.BlockSpec((N_EXPERTS, D_MODEL), lambda b: (0, 0))
    return pl.pallas_call(
        functools.partial(_route_kernel, L=L, cap=cap),
        out_shape=[
            jax.ShapeDtypeStruct((N_EXPERTS, B, cap, D_MODEL), BF),
            jax.ShapeDtypeStruct((N_EXPERTS, B, cap, N_EXPERTS), F32),
            jax.ShapeDtypeStruct((B, L, N_EXPERTS), F32),
        ],
        grid=(B,),
        in_specs=[
            pl.BlockSpec((1, L, D_MODEL), lambda b: (b, 0, 0)),
            pl.BlockSpec((1, D_MODEL), lambda b: (0, 0)),
            _mod_spec(bm, 4, 1), _mod_spec(bm, 3, 1),
            wspec, wspec,
            pl.BlockSpec((LANES, LANES), lambda b: (0, 0)),
        ],
        out_specs=[
            pl.BlockSpec((N_EXPERTS, 1, cap, D_MODEL), lambda b: (0, b, 0, 0)),
            pl.BlockSpec((N_EXPERTS, 1, cap, N_EXPERTS), lambda b: (0, b, 0, 0)),
            pl.BlockSpec((1, L, N_EXPERTS), lambda b: (b, 0, 0)),
        ],
        scratch_shapes=[
            pltpu.VMEM((L, D_MODEL), BF),
            pltpu.VMEM((N_EXPERTS, L), F32),
            pltpu.VMEM((N_EXPERTS, L), F32),
            pltpu.VMEM((N_EXPERTS, L), F32),
            pltpu.VMEM((ge * cap, L), BF),
        ],
        compiler_params=_params("arbitrary"),
        name="ec_route_gather",
    )(x, norm_g, mod, mod, wr_hi, wr_lo, tri)


EXPERT_TR = 512


def _expert_kernel(xa_ref, ga_ref, xb_ref, gb_ref, wg_ref, wu_ref, wd_ref, ya_ref, yb_ref,
                   wgb_ref, wub_ref, wdb_ref, *, tiles_a):
    ex = pl.program_id(0)
    j = pl.program_id(1)

    @pl.when(j == 0)
    def _():
        wgb_ref[...] = wg_ref[0, 0].astype(BF)
        wub_ref[...] = wu_ref[0, 0].astype(BF)
        wdb_ref[...] = wd_ref[0, 0].astype(BF)

    def ffn(x_ref, g_ref, y_ref):
        x = x_ref[0]
        a = _dot(x, wgb_ref[...])
        u = _dot(x, wub_ref[...])
        y = _dot((_silu(a) * u).astype(BF), wdb_ref[...])
        gates = g_ref[0]
        lane = lax.broadcasted_iota(jnp.int32, gates.shape, 1)
        gate = jnp.sum(jnp.where(lane == ex, gates, 0.0), axis=-1, keepdims=True)
        y_ref[0] = (y * gate).astype(BF)

    @pl.when(j < tiles_a)
    def _():
        ffn(xa_ref, ga_ref, ya_ref)

    @pl.when(j >= tiles_a)
    def _():
        ffn(xb_ref, gb_ref, yb_ref)


def _expert_call(xs_a, gs_a, xs_b, gs_b, w_gate, w_up, w_down, layer):
    E, ra, _ = xs_a.shape
    rb = xs_b.shape[1]
    tr = math.gcd(math.gcd(ra, rb), EXPERT_TR)
    tiles_a, tiles_b = ra // tr, rb // tr
    amap = lambda e, j: (e, jnp.minimum(j, tiles_a - 1), 0)
    bmap = lambda e, j: (e, jnp.maximum(j - tiles_a, 0), 0)
    wspec_in = pl.BlockSpec((1, 1, D_MODEL, EXPERT_FF), lambda e, j: (layer, e, 0, 0))
    wspec_out = pl.BlockSpec((1, 1, EXPERT_FF, D_MODEL), lambda e, j: (layer, e, 0, 0))
    return pl.pallas_call(
        functools.partial(_expert_kernel, tiles_a=tiles_a),
        out_shape=[jax.ShapeDtypeStruct((E, ra, D_MODEL), BF), jax.ShapeDtypeStruct((E, rb, D_MODEL), BF)],
        grid=(E, tiles_a + tiles_b),
        in_specs=[
            pl.BlockSpec((1, tr, D_MODEL), amap), pl.BlockSpec((1, tr, N_EXPERTS), amap),
            pl.BlockSpec((1, tr, D_MODEL), bmap), pl.BlockSpec((1, tr, N_EXPERTS), bmap),
            wspec_in, wspec_in, wspec_out,
        ],
        out_specs=[pl.BlockSpec((1, tr, D_MODEL), amap), pl.BlockSpec((1, tr, D_MODEL), bmap)],
        scratch_shapes=[
            pltpu.VMEM((D_MODEL, EXPERT_FF), BF),
            pltpu.VMEM((D_MODEL, EXPERT_FF), BF),
            pltpu.VMEM((EXPERT_FF, D_MODEL), BF),
        ],
        compiler_params=_params("arbitrary", "arbitrary"),
        name="ec_experts",
    )(xs_a, gs_a, xs_b, gs_b, w_gate, w_up, w_down)


COMBINE_TL = 256


def _combine_kernel(st_ref, y_ref, x_ref, gate_ref, fg_ref, o_ref, *, cap, final):
    st = st_ref[0]
    tl = st.shape[0]
    if cap % LANES == 0:
        lane = lax.broadcasted_iota(jnp.int32, (tl, cap), 1).astype(F32)
        onehot = jnp.concatenate(
            [jnp.where(st[:, ex:ex + 1] == lane, 1.0, 0.0).astype(BF) for ex in range(N_EXPERTS)], axis=1)
    else:
        lane = lax.broadcasted_iota(jnp.int32, (tl, N_EXPERTS * cap), 1).astype(F32)
        hit = None
        for ex in range(N_EXPERTS):
            col = st[:, ex:ex + 1]
            m = jnp.where(col >= 0.0, col + float(ex * cap), -1.0) == lane
            hit = m if hit is None else (hit | m)
        onehot = jnp.where(hit, 1.0, 0.0).astype(BF)
    y_all = y_ref[:, 0].reshape(N_EXPERTS * cap, D_MODEL)
    v = x_ref[0] + gate_ref[0] * _dot(onehot, y_all)
    if final:
        v = v * lax.rsqrt(jnp.mean(v * v, axis=-1, keepdims=True) + EPS) * fg_ref[...]
    o_ref[0] = v


def _combine_call(slot_t, y, x, mod, final_g, final):
    B, L, _ = x.shape
    bm = mod.shape[0]
    cap = y.shape[2]
    tl = min(L, COMBINE_TL)
    return pl.pallas_call(
        functools.partial(_combine_kernel, cap=cap, final=final),
        out_shape=jax.ShapeDtypeStruct((B, L, D_MODEL), F32),
        grid=(B, L // tl),
        in_specs=[
            pl.BlockSpec((1, tl, N_EXPERTS), lambda b, i: (b, i, 0)),
            pl.BlockSpec((N_EXPERTS, 1, cap, D_MODEL), lambda b, i: (0, b, 0, 0)),
            pl.BlockSpec((1, tl, D_MODEL), lambda b, i: (b, i, 0)),
            _mod_spec(bm, 5, 2),
            pl.BlockSpec((1, D_MODEL), lambda b, i: (0, 0)),
        ],
        out_specs=pl.BlockSpec((1, tl, D_MODEL), lambda b, i: (b, i, 0)),
        compiler_params=_params("arbitrary", "arbitrary"),
        name="ec_combine",
    )(slot_t, y, x, mod, final_g)


def _mixer_half(x, mod, lw, ctx, tm):
    latent = ctx is not None
    layer = lw["layer"]
    res = _inproj_call(x, lw["norm1_g"], mod, lw["w_in"], layer, tm, rope=latent, want_kv=not latent)
    z = res[0]
    br_a = _conv_call(z, lw["conv_w"], lw["conv_b"], lw["conv_ln_g"], lw["conv_ln_b"])
    br_b = _fourier_call(z, *lw["dft"][z.shape[1]])
    if latent:
        br_c = _win_attn_call(z, ctx["win_k"], ctx["win_v"], lw["win_sink"], layer)
        br_d = _na_attn_call(z, ctx["nat_k"], ctx["nat_v"], lw["bias_table"], layer)
    else:
        br_c, br_d = _ctx_attn_call(z, lw["win_sink"])
    x = _merge_call(x, lw["norm1_g"], mod, (br_a, br_b, br_c, br_d),
                    lw["w_in"], lw["w_branch"], lw["w_out"], layer, tm)
    return x, (res[1] if not latent else None)


def kernel(x_prompt, x_sample, cache_win_k, cache_win_v, cache_nat_k, cache_nat_v, c, c_ctx,
           norm1_g, norm2_g, w_ada, b_ada, w_in, conv_w, conv_b, conv_ln_g, conv_ln_b,
           win_sink, na_bias, w_branch, w_out, w_router, w_e_gate, w_e_up, w_e_down, final_norm_g):
    n_dec = c.shape[0]
    cond = jnp.zeros((ADA_ROWS, D_MODEL), F32).at[0].set(c_ctx).at[1:1 + n_dec].set(c)
    mods = _ada_call(cond, w_ada, b_ada)
    final_g = final_norm_g.reshape(1, D_MODEL)
    w_in_bf = _cast_call(w_in)
    w_branch_bf = _cast_call(w_branch)
    w_out_bf = _cast_call(w_out)
    dft = {L: _dft_tables(L) for L in (x_prompt.shape[1], x_sample.shape[1])}
    ctx = {"win_k": cache_win_k, "win_v": cache_win_v, "nat_k": cache_nat_k, "nat_v": cache_nat_v}

    xp, xs = x_prompt, x_sample
    kvs = []
    for l in range(DEPTH):
        wr_t = w_router[l].T
        wr_hi = wr_t.astype(BF)
        wr_lo = (wr_t - wr_hi.astype(F32)).astype(BF)
        lw = {
            "layer": l,
            "norm1_g": norm1_g[l].reshape(1, D_MODEL),
            "w_in": w_in_bf, "w_branch": w_branch_bf, "w_out": w_out_bf, "dft": dft,
            "conv_w": conv_w[l], "conv_b": conv_b[l],
            "conv_ln_g": conv_ln_g[l], "conv_ln_b": conv_ln_b[l],
            "win_sink": win_sink[l],
            "bias_table": _na_bias_call(na_bias[l], x_sample.shape[1] // GRID_W),
        }
        norm2 = norm2_g[l].reshape(1, D_MODEL)
        final = l == DEPTH - 1
        mod_p = mods[l, 0:1].reshape(1, 1, 6 * D_MODEL)
        mod_s = mods[l, 1:1 + n_dec].reshape(n_dec, 1, 6 * D_MODEL)
        xp, kv = _mixer_half(xp, mod_p, lw, None, tm=256)
        kvs.append(kv)
        xs, _ = _mixer_half(xs, mod_s, lw, ctx, tm=512)
        rows_p, gates_p, slot_p = _route_call(xp, norm2, mod_p, wr_hi, wr_lo)
        rows_s, gates_s, slot_s = _route_call(xs, norm2, mod_s, wr_hi, wr_lo)
        flat = lambda a: a.reshape(N_EXPERTS, a.shape[1] * a.shape[2], a.shape[3])
        y_p, y_s = _expert_call(flat(rows_p), flat(gates_p), flat(rows_s), flat(gates_s),
                                w_e_gate, w_e_up, w_e_down, l)
        xp = _combine_call(slot_p, y_p.reshape(rows_p.shape), xp, mod_p, final_g, final)
        xs = _combine_call(slot_s, y_s.reshape(rows_s.shape), xs, mod_s, final_g, final)

    def heads(lo, n):
        per_layer = [kv[..., lo:lo + n * HEAD_DIM].reshape(kv.shape[0], kv.shape[1], n, HEAD_DIM)
                     .transpose(0, 2, 1, 3) for kv in kvs]
        return jnp.stack(per_layer, axis=1)

    return (xp, xs, heads(0, WIN_KV_HEADS), heads(128, WIN_KV_HEADS), heads(256, NA_HEADS),
            heads(512, NA_HEADS))
```

```python
import functools
import math

import ml_dtypes
import numpy as np
import jax
import jax.numpy as jnp
from jax import lax
from jax.experimental import pallas as pl
from jax.experimental.pallas import tpu as pltpu

D_MODEL = 1024
DEPTH = 2
GRID_W = 64
BRANCH_W = 256
HEAD_DIM = 64
N_BRANCH = 4
CONV_K = 31
FNET_GROUPS = 4
FNET_GDIM = BRANCH_W // FNET_GROUPS
WIN_Q_HEADS = 4
WIN_KV_HEADS = 2
WIN_BLOCK = 128
NA_HEADS = 4
NA_ROWS = 8
NA_COLS = 16
ROPE_THETA = 10000.0
ATTN_SCALE = HEAD_DIM ** -0.5
N_EXPERTS = 16
EXPERT_FF = 1024
EC_CAPACITY = 2
EPS = 1e-6
NEG = -1e30

MIX_COLS = 2048
GATE_COLS = N_BRANCH * D_MODEL
IN_COLS = MIX_COLS + GATE_COLS
COL_A, COL_F, COL_CQ, COL_CKV, COL_DQ, COL_DK, COL_DV = 0, 512, 768, 1024, 1280, 1536, 1792
ROPE_LO, ROPE_HI = COL_CQ, COL_CKV + WIN_KV_HEADS * HEAD_DIM

LANES = 128
SUBLANES = 8
VMEM_LIMIT = 56 * 1024 * 1024

BF = jnp.bfloat16
F32 = jnp.float32


def _params(*sem):
    return pltpu.CompilerParams(dimension_semantics=sem, vmem_limit_bytes=VMEM_LIMIT)


def _dot(a, b):
    return jnp.dot(a, b, preferred_element_type=F32)


def _dot_nt(a, b):
    return lax.dot_general(a, b, (((1,), (1,)), ((), ())), preferred_element_type=F32)


def _norm_mod(x, g, sc, sh):
    y = x * lax.rsqrt(jnp.mean(x * x, axis=-1, keepdims=True) + EPS)
    return (y * g) * (1.0 + sc) + sh


def _silu(x):
    return x * jax.nn.sigmoid(x)


CAST_BYTES = 4 * 1024 * 1024


def _cast_kernel(x_ref, o_ref):
    o_ref[...] = x_ref[...].astype(BF)


def _cast_call(x):
    shape = x.shape
    x2 = x.reshape(-1, shape[-1])
    rows, cols = x2.shape
    tr = min(rows, 1 << (max(16, CAST_BYTES // (4 * cols)).bit_length() - 1))
    assert rows % tr == 0
    out = pl.pallas_call(
        _cast_kernel,
        out_shape=jax.ShapeDtypeStruct((rows, cols), BF),
        grid=(rows // tr,),
        in_specs=[pl.BlockSpec((tr, cols), lambda i: (i, 0))],
        out_specs=pl.BlockSpec((tr, cols), lambda i: (i, 0)),
        compiler_params=_params("arbitrary"),
        name="cast_bf16",
    )(x2)
    return out.reshape(shape)


ADA_ROWS = 16
ADA_TN = 1536


def _ada_kernel(c_ref, w_ref, b_ref, o_ref):
    s = _silu(c_ref[...]).astype(BF)
    o_ref[0] = _dot(s, w_ref[0].astype(BF)) + b_ref[0]


def _ada_call(cond, w_ada, b_ada):
    n = 6 * D_MODEL
    return pl.pallas_call(
        _ada_kernel,
        out_shape=jax.ShapeDtypeStruct((DEPTH, ADA_ROWS, n), F32),
        grid=(DEPTH, n // ADA_TN),
        in_specs=[
            pl.BlockSpec((ADA_ROWS, D_MODEL), lambda l, j: (0, 0)),
            pl.BlockSpec((1, D_MODEL, ADA_TN), lambda l, j: (l, 0, j)),
            pl.BlockSpec((1, 1, ADA_TN), lambda l, j: (l, 0, j)),
        ],
        out_specs=pl.BlockSpec((1, ADA_ROWS, ADA_TN), lambda l, j: (l, 0, j)),
        compiler_params=_params("arbitrary", "arbitrary"),
        name="ada_mod",
    )(cond, w_ada, b_ada.reshape(DEPTH, 1, n))


def _mod_spec(bm, chunk, ngrid):
    if ngrid == 1:
        imap = (lambda b: (b, 0, chunk)) if bm > 1 else (lambda b: (0, 0, chunk))
    else:
        imap = (lambda b, i: (b, 0, chunk)) if bm > 1 else (lambda b, i: (0, 0, chunk))
    return pl.BlockSpec((1, 1, D_MODEL), imap)


def _inproj_kernel(*refs, rope, want_kv):
    x_ref, g_ref, sc_ref, sh_ref, w_ref = refs[:5]
    pos = 5
    if rope:
        cos_ref, sin_ref = refs[pos:pos + 2]
        pos += 2
    z_ref = refs[pos]
    h = _norm_mod(x_ref[0], g_ref[...], sc_ref[0], sh_ref[0]).astype(BF)
    z = _dot(h, w_ref[0])
    if want_kv:
        kv_ref = refs[pos + 1]
        kv_ref[0, :, 0:256] = z[:, COL_CKV:COL_DQ]
        kv_ref[0, :, 256:768] = z[:, COL_DK:MIX_COLS]
    if rope:
        z_ref[0, :, 0:ROPE_LO] = z[:, 0:ROPE_LO].astype(BF)
        lane = lax.broadcasted_iota(jnp.int32, (z.shape[0], LANES), 1)
        first_half = (lane % 32) < 16
        cos = cos_ref[...]
        sin = sin_ref[...]
        for c0 in range(ROPE_LO, ROPE_HI, LANES):
            xc = z[:, c0:c0 + LANES]
            partner = jnp.where(first_half, pltpu.roll(xc, LANES - 16, 1), pltpu.roll(xc, 16, 1))
            z_ref[0, :, c0:c0 + LANES] = (xc * cos + partner * sin).astype(BF)
        z_ref[0, :, ROPE_HI:MIX_COLS] = z[:, ROPE_HI:MIX_COLS].astype(BF)
    else:
        z_ref[0] = z.astype(BF)


def _rope_tables(L):
    t = np.arange(L)
    half = HEAD_DIM // 4
    freqs = 1.0 / (ROPE_THETA ** (np.arange(half, dtype=np.float64) / half))
    ang_r = (t // GRID_W).astype(np.float64)[:, None] * freqs[None, :]
    ang_c = (t % GRID_W).astype(np.float64)[:, None] * freqs[None, :]
    cos_h = np.concatenate([np.cos(ang_r), np.cos(ang_r), np.cos(ang_c), np.cos(ang_c)], axis=1)
    sin_h = np.concatenate([-np.sin(ang_r), np.sin(ang_r), -np.sin(ang_c), np.sin(ang_c)], axis=1)
    reps = LANES // HEAD_DIM
    return (jnp.asarray(np.tile(cos_h, (1, reps)), F32), jnp.asarray(np.tile(sin_h, (1, reps)), F32))


def _inproj_call(x, norm_g, mod, w_in_bf, layer, tm, rope, want_kv):
    B, L, _ = x.shape
    bm = mod.shape[0]
    in_specs = [
        pl.BlockSpec((1, tm, D_MODEL), lambda b, i: (b, i, 0)),
        pl.BlockSpec((1, D_MODEL), lambda b, i: (0, 0)),
        _mod_spec(bm, 1, 2),
        _mod_spec(bm, 0, 2),
        pl.BlockSpec((1, D_MODEL, MIX_COLS), lambda b, i: (layer, 0, 0)),
    ]
    args = [x, norm_g, mod, mod, w_in_bf]
    if rope:
        cos, sin = _rope_tables(L)
        in_specs += [pl.BlockSpec((tm, LANES), lambda b, i: (i, 0))] * 2
        args += [cos, sin]
    out_shape = [jax.ShapeDtypeStruct((B, L, MIX_COLS), BF)]
    out_specs = [pl.BlockSpec((1, tm, MIX_COLS), lambda b, i: (b, i, 0))]
    if want_kv:
        out_shape.append(jax.ShapeDtypeStruct((B, L, 768), F32))
        out_specs.append(pl.BlockSpec((1, tm, 768), lambda b, i: (b, i, 0)))
    return pl.pallas_call(
        functools.partial(_inproj_kernel, rope=rope, want_kv=want_kv),
        out_shape=out_shape,
        grid=(B, L // tm),
        in_specs=in_specs,
        out_specs=out_specs,
        compiler_params=_params("arbitrary", "arbitrary"),
        name="in_proj",
    )(*args)


CONV_PAD = 16
CONV_TR = 64
CONV_CHUNK = 256


def _conv_kernel(a_ref, w_ref, cb_ref, lg_ref, lb_ref, o_ref, p_ref, u_ref, *, L):
    zeros = jnp.zeros((CONV_PAD, BRANCH_W), F32)
    p_ref[0:CONV_PAD, :] = zeros
    p_ref[L + CONV_PAD:L + 2 * CONV_PAD, :] = zeros
    for r0 in range(0, L, CONV_CHUNK):
        a = a_ref[0, r0:r0 + CONV_CHUNK, :].astype(F32)
        p_ref[CONV_PAD + r0:CONV_PAD + r0 + CONV_CHUNK, :] = a[:, :BRANCH_W] * jax.nn.sigmoid(a[:, BRANCH_W:])
    n_u = L + 2 * CONV_PAD - SUBLANES
    for s in range(SUBLANES):
        u_ref[s] = p_ref[s:s + n_u, :]
    cb = cb_ref[...]
    lg = lg_ref[...]
    lb = lb_ref[...]

    def body(i, carry):
        base = pl.multiple_of(i * CONV_TR, CONV_TR)
        acc = jnp.zeros((CONV_TR, BRANCH_W), F32) + cb
        for k in range(CONV_K):
            q, s = divmod(k + CONV_PAD - CONV_K // 2, SUBLANES)
            acc = acc + w_ref[k:k + 1, :] * u_ref[s, pl.ds(base + SUBLANES * q, CONV_TR), :]
        mu = jnp.mean(acc, axis=-1, keepdims=True)
        d = acc - mu
        var = jnp.mean(d * d, axis=-1, keepdims=True)
        y = d * lax.rsqrt(var + EPS) * lg + lb
        o_ref[0, pl.ds(base, CONV_TR), :] = _silu(y).astype(BF)
        return carry

    lax.fori_loop(0, L // CONV_TR, body, 0)


def _conv_call(z, conv_w, conv_b, ln_g, ln_b):
    B, L, _ = z.shape
    n_u = L + 2 * CONV_PAD - SUBLANES
    vec = lambda b: (0, 0)
    return pl.pallas_call(
        functools.partial(_conv_kernel, L=L),
        out_shape=jax.ShapeDtypeStruct((B, L, BRANCH_W), BF),
        grid=(B,),
        in_specs=[
            pl.BlockSpec((1, L, 2 * BRANCH_W), lambda b: (b, 0, COL_A // (2 * BRANCH_W))),
            pl.BlockSpec((CONV_K, BRANCH_W), vec),
            pl.BlockSpec((1, BRANCH_W), vec),
            pl.BlockSpec((1, BRANCH_W), vec),
            pl.BlockSpec((1, BRANCH_W), vec),
        ],
        out_specs=pl.BlockSpec((1, L, BRANCH_W), lambda b: (b, 0, 0)),
        scratch_shapes=[
            pltpu.VMEM((L + 2 * CONV_PAD, BRANCH_W), F32),
            pltpu.VMEM((SUBLANES, n_u, BRANCH_W), F32),
        ],
        compiler_params=_params("arbitrary"),
        name="conv_module",
    )(z, conv_w, conv_b.reshape(1, -1), ln_g.reshape(1, -1), ln_b.reshape(1, -1))


def _dft_tables(L):
    c = np.arange(FNET_GDIM)
    ang_c = 2.0 * np.pi * ((c[:, None] * c[None, :]) % FNET_GDIM) / FNET_GDIM
    cc = np.zeros((BRANCH_W, BRANCH_W))
    sc = np.zeros((BRANCH_W, BRANCH_W))
    for g in range(FNET_GROUPS):
        sl = slice(g * FNET_GDIM, (g + 1) * FNET_GDIM)
        cc[sl, sl] = np.cos(ang_c)
        sc[sl, sl] = np.sin(ang_c)
    ccs = np.concatenate([cc, sc], axis=1)
    t = np.arange(L)
    ang_l = 2.0 * np.pi * ((t[:, None] * t[None, :]) % L) / L
    csl = np.concatenate([np.cos(ang_l), -np.sin(ang_l)], axis=1)
    return _cast_call(jnp.asarray(ccs, F32)), _cast_call(jnp.asarray(csl, F32))


FOURIER_TR = 512


def _fourier_kernel(f_ref, ccs_ref, csl_ref, o_ref, pq_ref, *, L):
    scale = float(1.0 / np.sqrt(L * FNET_GDIM))
    tr = min(L, FOURIER_TR)
    for r0 in range(0, L, tr):
        pq = _dot(f_ref[0, r0:r0 + tr, :], ccs_ref[...])
        pq_ref[r0:r0 + tr, :] = pq[:, :BRANCH_W].astype(BF)
        pq_ref[L + r0:L + r0 + tr, :] = pq[:, BRANCH_W:].astype(BF)
    for r0 in range(0, L, tr):
        o = _dot(csl_ref[r0:r0 + tr, :], pq_ref[...])
        o_ref[0, r0:r0 + tr, :] = (o * scale).astype(BF)


def _fourier_call(z, ccs, csl):
    B, L, _ = z.shape
    return pl.pallas_call(
        functools.partial(_fourier_kernel, L=L),
        out_shape=jax.ShapeDtypeStruct((B, L, BRANCH_W), BF),
        grid=(B,),
        in_specs=[
            pl.BlockSpec((1, L, BRANCH_W), lambda b: (b, 0, COL_F // BRANCH_W)),
            pl.BlockSpec((BRANCH_W, 2 * BRANCH_W), lambda b: (0, 0)),
            pl.BlockSpec((L, 2 * L), lambda b: (0, 0), pipeline_mode=pl.Buffered(1)),
        ],
        out_specs=pl.BlockSpec((1, L, BRANCH_W), lambda b: (b, 0, 0)),
        scratch_shapes=[pltpu.VMEM((2 * L, BRANCH_W), BF)],
        compiler_params=_params("arbitrary"),
        name="fourier_mix",
    )(z, ccs, csl)


def _head(ref, rows, h):
    return ref[0, rows, h * HEAD_DIM:(h + 1) * HEAD_DIM]


def _ctx_attn_kernel(sink_ref, qc_ref, kvc_ref, qd_ref, kd_ref, vd_ref, oc_ref, od_ref):
    rows = slice(None)
    group = WIN_Q_HEADS // WIN_KV_HEADS
    for h in range(WIN_Q_HEADS):
        kvh = h // group
        s = _dot_nt(_head(qc_ref, rows, h), _head(kvc_ref, rows, kvh)) * ATTN_SCALE
        sink = sink_ref[h]
        m = jnp.maximum(jnp.max(s, axis=-1, keepdims=True), sink)
        p = jnp.exp(s - m)
        den = jnp.sum(p, axis=-1, keepdims=True) + jnp.exp(sink - m)
        o = _dot(p.astype(BF), _head(kvc_ref, rows, WIN_KV_HEADS + kvh)) / den
        oc_ref[0, :, h * HEAD_DIM:(h + 1) * HEAD_DIM] = o.astype(BF)
    for h in range(NA_HEADS):
        s = _dot_nt(_head(qd_ref, rows, h), _head(kd_ref, rows, h)) * ATTN_SCALE
        m = jnp.max(s, axis=-1, keepdims=True)
        p = jnp.exp(s - m)
        den = jnp.sum(p, axis=-1, keepdims=True)
        o = _dot(p.astype(BF), _head(vd_ref, rows, h)) / den
        od_ref[0, :, h * HEAD_DIM:(h + 1) * HEAD_DIM] = o.astype(BF)


def _zcol_spec(L, col):
    return pl.BlockSpec((1, L, BRANCH_W), lambda b: (b, 0, col // BRANCH_W))


def _ctx_attn_call(z, sink):
    B, L, _ = z.shape
    out = jax.ShapeDtypeStruct((B, L, BRANCH_W), BF)
    ospec = pl.BlockSpec((1, L, BRANCH_W), lambda b: (b, 0, 0))
    return pl.pallas_call(
        _ctx_attn_kernel,
        out_shape=[out, out],
        grid=(B,),
        in_specs=[
            pl.BlockSpec(memory_space=pltpu.SMEM),
            _zcol_spec(L, COL_CQ), _zcol_spec(L, COL_CKV),
            _zcol_spec(L, COL_DQ), _zcol_spec(L, COL_DK), _zcol_spec(L, COL_DV),
        ],
        out_specs=[ospec, ospec],
        compiler_params=_params("arbitrary"),
        name="ctx_attention",
    )(sink, z, z, z, z, z)


PAIR = 2 * HEAD_DIM


def _one_head(x, lane, g):
    return jnp.where(lane // HEAD_DIM == g, x, jnp.zeros_like(x))


def _win_attn_kernel(sink_ref, q_ref, kv_ref, ck_ref, cv_ref, o_ref, *, L):
    nb = L // WIN_BLOCK
    group = WIN_Q_HEADS // WIN_KV_HEADS
    m_rows = group * WIN_BLOCK
    assert WIN_BLOCK == LANES
    row = lax.broadcasted_iota(jnp.int32, (m_rows, WIN_BLOCK), 0)
    a = row % WIN_BLOCK
    j = lax.broadcasted_iota(jnp.int32, (m_rows, WIN_BLOCK), 1)
    row1 = lax.broadcasted_iota(jnp.int32, (m_rows, 1), 0)
    for kvh in range(WIN_KV_HEADS):
        ctx_k = ck_ref[0, 0, kvh].astype(BF)
        ctx_v = cv_ref[0, 0, kvh].astype(BF)
        sink = jnp.zeros((m_rows, 1), F32)
        for g in range(group):
            sink = jnp.where(row1 // WIN_BLOCK == g, sink_ref[kvh * group + g], sink)

        def body(n, carry, kvh=kvh, ctx_k=ctx_k, ctx_v=ctx_v, sink=sink):
            r0 = pl.multiple_of(n * WIN_BLOCK, WIN_BLOCK)
            q = jnp.concatenate(
                [_head(q_ref, pl.ds(r0, WIN_BLOCK), kvh * group + g) for g in range(group)], axis=0)
            q = q * ATTN_SCALE
            s_ctx = _dot_nt(q, ctx_k)
            m_el = jnp.maximum(s_ctx[:, :LANES], s_ctx[:, LANES:])
            scores = []
            vals = []
            for dj in (-1, 0, 1):
                nk = jnp.clip(n + dj, 0, nb - 1)
                rk = pl.multiple_of(nk * WIN_BLOCK, WIN_BLOCK)
                s = _dot_nt(q, _head(kv_ref, pl.ds(rk, WIN_BLOCK), kvh))
                if dj == -1:
                    s = jnp.where((j >= a) & (n >= 1), s, NEG)
                elif dj == 1:
                    s = jnp.where((j <= a) & (n <= nb - 2), s, NEG)
                m_el = jnp.maximum(m_el, s)
                scores.append(s)
                vals.append(_head(kv_ref, pl.ds(rk, WIN_BLOCK), WIN_KV_HEADS + kvh))
            m = jnp.maximum(jnp.max(m_el, axis=-1, keepdims=True), sink)
            p_ctx = jnp.exp(s_ctx - m)
            den_el = p_ctx[:, :LANES] + p_ctx[:, LANES:]
            o = _dot(p_ctx.astype(BF), ctx_v)
            for s, v in zip(scores, vals):
                p = jnp.exp(s - m)
                den_el = den_el + p
                o = o + _dot(p.astype(BF), v)
            den = jnp.sum(den_el, axis=-1, keepdims=True) + jnp.exp(sink - m)
            o = (o / den).astype(BF)
            for g in range(group):
                h = kvh * group + g
                o_ref[0, pl.ds(r0, WIN_BLOCK), h * HEAD_DIM:(h + 1) * HEAD_DIM] = (
                    o[g * WIN_BLOCK:(g + 1) * WIN_BLOCK])
            return carry

        lax.fori_loop(0, nb, body, 0, unroll=2)


def _win_attn_call(z, cache_k, cache_v, sink, layer):
    B, L, _ = z.shape
    P = cache_k.shape[3]
    assert P == 2 * LANES
    cspec = pl.BlockSpec((1, 1, WIN_KV_HEADS, P, HEAD_DIM), lambda b: (b, layer, 0, 0, 0))
    return pl.pallas_call(
        functools.partial(_win_attn_kernel, L=L),
        out_shape=jax.ShapeDtypeStruct((B, L, BRANCH_W), BF),
        grid=(B,),
        in_specs=[
            pl.BlockSpec(memory_space=pltpu.SMEM),
            _zcol_spec(L, COL_CQ), _zcol_spec(L, COL_CKV), cspec, cspec,
        ],
        out_specs=pl.BlockSpec((1, L, BRANCH_W), lambda b: (b, 0, 0)),
        compiler_params=_params("arbitrary"),
        name="window_attention",
    )(sink, z, z, cache_k, cache_v)


N_ROW_OFF = 2 * NA_ROWS - 1
N_COL_OFF = 2 * NA_COLS - 1


NA_QROWS = 4
NA_WIN_ROWS = NA_ROWS + NA_QROWS
NA_PATTERNS = 3


def _na_window_start(r0, rows):
    return min(max(r0 - NA_ROWS // 2, 0), rows - NA_WIN_ROWS)


def _na_pattern_blocks(p, rows):
    r0 = (0, 2 * NA_QROWS, rows - NA_QROWS)[p]
    ws = _na_window_start(r0, rows)
    blocks = {}
    for i in range(NA_QROWS):
        r = r0 + i
        start = min(max(r - NA_ROWS // 2, 0), rows - NA_ROWS)
        for kk in range(NA_WIN_ROWS):
            rel = ws + kk - start
            blocks[i, kk] = ws + kk - r + (NA_ROWS - 1) if 0 <= rel < NA_ROWS else None
    return blocks


def _na_bias_kernel(rb_ref, o_ref, *, rows):
    h = pl.program_id(0)
    cq = lax.broadcasted_iota(jnp.int32, (GRID_W, GRID_W), 0)
    ck = lax.broadcasted_iota(jnp.int32, (GRID_W, GRID_W), 1)
    col_start = jnp.clip(cq - NA_COLS // 2, 0, GRID_W - NA_COLS)
    col_ok = (ck >= col_start) & (ck < col_start + NA_COLS)
    col_off = jnp.clip(ck - cq + (NA_COLS - 1), 0, N_COL_OFF - 1)
    tiles = []
    for ro in range(N_ROW_OFF):
        t = jnp.zeros((GRID_W, GRID_W), F32)
        for co in range(N_COL_OFF):
            t = jnp.where(col_off == co, rb_ref[(h * N_ROW_OFF + ro) * N_COL_OFF + co], t)
        tiles.append(jnp.where(col_ok, t, NEG))
    masked = jnp.full((GRID_W, GRID_W), NEG, F32)
    for p in range(NA_PATTERNS):
        for (i, kk), ro in _na_pattern_blocks(p, rows).items():
            o_ref[0, p, i * GRID_W:(i + 1) * GRID_W, kk * GRID_W:(kk + 1) * GRID_W] = (
                masked if ro is None else tiles[ro])


def _na_bias_call(na_bias, rows):
    shape = (NA_HEADS, NA_PATTERNS, NA_QROWS * GRID_W, NA_WIN_ROWS * GRID_W)
    return pl.pallas_call(
        functools.partial(_na_bias_kernel, rows=rows),
        out_shape=jax.ShapeDtypeStruct(shape, F32),
        grid=(NA_HEADS,),
        in_specs=[pl.BlockSpec(memory_space=pltpu.SMEM)],
        out_specs=pl.BlockSpec((1,) + shape[1:], lambda h: (h, 0, 0, 0)),
        compiler_params=_params("arbitrary"),
        name="na_bias_table",
    )(na_bias.reshape(-1))


def _na_attn_kernel(q_ref, k_ref, v_ref, ck_ref, cv_ref, bt_ref, o_ref, *, L):
    rows = L // GRID_W
    groups = rows // NA_QROWS
    lane = lax.broadcasted_iota(jnp.int32, (NA_QROWS * GRID_W, PAIR), 1)
    for hp in range(NA_HEADS * HEAD_DIM // PAIR):
        heads = (2 * hp, 2 * hp + 1)
        lanes = slice(hp * PAIR, (hp + 1) * PAIR)
        ctx_k = jnp.concatenate([ck_ref[0, 0, h].astype(BF) for h in heads], axis=1)
        ctx_v = jnp.concatenate([cv_ref[0, 0, h].astype(BF) for h in heads], axis=1)

        def body(qi, carry, heads=heads, lanes=lanes, ctx_k=ctx_k, ctx_v=ctx_v):
            r0 = qi * NA_QROWS
            ws = jnp.clip(r0 - NA_ROWS // 2, 0, rows - NA_WIN_ROWS)
            pat = jnp.where(qi == 0, 0, jnp.where(qi == groups - 1, 2, 1))
            rq = pl.ds(pl.multiple_of(r0 * GRID_W, NA_QROWS * GRID_W), NA_QROWS * GRID_W)
            rk = pl.ds(pl.multiple_of(ws * GRID_W, GRID_W), NA_WIN_ROWS * GRID_W)
            q2 = q_ref[0, rq, lanes] * ATTN_SCALE
            k2 = k_ref[0, rk, lanes]
            v2 = v_ref[0, rk, lanes]
            outs = []
            for g, h in enumerate(heads):
                qg = _one_head(q2, lane, g)
                s = _dot_nt(qg, k2) + bt_ref[h, pat]
                s_ctx = _dot_nt(qg, ctx_k)
                m = jnp.maximum(jnp.max(s, axis=-1, keepdims=True), jnp.max(s_ctx, axis=-1, keepdims=True))
                p = jnp.exp(s - m)
                p_ctx = jnp.exp(s_ctx - m)
                den = jnp.sum(p, axis=-1, keepdims=True) + jnp.sum(p_ctx, axis=-1, keepdims=True)
                outs.append((_dot(p.astype(BF), v2) + _dot(p_ctx.astype(BF), ctx_v)) / den)
            o_ref[0, rq, lanes] = jnp.where(lane // HEAD_DIM == 0, outs[0], outs[1]).astype(BF)
            return carry

        lax.fori_loop(0, groups, body, 0)


def _na_attn_call(z, cache_k, cache_v, bias_table, layer):
    B, L, _ = z.shape
    P = cache_k.shape[3]
    assert L // GRID_W >= NA_WIN_ROWS + NA_QROWS and (L // GRID_W) % NA_QROWS == 0
    cspec = pl.BlockSpec((1, 1, NA_HEADS, P, HEAD_DIM), lambda b: (b, layer, 0, 0, 0))
    return pl.pallas_call(
        functools.partial(_na_attn_kernel, L=L),
        out_shape=jax.ShapeDtypeStruct((B, L, BRANCH_W), BF),
        grid=(B,),
        in_specs=[
            _zcol_spec(L, COL_DQ), _zcol_spec(L, COL_DK), _zcol_spec(L, COL_DV), cspec, cspec,
            pl.BlockSpec(bias_table.shape, lambda b: (0, 0, 0, 0), pipeline_mode=pl.Buffered(1)),
        ],
        out_specs=pl.BlockSpec((1, L, BRANCH_W), lambda b: (b, 0, 0)),
        compiler_params=_params("arbitrary"),
        name="neighbourhood_attention",
    )(z, z, z, cache_k, cache_v, bias_table)


def _merge_kernel(x_ref, g_ref, sc_ref, sh_ref, gate_ref, ba_ref, bb_ref, bc_ref, bd_ref,
                  wg0_ref, wg1_ref, wb_ref, wo_ref, o_ref):
    x = x_ref[0]
    h = _norm_mod(x, g_ref[...], sc_ref[0], sh_ref[0]).astype(BF)
    merged = jnp.zeros(x.shape, F32)
    per_ref = MIX_COLS // D_MODEL
    for n, br_ref in enumerate((ba_ref, bb_ref, bc_ref, bd_ref)):
        wg_ref = (wg0_ref, wg1_ref)[n // per_ref]
        c0 = (n % per_ref) * D_MODEL
        gate = jax.nn.sigmoid(_dot(h, wg_ref[0, :, c0:c0 + D_MODEL]))
        merged = merged + gate * _dot(br_ref[0], wb_ref[0, n])
    o_ref[0] = x + gate_ref[0] * _dot(merged.astype(BF), wo_ref[0])


def _merge_call(x, norm_g, mod, branches, w_in_bf, w_branch, w_out, layer, tm):
    B, L, _ = x.shape
    bm = mod.shape[0]
    xspec = pl.BlockSpec((1, tm, D_MODEL), lambda b, i: (b, i, 0))
    bspec = pl.BlockSpec((1, tm, BRANCH_W), lambda b, i: (b, i, 0))
    return pl.pallas_call(
        _merge_kernel,
        out_shape=jax.ShapeDtypeStruct((B, L, D_MODEL), F32),
        grid=(B, L // tm),
        in_specs=[
            xspec,
            pl.BlockSpec((1, D_MODEL), lambda b, i: (0, 0)),
            _mod_spec(bm, 1, 2), _mod_spec(bm, 0, 2), _mod_spec(bm, 2, 2),
            bspec, bspec, bspec, bspec,
            pl.BlockSpec((1, D_MODEL, MIX_COLS), lambda b, i: (layer, 0, 1)),
            pl.BlockSpec((1, D_MODEL, MIX_COLS), lambda b, i: (layer, 0, 2)),
            pl.BlockSpec((1, N_BRANCH, BRANCH_W, D_MODEL), lambda b, i: (layer, 0, 0, 0)),
            pl.BlockSpec((1, D_MODEL, D_MODEL), lambda b, i: (layer, 0, 0)),
        ],
        out_specs=xspec,
        compiler_params=_params("arbitrary", "arbitrary"),
        name="merge_out_proj",
    )(x, norm_g, mod, mod, mod, *branches, w_in_bf, w_in_bf, w_branch, w_out)


ROUTE_CHUNK = 256
GATHER_ROWS = 512
SIGN_BIT = 31
SEARCH_BITS = 4
SEARCH_TOP_SHIFT = ((SIGN_BIT - 1) // SEARCH_BITS) * SEARCH_BITS


def _prefix_excl(mask_ref, out_ref, tri, L):
    totals = [jnp.sum(mask_ref[:, c0:c0 + LANES], axis=-1, keepdims=True) for c0 in range(0, L, LANES)]
    run = jnp.zeros((N_EXPERTS, 1), F32)
    for i, c0 in enumerate(range(0, L, LANES)):
        out_ref[:, c0:c0 + LANES] = _dot(mask_ref[:, c0:c0 + LANES].astype(BF), tri) + run
        run = run + totals[i]


def _split3(v):
    hi = v.astype(BF)
    r = v - hi.astype(F32)
    mid = r.astype(BF)
    lo = (r - mid.astype(F32)).astype(BF)
    return hi, mid, lo


def _route_kernel(x_ref, g_ref, sc_ref, sh_ref, wh_ref, wl_ref, tri_ref,
                  xs_ref, gs_ref, slot_t_ref, h_ref, lt_ref, msk_ref, pre_ref, oh_ref, *, L, cap):
    g = g_ref[...]
    sc = sc_ref[0]
    sh = sh_ref[0]
    wh = wh_ref[...]
    wl = wl_ref[...]
    rc = min(L, ROUTE_CHUNK)
    for r0 in range(0, L, rc):
        h = _norm_mod(x_ref[0, r0:r0 + rc, :], g, sc, sh)
        h_hi = h.astype(BF)
        h_ref[r0:r0 + rc, :] = h_hi
        h_lo = (h - h_hi.astype(F32)).astype(BF)
        lt_ref[:, r0:r0 + rc] = _dot_nt(wh, h_hi) + (_dot_nt(wh, h_lo) + _dot_nt(wl, h_hi))
    lt = lt_ref[...]
    e = jnp.exp(lt - jnp.max(lt, axis=0, keepdims=True))
    aff = e / jnp.sum(e, axis=0, keepdims=True)
    thr = jnp.zeros((N_EXPERTS, 1), jnp.int32)
    for shift in range(SEARCH_TOP_SHIFT, -1, -SEARCH_BITS):
        n_digits = min(1 << SEARCH_BITS, 1 << (SIGN_BIT - shift))
        digit = jnp.zeros((N_EXPERTS, 1), jnp.int32)
        for d in range(1, n_digits):
            cand = pltpu.bitcast(thr | (d << shift), F32)
            cnt = jnp.sum((aff >= cand).astype(F32), axis=-1, keepdims=True)
            digit = digit + (cnt >= cap).astype(jnp.int32)
        thr = thr | (digit << shift)
    thr_f = pltpu.bitcast(thr, F32)
    gt = aff > thr_f
    eq = aff == thr_f
    need = cap - jnp.sum(gt.astype(F32), axis=-1, keepdims=True)
    tri = tri_ref[...]
    msk_ref[...] = eq.astype(F32)
    _prefix_excl(msk_ref, pre_ref, tri, L)
    sel = gt | (eq & (pre_ref[...] < need))
    msk_ref[...] = sel.astype(F32)
    _prefix_excl(msk_ref, pre_ref, tri, L)
    slot = jnp.where(sel, pre_ref[...], -1.0)
    pre_ref[...] = slot
    slot_t_ref[0] = slot.T
    pieces = [p.astype(F32) for p in _split3(jnp.where(sel, aff, 0.0))]
    pad = jnp.zeros((LANES - len(pieces) * N_EXPERTS, L), F32)
    gate_cols = jnp.concatenate(pieces + [pad], axis=0).T.astype(BF)

    ge = max(1, min(N_EXPERTS, GATHER_ROWS // cap))
    slot_iota = lax.broadcasted_iota(jnp.int32, (cap, L), 0).astype(F32).astype(BF)
    one = jnp.ones((cap, L), BF)
    zero = jnp.zeros((cap, L), BF)
    for e0 in range(0, N_EXPERTS, ge):
        buf = (e0 // ge) % 2
        for j in range(ge):
            slot_row = pre_ref[e0 + j:e0 + j + 1, :].astype(BF)
            oh_ref[buf, j * cap:(j + 1) * cap, :] = jnp.where(slot_row == slot_iota, one, zero)
        oh = oh_ref[buf]
        rows = _dot(oh, h_ref[...])
        gates = _dot(oh, gate_cols)
        for j in range(ge):
            xs_ref[e0 + j, 0] = rows[j * cap:(j + 1) * cap].astype(BF)
            gs_ref[e0 + j, 0] = gates[j * cap:(j + 1) * cap]


def _route_call(x, norm_g, mod, wr_hi, wr_lo):
    B, L, _ = x.shape
    bm = mod.shape[0]
    cap = EC_CAPACITY * L // N_EXPERTS
    assert cap <= 256
    ge = max(1, min(N_EXPERTS, GATHER_ROWS // cap))
    tri =jnp.asarray(np.triu(np.ones((LANES, LANES)), 1).astype(ml_dtypes.bfloat16))
    wspec = pl.BlockSpec((N_EXPERTS, D_MODEL), lambda b: (0, 0))
    return pl.pallas_call(
        functools.partial(_route_kernel, L=L, cap=cap),
        out_shape=[
            jax.ShapeDtypeStruct((N_EXPERTS, B, cap, D_MODEL), BF),
            jax.ShapeDtypeStruct((N_EXPERTS, B, cap, LANES), F32),
            jax.ShapeDtypeStruct((B, L, N_EXPERTS), F32),
        ],
        grid=(B,),
        in_specs=[
            pl.BlockSpec((1, L, D_MODEL), lambda b: (b, 0, 0)),
            pl.BlockSpec((1, D_MODEL), lambda b: (0, 0)),
            _mod_spec(bm, 4, 1), _mod_spec(bm, 3, 1),
            wspec, wspec,
            pl.BlockSpec((LANES, LANES), lambda b: (0, 0)),
        ],
        out_specs=[
            pl.BlockSpec((N_EXPERTS, 1, cap, D_MODEL), lambda b: (0, b, 0, 0)),
            pl.BlockSpec((N_EXPERTS, 1, cap, LANES), lambda b: (0, b, 0, 0)),
            pl.BlockSpec((1, L, N_EXPERTS), lambda b: (b, 0, 0)),
        ],
        scratch_shapes=[
            pltpu.VMEM((L, D_MODEL), BF),
            pltpu.VMEM((N_EXPERTS, L), F32),
            pltpu.VMEM((N_EXPERTS, L), F32),
            pltpu.VMEM((N_EXPERTS, L), F32),
            pltpu.VMEM((2, ge * cap, L), BF),
        ],
        compiler_params=_params("arbitrary"),
        name="ec_route_gather",
    )(x, norm_g, mod, mod, wr_hi, wr_lo, tri)


EXPERT_TR = 512


def _expert_kernel(xa_ref, ga_ref, xb_ref, gb_ref, wg_ref, wu_ref, wd_ref, ya_ref, yb_ref,
                   wgb_ref, wub_ref, wdb_ref, *, tiles_a):
    ex = pl.program_id(0)
    j = pl.program_id(1)

    @pl.when(j == 0)
    def _():
        wgb_ref[...] = wg_ref[0, 0].astype(BF)
        wub_ref[...] = wu_ref[0, 0].astype(BF)
        wdb_ref[...] = wd_ref[0, 0].astype(BF)

    def ffn(x_ref, g_ref, y_ref):
        x = x_ref[0]
        a = _dot(x, wgb_ref[...])
        u = _dot(x, wub_ref[...])
        y = _dot((_silu(a) * u).astype(BF), wdb_ref[...])
        gates = g_ref[0]
        lane = lax.broadcasted_iota(jnp.int32, gates.shape, 1)
        gate = jnp.sum(jnp.where(lane % N_EXPERTS == ex, gates, 0.0), axis=-1, keepdims=True)
        y_ref[0] = (y * gate).astype(BF)

    @pl.when(j < tiles_a)
    def _():
        ffn(xa_ref, ga_ref, ya_ref)

    @pl.when(j >= tiles_a)
    def _():
        ffn(xb_ref, gb_ref, yb_ref)


def _expert_call(xs_a, gs_a, xs_b, gs_b, w_gate, w_up, w_down, layer):
    E, ra, _ = xs_a.shape
    rb = xs_b.shape[1]
    tr = math.gcd(math.gcd(ra, rb), EXPERT_TR)
    tiles_a, tiles_b = ra // tr, rb // tr
    amap = lambda e, j: (e, jnp.minimum(j, tiles_a - 1), 0)
    bmap = lambda e, j: (e, jnp.maximum(j - tiles_a, 0), 0)
    wspec_in = pl.BlockSpec((1, 1, D_MODEL, EXPERT_FF), lambda e, j: (layer, e, 0, 0))
    wspec_out = pl.BlockSpec((1, 1, EXPERT_FF, D_MODEL), lambda e, j: (layer, e, 0, 0))
    return pl.pallas_call(
        functools.partial(_expert_kernel, tiles_a=tiles_a),
        out_shape=[jax.ShapeDtypeStruct((E, ra, D_MODEL), BF), jax.ShapeDtypeStruct((E, rb, D_MODEL), BF)],
        grid=(E, tiles_a + tiles_b),
        in_specs=[
            pl.BlockSpec((1, tr, D_MODEL), amap), pl.BlockSpec((1, tr, LANES), amap),
            pl.BlockSpec((1, tr, D_MODEL), bmap), pl.BlockSpec((1, tr, LANES), bmap),
            wspec_in, wspec_in, wspec_out,
        ],
        out_specs=[pl.BlockSpec((1, tr, D_MODEL), amap), pl.BlockSpec((1, tr, D_MODEL), bmap)],
        scratch_shapes=[
            pltpu.VMEM((D_MODEL, EXPERT_FF), BF),
            pltpu.VMEM((D_MODEL, EXPERT_FF), BF),
            pltpu.VMEM((EXPERT_FF, D_MODEL), BF),
        ],
        compiler_params=_params("arbitrary", "arbitrary"),
        name="ec_experts",
    )(xs_a, gs_a, xs_b, gs_b, w_gate, w_up, w_down)


COMBINE_TL = 256


def _combine_kernel(st_ref, y_ref, x_ref, gate_ref, fg_ref, o_ref, *, cap, final):
    st = st_ref[0]
    tl = st.shape[0]
    if cap % LANES == 0 and cap <= 256:
        lane = lax.broadcasted_iota(jnp.int32, (tl, cap), 1).astype(F32).astype(BF)
        one = jnp.ones((tl, cap), BF)
        zero = jnp.zeros((tl, cap), BF)
        st_bf = st.astype(BF)
        onehot = jnp.concatenate(
            [jnp.where(st_bf[:, ex:ex + 1] == lane, one, zero) for ex in range(N_EXPERTS)], axis=1)
    else:
        lane = lax.broadcasted_iota(jnp.int32, (tl, N_EXPERTS * cap), 1).astype(F32)
        hit = None
        for ex in range(N_EXPERTS):
            col = st[:, ex:ex + 1]
            m = jnp.where(col >= 0.0, col + float(ex * cap), -1.0) == lane
            hit = m if hit is None else (hit | m)
        onehot = jnp.where(hit, 1.0, 0.0).astype(BF)
    y_all = y_ref[:, 0].reshape(N_EXPERTS * cap, D_MODEL)
    v = x_ref[0] + gate_ref[0] * _dot(onehot, y_all)
    if final:
        v = v * lax.rsqrt(jnp.mean(v * v, axis=-1, keepdims=True) + EPS) * fg_ref[...]
    o_ref[0] = v


def _combine_call(slot_t, y, x, mod, final_g, final):
    B, L, _ = x.shape
    bm = mod.shape[0]
    cap = y.shape[2]
    tl = min(L, COMBINE_TL)
    return pl.pallas_call(
        functools.partial(_combine_kernel, cap=cap, final=final),
        out_shape=jax.ShapeDtypeStruct((B, L, D_MODEL), F32),
        grid=(B, L // tl),
        in_specs=[
            pl.BlockSpec((1, tl, N_EXPERTS), lambda b, i: (b, i, 0)),
            pl.BlockSpec((N_EXPERTS, 1, cap, D_MODEL), lambda b, i: (0, b, 0, 0)),
            pl.BlockSpec((1, tl, D_MODEL), lambda b, i: (b, i, 0)),
            _mod_spec(bm, 5, 2),
            pl.BlockSpec((1, D_MODEL), lambda b, i: (0, 0)),
        ],
        out_specs=pl.BlockSpec((1, tl, D_MODEL), lambda b, i: (b, i, 0)),
        compiler_params=_params("arbitrary", "arbitrary"),
        name="ec_combine",
    )(slot_t, y, x, mod, final_g)


def _mixer_half(x, mod, lw, ctx, tm):
    latent = ctx is not None
    layer = lw["layer"]
    res = _inproj_call(x, lw["norm1_g"], mod, lw["w_in"], layer, tm, rope=latent, want_kv=not latent)
    z = res[0]
    br_a = _conv_call(z, lw["conv_w"], lw["conv_b"], lw["conv_ln_g"], lw["conv_ln_b"])
    br_b = _fourier_call(z, *lw["dft"][z.shape[1]])
    if latent:
        br_c = _win_attn_call(z, ctx["win_k"], ctx["win_v"], lw["win_sink"], layer)
        br_d = _na_attn_call(z, ctx["nat_k"], ctx["nat_v"], lw["bias_table"], layer)
    else:
        br_c, br_d = _ctx_attn_call(z, lw["win_sink"])
    x = _merge_call(x, lw["norm1_g"], mod, (br_a, br_b, br_c, br_d),
                    lw["w_in"], lw["w_branch"], lw["w_out"], layer, tm)
    return x, (res[1] if not latent else None)


def kernel(x_prompt, x_sample, cache_win_k, cache_win_v, cache_nat_k, cache_nat_v, c, c_ctx,
           norm1_g, norm2_g, w_ada, b_ada, w_in, conv_w, conv_b, conv_ln_g, conv_ln_b,
           win_sink, na_bias, w_branch, w_out, w_router, w_e_gate, w_e_up, w_e_down, final_norm_g):
    n_dec = c.shape[0]
    cond = jnp.zeros((ADA_ROWS, D_MODEL), F32).at[0].set(c_ctx).at[1:1 + n_dec].set(c)
    mods = _ada_call(cond, w_ada, b_ada)
    final_g = final_norm_g.reshape(1, D_MODEL)
    w_in_bf = _cast_call(w_in)
    w_branch_bf = _cast_call(w_branch)
    w_out_bf = _cast_call(w_out)
    dft = {L: _dft_tables(L) for L in (x_prompt.shape[1], x_sample.shape[1])}
    ctx = {"win_k": cache_win_k, "win_v": cache_win_v, "nat_k": cache_nat_k, "nat_v": cache_nat_v}

    xp, xs = x_prompt, x_sample
    kvs = []
    for l in range(DEPTH):
        wr_t = w_router[l].T
        wr_hi = wr_t.astype(BF)
        wr_lo = (wr_t - wr_hi.astype(F32)).astype(BF)
        lw = {
            "layer": l,
            "norm1_g": norm1_g[l].reshape(1, D_MODEL),
            "w_in": w_in_bf, "w_branch": w_branch_bf, "w_out": w_out_bf, "dft": dft,
            "conv_w": conv_w[l], "conv_b": conv_b[l],
            "conv_ln_g": conv_ln_g[l], "conv_ln_b": conv_ln_b[l],
            "win_sink": win_sink[l],
            "bias_table": _na_bias_call(na_bias[l], x_sample.shape[1] // GRID_W),
        }
        norm2 = norm2_g[l].reshape(1, D_MODEL)
        final = l == DEPTH - 1
        mod_p = mods[l, 0:1].reshape(1, 1, 6 * D_MODEL)
        mod_s = mods[l, 1:1 + n_dec].reshape(n_dec, 1, 6 * D_MODEL)
        xp, kv = _mixer_half(xp, mod_p, lw, None, tm=256)
        kvs.append(kv)
        xs, _ = _mixer_half(xs, mod_s, lw, ctx, tm=512)
        rows_p, gates_p, slot_p = _route_call(xp, norm2, mod_p, wr_hi, wr_lo)
        rows_s, gates_s, slot_s = _route_call(xs, norm2, mod_s, wr_hi, wr_lo)
        flat = lambda a: a.reshape(N_EXPERTS, a.shape[1] * a.shape[2], a.shape[3])
        y_p, y_s = _expert_call(flat(rows_p), flat(gates_p), flat(rows_s), flat(gates_s),
                                w_e_gate, w_e_up, w_e_down, l)
        xp = _combine_call(slot_p, y_p.reshape(rows_p.shape), xp, mod_p, final_g, final)
        xs = _combine_call(slot_s, y_s.reshape(rows_s.shape), xs, mod_s, final_g, final)

    def heads(lo, n):
        per_layer = [kv[..., lo:lo + n * HEAD_DIM].reshape(kv.shape[0], kv.shape[1], n, HEAD_DIM)
                     .transpose(0, 2, 1, 3) for kv in kvs]
        return jnp.stack(per_layer, axis=1)

    return (xp, xs, heads(0, WIN_KV_HEADS), heads(128, WIN_KV_HEADS), heads(256, NA_HEADS),
            heads(512, NA_HEADS))
```

```python
import functools
import math

import ml_dtypes
import numpy as np
import jax
import jax.numpy as jnp
from jax import lax
from jax.experimental import pallas as pl
from jax.experimental.pallas import tpu as pltpu

D_MODEL = 1024
DEPTH = 2
GRID_W = 64
BRANCH_W = 256
HEAD_DIM = 64
N_BRANCH = 4
CONV_K = 31
FNET_GROUPS = 4
FNET_GDIM = BRANCH_W // FNET_GROUPS
WIN_Q_HEADS = 4
WIN_KV_HEADS = 2
WIN_BLOCK = 128
NA_HEADS = 4
NA_ROWS = 8
NA_COLS = 16
ROPE_THETA = 10000.0
ATTN_SCALE = HEAD_DIM ** -0.5
N_EXPERTS = 16
EXPERT_FF = 1024
EC_CAPACITY = 2
EPS = 1e-6
NEG = -1e30

MIX_COLS = 2048
GATE_COLS = N_BRANCH * D_MODEL
IN_COLS = MIX_COLS + GATE_COLS
COL_A, COL_F, COL_CQ, COL_CKV, COL_DQ, COL_DK, COL_DV = 0, 512, 768, 1024, 1280, 1536, 1792
ROPE_LO, ROPE_HI = COL_CQ, COL_CKV + WIN_KV_HEADS * HEAD_DIM

LANES = 128
SUBLANES = 8
VMEM_LIMIT = 56 * 1024 * 1024

BF = jnp.bfloat16
F32 = jnp.float32


def _params(*sem):
    return pltpu.CompilerParams(dimension_semantics=sem, vmem_limit_bytes=VMEM_LIMIT)


def _dot(a, b):
    return jnp.dot(a, b, preferred_element_type=F32)


def _dot_nt(a, b):
    return lax.dot_general(a, b, (((1,), (1,)), ((), ())), preferred_element_type=F32)


def _norm_mod(x, g, sc, sh):
    y = x * lax.rsqrt(jnp.mean(x * x, axis=-1, keepdims=True) + EPS)
    return (y * g) * (1.0 + sc) + sh


def _silu(x):
    return x * jax.nn.sigmoid(x)


CAST_BYTES = 4 * 1024 * 1024


def _cast_kernel(x_ref, o_ref):
    o_ref[...] = x_ref[...].astype(BF)


def _cast_call(x):
    shape = x.shape
    x2 = x.reshape(-1, shape[-1])
    rows, cols = x2.shape
    tr = min(rows, 1 << (max(16, CAST_BYTES // (4 * cols)).bit_length() - 1))
    assert rows % tr == 0
    out = pl.pallas_call(
        _cast_kernel,
        out_shape=jax.ShapeDtypeStruct((rows, cols), BF),
        grid=(rows // tr,),
        in_specs=[pl.BlockSpec((tr, cols), lambda i: (i, 0))],
        out_specs=pl.BlockSpec((tr, cols), lambda i: (i, 0)),
        compiler_params=_params("arbitrary"),
        name="cast_bf16",
    )(x2)
    return out.reshape(shape)


ADA_ROWS = 16
ADA_TN = 1536


def _ada_kernel(c_ref, w_ref, b_ref, o_ref):
    s = _silu(c_ref[...]).astype(BF)
    o_ref[0] = _dot(s, w_ref[0].astype(BF)) + b_ref[0]


def _ada_call(cond, w_ada, b_ada):
    n = 6 * D_MODEL
    return pl.pallas_call(
        _ada_kernel,
        out_shape=jax.ShapeDtypeStruct((DEPTH, ADA_ROWS, n), F32),
        grid=(DEPTH, n // ADA_TN),
        in_specs=[
            pl.BlockSpec((ADA_ROWS, D_MODEL), lambda l, j: (0, 0)),
            pl.BlockSpec((1, D_MODEL, ADA_TN), lambda l, j: (l, 0, j)),
            pl.BlockSpec((1, 1, ADA_TN), lambda l, j: (l, 0, j)),
        ],
        out_specs=pl.BlockSpec((1, ADA_ROWS, ADA_TN), lambda l, j: (l, 0, j)),
        compiler_params=_params("arbitrary", "arbitrary"),
        name="ada_mod",
    )(cond, w_ada, b_ada.reshape(DEPTH, 1, n))


def _mod_spec(bm, chunk, ngrid):
    if ngrid == 1:
        imap = (lambda b: (b, 0, chunk)) if bm > 1 else (lambda b: (0, 0, chunk))
    else:
        imap = (lambda b, i: (b, 0, chunk)) if bm > 1 else (lambda b, i: (0, 0, chunk))
    return pl.BlockSpec((1, 1, D_MODEL), imap)


def _inproj_kernel(*refs, rope, want_kv):
    x_ref, g_ref, sc_ref, sh_ref, w_ref = refs[:5]
    pos = 5
    if rope:
        cos_ref, sin_ref = refs[pos:pos + 2]
        pos += 2
    z_ref = refs[pos]
    h = _norm_mod(x_ref[0], g_ref[...], sc_ref[0], sh_ref[0]).astype(BF)
    z = _dot(h, w_ref[0])
    if want_kv:
        kv_ref = refs[pos + 1]
        kv_ref[0, :, 0:256] = z[:, COL_CKV:COL_DQ]
        kv_ref[0, :, 256:768] = z[:, COL_DK:MIX_COLS]
    if rope:
        z_ref[0, :, 0:ROPE_LO] = z[:, 0:ROPE_LO].astype(BF)
        lane = lax.broadcasted_iota(jnp.int32, (z.shape[0], LANES), 1)
        first_half = (lane % 32) < 16
        cos = cos_ref[...]
        sin = sin_ref[...]
        for c0 in range(ROPE_LO, ROPE_HI, LANES):
            xc = z[:, c0:c0 + LANES]
            partner = jnp.where(first_half, pltpu.roll(xc, LANES - 16, 1), pltpu.roll(xc, 16, 1))
            z_ref[0, :, c0:c0 + LANES] = (xc * cos + partner * sin).astype(BF)
        z_ref[0, :, ROPE_HI:MIX_COLS] = z[:, ROPE_HI:MIX_COLS].astype(BF)
    else:
        z_ref[0] = z.astype(BF)


def _rope_tables(L):
    t = np.arange(L)
    half = HEAD_DIM // 4
    freqs = 1.0 / (ROPE_THETA ** (np.arange(half, dtype=np.float64) / half))
    ang_r = (t // GRID_W).astype(np.float64)[:, None] * freqs[None, :]
    ang_c = (t % GRID_W).astype(np.float64)[:, None] * freqs[None, :]
    cos_h = np.concatenate([np.cos(ang_r), np.cos(ang_r), np.cos(ang_c), np.cos(ang_c)], axis=1)
    sin_h = np.concatenate([-np.sin(ang_r), np.sin(ang_r), -np.sin(ang_c), np.sin(ang_c)], axis=1)
    reps = LANES // HEAD_DIM
    return (jnp.asarray(np.tile(cos_h, (1, reps)), F32), jnp.asarray(np.tile(sin_h, (1, reps)), F32))


def _inproj_call(x, norm_g, mod, w_in_bf, layer, tm, rope, want_kv):
    B, L, _ = x.shape
    bm = mod.shape[0]
    in_specs = [
        pl.BlockSpec((1, tm, D_MODEL), lambda b, i: (b, i, 0)),
        pl.BlockSpec((1, D_MODEL), lambda b, i: (0, 0)),
        _mod_spec(bm, 1, 2),
        _mod_spec(bm, 0, 2),
        pl.BlockSpec((1, D_MODEL, MIX_COLS), lambda b, i: (layer, 0, 0)),
    ]
    args = [x, norm_g, mod, mod, w_in_bf]
    if rope:
        cos, sin = _rope_tables(L)
        in_specs += [pl.BlockSpec((tm, LANES), lambda b, i: (i, 0))] * 2
        args += [cos, sin]
    out_shape = [jax.ShapeDtypeStruct((B, L, MIX_COLS), BF)]
    out_specs = [pl.BlockSpec((1, tm, MIX_COLS), lambda b, i: (b, i, 0))]
    if want_kv:
        out_shape.append(jax.ShapeDtypeStruct((B, L, 768), F32))
        out_specs.append(pl.BlockSpec((1, tm, 768), lambda b, i: (b, i, 0)))
    return pl.pallas_call(
        functools.partial(_inproj_kernel, rope=rope, want_kv=want_kv),
        out_shape=out_shape,
        grid=(B, L // tm),
        in_specs=in_specs,
        out_specs=out_specs,
        compiler_params=_params("arbitrary", "arbitrary"),
        name="in_proj",
    )(*args)


CONV_PAD = 16
CONV_TR = 64
CONV_CHUNK = 256


def _conv_kernel(a_ref, w_ref, cb_ref, lg_ref, lb_ref, o_ref, p_ref, u_ref, wb_ref, *, L):
    for k in range(CONV_K):
        wb_ref[k] = jnp.broadcast_to(w_ref[k:k + 1, :], (SUBLANES, BRANCH_W))
    zeros = jnp.zeros((CONV_PAD, BRANCH_W), F32)
    p_ref[0:CONV_PAD, :] = zeros
    p_ref[L + CONV_PAD:L + 2 * CONV_PAD, :] = zeros
    for r0 in range(0, L, CONV_CHUNK):
        a = a_ref[0, r0:r0 + CONV_CHUNK, :].astype(F32)
        p_ref[CONV_PAD + r0:CONV_PAD + r0 + CONV_CHUNK, :] = a[:, :BRANCH_W] * jax.nn.sigmoid(a[:, BRANCH_W:])
    n_u = L + 2 * CONV_PAD - SUBLANES
    for s in range(SUBLANES):
        u_ref[s] = p_ref[s:s + n_u, :]
    cb = cb_ref[...]

    def body(i, carry):
        base = pl.multiple_of(i * CONV_TR, CONV_TR)
        groups = (CONV_TR // SUBLANES, SUBLANES, BRANCH_W)
        acc = jnp.zeros(groups, F32) + cb
        for k in range(CONV_K):
            q, s = divmod(k + CONV_PAD - CONV_K // 2, SUBLANES)
            u = u_ref[s, pl.ds(base + SUBLANES * q, CONV_TR), :]
            acc = acc + wb_ref[k] * u.reshape(groups)
        p_ref[pl.ds(base, CONV_TR), :] = acc.reshape(CONV_TR, BRANCH_W)
        return carry

    lax.fori_loop(0, L // CONV_TR, body, 0)
    lg = lg_ref[...]
    lb = lb_ref[...]
    for r0 in range(0, L, CONV_CHUNK):
        acc = p_ref[r0:r0 + CONV_CHUNK, :]
        mu = jnp.mean(acc, axis=-1, keepdims=True)
        d = acc - mu
        var = jnp.mean(d * d, axis=-1, keepdims=True)
        y = d * lax.rsqrt(var + EPS) * lg + lb
        o_ref[0, r0:r0 + CONV_CHUNK, :] = _silu(y).astype(BF)


def _conv_call(z, conv_w, conv_b, ln_g, ln_b):
    B, L, _ = z.shape
    n_u = L + 2 * CONV_PAD - SUBLANES
    vec = lambda b: (0, 0)
    return pl.pallas_call(
        functools.partial(_conv_kernel, L=L),
        out_shape=jax.ShapeDtypeStruct((B, L, BRANCH_W), BF),
        grid=(B,),
        in_specs=[
            pl.BlockSpec((1, L, 2 * BRANCH_W), lambda b: (b, 0, COL_A // (2 * BRANCH_W))),
            pl.BlockSpec((CONV_K, BRANCH_W), vec),
            pl.BlockSpec((1, BRANCH_W), vec),
            pl.BlockSpec((1, BRANCH_W), vec),
            pl.BlockSpec((1, BRANCH_W), vec),
        ],
        out_specs=pl.BlockSpec((1, L, BRANCH_W), lambda b: (b, 0, 0)),
        scratch_shapes=[
            pltpu.VMEM((L + 2 * CONV_PAD, BRANCH_W), F32),
            pltpu.VMEM((SUBLANES, n_u, BRANCH_W), F32),
            pltpu.VMEM((CONV_K, SUBLANES, BRANCH_W), F32),
        ],
        compiler_params=_params("arbitrary"),
        name="conv_module",
    )(z, conv_w, conv_b.reshape(1, -1), ln_g.reshape(1, -1), ln_b.reshape(1, -1))


def _dft_tables(L):
    c = np.arange(FNET_GDIM)
    ang_c = 2.0 * np.pi * ((c[:, None] * c[None, :]) % FNET_GDIM) / FNET_GDIM
    cc = np.zeros((BRANCH_W, BRANCH_W))
    sc = np.zeros((BRANCH_W, BRANCH_W))
    for g in range(FNET_GROUPS):
        sl = slice(g * FNET_GDIM, (g + 1) * FNET_GDIM)
        cc[sl, sl] = np.cos(ang_c)
        sc[sl, sl] = np.sin(ang_c)
    ccs = np.concatenate([cc, sc], axis=1)
    t = np.arange(L)
    ang_l = 2.0 * np.pi * ((t[:, None] * t[None, :]) % L) / L
    csl = np.concatenate([np.cos(ang_l), -np.sin(ang_l)], axis=1)
    return _cast_call(jnp.asarray(ccs, F32)), _cast_call(jnp.asarray(csl, F32))


FOURIER_TR = 512


def _fourier_kernel(f_ref, ccs_ref, csl_ref, o_ref, pq_ref, *, L):
    scale = float(1.0 / np.sqrt(L * FNET_GDIM))
    tr = min(L, FOURIER_TR)
    for r0 in range(0, L, tr):
        pq = _dot(f_ref[0, r0:r0 + tr, :], ccs_ref[...])
        pq_ref[r0:r0 + tr, :] = pq[:, :BRANCH_W].astype(BF)
        pq_ref[L + r0:L + r0 + tr, :] = pq[:, BRANCH_W:].astype(BF)
    for r0 in range(0, L, tr):
        o = _dot(csl_ref[r0:r0 + tr, :], pq_ref[...])
        o_ref[0, r0:r0 + tr, :] = (o * scale).astype(BF)


def _fourier_call(z, ccs, csl):
    B, L, _ = z.shape
    return pl.pallas_call(
        functools.partial(_fourier_kernel, L=L),
        out_shape=jax.ShapeDtypeStruct((B, L, BRANCH_W), BF),
        grid=(B,),
        in_specs=[
            pl.BlockSpec((1, L, BRANCH_W), lambda b: (b, 0, COL_F // BRANCH_W)),
            pl.BlockSpec((BRANCH_W, 2 * BRANCH_W), lambda b: (0, 0)),
            pl.BlockSpec((L, 2 * L), lambda b: (0, 0), pipeline_mode=pl.Buffered(1)),
        ],
        out_specs=pl.BlockSpec((1, L, BRANCH_W), lambda b: (b, 0, 0)),
        scratch_shapes=[pltpu.VMEM((2 * L, BRANCH_W), BF)],
        compiler_params=_params("arbitrary"),
        name="fourier_mix",
    )(z, ccs, csl)


def _head(ref, rows, h):
    return ref[0, rows, h * HEAD_DIM:(h + 1) * HEAD_DIM]


def _ctx_attn_kernel(sink_ref, qc_ref, kvc_ref, qd_ref, kd_ref, vd_ref, oc_ref, od_ref):
    rows = slice(None)
    group = WIN_Q_HEADS // WIN_KV_HEADS
    for h in range(WIN_Q_HEADS):
        kvh = h // group
        s = _dot_nt(_head(qc_ref, rows, h), _head(kvc_ref, rows, kvh)) * ATTN_SCALE
        sink = sink_ref[h]
        m = jnp.maximum(jnp.max(s, axis=-1, keepdims=True), sink)
        p = jnp.exp(s - m)
        den = jnp.sum(p, axis=-1, keepdims=True) + jnp.exp(sink - m)
        o = _dot(p.astype(BF), _head(kvc_ref, rows, WIN_KV_HEADS + kvh)) / den
        oc_ref[0, :, h * HEAD_DIM:(h + 1) * HEAD_DIM] = o.astype(BF)
    for h in range(NA_HEADS):
        s = _dot_nt(_head(qd_ref, rows, h), _head(kd_ref, rows, h)) * ATTN_SCALE
        m = jnp.max(s, axis=-1, keepdims=True)
        p = jnp.exp(s - m)
        den = jnp.sum(p, axis=-1, keepdims=True)
        o = _dot(p.astype(BF), _head(vd_ref, rows, h)) / den
        od_ref[0, :, h * HEAD_DIM:(h + 1) * HEAD_DIM] = o.astype(BF)


def _zcol_spec(L, col):
    return pl.BlockSpec((1, L, BRANCH_W), lambda b: (b, 0, col // BRANCH_W))


def _ctx_attn_call(z, sink):
    B, L, _ = z.shape
    out = jax.ShapeDtypeStruct((B, L, BRANCH_W), BF)
    ospec = pl.BlockSpec((1, L, BRANCH_W), lambda b: (b, 0, 0))
    return pl.pallas_call(
        _ctx_attn_kernel,
        out_shape=[out, out],
        grid=(B,),
        in_specs=[
            pl.BlockSpec(memory_space=pltpu.SMEM),
            _zcol_spec(L, COL_CQ), _zcol_spec(L, COL_CKV),
            _zcol_spec(L, COL_DQ), _zcol_spec(L, COL_DK), _zcol_spec(L, COL_DV),
        ],
        out_specs=[ospec, ospec],
        compiler_params=_params("arbitrary"),
        name="ctx_attention",
    )(sink, z, z, z, z, z)


PAIR = 2 * HEAD_DIM


def _one_head(x, lane, g):
    return jnp.where(lane // HEAD_DIM == g, x, jnp.zeros_like(x))


def _win_attn_kernel(sink_ref, q_ref, kv_ref, ck_ref, cv_ref, o_ref, *, L):
    nb = L // WIN_BLOCK
    group = WIN_Q_HEADS // WIN_KV_HEADS
    m_rows = group * WIN_BLOCK
    assert WIN_BLOCK == LANES
    row = lax.broadcasted_iota(jnp.int32, (m_rows, WIN_BLOCK), 0)
    a = row % WIN_BLOCK
    j = lax.broadcasted_iota(jnp.int32, (m_rows, WIN_BLOCK), 1)
    row1 = lax.broadcasted_iota(jnp.int32, (m_rows, 1), 0)
    for kvh in range(WIN_KV_HEADS):
        ctx_k = ck_ref[0, 0, kvh].astype(BF)
        ctx_v = cv_ref[0, 0, kvh].astype(BF)
        sink = jnp.zeros((m_rows, 1), F32)
        for g in range(group):
            sink = jnp.where(row1 // WIN_BLOCK == g, sink_ref[kvh * group + g], sink)

        def body(n, carry, kvh=kvh, ctx_k=ctx_k, ctx_v=ctx_v, sink=sink):
            r0 = pl.multiple_of(n * WIN_BLOCK, WIN_BLOCK)
            q = jnp.concatenate(
                [_head(q_ref, pl.ds(r0, WIN_BLOCK), kvh * group + g) for g in range(group)], axis=0)
            q = q * ATTN_SCALE
            s_ctx = _dot_nt(q, ctx_k)
            m_el = jnp.maximum(s_ctx[:, :LANES], s_ctx[:, LANES:])
            scores = []
            vals = []
            for dj in (-1, 0, 1):
                nk = jnp.clip(n + dj, 0, nb - 1)
                rk = pl.multiple_of(nk * WIN_BLOCK, WIN_BLOCK)
                s = _dot_nt(q, _head(kv_ref, pl.ds(rk, WIN_BLOCK), kvh))
                if dj == -1:
                    s = jnp.where((j >= a) & (n >= 1), s, NEG)
                elif dj == 1:
                    s = jnp.where((j <= a) & (n <= nb - 2), s, NEG)
                m_el = jnp.maximum(m_el, s)
                scores.append(s)
                vals.append(_head(kv_ref, pl.ds(rk, WIN_BLOCK), WIN_KV_HEADS + kvh))
            m = jnp.maximum(jnp.max(m_el, axis=-1, keepdims=True), sink)
            p_ctx = jnp.exp(s_ctx - m)
            den_el = p_ctx[:, :LANES] + p_ctx[:, LANES:]
            o = _dot(p_ctx.astype(BF), ctx_v)
            for s, v in zip(scores, vals):
                p = jnp.exp(s - m)
                den_el = den_el + p
                o = o + _dot(p.astype(BF), v)
            den = jnp.sum(den_el, axis=-1, keepdims=True) + jnp.exp(sink - m)
            o = (o / den).astype(BF)
            for g in range(group):
                h = kvh * group + g
                o_ref[0, pl.ds(r0, WIN_BLOCK), h * HEAD_DIM:(h + 1) * HEAD_DIM] = (
                    o[g * WIN_BLOCK:(g + 1) * WIN_BLOCK])
            return carry

        lax.fori_loop(0, nb, body, 0, unroll=2)


def _win_attn_call(z, cache_k, cache_v, sink, layer):
    B, L, _ = z.shape
    P = cache_k.shape[3]
    assert P == 2 * LANES
    cspec = pl.BlockSpec((1, 1, WIN_KV_HEADS, P, HEAD_DIM), lambda b: (b, layer, 0, 0, 0))
    return pl.pallas_call(
        functools.partial(_win_attn_kernel, L=L),
        out_shape=jax.ShapeDtypeStruct((B, L, BRANCH_W), BF),
        grid=(B,),
        in_specs=[
            pl.BlockSpec(memory_space=pltpu.SMEM),
            _zcol_spec(L, COL_CQ), _zcol_spec(L, COL_CKV), cspec, cspec,
        ],
        out_specs=pl.BlockSpec((1, L, BRANCH_W), lambda b: (b, 0, 0)),
        compiler_params=_params("arbitrary"),
        name="window_attention",
    )(sink, z, z, cache_k, cache_v)


N_ROW_OFF = 2 * NA_ROWS - 1
N_COL_OFF = 2 * NA_COLS - 1


NA_QROWS = 4
NA_WIN_ROWS = NA_ROWS + NA_QROWS
NA_PATTERNS = 3


def _na_window_start(r0, rows):
    return min(max(r0 - NA_ROWS // 2, 0), rows - NA_WIN_ROWS)


def _na_pattern_blocks(p, rows):
    r0 = (0, 2 * NA_QROWS, rows - NA_QROWS)[p]
    ws = _na_window_start(r0, rows)
    blocks = {}
    for i in range(NA_QROWS):
        r = r0 + i
        start = min(max(r - NA_ROWS // 2, 0), rows - NA_ROWS)
        for kk in range(NA_WIN_ROWS):
            rel = ws + kk - start
            blocks[i, kk] = ws + kk - r + (NA_ROWS - 1) if 0 <= rel < NA_ROWS else None
    return blocks


def _na_bias_kernel(rb_ref, o_ref, *, rows):
    h = pl.program_id(0)
    cq = lax.broadcasted_iota(jnp.int32, (GRID_W, GRID_W), 0)
    ck = lax.broadcasted_iota(jnp.int32, (GRID_W, GRID_W), 1)
    col_start = jnp.clip(cq - NA_COLS // 2, 0, GRID_W - NA_COLS)
    col_ok = (ck >= col_start) & (ck < col_start + NA_COLS)
    col_off = jnp.clip(ck - cq + (NA_COLS - 1), 0, N_COL_OFF - 1)
    tiles = []
    for ro in range(N_ROW_OFF):
        t = jnp.zeros((GRID_W, GRID_W), F32)
        for co in range(N_COL_OFF):
            t = jnp.where(col_off == co, rb_ref[(h * N_ROW_OFF + ro) * N_COL_OFF + co], t)
        tiles.append(jnp.where(col_ok, t, NEG))
    masked = jnp.full((GRID_W, GRID_W), NEG, F32)
    for p in range(NA_PATTERNS):
        for (i, kk), ro in _na_pattern_blocks(p, rows).items():
            o_ref[0, p, i * GRID_W:(i + 1) * GRID_W, kk * GRID_W:(kk + 1) * GRID_W] = (
                masked if ro is None else tiles[ro])


def _na_bias_call(na_bias, rows):
    shape = (NA_HEADS, NA_PATTERNS, NA_QROWS * GRID_W, NA_WIN_ROWS * GRID_W)
    return pl.pallas_call(
        functools.partial(_na_bias_kernel, rows=rows),
        out_shape=jax.ShapeDtypeStruct(shape, F32),
        grid=(NA_HEADS,),
        in_specs=[pl.BlockSpec(memory_space=pltpu.SMEM)],
        out_specs=pl.BlockSpec((1,) + shape[1:], lambda h: (h, 0, 0, 0)),
        compiler_params=_params("arbitrary"),
        name="na_bias_table",
    )(na_bias.reshape(-1))


def _na_attn_kernel(q_ref, k_ref, v_ref, ck_ref, cv_ref, bt_ref, o_ref, *, L):
    rows = L // GRID_W
    groups = rows // NA_QROWS
    lane = lax.broadcasted_iota(jnp.int32, (NA_QROWS * GRID_W, PAIR), 1)
    for hp in range(NA_HEADS * HEAD_DIM // PAIR):
        heads = (2 * hp, 2 * hp + 1)
        lanes = slice(hp * PAIR, (hp + 1) * PAIR)
        ctx_k = jnp.concatenate([ck_ref[0, 0, h].astype(BF) for h in heads], axis=1)
        ctx_v = jnp.concatenate([cv_ref[0, 0, h].astype(BF) for h in heads], axis=1)

        def body(qi, carry, heads=heads, lanes=lanes, ctx_k=ctx_k, ctx_v=ctx_v):
            r0 = qi * NA_QROWS
            ws = jnp.clip(r0 - NA_ROWS // 2, 0, rows - NA_WIN_ROWS)
            pat = jnp.where(qi == 0, 0, jnp.where(qi == groups - 1, 2, 1))
            rq = pl.ds(pl.multiple_of(r0 * GRID_W, NA_QROWS * GRID_W), NA_QROWS * GRID_W)
            rk = pl.ds(pl.multiple_of(ws * GRID_W, GRID_W), NA_WIN_ROWS * GRID_W)
            q2 = q_ref[0, rq, lanes] * ATTN_SCALE
            k2 = k_ref[0, rk, lanes]
            v2 = v_ref[0, rk, lanes]
            outs = []
            for g, h in enumerate(heads):
                qg = _one_head(q2, lane, g)
                s = _dot_nt(qg, k2) + bt_ref[h, pat]
                s_ctx = _dot_nt(qg, ctx_k)
                m = jnp.maximum(jnp.max(s, axis=-1, keepdims=True), jnp.max(s_ctx, axis=-1, keepdims=True))
                p = jnp.exp(s - m)
                p_ctx = jnp.exp(s_ctx - m)
                den = jnp.sum(p, axis=-1, keepdims=True) + jnp.sum(p_ctx, axis=-1, keepdims=True)
                outs.append((_dot(p.astype(BF), v2) + _dot(p_ctx.astype(BF), ctx_v)) / den)
            o_ref[0, rq, lanes] = jnp.where(lane // HEAD_DIM == 0, outs[0], outs[1]).astype(BF)
            return carry

        lax.fori_loop(0, groups, body, 0)


def _na_attn_call(z, cache_k, cache_v, bias_table, layer):
    B, L, _ = z.shape
    P = cache_k.shape[3]
    assert L // GRID_W >= NA_WIN_ROWS + NA_QROWS and (L // GRID_W) % NA_QROWS == 0
    cspec = pl.BlockSpec((1, 1, NA_HEADS, P, HEAD_DIM), lambda b: (b, layer, 0, 0, 0))
    return pl.pallas_call(
        functools.partial(_na_attn_kernel, L=L),
        out_shape=jax.ShapeDtypeStruct((B, L, BRANCH_W), BF),
        grid=(B,),
        in_specs=[
            _zcol_spec(L, COL_DQ), _zcol_spec(L, COL_DK), _zcol_spec(L, COL_DV), cspec, cspec,
            pl.BlockSpec(bias_table.shape, lambda b: (0, 0, 0, 0), pipeline_mode=pl.Buffered(1)),
        ],
        out_specs=pl.BlockSpec((1, L, BRANCH_W), lambda b: (b, 0, 0)),
        compiler_params=_params("arbitrary"),
        name="neighbourhood_attention",
    )(z, z, z, cache_k, cache_v, bias_table)


def _merge_kernel(x_ref, g_ref, sc_ref, sh_ref, gate_ref, ba_ref, bb_ref, bc_ref, bd_ref,
                  wg0_ref, wg1_ref, wb_ref, wo_ref, o_ref):
    x = x_ref[0]
    h = _norm_mod(x, g_ref[...], sc_ref[0], sh_ref[0]).astype(BF)
    merged = jnp.zeros(x.shape, F32)
    per_ref = MIX_COLS // D_MODEL
    for n, br_ref in enumerate((ba_ref, bb_ref, bc_ref, bd_ref)):
        wg_ref = (wg0_ref, wg1_ref)[n // per_ref]
        c0 = (n % per_ref) * D_MODEL
        gate = jax.nn.sigmoid(_dot(h, wg_ref[0, :, c0:c0 + D_MODEL]))
        merged = merged + gate * _dot(br_ref[0], wb_ref[0, n])
    o_ref[0] = x + gate_ref[0] * _dot(merged.astype(BF), wo_ref[0])


def _merge_call(x, norm_g, mod, branches, w_in_bf, w_branch, w_out, layer, tm):
    B, L, _ = x.shape
    bm = mod.shape[0]
    xspec = pl.BlockSpec((1, tm, D_MODEL), lambda b, i: (b, i, 0))
    bspec = pl.BlockSpec((1, tm, BRANCH_W), lambda b, i: (b, i, 0))
    return pl.pallas_call(
        _merge_kernel,
        out_shape=jax.ShapeDtypeStruct((B, L, D_MODEL), F32),
        grid=(B, L // tm),
        in_specs=[
            xspec,
            pl.BlockSpec((1, D_MODEL), lambda b, i: (0, 0)),
            _mod_spec(bm, 1, 2), _mod_spec(bm, 0, 2), _mod_spec(bm, 2, 2),
            bspec, bspec, bspec, bspec,
            pl.BlockSpec((1, D_MODEL, MIX_COLS), lambda b, i: (layer, 0, 1)),
            pl.BlockSpec((1, D_MODEL, MIX_COLS), lambda b, i: (layer, 0, 2)),
            pl.BlockSpec((1, N_BRANCH, BRANCH_W, D_MODEL), lambda b, i: (layer, 0, 0, 0)),
            pl.BlockSpec((1, D_MODEL, D_MODEL), lambda b, i: (layer, 0, 0)),
        ],
        out_specs=xspec,
        compiler_params=_params("arbitrary", "arbitrary"),
        name="merge_out_proj",
    )(x, norm_g, mod, mod, mod, *branches, w_in_bf, w_in_bf, w_branch, w_out)


ROUTE_CHUNK = 256
GATHER_ROWS = 512
SIGN_BIT = 31
SEARCH_BITS = 4
SEARCH_TOP_SHIFT = ((SIGN_BIT - 1) // SEARCH_BITS) * SEARCH_BITS


def _prefix_excl(mask_ref, out_ref, tri, L):
    totals = [jnp.sum(mask_ref[:, c0:c0 + LANES], axis=-1, keepdims=True) for c0 in range(0, L, LANES)]
    run = jnp.zeros((N_EXPERTS, 1), F32)
    for i, c0 in enumerate(range(0, L, LANES)):
        out_ref[:, c0:c0 + LANES] = _dot(mask_ref[:, c0:c0 + LANES].astype(BF), tri) + run
        run = run + totals[i]


def _split3(v):
    hi = v.astype(BF)
    r = v - hi.astype(F32)
    mid = r.astype(BF)
    lo = (r - mid.astype(F32)).astype(BF)
    return hi, mid, lo


def _route_kernel(x_ref, g_ref, sc_ref, sh_ref, wh_ref, wl_ref, tri_ref,
                  xs_ref, gs_ref, slot_t_ref, off_ref, h_ref, lt_ref, msk_ref, pre_ref, oh_ref, *, L, cap):
    g = g_ref[...]
    sc = sc_ref[0]
    sh = sh_ref[0]
    wh = wh_ref[...]
    wl = wl_ref[...]
    rc = min(L, ROUTE_CHUNK)
    for r0 in range(0, L, rc):
        h = _norm_mod(x_ref[0, r0:r0 + rc, :], g, sc, sh)
        h_hi = h.astype(BF)
        h_ref[r0:r0 + rc, :] = h_hi
        h_lo = (h - h_hi.astype(F32)).astype(BF)
        lt_ref[:, r0:r0 + rc] = _dot_nt(wh, h_hi) + (_dot_nt(wh, h_lo) + _dot_nt(wl, h_hi))
    lt = lt_ref[...]
    e = jnp.exp(lt - jnp.max(lt, axis=0, keepdims=True))
    aff = e / jnp.sum(e, axis=0, keepdims=True)
    thr = jnp.zeros((N_EXPERTS, 1), jnp.int32)
    for shift in range(SEARCH_TOP_SHIFT, -1, -SEARCH_BITS):
        n_digits = min(1 << SEARCH_BITS, 1 << (SIGN_BIT - shift))
        digit = jnp.zeros((N_EXPERTS, 1), jnp.int32)
        for d in range(1, n_digits):
            cand = pltpu.bitcast(thr | (d << shift), F32)
            cnt = jnp.sum((aff >= cand).astype(F32), axis=-1, keepdims=True)
            digit = digit + (cnt >= cap).astype(jnp.int32)
        thr = thr | (digit << shift)
    thr_f = pltpu.bitcast(thr, F32)
    gt = aff > thr_f
    eq = aff == thr_f
    need = cap - jnp.sum(gt.astype(F32), axis=-1, keepdims=True)
    tri = tri_ref[...]
    msk_ref[...] = eq.astype(F32)
    _prefix_excl(msk_ref, pre_ref, tri, L)
    sel = gt | (eq & (pre_ref[...] < need))
    msk_ref[...] = sel.astype(F32)
    _prefix_excl(msk_ref, pre_ref, tri, L)
    off_ref[0] = jnp.concatenate([pre_ref[:, c:c + 1] for c in range(0, L, min(L, COMBINE_TL))], axis=1)
    slot = jnp.where(sel, pre_ref[...], -1.0)
    pre_ref[...] = slot
    slot_t_ref[0] = slot.T
    pieces = [p.astype(F32) for p in _split3(jnp.where(sel, aff, 0.0))]
    pad = jnp.zeros((LANES - len(pieces) * N_EXPERTS, L), F32)
    gate_cols = jnp.concatenate(pieces + [pad], axis=0).T.astype(BF)

    ge = max(1, min(N_EXPERTS, GATHER_ROWS // cap))
    slot_iota = lax.broadcasted_iota(jnp.int32, (cap, L), 0).astype(F32).astype(BF)
    one = jnp.ones((cap, L), BF)
    zero = jnp.zeros((cap, L), BF)
    for e0 in range(0, N_EXPERTS, ge):
        buf = (e0 // ge) % 2
        for j in range(ge):
            slot_row = pre_ref[e0 + j:e0 + j + 1, :].astype(BF)
            oh_ref[buf, j * cap:(j + 1) * cap, :] = jnp.where(slot_row == slot_iota, one, zero)
        oh = oh_ref[buf]
        rows = _dot(oh, h_ref[...])
        gates = _dot(oh, gate_cols)
        for j in range(ge):
            xs_ref[e0 + j, 0] = rows[j * cap:(j + 1) * cap].astype(BF)
            gs_ref[e0 + j, 0] = gates[j * cap:(j + 1) * cap]


def _route_call(x, norm_g, mod, wr_hi, wr_lo):
    B, L, _ = x.shape
    bm = mod.shape[0]
    cap = EC_CAPACITY * L // N_EXPERTS
    assert cap <= 256
    n_tiles = L // min(L, COMBINE_TL)
    ge = max(1, min(N_EXPERTS, GATHER_ROWS // cap))
    tri =jnp.asarray(np.triu(np.ones((LANES, LANES)), 1).astype(ml_dtypes.bfloat16))
    wspec = pl.BlockSpec((N_EXPERTS, D_MODEL), lambda b: (0, 0))
    return pl.pallas_call(
        functools.partial(_route_kernel, L=L, cap=cap),
        out_shape=[
            jax.ShapeDtypeStruct((N_EXPERTS, B, cap, D_MODEL), BF),
            jax.ShapeDtypeStruct((N_EXPERTS, B, cap, LANES), F32),
            jax.ShapeDtypeStruct((B, L, N_EXPERTS), F32),
            jax.ShapeDtypeStruct((B, N_EXPERTS, n_tiles), F32),
        ],
        grid=(B,),
        in_specs=[
            pl.BlockSpec((1, L, D_MODEL), lambda b: (b, 0, 0)),
            pl.BlockSpec((1, D_MODEL), lambda b: (0, 0)),
            _mod_spec(bm, 4, 1), _mod_spec(bm, 3, 1),
            wspec, wspec,
            pl.BlockSpec((LANES, LANES), lambda b: (0, 0)),
        ],
        out_specs=[
            pl.BlockSpec((N_EXPERTS, 1, cap, D_MODEL), lambda b: (0, b, 0, 0)),
            pl.BlockSpec((N_EXPERTS, 1, cap, LANES), lambda b: (0, b, 0, 0)),
            pl.BlockSpec((1, L, N_EXPERTS), lambda b: (b, 0, 0)),
            pl.BlockSpec((1, N_EXPERTS, n_tiles), lambda b: (b, 0, 0)),
        ],
        scratch_shapes=[
            pltpu.VMEM((L, D_MODEL), BF),
            pltpu.VMEM((N_EXPERTS, L), F32),
            pltpu.VMEM((N_EXPERTS, L), F32),
            pltpu.VMEM((N_EXPERTS, L), F32),
            pltpu.VMEM((2, ge * cap, L), BF),
        ],
        compiler_params=_params("arbitrary"),
        name="ec_route_gather",
    )(x, norm_g, mod, mod, wr_hi, wr_lo, tri)


EXPERT_TR = 512


def _expert_kernel(xa_ref, ga_ref, xb_ref, gb_ref, wg_ref, wu_ref, wd_ref, ya_ref, yb_ref,
                   wgb_ref, wub_ref, wdb_ref, *, tiles_a):
    ex = pl.program_id(0)
    j = pl.program_id(1)

    @pl.when(j == 0)
    def _():
        wgb_ref[...] = wg_ref[0, 0].astype(BF)
        wub_ref[...] = wu_ref[0, 0].astype(BF)
        wdb_ref[...] = wd_ref[0, 0].astype(BF)

    def ffn(x_ref, g_ref, y_ref):
        x = x_ref[0]
        a = _dot(x, wgb_ref[...])
        u = _dot(x, wub_ref[...])
        y = _dot((_silu(a) * u).astype(BF), wdb_ref[...])
        gates = g_ref[0]
        lane = lax.broadcasted_iota(jnp.int32, gates.shape, 1)
        gate = jnp.sum(jnp.where(lane % N_EXPERTS == ex, gates, 0.0), axis=-1, keepdims=True)
        y_ref[0] = (y * gate).astype(BF)

    @pl.when(j < tiles_a)
    def _():
        ffn(xa_ref, ga_ref, ya_ref)

    @pl.when(j >= tiles_a)
    def _():
        ffn(xb_ref, gb_ref, yb_ref)


def _expert_call(xs_a, gs_a, xs_b, gs_b, w_gate, w_up, w_down, layer):
    E, ra, _ = xs_a.shape
    rb = xs_b.shape[1]
    tr = math.gcd(math.gcd(ra, rb), EXPERT_TR)
    tiles_a, tiles_b = ra // tr, rb // tr
    amap = lambda e, j: (e, jnp.minimum(j, tiles_a - 1), 0)
    bmap = lambda e, j: (e, jnp.maximum(j - tiles_a, 0), 0)
    wspec_in = pl.BlockSpec((1, 1, D_MODEL, EXPERT_FF), lambda e, j: (layer, e, 0, 0))
    wspec_out = pl.BlockSpec((1, 1, EXPERT_FF, D_MODEL), lambda e, j: (layer, e, 0, 0))
    return pl.pallas_call(
        functools.partial(_expert_kernel, tiles_a=tiles_a),
        out_shape=[jax.ShapeDtypeStruct((E, ra, D_MODEL), BF), jax.ShapeDtypeStruct((E, rb, D_MODEL), BF)],
        grid=(E, tiles_a + tiles_b),
        in_specs=[
            pl.BlockSpec((1, tr, D_MODEL), amap), pl.BlockSpec((1, tr, LANES), amap),
            pl.BlockSpec((1, tr, D_MODEL), bmap), pl.BlockSpec((1, tr, LANES), bmap),
            wspec_in, wspec_in, wspec_out,
        ],
        out_specs=[pl.BlockSpec((1, tr, D_MODEL), amap), pl.BlockSpec((1, tr, D_MODEL), bmap)],
        scratch_shapes=[
            pltpu.VMEM((D_MODEL, EXPERT_FF), BF),
            pltpu.VMEM((D_MODEL, EXPERT_FF), BF),
            pltpu.VMEM((EXPERT_FF, D_MODEL), BF),
        ],
        compiler_params=_params("arbitrary", "arbitrary"),
        name="ec_experts",
    )(xs_a, gs_a, xs_b, gs_b, w_gate, w_up, w_down)


COMBINE_TL = 256
COMBINE_WIN = 64
BF_TILE = 16


def _combine_kernel(off_ref, st_ref, y_ref, x_ref, gate_ref, fg_ref, o_ref, *, cap, final):
    st = st_ref[0]
    tl = st.shape[0]

    def finish(scattered):
        v = x_ref[0] + gate_ref[0] * scattered
        if final:
            v = v * lax.rsqrt(jnp.mean(v * v, axis=-1, keepdims=True) + EPS) * fg_ref[...]
        o_ref[0] = v

    def full_onehot():
        lane = lax.broadcasted_iota(jnp.int32, (tl, N_EXPERTS * cap), 1).astype(F32)
        hit = None
        for ex in range(N_EXPERTS):
            col = st[:, ex:ex + 1]
            m = jnp.where(col >= 0.0, col + float(ex * cap), -1.0) == lane
            hit = m if hit is None else (hit | m)
        return jnp.where(hit, 1.0, 0.0).astype(BF)

    def scatter_all():
        finish(_dot(full_onehot(), y_ref[:, 0].reshape(N_EXPERTS * cap, D_MODEL)))

    if cap <= COMBINE_WIN:
        scatter_all()
        return

    n_tiles = pl.num_programs(1)
    base = (pl.program_id(0) * n_tiles + pl.program_id(1)) * N_EXPERTS
    starts = []
    fits = None
    for ex in range(N_EXPERTS):
        first = off_ref[base + ex]
        end = jnp.where(pl.program_id(1) == n_tiles - 1, cap, off_ref[base + N_EXPERTS + ex])
        start = jnp.minimum((first // BF_TILE) * BF_TILE, cap - COMBINE_WIN)
        ok = end <= start + COMBINE_WIN
        fits = ok if fits is None else (fits & ok)
        starts.append(start)

    @pl.when(fits)
    def _():
        lane = lax.broadcasted_iota(jnp.int32, (tl, LANES), 1)
        per_vreg = LANES // COMBINE_WIN
        pieces = []
        rows = []
        for e0 in range(0, N_EXPERTS, per_vreg):
            target = jnp.full((tl, LANES), -1.0, F32)
            for k in range(per_vreg):
                ex = e0 + k
                col = st[:, ex:ex + 1]
                rel = jnp.where(col >= 0.0, col - starts[ex].astype(F32) + float(k * COMBINE_WIN), -1.0)
                target = jnp.where(lane // COMBINE_WIN == k, rel, target)
                rows.append(y_ref[ex, 0, pl.ds(pl.multiple_of(starts[ex], BF_TILE), COMBINE_WIN), :])
            pieces.append(jnp.where(target == lane.astype(F32), 1.0, 0.0).astype(BF))
        finish(_dot(jnp.concatenate(pieces, axis=1), jnp.concatenate(rows, axis=0)))

    @pl.when(jnp.logical_not(fits))
    def _():
        scatter_all()


def _combine_call(slot_t, offsets, y, x, mod, final_g, final):
    B, L, _ = x.shape
    bm = mod.shape[0]
    cap = y.shape[2]
    tl = min(L, COMBINE_TL)
    assert cap <= COMBINE_WIN or (cap % BF_TILE == 0 and COMBINE_WIN % BF_TILE == 0)
    gate_map = (lambda b, i, off: (b, 0, 5)) if bm > 1 else (lambda b, i, off: (0, 0, 5))
    return pl.pallas_call(
        functools.partial(_combine_kernel, cap=cap, final=final),
        out_shape=jax.ShapeDtypeStruct((B, L, D_MODEL), F32),
        grid_spec=pltpu.PrefetchScalarGridSpec(
            num_scalar_prefetch=1,
            grid=(B, L // tl),
            in_specs=[
                pl.BlockSpec((1, tl, N_EXPERTS), lambda b, i, off: (b, i, 0)),
                pl.BlockSpec((N_EXPERTS, 1, cap, D_MODEL), lambda b, i, off: (0, b, 0, 0)),
                pl.BlockSpec((1, tl, D_MODEL), lambda b, i, off: (b, i, 0)),
                pl.BlockSpec((1, 1, D_MODEL), gate_map),
                pl.BlockSpec((1, D_MODEL), lambda b, i, off: (0, 0)),
            ],
            out_specs=pl.BlockSpec((1, tl, D_MODEL), lambda b, i, off: (b, i, 0)),
        ),
        compiler_params=_params("arbitrary", "arbitrary"),
        name="ec_combine",
    )(offsets, slot_t, y, x, mod, final_g)


def _mixer_half(x, mod, lw, ctx, tm):
    latent = ctx is not None
    layer = lw["layer"]
    res = _inproj_call(x, lw["norm1_g"], mod, lw["w_in"], layer, tm, rope=latent, want_kv=not latent)
    z = res[0]
    br_a = _conv_call(z, lw["conv_w"], lw["conv_b"], lw["conv_ln_g"], lw["conv_ln_b"])
    br_b = _fourier_call(z, *lw["dft"][z.shape[1]])
    if latent:
        br_c = _win_attn_call(z, ctx["win_k"], ctx["win_v"], lw["win_sink"], layer)
        br_d = _na_attn_call(z, ctx["nat_k"], ctx["nat_v"], lw["bias_table"], layer)
    else:
        br_c, br_d = _ctx_attn_call(z, lw["win_sink"])
    x = _merge_call(x, lw["norm1_g"], mod, (br_a, br_b, br_c, br_d),
                    lw["w_in"], lw["w_branch"], lw["w_out"], layer, tm)
    return x, (res[1] if not latent else None)


def kernel(x_prompt, x_sample, cache_win_k, cache_win_v, cache_nat_k, cache_nat_v, c, c_ctx,
           norm1_g, norm2_g, w_ada, b_ada, w_in, conv_w, conv_b, conv_ln_g, conv_ln_b,
           win_sink, na_bias, w_branch, w_out, w_router, w_e_gate, w_e_up, w_e_down, final_norm_g):
    n_dec = c.shape[0]
    cond = jnp.zeros((ADA_ROWS, D_MODEL), F32).at[0].set(c_ctx).at[1:1 + n_dec].set(c)
    mods = _ada_call(cond, w_ada, b_ada)
    final_g = final_norm_g.reshape(1, D_MODEL)
    w_in_bf = _cast_call(w_in)
    w_branch_bf = _cast_call(w_branch)
    w_out_bf = _cast_call(w_out)
    dft = {L: _dft_tables(L) for L in (x_prompt.shape[1], x_sample.shape[1])}
    ctx = {"win_k": cache_win_k, "win_v": cache_win_v, "nat_k": cache_nat_k, "nat_v": cache_nat_v}

    xp, xs = x_prompt, x_sample
    kvs = []
    for l in range(DEPTH):
        wr_t = w_router[l].T
        wr_hi = wr_t.astype(BF)
        wr_lo = (wr_t - wr_hi.astype(F32)).astype(BF)
        lw = {
            "layer": l,
            "norm1_g": norm1_g[l].reshape(1, D_MODEL),
            "w_in": w_in_bf, "w_branch": w_branch_bf, "w_out": w_out_bf, "dft": dft,
            "conv_w": conv_w[l], "conv_b": conv_b[l],
            "conv_ln_g": conv_ln_g[l], "conv_ln_b": conv_ln_b[l],
            "win_sink": win_sink[l],
            "bias_table": _na_bias_call(na_bias[l], x_sample.shape[1] // GRID_W),
        }
        norm2 = norm2_g[l].reshape(1, D_MODEL)
        final = l == DEPTH - 1
        mod_p = mods[l, 0:1].reshape(1, 1, 6 * D_MODEL)
        mod_s = mods[l, 1:1 + n_dec].reshape(n_dec, 1, 6 * D_MODEL)
        xp, kv = _mixer_half(xp, mod_p, lw, None, tm=256)
        kvs.append(kv)
        xs, _ = _mixer_half(xs, mod_s, lw, ctx, tm=512)
        rows_p, gates_p, slot_p, off_p = _route_call(xp, norm2, mod_p, wr_hi, wr_lo)
        rows_s, gates_s, slot_s, off_s = _route_call(xs, norm2, mod_s, wr_hi, wr_lo)
        flat = lambda a: a.reshape(N_EXPERTS, a.shape[1] * a.shape[2], a.shape[3])
        y_p, y_s = _expert_call(flat(rows_p), flat(gates_p), flat(rows_s), flat(gates_s),
                                w_e_gate, w_e_up, w_e_down, l)
        smem = lambda off: jnp.pad(off.transpose(0, 2, 1).astype(jnp.int32).reshape(-1), (0, N_EXPERTS))
        xp = _combine_call(slot_p, smem(off_p), y_p.reshape(rows_p.shape), xp, mod_p, final_g, final)
        xs = _combine_call(slot_s, smem(off_s), y_s.reshape(rows_s.shape), xs, mod_s, final_g, final)

    def heads(lo, n):
        per_layer = [kv[..., lo:lo + n * HEAD_DIM].reshape(kv.shape[0], kv.shape[1], n, HEAD_DIM)
                     .transpose(0, 2, 1, 3) for kv in kvs]
        return jnp.stack(per_layer, axis=1)

    return (xp, xs, heads(0, WIN_KV_HEADS), heads(128, WIN_KV_HEADS), heads(256, NA_HEADS),
            heads(512, NA_HEADS))
```

```python
import functools
import math

import ml_dtypes
import numpy as np
import jax
import jax.numpy as jnp
from jax import lax
from jax.experimental import pallas as pl
from jax.experimental.pallas import tpu as pltpu

D_MODEL = 1024
DEPTH = 2
GRID_W = 64
BRANCH_W = 256
HEAD_DIM = 64
N_BRANCH = 4
CONV_K = 31
FNET_GROUPS = 4
FNET_GDIM = BRANCH_W // FNET_GROUPS
WIN_Q_HEADS = 4
WIN_KV_HEADS = 2
WIN_BLOCK = 128
NA_HEADS = 4
NA_ROWS = 8
NA_COLS = 16
ROPE_THETA = 10000.0
ATTN_SCALE = HEAD_DIM ** -0.5
N_EXPERTS = 16
EXPERT_FF = 1024
EC_CAPACITY = 2
EPS = 1e-6
NEG = -1e30

MIX_COLS = 2048
GATE_COLS = N_BRANCH * D_MODEL
IN_COLS = MIX_COLS + GATE_COLS
COL_A, COL_F, COL_CQ, COL_CKV, COL_DQ, COL_DK, COL_DV = 0, 512, 768, 1024, 1280, 1536, 1792
ROPE_LO, ROPE_HI = COL_CQ, COL_CKV + WIN_KV_HEADS * HEAD_DIM

LANES = 128
SUBLANES = 8
VMEM_LIMIT = 56 * 1024 * 1024

BF = jnp.bfloat16
F32 = jnp.float32


def _params(*sem):
    return pltpu.CompilerParams(dimension_semantics=sem, vmem_limit_bytes=VMEM_LIMIT)


def _dot(a, b):
    return jnp.dot(a, b, preferred_element_type=F32)


def _dot_nt(a, b):
    return lax.dot_general(a, b, (((1,), (1,)), ((), ())), preferred_element_type=F32)


def _norm_mod(x, g, sc, sh):
    y = x * lax.rsqrt(jnp.mean(x * x, axis=-1, keepdims=True) + EPS)
    return (y * g) * (1.0 + sc) + sh


def _silu(x):
    return x * jax.nn.sigmoid(x)


CAST_BYTES = 4 * 1024 * 1024


def _cast_kernel(x_ref, o_ref):
    o_ref[...] = x_ref[...].astype(BF)


def _cast_call(x):
    shape = x.shape
    x2 = x.reshape(-1, shape[-1])
    rows, cols = x2.shape
    tr = min(rows, 1 << (max(16, CAST_BYTES // (4 * cols)).bit_length() - 1))
    assert rows % tr == 0
    out = pl.pallas_call(
        _cast_kernel,
        out_shape=jax.ShapeDtypeStruct((rows, cols), BF),
        grid=(rows // tr,),
        in_specs=[pl.BlockSpec((tr, cols), lambda i: (i, 0))],
        out_specs=pl.BlockSpec((tr, cols), lambda i: (i, 0)),
        compiler_params=_params("arbitrary"),
        name="cast_bf16",
    )(x2)
    return out.reshape(shape)


ADA_ROWS = 16
ADA_TN = 1536


def _ada_kernel(c_ref, w_ref, b_ref, o_ref):
    s = _silu(c_ref[...]).astype(BF)
    o_ref[0] = _dot(s, w_ref[0].astype(BF)) + b_ref[0]


def _ada_call(cond, w_ada, b_ada):
    n = 6 * D_MODEL
    return pl.pallas_call(
        _ada_kernel,
        out_shape=jax.ShapeDtypeStruct((DEPTH, ADA_ROWS, n), F32),
        grid=(DEPTH, n // ADA_TN),
        in_specs=[
            pl.BlockSpec((ADA_ROWS, D_MODEL), lambda l, j: (0, 0)),
            pl.BlockSpec((1, D_MODEL, ADA_TN), lambda l, j: (l, 0, j)),
            pl.BlockSpec((1, 1, ADA_TN), lambda l, j: (l, 0, j)),
        ],
        out_specs=pl.BlockSpec((1, ADA_ROWS, ADA_TN), lambda l, j: (l, 0, j)),
        compiler_params=_params("arbitrary", "arbitrary"),
        name="ada_mod",
    )(cond, w_ada, b_ada.reshape(DEPTH, 1, n))


def _mod_spec(bm, chunk, ngrid):
    if ngrid == 1:
        imap = (lambda b: (b, 0, chunk)) if bm > 1 else (lambda b: (0, 0, chunk))
    else:
        imap = (lambda b, i: (b, 0, chunk)) if bm > 1 else (lambda b, i: (0, 0, chunk))
    return pl.BlockSpec((1, 1, D_MODEL), imap)


def _inproj_kernel(*refs, rope, want_kv):
    x_ref, g_ref, sc_ref, sh_ref, w_ref = refs[:5]
    pos = 5
    if rope:
        cos_ref, sin_ref = refs[pos:pos + 2]
        pos += 2
    z_ref = refs[pos]
    h = _norm_mod(x_ref[0], g_ref[...], sc_ref[0], sh_ref[0]).astype(BF)
    z = _dot(h, w_ref[0])
    if want_kv:
        kv_ref = refs[pos + 1]
        kv_ref[0, :, 0:256] = z[:, COL_CKV:COL_DQ]
        kv_ref[0, :, 256:768] = z[:, COL_DK:MIX_COLS]
    if rope:
        z_ref[0, :, 0:ROPE_LO] = z[:, 0:ROPE_LO].astype(BF)
        lane = lax.broadcasted_iota(jnp.int32, (z.shape[0], LANES), 1)
        first_half = (lane % 32) < 16
        cos = cos_ref[...]
        sin = sin_ref[...]
        for c0 in range(ROPE_LO, ROPE_HI, LANES):
            xc = z[:, c0:c0 + LANES]
            partner = jnp.where(first_half, pltpu.roll(xc, LANES - 16, 1), pltpu.roll(xc, 16, 1))
            z_ref[0, :, c0:c0 + LANES] = (xc * cos + partner * sin).astype(BF)
        z_ref[0, :, ROPE_HI:MIX_COLS] = z[:, ROPE_HI:MIX_COLS].astype(BF)
    else:
        z_ref[0] = z.astype(BF)


def _rope_tables(L):
    t = np.arange(L)
    half = HEAD_DIM // 4
    freqs = 1.0 / (ROPE_THETA ** (np.arange(half, dtype=np.float64) / half))
    ang_r = (t // GRID_W).astype(np.float64)[:, None] * freqs[None, :]
    ang_c = (t % GRID_W).astype(np.float64)[:, None] * freqs[None, :]
    cos_h = np.concatenate([np.cos(ang_r), np.cos(ang_r), np.cos(ang_c), np.cos(ang_c)], axis=1)
    sin_h = np.concatenate([-np.sin(ang_r), np.sin(ang_r), -np.sin(ang_c), np.sin(ang_c)], axis=1)
    reps = LANES // HEAD_DIM
    return (jnp.asarray(np.tile(cos_h, (1, reps)), F32), jnp.asarray(np.tile(sin_h, (1, reps)), F32))


def _inproj_call(x, norm_g, mod, w_in_bf, layer, tm, rope, want_kv):
    B, L, _ = x.shape
    bm = mod.shape[0]
    in_specs = [
        pl.BlockSpec((1, tm, D_MODEL), lambda b, i: (b, i, 0)),
        pl.BlockSpec((1, D_MODEL), lambda b, i: (0, 0)),
        _mod_spec(bm, 1, 2),
        _mod_spec(bm, 0, 2),
        pl.BlockSpec((1, D_MODEL, MIX_COLS), lambda b, i: (layer, 0, 0)),
    ]
    args = [x, norm_g, mod, mod, w_in_bf]
    if rope:
        cos, sin = _rope_tables(L)
        in_specs += [pl.BlockSpec((tm, LANES), lambda b, i: (i, 0))] * 2
        args += [cos, sin]
    out_shape = [jax.ShapeDtypeStruct((B, L, MIX_COLS), BF)]
    out_specs = [pl.BlockSpec((1, tm, MIX_COLS), lambda b, i: (b, i, 0))]
    if want_kv:
        out_shape.append(jax.ShapeDtypeStruct((B, L, 768), F32))
        out_specs.append(pl.BlockSpec((1, tm, 768), lambda b, i: (b, i, 0)))
    return pl.pallas_call(
        functools.partial(_inproj_kernel, rope=rope, want_kv=want_kv),
        out_shape=out_shape,
        grid=(B, L // tm),
        in_specs=in_specs,
        out_specs=out_specs,
        compiler_params=_params("arbitrary", "arbitrary"),
        name="in_proj",
    )(*args)


CONV_PAD = 16
CONV_TR = 64
CONV_CHUNK = 256


def _conv_kernel(a_ref, w_ref, cb_ref, lg_ref, lb_ref, o_ref, p_ref, u_ref, wb_ref, *, L):
    for k in range(CONV_K):
        wb_ref[k] = jnp.broadcast_to(w_ref[k:k + 1, :], (SUBLANES, BRANCH_W))
    zeros = jnp.zeros((CONV_PAD, BRANCH_W), F32)
    p_ref[0:CONV_PAD, :] = zeros
    p_ref[L + CONV_PAD:L + 2 * CONV_PAD, :] = zeros
    for r0 in range(0, L, CONV_CHUNK):
        a = a_ref[0, r0:r0 + CONV_CHUNK, :].astype(F32)
        p_ref[CONV_PAD + r0:CONV_PAD + r0 + CONV_CHUNK, :] = a[:, :BRANCH_W] * jax.nn.sigmoid(a[:, BRANCH_W:])
    n_u = L + 2 * CONV_PAD - SUBLANES
    for s in range(SUBLANES):
        u_ref[s] = p_ref[s:s + n_u, :]
    cb = cb_ref[...]

    def body(i, carry):
        base = pl.multiple_of(i * CONV_TR, CONV_TR)
        groups = (CONV_TR // SUBLANES, SUBLANES, BRANCH_W)
        acc = jnp.zeros(groups, F32) + cb
        for k in range(CONV_K):
            q, s = divmod(k + CONV_PAD - CONV_K // 2, SUBLANES)
            u = u_ref[s, pl.ds(base + SUBLANES * q, CONV_TR), :]
            acc = acc + wb_ref[k] * u.reshape(groups)
        p_ref[pl.ds(base, CONV_TR), :] = acc.reshape(CONV_TR, BRANCH_W)
        return carry

    lax.fori_loop(0, L // CONV_TR, body, 0)
    lg = lg_ref[...]
    lb = lb_ref[...]
    for r0 in range(0, L, CONV_CHUNK):
        acc = p_ref[r0:r0 + CONV_CHUNK, :]
        mu = jnp.mean(acc, axis=-1, keepdims=True)
        d = acc - mu
        var = jnp.mean(d * d, axis=-1, keepdims=True)
        y = d * lax.rsqrt(var + EPS) * lg + lb
        o_ref[0, r0:r0 + CONV_CHUNK, :] = _silu(y).astype(BF)


def _conv_call(z, conv_w, conv_b, ln_g, ln_b):
    B, L, _ = z.shape
    n_u = L + 2 * CONV_PAD - SUBLANES
    vec = lambda b: (0, 0)
    return pl.pallas_call(
        functools.partial(_conv_kernel, L=L),
        out_shape=jax.ShapeDtypeStruct((B, L, BRANCH_W), BF),
        grid=(B,),
        in_specs=[
            pl.BlockSpec((1, L, 2 * BRANCH_W), lambda b: (b, 0, COL_A // (2 * BRANCH_W))),
            pl.BlockSpec((CONV_K, BRANCH_W), vec),
            pl.BlockSpec((1, BRANCH_W), vec),
            pl.BlockSpec((1, BRANCH_W), vec),
            pl.BlockSpec((1, BRANCH_W), vec),
        ],
        out_specs=pl.BlockSpec((1, L, BRANCH_W), lambda b: (b, 0, 0)),
        scratch_shapes=[
            pltpu.VMEM((L + 2 * CONV_PAD, BRANCH_W), F32),
            pltpu.VMEM((SUBLANES, n_u, BRANCH_W), F32),
            pltpu.VMEM((CONV_K, SUBLANES, BRANCH_W), F32),
        ],
        compiler_params=_params("arbitrary"),
        name="conv_module",
    )(z, conv_w, conv_b.reshape(1, -1), ln_g.reshape(1, -1), ln_b.reshape(1, -1))


def _dft_tables(L):
    c = np.arange(FNET_GDIM)
    ang_c = 2.0 * np.pi * ((c[:, None] * c[None, :]) % FNET_GDIM) / FNET_GDIM
    cc = np.zeros((BRANCH_W, BRANCH_W))
    sc = np.zeros((BRANCH_W, BRANCH_W))
    for g in range(FNET_GROUPS):
        sl = slice(g * FNET_GDIM, (g + 1) * FNET_GDIM)
        cc[sl, sl] = np.cos(ang_c)
        sc[sl, sl] = np.sin(ang_c)
    ccs = np.concatenate([cc, sc], axis=1)
    t = np.arange(L)
    ang_l = 2.0 * np.pi * ((t[:, None] * t[None, :]) % L) / L
    csl = np.concatenate([np.cos(ang_l), -np.sin(ang_l)], axis=1)
    return _cast_call(jnp.asarray(ccs, F32)), _cast_call(jnp.asarray(csl, F32))


FOURIER_TR = 512


def _fourier_kernel(f_ref, ccs_ref, csl_ref, o_ref, pq_ref, *, L):
    scale = float(1.0 / np.sqrt(L * FNET_GDIM))
    tr = min(L, FOURIER_TR)
    for r0 in range(0, L, tr):
        pq = _dot(f_ref[0, r0:r0 + tr, :], ccs_ref[...])
        pq_ref[r0:r0 + tr, :] = pq[:, :BRANCH_W].astype(BF)
        pq_ref[L + r0:L + r0 + tr, :] = pq[:, BRANCH_W:].astype(BF)
    for r0 in range(0, L, tr):
        o = _dot(csl_ref[r0:r0 + tr, :], pq_ref[...])
        o_ref[0, r0:r0 + tr, :] = (o * scale).astype(BF)


def _fourier_call(z, ccs, csl):
    B, L, _ = z.shape
    return pl.pallas_call(
        functools.partial(_fourier_kernel, L=L),
        out_shape=jax.ShapeDtypeStruct((B, L, BRANCH_W), BF),
        grid=(B,),
        in_specs=[
            pl.BlockSpec((1, L, BRANCH_W), lambda b: (b, 0, COL_F // BRANCH_W)),
            pl.BlockSpec((BRANCH_W, 2 * BRANCH_W), lambda b: (0, 0)),
            pl.BlockSpec((L, 2 * L), lambda b: (0, 0), pipeline_mode=pl.Buffered(1)),
        ],
        out_specs=pl.BlockSpec((1, L, BRANCH_W), lambda b: (b, 0, 0)),
        scratch_shapes=[pltpu.VMEM((2 * L, BRANCH_W), BF)],
        compiler_params=_params("arbitrary"),
        name="fourier_mix",
    )(z, ccs, csl)


def _head(ref, rows, h):
    return ref[0, rows, h * HEAD_DIM:(h + 1) * HEAD_DIM]


def _ctx_attn_kernel(sink_ref, qc_ref, kvc_ref, qd_ref, kd_ref, vd_ref, oc_ref, od_ref):
    rows = slice(None)
    group = WIN_Q_HEADS // WIN_KV_HEADS
    for h in range(WIN_Q_HEADS):
        kvh = h // group
        s = _dot_nt(_head(qc_ref, rows, h), _head(kvc_ref, rows, kvh)) * ATTN_SCALE
        sink = sink_ref[h]
        m = jnp.maximum(jnp.max(s, axis=-1, keepdims=True), sink)
        p = jnp.exp(s - m)
        den = jnp.sum(p, axis=-1, keepdims=True) + jnp.exp(sink - m)
        o = _dot(p.astype(BF), _head(kvc_ref, rows, WIN_KV_HEADS + kvh)) / den
        oc_ref[0, :, h * HEAD_DIM:(h + 1) * HEAD_DIM] = o.astype(BF)
    for h in range(NA_HEADS):
        s = _dot_nt(_head(qd_ref, rows, h), _head(kd_ref, rows, h)) * ATTN_SCALE
        m = jnp.max(s, axis=-1, keepdims=True)
        p = jnp.exp(s - m)
        den = jnp.sum(p, axis=-1, keepdims=True)
        o = _dot(p.astype(BF), _head(vd_ref, rows, h)) / den
        od_ref[0, :, h * HEAD_DIM:(h + 1) * HEAD_DIM] = o.astype(BF)


def _zcol_spec(L, col):
    return pl.BlockSpec((1, L, BRANCH_W), lambda b: (b, 0, col // BRANCH_W))


def _ctx_attn_call(z, sink):
    B, L, _ = z.shape
    out = jax.ShapeDtypeStruct((B, L, BRANCH_W), BF)
    ospec = pl.BlockSpec((1, L, BRANCH_W), lambda b: (b, 0, 0))
    return pl.pallas_call(
        _ctx_attn_kernel,
        out_shape=[out, out],
        grid=(B,),
        in_specs=[
            pl.BlockSpec(memory_space=pltpu.SMEM),
            _zcol_spec(L, COL_CQ), _zcol_spec(L, COL_CKV),
            _zcol_spec(L, COL_DQ), _zcol_spec(L, COL_DK), _zcol_spec(L, COL_DV),
        ],
        out_specs=[ospec, ospec],
        compiler_params=_params("arbitrary"),
        name="ctx_attention",
    )(sink, z, z, z, z, z)


PAIR = 2 * HEAD_DIM


def _one_head(x, lane, g):
    return jnp.where(lane // HEAD_DIM == g, x, jnp.zeros_like(x))


def _win_attn_kernel(sink_ref, q_ref, kv_ref, ck_ref, cv_ref, o_ref, *, L):
    nb = L // WIN_BLOCK
    group = WIN_Q_HEADS // WIN_KV_HEADS
    m_rows = group * WIN_BLOCK
    assert WIN_BLOCK == LANES
    row = lax.broadcasted_iota(jnp.int32, (m_rows, WIN_BLOCK), 0)
    a = row % WIN_BLOCK
    j = lax.broadcasted_iota(jnp.int32, (m_rows, WIN_BLOCK), 1)
    row1 = lax.broadcasted_iota(jnp.int32, (m_rows, 1), 0)
    for kvh in range(WIN_KV_HEADS):
        ctx_k = ck_ref[0, 0, kvh].astype(BF)
        ctx_v = cv_ref[0, 0, kvh].astype(BF)
        sink = jnp.zeros((m_rows, 1), F32)
        for g in range(group):
            sink = jnp.where(row1 // WIN_BLOCK == g, sink_ref[kvh * group + g], sink)

        def body(n, carry, kvh=kvh, ctx_k=ctx_k, ctx_v=ctx_v, sink=sink):
            r0 = pl.multiple_of(n * WIN_BLOCK, WIN_BLOCK)
            q = jnp.concatenate(
                [_head(q_ref, pl.ds(r0, WIN_BLOCK), kvh * group + g) for g in range(group)], axis=0)
            q = q * ATTN_SCALE
            s_ctx = _dot_nt(q, ctx_k)
            m_el = jnp.maximum(s_ctx[:, :LANES], s_ctx[:, LANES:])
            scores = []
            vals = []
            for dj in (-1, 0, 1):
                nk = jnp.clip(n + dj, 0, nb - 1)
                rk = pl.multiple_of(nk * WIN_BLOCK, WIN_BLOCK)
                s = _dot_nt(q, _head(kv_ref, pl.ds(rk, WIN_BLOCK), kvh))
                if dj == -1:
                    s = jnp.where((j >= a) & (n >= 1), s, NEG)
                elif dj == 1:
                    s = jnp.where((j <= a) & (n <= nb - 2), s, NEG)
                m_el = jnp.maximum(m_el, s)
                scores.append(s)
                vals.append(_head(kv_ref, pl.ds(rk, WIN_BLOCK), WIN_KV_HEADS + kvh))
            m = jnp.maximum(jnp.max(m_el, axis=-1, keepdims=True), sink)
            p_ctx = jnp.exp(s_ctx - m)
            den_el = p_ctx[:, :LANES] + p_ctx[:, LANES:]
            o = _dot(p_ctx.astype(BF), ctx_v)
            for s, v in zip(scores, vals):
                p = jnp.exp(s - m)
                den_el = den_el + p
                o = o + _dot(p.astype(BF), v)
            den = jnp.sum(den_el, axis=-1, keepdims=True) + jnp.exp(sink - m)
            o = (o / den).astype(BF)
            for g in range(group):
                h = kvh * group + g
                o_ref[0, pl.ds(r0, WIN_BLOCK), h * HEAD_DIM:(h + 1) * HEAD_DIM] = (
                    o[g * WIN_BLOCK:(g + 1) * WIN_BLOCK])
            return carry

        lax.fori_loop(0, nb, body, 0, unroll=2)


def _win_attn_call(z, cache_k, cache_v, sink, layer):
    B, L, _ = z.shape
    P = cache_k.shape[3]
    assert P == 2 * LANES
    cspec = pl.BlockSpec((1, 1, WIN_KV_HEADS, P, HEAD_DIM), lambda b: (b, layer, 0, 0, 0))
    return pl.pallas_call(
        functools.partial(_win_attn_kernel, L=L),
        out_shape=jax.ShapeDtypeStruct((B, L, BRANCH_W), BF),
        grid=(B,),
        in_specs=[
            pl.BlockSpec(memory_space=pltpu.SMEM),
            _zcol_spec(L, COL_CQ), _zcol_spec(L, COL_CKV), cspec, cspec,
        ],
        out_specs=pl.BlockSpec((1, L, BRANCH_W), lambda b: (b, 0, 0)),
        compiler_params=_params("arbitrary"),
        name="window_attention",
    )(sink, z, z, cache_k, cache_v)


N_ROW_OFF = 2 * NA_ROWS - 1
N_COL_OFF = 2 * NA_COLS - 1


NA_QROWS = 4
NA_WIN_ROWS = NA_ROWS + NA_QROWS
NA_PATTERNS = 3


def _na_window_start(r0, rows):
    return min(max(r0 - NA_ROWS // 2, 0), rows - NA_WIN_ROWS)


def _na_pattern_blocks(p, rows):
    r0 = (0, 2 * NA_QROWS, rows - NA_QROWS)[p]
    ws = _na_window_start(r0, rows)
    blocks = {}
    for i in range(NA_QROWS):
        r = r0 + i
        start = min(max(r - NA_ROWS // 2, 0), rows - NA_ROWS)
        for kk in range(NA_WIN_ROWS):
            rel = ws + kk - start
            blocks[i, kk] = ws + kk - r + (NA_ROWS - 1) if 0 <= rel < NA_ROWS else None
    return blocks


def _na_bias_kernel(rb_ref, o_ref, *, rows):
    h = pl.program_id(0)
    cq = lax.broadcasted_iota(jnp.int32, (GRID_W, GRID_W), 0)
    ck = lax.broadcasted_iota(jnp.int32, (GRID_W, GRID_W), 1)
    col_start = jnp.clip(cq - NA_COLS // 2, 0, GRID_W - NA_COLS)
    col_ok = (ck >= col_start) & (ck < col_start + NA_COLS)
    col_off = jnp.clip(ck - cq + (NA_COLS - 1), 0, N_COL_OFF - 1)
    tiles = []
    for ro in range(N_ROW_OFF):
        t = jnp.zeros((GRID_W, GRID_W), F32)
        for co in range(N_COL_OFF):
            t = jnp.where(col_off == co, rb_ref[(h * N_ROW_OFF + ro) * N_COL_OFF + co], t)
        tiles.append(jnp.where(col_ok, t, NEG))
    masked = jnp.full((GRID_W, GRID_W), NEG, F32)
    for p in range(NA_PATTERNS):
        for (i, kk), ro in _na_pattern_blocks(p, rows).items():
            o_ref[0, p, i * GRID_W:(i + 1) * GRID_W, kk * GRID_W:(kk + 1) * GRID_W] = (
                masked if ro is None else tiles[ro])


def _na_bias_call(na_bias, rows):
    shape = (NA_HEADS, NA_PATTERNS, NA_QROWS * GRID_W, NA_WIN_ROWS * GRID_W)
    return pl.pallas_call(
        functools.partial(_na_bias_kernel, rows=rows),
        out_shape=jax.ShapeDtypeStruct(shape, F32),
        grid=(NA_HEADS,),
        in_specs=[pl.BlockSpec(memory_space=pltpu.SMEM)],
        out_specs=pl.BlockSpec((1,) + shape[1:], lambda h: (h, 0, 0, 0)),
        compiler_params=_params("arbitrary"),
        name="na_bias_table",
    )(na_bias.reshape(-1))


def _na_attn_kernel(q_ref, k_ref, v_ref, ck_ref, cv_ref, bt_ref, o_ref, *, L):
    rows = L // GRID_W
    groups = rows // NA_QROWS
    lane = lax.broadcasted_iota(jnp.int32, (NA_QROWS * GRID_W, PAIR), 1)
    for hp in range(NA_HEADS * HEAD_DIM // PAIR):
        heads = (2 * hp, 2 * hp + 1)
        lanes = slice(hp * PAIR, (hp + 1) * PAIR)
        ctx_k = jnp.concatenate([ck_ref[0, 0, h].astype(BF) for h in heads], axis=1)
        ctx_v = jnp.concatenate([cv_ref[0, 0, h].astype(BF) for h in heads], axis=1)

        def body(qi, carry, heads=heads, lanes=lanes, ctx_k=ctx_k, ctx_v=ctx_v):
            r0 = qi * NA_QROWS
            ws = jnp.clip(r0 - NA_ROWS // 2, 0, rows - NA_WIN_ROWS)
            pat = jnp.where(qi == 0, 0, jnp.where(qi == groups - 1, 2, 1))
            rq = pl.ds(pl.multiple_of(r0 * GRID_W, NA_QROWS * GRID_W), NA_QROWS * GRID_W)
            rk = pl.ds(pl.multiple_of(ws * GRID_W, GRID_W), NA_WIN_ROWS * GRID_W)
            q2 = q_ref[0, rq, lanes] * ATTN_SCALE
            k2 = k_ref[0, rk, lanes]
            v2 = v_ref[0, rk, lanes]
            outs = []
            for g, h in enumerate(heads):
                qg = _one_head(q2, lane, g)
                s = _dot_nt(qg, k2) + bt_ref[h, pat]
                s_ctx = _dot_nt(qg, ctx_k)
                m = jnp.maximum(jnp.max(s, axis=-1, keepdims=True), jnp.max(s_ctx, axis=-1, keepdims=True))
                p = jnp.exp(s - m)
                p_ctx = jnp.exp(s_ctx - m)
                den = jnp.sum(p, axis=-1, keepdims=True) + jnp.sum(p_ctx, axis=-1, keepdims=True)
                outs.append((_dot(p.astype(BF), v2) + _dot(p_ctx.astype(BF), ctx_v)) / den)
            o_ref[0, rq, lanes] = jnp.where(lane // HEAD_DIM == 0, outs[0], outs[1]).astype(BF)
            return carry

        lax.fori_loop(0, groups, body, 0)


def _na_attn_call(z, cache_k, cache_v, bias_table, layer):
    B, L, _ = z.shape
    P = cache_k.shape[3]
    assert L // GRID_W >= NA_WIN_ROWS + NA_QROWS and (L // GRID_W) % NA_QROWS == 0
    cspec = pl.BlockSpec((1, 1, NA_HEADS, P, HEAD_DIM), lambda b: (b, layer, 0, 0, 0))
    return pl.pallas_call(
        functools.partial(_na_attn_kernel, L=L),
        out_shape=jax.ShapeDtypeStruct((B, L, BRANCH_W), BF),
        grid=(B,),
        in_specs=[
            _zcol_spec(L, COL_DQ), _zcol_spec(L, COL_DK), _zcol_spec(L, COL_DV), cspec, cspec,
            pl.BlockSpec(bias_table.shape, lambda b: (0, 0, 0, 0), pipeline_mode=pl.Buffered(1)),
        ],
        out_specs=pl.BlockSpec((1, L, BRANCH_W), lambda b: (b, 0, 0)),
        compiler_params=_params("arbitrary"),
        name="neighbourhood_attention",
    )(z, z, z, cache_k, cache_v, bias_table)


def _merge_kernel(x_ref, g_ref, sc_ref, sh_ref, gate_ref, ba_ref, bb_ref, bc_ref, bd_ref,
                  wg0_ref, wg1_ref, wb_ref, wo_ref, o_ref):
    x = x_ref[0]
    h = _norm_mod(x, g_ref[...], sc_ref[0], sh_ref[0]).astype(BF)
    merged = jnp.zeros(x.shape, F32)
    per_ref = MIX_COLS // D_MODEL
    for n, br_ref in enumerate((ba_ref, bb_ref, bc_ref, bd_ref)):
        wg_ref = (wg0_ref, wg1_ref)[n // per_ref]
        c0 = (n % per_ref) * D_MODEL
        gate = jax.nn.sigmoid(_dot(h, wg_ref[0, :, c0:c0 + D_MODEL]))
        merged = merged + gate * _dot(br_ref[0], wb_ref[0, n])
    o_ref[0] = x + gate_ref[0] * _dot(merged.astype(BF), wo_ref[0])


def _merge_call(x, norm_g, mod, branches, w_in_bf, w_branch, w_out, layer, tm):
    B, L, _ = x.shape
    bm = mod.shape[0]
    xspec = pl.BlockSpec((1, tm, D_MODEL), lambda b, i: (b, i, 0))
    bspec = pl.BlockSpec((1, tm, BRANCH_W), lambda b, i: (b, i, 0))
    return pl.pallas_call(
        _merge_kernel,
        out_shape=jax.ShapeDtypeStruct((B, L, D_MODEL), F32),
        grid=(B, L // tm),
        in_specs=[
            xspec,
            pl.BlockSpec((1, D_MODEL), lambda b, i: (0, 0)),
            _mod_spec(bm, 1, 2), _mod_spec(bm, 0, 2), _mod_spec(bm, 2, 2),
            bspec, bspec, bspec, bspec,
            pl.BlockSpec((1, D_MODEL, MIX_COLS), lambda b, i: (layer, 0, 1)),
            pl.BlockSpec((1, D_MODEL, MIX_COLS), lambda b, i: (layer, 0, 2)),
            pl.BlockSpec((1, N_BRANCH, BRANCH_W, D_MODEL), lambda b, i: (layer, 0, 0, 0)),
            pl.BlockSpec((1, D_MODEL, D_MODEL), lambda b, i: (layer, 0, 0)),
        ],
        out_specs=xspec,
        compiler_params=_params("arbitrary", "arbitrary"),
        name="merge_out_proj",
    )(x, norm_g, mod, mod, mod, *branches, w_in_bf, w_in_bf, w_branch, w_out)


ROUTE_CHUNK = 256
GATHER_ROWS = 512
SIGN_BIT = 31
SEARCH_BITS = 4
SEARCH_TOP_SHIFT = ((SIGN_BIT - 1) // SEARCH_BITS) * SEARCH_BITS


def _prefix_excl(mask_ref, out_ref, tri, L):
    totals = [jnp.sum(mask_ref[:, c0:c0 + LANES], axis=-1, keepdims=True) for c0 in range(0, L, LANES)]
    run = jnp.zeros((N_EXPERTS, 1), F32)
    for i, c0 in enumerate(range(0, L, LANES)):
        out_ref[:, c0:c0 + LANES] = _dot(mask_ref[:, c0:c0 + LANES].astype(BF), tri) + run
        run = run + totals[i]


def _split3(v):
    hi = v.astype(BF)
    r = v - hi.astype(F32)
    mid = r.astype(BF)
    lo = (r - mid.astype(F32)).astype(BF)
    return hi, mid, lo


def _route_kernel(x_ref, g_ref, sc_ref, sh_ref, wh_ref, wl_ref, tri_ref,
                  h_ref, gc_ref, slot_ref, slot_t_ref, off_ref, lt_ref, msk_ref, pre_ref, *, L, cap):
    g = g_ref[...]
    sc = sc_ref[0]
    sh = sh_ref[0]
    wh = wh_ref[...]
    wl = wl_ref[...]
    rc = min(L, ROUTE_CHUNK)
    for r0 in range(0, L, rc):
        h = _norm_mod(x_ref[0, r0:r0 + rc, :], g, sc, sh)
        h_hi = h.astype(BF)
        h_ref[0, r0:r0 + rc, :] = h_hi
        h_lo = (h - h_hi.astype(F32)).astype(BF)
        lt_ref[:, r0:r0 + rc] = _dot_nt(wh, h_hi) + (_dot_nt(wh, h_lo) + _dot_nt(wl, h_hi))
    lt = lt_ref[...]
    e = jnp.exp(lt - jnp.max(lt, axis=0, keepdims=True))
    aff = e / jnp.sum(e, axis=0, keepdims=True)
    thr = jnp.zeros((N_EXPERTS, 1), jnp.int32)
    for shift in range(SEARCH_TOP_SHIFT, -1, -SEARCH_BITS):
        n_digits = min(1 << SEARCH_BITS, 1 << (SIGN_BIT - shift))
        digit = jnp.zeros((N_EXPERTS, 1), jnp.int32)
        for d in range(1, n_digits):
            cand = pltpu.bitcast(thr | (d << shift), F32)
            cnt = jnp.sum((aff >= cand).astype(F32), axis=-1, keepdims=True)
            digit = digit + (cnt >= cap).astype(jnp.int32)
        thr = thr | (digit << shift)
    thr_f = pltpu.bitcast(thr, F32)
    gt = aff > thr_f
    eq = aff == thr_f
    need = cap - jnp.sum(gt.astype(F32), axis=-1, keepdims=True)
    tri = tri_ref[...]
    msk_ref[...] = eq.astype(F32)
    _prefix_excl(msk_ref, pre_ref, tri, L)
    sel = gt | (eq & (pre_ref[...] < need))
    msk_ref[...] = sel.astype(F32)
    _prefix_excl(msk_ref, pre_ref, tri, L)
    off_ref[0] = jnp.concatenate([pre_ref[:, c:c + 1] for c in range(0, L, min(L, COMBINE_TL))], axis=1)
    slot = jnp.where(sel, pre_ref[...], -1.0)
    slot_ref[0] = slot
    slot_t_ref[0] = slot.T
    pieces = [p.astype(F32) for p in _split3(jnp.where(sel, aff, 0.0))]
    pad = jnp.zeros((LANES - len(pieces) * N_EXPERTS, L), F32)
    gc_ref[0] = jnp.concatenate(pieces + [pad], axis=0).T.astype(BF)


def _route_call(x, norm_g, mod, wr_hi, wr_lo):
    B, L, _ = x.shape
    bm = mod.shape[0]
    cap = EC_CAPACITY * L // N_EXPERTS
    n_tiles = L // min(L, COMBINE_TL)
    tri = jnp.asarray(np.triu(np.ones((LANES, LANES)), 1).astype(ml_dtypes.bfloat16))
    wspec = pl.BlockSpec((N_EXPERTS, D_MODEL), lambda b: (0, 0))
    return pl.pallas_call(
        functools.partial(_route_kernel, L=L, cap=cap),
        out_shape=[
            jax.ShapeDtypeStruct((B, L, D_MODEL), BF),
            jax.ShapeDtypeStruct((B, L, LANES), BF),
            jax.ShapeDtypeStruct((B, N_EXPERTS, L), F32),
            jax.ShapeDtypeStruct((B, L, N_EXPERTS), F32),
            jax.ShapeDtypeStruct((B, N_EXPERTS, n_tiles), F32),
        ],
        grid=(B,),
        in_specs=[
            pl.BlockSpec((1, L, D_MODEL), lambda b: (b, 0, 0)),
            pl.BlockSpec((1, D_MODEL), lambda b: (0, 0)),
            _mod_spec(bm, 4, 1), _mod_spec(bm, 3, 1),
            wspec, wspec,
            pl.BlockSpec((LANES, LANES), lambda b: (0, 0)),
        ],
        out_specs=[
            pl.BlockSpec((1, L, D_MODEL), lambda b: (b, 0, 0)),
            pl.BlockSpec((1, L, LANES), lambda b: (b, 0, 0)),
            pl.BlockSpec((1, N_EXPERTS, L), lambda b: (b, 0, 0)),
            pl.BlockSpec((1, L, N_EXPERTS), lambda b: (b, 0, 0)),
            pl.BlockSpec((1, N_EXPERTS, n_tiles), lambda b: (b, 0, 0)),
        ],
        scratch_shapes=[
            pltpu.VMEM((N_EXPERTS, L), F32),
            pltpu.VMEM((N_EXPERTS, L), F32),
            pltpu.VMEM((N_EXPERTS, L), F32),
        ],
        compiler_params=_params("arbitrary"),
        name="ec_route",
    )(x, norm_g, mod, mod, wr_hi, wr_lo, tri)


COMBINE_TL = 256
COMBINE_WIN = 64
BF_TILE = 16


def _list_windows(off_ref, cap):
    n_tiles = pl.num_programs(1)
    base = (pl.program_id(0) * n_tiles + pl.program_id(1)) * N_EXPERTS
    starts = []
    fits = None
    for ex in range(N_EXPERTS):
        first = off_ref[base + ex]
        end = jnp.where(pl.program_id(1) == n_tiles - 1, cap, off_ref[base + N_EXPERTS + ex])
        start = jnp.minimum((first // BF_TILE) * BF_TILE, cap - COMBINE_WIN)
        ok = end <= start + COMBINE_WIN
        fits = ok if fits is None else (fits & ok)
        starts.append(start)
    return starts, fits


def _gather_kernel(off_ref, slot_ref, h_ref, gc_ref, xs_ref, gs_ref, *, cap):
    h = h_ref[0]
    gc = gc_ref[0]
    tl = h.shape[0]

    @pl.when(pl.program_id(1) == 0)
    def _():
        xs_ref[...] = jnp.zeros(xs_ref.shape, BF)
        gs_ref[...] = jnp.zeros(gs_ref.shape, F32)

    def gather_all():
        ge = max(1, min(N_EXPERTS, GATHER_ROWS // cap))
        slot_iota = lax.broadcasted_iota(jnp.int32, (cap, tl), 0).astype(F32)
        for e0 in range(0, N_EXPERTS, ge):
            onehot = jnp.concatenate(
                [jnp.where(slot_ref[0, ex:ex + 1, :] == slot_iota, 1.0, 0.0).astype(BF)
                 for ex in range(e0, e0 + ge)], axis=0)
            rows = _dot(onehot, h).astype(BF)
            gates = _dot(onehot, gc)
            for j in range(ge):
                xs_ref[e0 + j, 0] += rows[j * cap:(j + 1) * cap]
                gs_ref[e0 + j, 0] += gates[j * cap:(j + 1) * cap]

    if cap <= COMBINE_WIN:
        gather_all()
        return

    starts, fits = _list_windows(off_ref, cap)

    @pl.when(fits)
    def _():
        win_iota = lax.broadcasted_iota(jnp.int32, (COMBINE_WIN, tl), 0).astype(F32)
        onehot = jnp.concatenate(
            [jnp.where(slot_ref[0, ex:ex + 1, :] - starts[ex].astype(F32) == win_iota, 1.0, 0.0).astype(BF)
             for ex in range(N_EXPERTS)], axis=0)
        rows = _dot(onehot, h).astype(BF)
        gates = _dot(onehot, gc)
        for ex in range(N_EXPERTS):
            win = pl.ds(pl.multiple_of(starts[ex], BF_TILE), COMBINE_WIN)
            piece = slice(ex * COMBINE_WIN, (ex + 1) * COMBINE_WIN)
            xs_ref[ex, 0, win, :] += rows[piece]
            gs_ref[ex, 0, win, :] += gates[piece]

    @pl.when(jnp.logical_not(fits))
    def _():
        gather_all()


def _gather_call(offsets, slot, h2, gate_cols):
    B, L, _ = h2.shape
    cap = EC_CAPACITY * L // N_EXPERTS
    tl = min(L, COMBINE_TL)
    assert cap <= COMBINE_WIN or (cap % BF_TILE == 0 and COMBINE_WIN % BF_TILE == 0)
    return pl.pallas_call(
        functools.partial(_gather_kernel, cap=cap),
        out_shape=[
            jax.ShapeDtypeStruct((N_EXPERTS, B, cap, D_MODEL), BF),
            jax.ShapeDtypeStruct((N_EXPERTS, B, cap, LANES), F32),
        ],
        grid_spec=pltpu.PrefetchScalarGridSpec(
            num_scalar_prefetch=1,
            grid=(B, L // tl),
            in_specs=[
                pl.BlockSpec((1, N_EXPERTS, tl), lambda b, i, off: (b, 0, i)),
                pl.BlockSpec((1, tl, D_MODEL), lambda b, i, off: (b, i, 0)),
                pl.BlockSpec((1, tl, LANES), lambda b, i, off: (b, i, 0)),
            ],
            out_specs=[
                pl.BlockSpec((N_EXPERTS, 1, cap, D_MODEL), lambda b, i, off: (0, b, 0, 0)),
                pl.BlockSpec((N_EXPERTS, 1, cap, LANES), lambda b, i, off: (0, b, 0, 0)),
            ],
        ),
        compiler_params=_params("arbitrary", "arbitrary"),
        name="ec_gather",
    )(offsets, slot, h2, gate_cols)


EXPERT_TR = 512


def _expert_kernel(xa_ref, ga_ref, xb_ref, gb_ref, wg_ref, wu_ref, wd_ref, ya_ref, yb_ref,
                   wgb_ref, wub_ref, wdb_ref, *, tiles_a):
    ex = pl.program_id(0)
    j = pl.program_id(1)

    @pl.when(j == 0)
    def _():
        wgb_ref[...] = wg_ref[0, 0].astype(BF)
        wub_ref[...] = wu_ref[0, 0].astype(BF)
        wdb_ref[...] = wd_ref[0, 0].astype(BF)

    def ffn(x_ref, g_ref, y_ref):
        x = x_ref[0]
        a = _dot(x, wgb_ref[...])
        u = _dot(x, wub_ref[...])
        y = _dot((_silu(a) * u).astype(BF), wdb_ref[...])
        gates = g_ref[0]
        lane = lax.broadcasted_iota(jnp.int32, gates.shape, 1)
        gate = jnp.sum(jnp.where(lane % N_EXPERTS == ex, gates, 0.0), axis=-1, keepdims=True)
        y_ref[0] = (y * gate).astype(BF)

    @pl.when(j < tiles_a)
    def _():
        ffn(xa_ref, ga_ref, ya_ref)

    @pl.when(j >= tiles_a)
    def _():
        ffn(xb_ref, gb_ref, yb_ref)


def _expert_call(xs_a, gs_a, xs_b, gs_b, w_gate, w_up, w_down, layer):
    E, ra, _ = xs_a.shape
    rb = xs_b.shape[1]
    tr = math.gcd(math.gcd(ra, rb), EXPERT_TR)
    tiles_a, tiles_b = ra // tr, rb // tr
    amap = lambda e, j: (e, jnp.minimum(j, tiles_a - 1), 0)
    bmap = lambda e, j: (e, jnp.maximum(j - tiles_a, 0), 0)
    wspec_in = pl.BlockSpec((1, 1, D_MODEL, EXPERT_FF), lambda e, j: (layer, e, 0, 0))
    wspec_out = pl.BlockSpec((1, 1, EXPERT_FF, D_MODEL), lambda e, j: (layer, e, 0, 0))
    return pl.pallas_call(
        functools.partial(_expert_kernel, tiles_a=tiles_a),
        out_shape=[jax.ShapeDtypeStruct((E, ra, D_MODEL), BF), jax.ShapeDtypeStruct((E, rb, D_MODEL), BF)],
        grid=(E, tiles_a + tiles_b),
        in_specs=[
            pl.BlockSpec((1, tr, D_MODEL), amap), pl.BlockSpec((1, tr, LANES), amap),
            pl.BlockSpec((1, tr, D_MODEL), bmap), pl.BlockSpec((1, tr, LANES), bmap),
            wspec_in, wspec_in, wspec_out,
        ],
        out_specs=[pl.BlockSpec((1, tr, D_MODEL), amap), pl.BlockSpec((1, tr, D_MODEL), bmap)],
        scratch_shapes=[
            pltpu.VMEM((D_MODEL, EXPERT_FF), BF),
            pltpu.VMEM((D_MODEL, EXPERT_FF), BF),
            pltpu.VMEM((EXPERT_FF, D_MODEL), BF),
        ],
        compiler_params=_params("arbitrary", "arbitrary"),
        name="ec_experts",
    )(xs_a, gs_a, xs_b, gs_b, w_gate, w_up, w_down)


def _combine_kernel(off_ref, st_ref, y_ref, x_ref, gate_ref, fg_ref, o_ref, *, cap, final):
    st = st_ref[0]
    tl = st.shape[0]

    def finish(scattered):
        v = x_ref[0] + gate_ref[0] * scattered
        if final:
            v = v * lax.rsqrt(jnp.mean(v * v, axis=-1, keepdims=True) + EPS) * fg_ref[...]
        o_ref[0] = v

    def full_onehot():
        lane = lax.broadcasted_iota(jnp.int32, (tl, N_EXPERTS * cap), 1).astype(F32)
        hit = None
        for ex in range(N_EXPERTS):
            col = st[:, ex:ex + 1]
            m = jnp.where(col >= 0.0, col + float(ex * cap), -1.0) == lane
            hit = m if hit is None else (hit | m)
        return jnp.where(hit, 1.0, 0.0).astype(BF)

    def scatter_all():
        finish(_dot(full_onehot(), y_ref[:, 0].reshape(N_EXPERTS * cap, D_MODEL)))

    if cap <= COMBINE_WIN:
        scatter_all()
        return

    starts, fits = _list_windows(off_ref, cap)

    @pl.when(fits)
    def _():
        lane = lax.broadcasted_iota(jnp.int32, (tl, LANES), 1)
        per_vreg = LANES // COMBINE_WIN
        pieces = []
        rows = []
        for e0 in range(0, N_EXPERTS, per_vreg):
            target = jnp.full((tl, LANES), -1.0, F32)
            for k in range(per_vreg):
                ex = e0 + k
                col = st[:, ex:ex + 1]
                rel = jnp.where(col >= 0.0, col - starts[ex].astype(F32) + float(k * COMBINE_WIN), -1.0)
                target = jnp.where(lane // COMBINE_WIN == k, rel, target)
                rows.append(y_ref[ex, 0, pl.ds(pl.multiple_of(starts[ex], BF_TILE), COMBINE_WIN), :])
            pieces.append(jnp.where(target == lane.astype(F32), 1.0, 0.0).astype(BF))
        finish(_dot(jnp.concatenate(pieces, axis=1), jnp.concatenate(rows, axis=0)))

    @pl.when(jnp.logical_not(fits))
    def _():
        scatter_all()


def _combine_call(slot_t, offsets, y, x, mod, final_g, final):
    B, L, _ = x.shape
    bm = mod.shape[0]
    cap = y.shape[2]
    tl = min(L, COMBINE_TL)
    assert cap <= COMBINE_WIN or (cap % BF_TILE == 0 and COMBINE_WIN % BF_TILE == 0)
    gate_map = (lambda b, i, off: (b, 0, 5)) if bm > 1 else (lambda b, i, off: (0, 0, 5))
    return pl.pallas_call(
        functools.partial(_combine_kernel, cap=cap, final=final),
        out_shape=jax.ShapeDtypeStruct((B, L, D_MODEL), F32),
        grid_spec=pltpu.PrefetchScalarGridSpec(
            num_scalar_prefetch=1,
            grid=(B, L // tl),
            in_specs=[
                pl.BlockSpec((1, tl, N_EXPERTS), lambda b, i, off: (b, i, 0)),
                pl.BlockSpec((N_EXPERTS, 1, cap, D_MODEL), lambda b, i, off: (0, b, 0, 0)),
                pl.BlockSpec((1, tl, D_MODEL), lambda b, i, off: (b, i, 0)),
                pl.BlockSpec((1, 1, D_MODEL), gate_map),
                pl.BlockSpec((1, D_MODEL), lambda b, i, off: (0, 0)),
            ],
            out_specs=pl.BlockSpec((1, tl, D_MODEL), lambda b, i, off: (b, i, 0)),
        ),
        compiler_params=_params("arbitrary", "arbitrary"),
        name="ec_combine",
    )(offsets, slot_t, y, x, mod, final_g)


def _mixer_half(x, mod, lw, ctx, tm):
    latent = ctx is not None
    layer = lw["layer"]
    res = _inproj_call(x, lw["norm1_g"], mod, lw["w_in"], layer, tm, rope=latent, want_kv=not latent)
    z = res[0]
    br_a = _conv_call(z, lw["conv_w"], lw["conv_b"], lw["conv_ln_g"], lw["conv_ln_b"])
    br_b = _fourier_call(z, *lw["dft"][z.shape[1]])
    if latent:
        br_c = _win_attn_call(z, ctx["win_k"], ctx["win_v"], lw["win_sink"], layer)
        br_d = _na_attn_call(z, ctx["nat_k"], ctx["nat_v"], lw["bias_table"], layer)
    else:
        br_c, br_d = _ctx_attn_call(z, lw["win_sink"])
    x = _merge_call(x, lw["norm1_g"], mod, (br_a, br_b, br_c, br_d),
                    lw["w_in"], lw["w_branch"], lw["w_out"], layer, tm)
    return x, (res[1] if not latent else None)


def kernel(x_prompt, x_sample, cache_win_k, cache_win_v, cache_nat_k, cache_nat_v, c, c_ctx,
           norm1_g, norm2_g, w_ada, b_ada, w_in, conv_w, conv_b, conv_ln_g, conv_ln_b,
           win_sink, na_bias, w_branch, w_out, w_router, w_e_gate, w_e_up, w_e_down, final_norm_g):
    n_dec = c.shape[0]
    cond = jnp.zeros((ADA_ROWS, D_MODEL), F32).at[0].set(c_ctx).at[1:1 + n_dec].set(c)
    mods = _ada_call(cond, w_ada, b_ada)
    final_g = final_norm_g.reshape(1, D_MODEL)
    w_in_bf = _cast_call(w_in)
    w_branch_bf = _cast_call(w_branch)
    w_out_bf = _cast_call(w_out)
    dft = {L: _dft_tables(L) for L in (x_prompt.shape[1], x_sample.shape[1])}
    ctx = {"win_k": cache_win_k, "win_v": cache_win_v, "nat_k": cache_nat_k, "nat_v": cache_nat_v}

    xp, xs = x_prompt, x_sample
    kvs = []
    for l in range(DEPTH):
        wr_t = w_router[l].T
        wr_hi = wr_t.astype(BF)
        wr_lo = (wr_t - wr_hi.astype(F32)).astype(BF)
        lw = {
            "layer": l,
            "norm1_g": norm1_g[l].reshape(1, D_MODEL),
            "w_in": w_in_bf, "w_branch": w_branch_bf, "w_out": w_out_bf, "dft": dft,
            "conv_w": conv_w[l], "conv_b": conv_b[l],
            "conv_ln_g": conv_ln_g[l], "conv_ln_b": conv_ln_b[l],
            "win_sink": win_sink[l],
            "bias_table": _na_bias_call(na_bias[l], x_sample.shape[1] // GRID_W),
        }
        norm2 = norm2_g[l].reshape(1, D_MODEL)
        final = l == DEPTH - 1
        mod_p = mods[l, 0:1].reshape(1, 1, 6 * D_MODEL)
        mod_s = mods[l, 1:1 + n_dec].reshape(n_dec, 1, 6 * D_MODEL)
        xp, kv = _mixer_half(xp, mod_p, lw, None, tm=256)
        kvs.append(kv)
        xs, _ = _mixer_half(xs, mod_s, lw, ctx, tm=512)
        smem = lambda off: jnp.pad(off.transpose(0, 2, 1).astype(jnp.int32).reshape(-1), (0, N_EXPERTS))
        h_p, gc_p, slot_p, slot_t_p, off_p = _route_call(xp, norm2, mod_p, wr_hi, wr_lo)
        h_s, gc_s, slot_s, slot_t_s, off_s = _route_call(xs, norm2, mod_s, wr_hi, wr_lo)
        off_p, off_s = smem(off_p), smem(off_s)
        rows_p, gates_p = _gather_call(off_p, slot_p, h_p, gc_p)
        rows_s, gates_s = _gather_call(off_s, slot_s, h_s, gc_s)
        flat = lambda a: a.reshape(N_EXPERTS, a.shape[1] * a.shape[2], a.shape[3])
        y_p, y_s = _expert_call(flat(rows_p), flat(gates_p), flat(rows_s), flat(gates_s),
                                w_e_gate, w_e_up, w_e_down, l)
        xp = _combine_call(slot_t_p, off_p, y_p.reshape(rows_p.shape), xp, mod_p, final_g, final)
        xs = _combine_call(slot_t_s, off_s, y_s.reshape(rows_s.shape), xs, mod_s, final_g, final)

    def heads(lo, n):
        per_layer = [kv[..., lo:lo + n * HEAD_DIM].reshape(kv.shape[0], kv.shape[1], n, HEAD_DIM)
                     .transpose(0, 2, 1, 3) for kv in kvs]
        return jnp.stack(per_layer, axis=1)

    return (xp, xs, heads(0, WIN_KV_HEADS), heads(128, WIN_KV_HEADS), heads(256, NA_HEADS),
            heads(512, NA_HEADS))
```

```python
import functools
import math

import ml_dtypes
import numpy as np
import jax
import jax.numpy as jnp
from jax import lax
from jax.experimental import pallas as pl
from jax.experimental.pallas import tpu as pltpu

D_MODEL = 1024
DEPTH = 2
GRID_W = 64
BRANCH_W = 256
HEAD_DIM = 64
N_BRANCH = 4
CONV_K = 31
FNET_GROUPS = 4
FNET_GDIM = BRANCH_W // FNET_GROUPS
WIN_Q_HEADS = 4
WIN_KV_HEADS = 2
WIN_BLOCK = 128
NA_HEADS = 4
NA_ROWS = 8
NA_COLS = 16
ROPE_THETA = 10000.0
ATTN_SCALE = HEAD_DIM ** -0.5
N_EXPERTS = 16
EXPERT_FF = 1024
EC_CAPACITY = 2
EPS = 1e-6
NEG = -1e30

MIX_COLS = 2048
GATE_COLS = N_BRANCH * D_MODEL
IN_COLS = MIX_COLS + GATE_COLS
COL_A, COL_F, COL_CQ, COL_CKV, COL_DQ, COL_DK, COL_DV = 0, 512, 768, 1024, 1280, 1536, 1792
ROPE_LO, ROPE_HI = COL_CQ, COL_CKV + WIN_KV_HEADS * HEAD_DIM
KV_COLS = (COL_CKV, COL_CKV + WIN_KV_HEADS * HEAD_DIM, COL_DK, COL_DV)
KV_HEADS = (WIN_KV_HEADS, WIN_KV_HEADS, NA_HEADS, NA_HEADS)

LANES = 128
SUBLANES = 8
VMEM_LIMIT = 56 * 1024 * 1024

BF = jnp.bfloat16
F32 = jnp.float32


def _params(*sem):
    return pltpu.CompilerParams(dimension_semantics=sem, vmem_limit_bytes=VMEM_LIMIT)


def _dot(a, b):
    return jnp.dot(a, b, preferred_element_type=F32)


def _dot_nt(a, b):
    return lax.dot_general(a, b, (((1,), (1,)), ((), ())), preferred_element_type=F32)


def _norm_mod(x, g, sc, sh):
    y = x * lax.rsqrt(jnp.mean(x * x, axis=-1, keepdims=True) + EPS)
    return (y * g) * (1.0 + sc) + sh


def _silu(x):
    return x * jax.nn.sigmoid(x)


CAST_BYTES = 4 * 1024 * 1024


def _cast_kernel(x_ref, o_ref):
    o_ref[...] = x_ref[...].astype(BF)


def _cast_call(x):
    shape = x.shape
    x2 = x.reshape(-1, shape[-1])
    rows, cols = x2.shape
    tr = min(rows, 1 << (max(16, CAST_BYTES // (4 * cols)).bit_length() - 1))
    assert rows % tr == 0
    out = pl.pallas_call(
        _cast_kernel,
        out_shape=jax.ShapeDtypeStruct((rows, cols), BF),
        grid=(rows // tr,),
        in_specs=[pl.BlockSpec((tr, cols), lambda i: (i, 0))],
        out_specs=pl.BlockSpec((tr, cols), lambda i: (i, 0)),
        compiler_params=_params("arbitrary"),
        name="cast_bf16",
    )(x2)
    return out.reshape(shape)


ADA_ROWS = 16
ADA_TN = 1536


def _ada_kernel(c_ref, w_ref, b_ref, o_ref):
    s = _silu(c_ref[...]).astype(BF)
    o_ref[0] = _dot(s, w_ref[0].astype(BF)) + b_ref[0]


def _ada_call(cond, w_ada, b_ada):
    n = 6 * D_MODEL
    return pl.pallas_call(
        _ada_kernel,
        out_shape=jax.ShapeDtypeStruct((DEPTH, ADA_ROWS, n), F32),
        grid=(DEPTH, n // ADA_TN),
        in_specs=[
            pl.BlockSpec((ADA_ROWS, D_MODEL), lambda l, j: (0, 0)),
            pl.BlockSpec((1, D_MODEL, ADA_TN), lambda l, j: (l, 0, j)),
            pl.BlockSpec((1, 1, ADA_TN), lambda l, j: (l, 0, j)),
        ],
        out_specs=pl.BlockSpec((1, ADA_ROWS, ADA_TN), lambda l, j: (l, 0, j)),
        compiler_params=_params("arbitrary", "arbitrary"),
        name="ada_mod",
    )(cond, w_ada, b_ada.reshape(DEPTH, 1, n))


def _mod_spec(bm, chunk, ngrid):
    if ngrid == 1:
        imap = (lambda b: (b, 0, chunk)) if bm > 1 else (lambda b: (0, 0, chunk))
    else:
        imap = (lambda b, i: (b, 0, chunk)) if bm > 1 else (lambda b, i: (0, 0, chunk))
    return pl.BlockSpec((1, 1, D_MODEL), imap)


def _inproj_kernel(*refs, rope, want_kv):
    x_ref, g_ref, sc_ref, sh_ref, w_ref = refs[:5]
    pos = 5
    if rope:
        cos_ref, sin_ref = refs[pos:pos + 2]
        pos += 2
    z_ref = refs[pos]
    h = _norm_mod(x_ref[0], g_ref[...], sc_ref[0], sh_ref[0]).astype(BF)
    z = _dot(h, w_ref[0])
    if want_kv:
        for kv_ref, c0, heads in zip(refs[pos + 1:pos + 5], KV_COLS, KV_HEADS):
            for hd in range(heads):
                kv_ref[0, hd] = z[:, c0 + hd * HEAD_DIM:c0 + (hd + 1) * HEAD_DIM]
    if rope:
        z_ref[0, :, 0:ROPE_LO] = z[:, 0:ROPE_LO].astype(BF)
        lane = lax.broadcasted_iota(jnp.int32, (z.shape[0], LANES), 1)
        first_half = (lane % 32) < 16
        cos = cos_ref[...]
        sin = sin_ref[...]
        for c0 in range(ROPE_LO, ROPE_HI, LANES):
            xc = z[:, c0:c0 + LANES]
            partner = jnp.where(first_half, pltpu.roll(xc, LANES - 16, 1), pltpu.roll(xc, 16, 1))
            z_ref[0, :, c0:c0 + LANES] = (xc * cos + partner * sin).astype(BF)
        z_ref[0, :, ROPE_HI:MIX_COLS] = z[:, ROPE_HI:MIX_COLS].astype(BF)
    else:
        z_ref[0] = z.astype(BF)


def _rope_tables(L):
    t = np.arange(L)
    half = HEAD_DIM // 4
    freqs = 1.0 / (ROPE_THETA ** (np.arange(half, dtype=np.float64) / half))
    ang_r = (t // GRID_W).astype(np.float64)[:, None] * freqs[None, :]
    ang_c = (t % GRID_W).astype(np.float64)[:, None] * freqs[None, :]
    cos_h = np.concatenate([np.cos(ang_r), np.cos(ang_r), np.cos(ang_c), np.cos(ang_c)], axis=1)
    sin_h = np.concatenate([-np.sin(ang_r), np.sin(ang_r), -np.sin(ang_c), np.sin(ang_c)], axis=1)
    reps = LANES // HEAD_DIM
    return (jnp.asarray(np.tile(cos_h, (1, reps)), F32), jnp.asarray(np.tile(sin_h, (1, reps)), F32))


def _inproj_call(x, norm_g, mod, w_in_bf, layer, tm, rope, want_kv):
    B, L, _ = x.shape
    bm = mod.shape[0]
    in_specs = [
        pl.BlockSpec((1, tm, D_MODEL), lambda b, i: (b, i, 0)),
        pl.BlockSpec((1, D_MODEL), lambda b, i: (0, 0)),
        _mod_spec(bm, 1, 2),
        _mod_spec(bm, 0, 2),
        pl.BlockSpec((1, D_MODEL, MIX_COLS), lambda b, i: (layer, 0, 0)),
    ]
    args = [x, norm_g, mod, mod, w_in_bf]
    if rope:
        cos, sin = _rope_tables(L)
        in_specs += [pl.BlockSpec((tm, LANES), lambda b, i: (i, 0))] * 2
        args += [cos, sin]
    out_shape = [jax.ShapeDtypeStruct((B, L, MIX_COLS), BF)]
    out_specs = [pl.BlockSpec((1, tm, MIX_COLS), lambda b, i: (b, i, 0))]
    if want_kv:
        for heads in KV_HEADS:
            out_shape.append(jax.ShapeDtypeStruct((B, heads, L, HEAD_DIM), F32))
            out_specs.append(pl.BlockSpec((1, heads, tm, HEAD_DIM), lambda b, i: (b, 0, i, 0)))
    return pl.pallas_call(
        functools.partial(_inproj_kernel, rope=rope, want_kv=want_kv),
        out_shape=out_shape,
        grid=(B, L // tm),
        in_specs=in_specs,
        out_specs=out_specs,
        compiler_params=_params("arbitrary", "arbitrary"),
        name="in_proj",
    )(*args)


CONV_PAD = 16
CONV_TR = 64
CONV_CHUNK = 256


def _conv_kernel(a_ref, w_ref, cb_ref, lg_ref, lb_ref, o_ref, p_ref, u_ref, wb_ref, *, L):
    for k in range(CONV_K):
        wb_ref[k] = jnp.broadcast_to(w_ref[k:k + 1, :], (SUBLANES, BRANCH_W))
    zeros = jnp.zeros((CONV_PAD, BRANCH_W), F32)
    p_ref[0:CONV_PAD, :] = zeros
    p_ref[L + CONV_PAD:L + 2 * CONV_PAD, :] = zeros
    for r0 in range(0, L, CONV_CHUNK):
        a = a_ref[0, r0:r0 + CONV_CHUNK, :].astype(F32)
        p_ref[CONV_PAD + r0:CONV_PAD + r0 + CONV_CHUNK, :] = a[:, :BRANCH_W] * jax.nn.sigmoid(a[:, BRANCH_W:])
    n_u = L + 2 * CONV_PAD - SUBLANES
    for s in range(SUBLANES):
        u_ref[s] = p_ref[s:s + n_u, :]
    cb = cb_ref[...]

    def body(i, carry):
        base = pl.multiple_of(i * CONV_TR, CONV_TR)
        groups = (CONV_TR // SUBLANES, SUBLANES, BRANCH_W)
        acc = jnp.zeros(groups, F32) + cb
        for k in range(CONV_K):
            q, s = divmod(k + CONV_PAD - CONV_K // 2, SUBLANES)
            u = u_ref[s, pl.ds(base + SUBLANES * q, CONV_TR), :]
            acc = acc + wb_ref[k] * u.reshape(groups)
        p_ref[pl.ds(base, CONV_TR), :] = acc.reshape(CONV_TR, BRANCH_W)
        return carry

    lax.fori_loop(0, L // CONV_TR, body, 0)
    lg = lg_ref[...]
    lb = lb_ref[...]
    for r0 in range(0, L, CONV_CHUNK):
        acc = p_ref[r0:r0 + CONV_CHUNK, :]
        mu = jnp.mean(acc, axis=-1, keepdims=True)
        d = acc - mu
        var = jnp.mean(d * d, axis=-1, keepdims=True)
        y = d * lax.rsqrt(var + EPS) * lg + lb
        o_ref[0, r0:r0 + CONV_CHUNK, :] = _silu(y).astype(BF)


def _conv_call(z, conv_w, conv_b, ln_g, ln_b):
    B, L, _ = z.shape
    n_u = L + 2 * CONV_PAD - SUBLANES
    vec = lambda b: (0, 0)
    return pl.pallas_call(
        functools.partial(_conv_kernel, L=L),
        out_shape=jax.ShapeDtypeStruct((B, L, BRANCH_W), BF),
        grid=(B,),
        in_specs=[
            pl.BlockSpec((1, L, 2 * BRANCH_W), lambda b: (b, 0, COL_A // (2 * BRANCH_W))),
            pl.BlockSpec((CONV_K, BRANCH_W), vec),
            pl.BlockSpec((1, BRANCH_W), vec),
            pl.BlockSpec((1, BRANCH_W), vec),
            pl.BlockSpec((1, BRANCH_W), vec),
        ],
        out_specs=pl.BlockSpec((1, L, BRANCH_W), lambda b: (b, 0, 0)),
        scratch_shapes=[
            pltpu.VMEM((L + 2 * CONV_PAD, BRANCH_W), F32),
            pltpu.VMEM((SUBLANES, n_u, BRANCH_W), F32),
            pltpu.VMEM((CONV_K, SUBLANES, BRANCH_W), F32),
        ],
        compiler_params=_params("arbitrary"),
        name="conv_module",
    )(z, conv_w, conv_b.reshape(1, -1), ln_g.reshape(1, -1), ln_b.reshape(1, -1))


def _dft_tables(L):
    c = np.arange(FNET_GDIM)
    ang_c = 2.0 * np.pi * ((c[:, None] * c[None, :]) % FNET_GDIM) / FNET_GDIM
    cc = np.zeros((BRANCH_W, BRANCH_W))
    sc = np.zeros((BRANCH_W, BRANCH_W))
    for g in range(FNET_GROUPS):
        sl = slice(g * FNET_GDIM, (g + 1) * FNET_GDIM)
        cc[sl, sl] = np.cos(ang_c)
        sc[sl, sl] = np.sin(ang_c)
    ccs = np.concatenate([cc, sc], axis=1)
    t = np.arange(L)
    ang_l = 2.0 * np.pi * ((t[:, None] * t[None, :]) % L) / L
    csl = np.concatenate([np.cos(ang_l), -np.sin(ang_l)], axis=1)
    return _cast_call(jnp.asarray(ccs, F32)), _cast_call(jnp.asarray(csl, F32))


FOURIER_TR = 512


def _fourier_kernel(f_ref, ccs_ref, csl_ref, o_ref, pq_ref, *, L):
    scale = float(1.0 / np.sqrt(L * FNET_GDIM))
    tr = min(L, FOURIER_TR)
    for r0 in range(0, L, tr):
        pq = _dot(f_ref[0, r0:r0 + tr, :], ccs_ref[...])
        pq_ref[r0:r0 + tr, :] = pq[:, :BRANCH_W].astype(BF)
        pq_ref[L + r0:L + r0 + tr, :] = pq[:, BRANCH_W:].astype(BF)
    for r0 in range(0, L, tr):
        o = _dot(csl_ref[r0:r0 + tr, :], pq_ref[...])
        o_ref[0, r0:r0 + tr, :] = (o * scale).astype(BF)


def _fourier_call(z, ccs, csl):
    B, L, _ = z.shape
    return pl.pallas_call(
        functools.partial(_fourier_kernel, L=L),
        out_shape=jax.ShapeDtypeStruct((B, L, BRANCH_W), BF),
        grid=(B,),
        in_specs=[
            pl.BlockSpec((1, L, BRANCH_W), lambda b: (b, 0, COL_F // BRANCH_W)),
            pl.BlockSpec((BRANCH_W, 2 * BRANCH_W), lambda b: (0, 0)),
            pl.BlockSpec((L, 2 * L), lambda b: (0, 0), pipeline_mode=pl.Buffered(1)),
        ],
        out_specs=pl.BlockSpec((1, L, BRANCH_W), lambda b: (b, 0, 0)),
        scratch_shapes=[pltpu.VMEM((2 * L, BRANCH_W), BF)],
        compiler_params=_params("arbitrary"),
        name="fourier_mix",
    )(z, ccs, csl)


def _head(ref, rows, h):
    return ref[0, rows, h * HEAD_DIM:(h + 1) * HEAD_DIM]


def _ctx_attn_kernel(sink_ref, qc_ref, kvc_ref, qd_ref, kd_ref, vd_ref, oc_ref, od_ref):
    rows = slice(None)
    group = WIN_Q_HEADS // WIN_KV_HEADS
    for h in range(WIN_Q_HEADS):
        kvh = h // group
        s = _dot_nt(_head(qc_ref, rows, h), _head(kvc_ref, rows, kvh)) * ATTN_SCALE
        sink = sink_ref[h]
        m = jnp.maximum(jnp.max(s, axis=-1, keepdims=True), sink)
        p = jnp.exp(s - m)
        den = jnp.sum(p, axis=-1, keepdims=True) + jnp.exp(sink - m)
        o = _dot(p.astype(BF), _head(kvc_ref, rows, WIN_KV_HEADS + kvh)) / den
        oc_ref[0, :, h * HEAD_DIM:(h + 1) * HEAD_DIM] = o.astype(BF)
    for h in range(NA_HEADS):
        s = _dot_nt(_head(qd_ref, rows, h), _head(kd_ref, rows, h)) * ATTN_SCALE
        m = jnp.max(s, axis=-1, keepdims=True)
        p = jnp.exp(s - m)
        den = jnp.sum(p, axis=-1, keepdims=True)
        o = _dot(p.astype(BF), _head(vd_ref, rows, h)) / den
        od_ref[0, :, h * HEAD_DIM:(h + 1) * HEAD_DIM] = o.astype(BF)


def _zcol_spec(L, col):
    return pl.BlockSpec((1, L, BRANCH_W), lambda b: (b, 0, col // BRANCH_W))


def _ctx_attn_call(z, sink):
    B, L, _ = z.shape
    out = jax.ShapeDtypeStruct((B, L, BRANCH_W), BF)
    ospec = pl.BlockSpec((1, L, BRANCH_W), lambda b: (b, 0, 0))
    return pl.pallas_call(
        _ctx_attn_kernel,
        out_shape=[out, out],
        grid=(B,),
        in_specs=[
            pl.BlockSpec(memory_space=pltpu.SMEM),
            _zcol_spec(L, COL_CQ), _zcol_spec(L, COL_CKV),
            _zcol_spec(L, COL_DQ), _zcol_spec(L, COL_DK), _zcol_spec(L, COL_DV),
        ],
        out_specs=[ospec, ospec],
        compiler_params=_params("arbitrary"),
        name="ctx_attention",
    )(sink, z, z, z, z, z)


PAIR = 2 * HEAD_DIM


def _one_head(x, lane, g):
    return jnp.where(lane // HEAD_DIM == g, x, jnp.zeros_like(x))


def _win_attn_kernel(sink_ref, q_ref, kv_ref, ck_ref, cv_ref, o_ref, *, L):
    nb = L // WIN_BLOCK
    group = WIN_Q_HEADS // WIN_KV_HEADS
    m_rows = group * WIN_BLOCK
    assert WIN_BLOCK == LANES
    row = lax.broadcasted_iota(jnp.int32, (m_rows, WIN_BLOCK), 0)
    a = row % WIN_BLOCK
    j = lax.broadcasted_iota(jnp.int32, (m_rows, WIN_BLOCK), 1)
    row1 = lax.broadcasted_iota(jnp.int32, (m_rows, 1), 0)
    for kvh in range(WIN_KV_HEADS):
        ctx_k = ck_ref[0, 0, kvh].astype(BF)
        ctx_v = cv_ref[0, 0, kvh].astype(BF)
        sink = jnp.zeros((m_rows, 1), F32)
        for g in range(group):
            sink = jnp.where(row1 // WIN_BLOCK == g, sink_ref[kvh * group + g], sink)

        def body(n, carry, kvh=kvh, ctx_k=ctx_k, ctx_v=ctx_v, sink=sink):
            r0 = pl.multiple_of(n * WIN_BLOCK, WIN_BLOCK)
            q = jnp.concatenate(
                [_head(q_ref, pl.ds(r0, WIN_BLOCK), kvh * group + g) for g in range(group)], axis=0)
            q = q * ATTN_SCALE
            s_ctx = _dot_nt(q, ctx_k)
            m_el = jnp.maximum(s_ctx[:, :LANES], s_ctx[:, LANES:])
            scores = []
            vals = []
            for dj in (-1, 0, 1):
                nk = jnp.clip(n + dj, 0, nb - 1)
                rk = pl.multiple_of(nk * WIN_BLOCK, WIN_BLOCK)
                s = _dot_nt(q, _head(kv_ref, pl.ds(rk, WIN_BLOCK), kvh))
                if dj == -1:
                    s = jnp.where((j >= a) & (n >= 1), s, NEG)
                elif dj == 1:
                    s = jnp.where((j <= a) & (n <= nb - 2), s, NEG)
                m_el = jnp.maximum(m_el, s)
                scores.append(s)
                vals.append(_head(kv_ref, pl.ds(rk, WIN_BLOCK), WIN_KV_HEADS + kvh))
            m = jnp.maximum(jnp.max(m_el, axis=-1, keepdims=True), sink)
            p_ctx = jnp.exp(s_ctx - m)
            den_el = p_ctx[:, :LANES] + p_ctx[:, LANES:]
            o = _dot(p_ctx.astype(BF), ctx_v)
            for s, v in zip(scores, vals):
                p = jnp.exp(s - m)
                den_el = den_el + p
                o = o + _dot(p.astype(BF), v)
            den = jnp.sum(den_el, axis=-1, keepdims=True) + jnp.exp(sink - m)
            o = (o / den).astype(BF)
            for g in range(group):
                h = kvh * group + g
                o_ref[0, pl.ds(r0, WIN_BLOCK), h * HEAD_DIM:(h + 1) * HEAD_DIM] = (
                    o[g * WIN_BLOCK:(g + 1) * WIN_BLOCK])
            return carry

        lax.fori_loop(0, nb, body, 0, unroll=2)


def _win_attn_call(z, cache_k, cache_v, sink, layer):
    B, L, _ = z.shape
    P = cache_k.shape[3]
    assert P == 2 * LANES
    cspec = pl.BlockSpec((1, 1, WIN_KV_HEADS, P, HEAD_DIM), lambda b: (b, layer, 0, 0, 0))
    return pl.pallas_call(
        functools.partial(_win_attn_kernel, L=L),
        out_shape=jax.ShapeDtypeStruct((B, L, BRANCH_W), BF),
        grid=(B,),
        in_specs=[
            pl.BlockSpec(memory_space=pltpu.SMEM),
            _zcol_spec(L, COL_CQ), _zcol_spec(L, COL_CKV), cspec, cspec,
        ],
        out_specs=pl.BlockSpec((1, L, BRANCH_W), lambda b: (b, 0, 0)),
        compiler_params=_params("arbitrary"),
        name="window_attention",
    )(sink, z, z, cache_k, cache_v)


N_ROW_OFF = 2 * NA_ROWS - 1
N_COL_OFF = 2 * NA_COLS - 1


NA_QROWS = 4
NA_WIN_ROWS = NA_ROWS + NA_QROWS
NA_PATTERNS = 3


def _na_window_start(r0, rows):
    return min(max(r0 - NA_ROWS // 2, 0), rows - NA_WIN_ROWS)


def _na_pattern_blocks(p, rows):
    r0 = (0, 2 * NA_QROWS, rows - NA_QROWS)[p]
    ws = _na_window_start(r0, rows)
    blocks = {}
    for i in range(NA_QROWS):
        r = r0 + i
        start = min(max(r - NA_ROWS // 2, 0), rows - NA_ROWS)
        for kk in range(NA_WIN_ROWS):
            rel = ws + kk - start
            blocks[i, kk] = ws + kk - r + (NA_ROWS - 1) if 0 <= rel < NA_ROWS else None
    return blocks


def _na_bias_kernel(rb_ref, o_ref, *, rows):
    h = pl.program_id(0)
    cq = lax.broadcasted_iota(jnp.int32, (GRID_W, GRID_W), 0)
    ck = lax.broadcasted_iota(jnp.int32, (GRID_W, GRID_W), 1)
    col_start = jnp.clip(cq - NA_COLS // 2, 0, GRID_W - NA_COLS)
    col_ok = (ck >= col_start) & (ck < col_start + NA_COLS)
    col_off = jnp.clip(ck - cq + (NA_COLS - 1), 0, N_COL_OFF - 1)
    tiles = []
    for ro in range(N_ROW_OFF):
        t = jnp.zeros((GRID_W, GRID_W), F32)
        for co in range(N_COL_OFF):
            t = jnp.where(col_off == co, rb_ref[(h * N_ROW_OFF + ro) * N_COL_OFF + co], t)
        tiles.append(jnp.where(col_ok, t, NEG))
    masked = jnp.full((GRID_W, GRID_W), NEG, F32)
    for p in range(NA_PATTERNS):
        for (i, kk), ro in _na_pattern_blocks(p, rows).items():
            o_ref[0, p, i * GRID_W:(i + 1) * GRID_W, kk * GRID_W:(kk + 1) * GRID_W] = (
                masked if ro is None else tiles[ro])


def _na_bias_call(na_bias, rows):
    shape = (NA_HEADS, NA_PATTERNS, NA_QROWS * GRID_W, NA_WIN_ROWS * GRID_W)
    return pl.pallas_call(
        functools.partial(_na_bias_kernel, rows=rows),
        out_shape=jax.ShapeDtypeStruct(shape, F32),
        grid=(NA_HEADS,),
        in_specs=[pl.BlockSpec(memory_space=pltpu.SMEM)],
        out_specs=pl.BlockSpec((1,) + shape[1:], lambda h: (h, 0, 0, 0)),
        compiler_params=_params("arbitrary"),
        name="na_bias_table",
    )(na_bias.reshape(-1))


def _na_attn_kernel(q_ref, k_ref, v_ref, ck_ref, cv_ref, bt_ref, o_ref, *, L):
    rows = L // GRID_W
    groups = rows // NA_QROWS
    lane = lax.broadcasted_iota(jnp.int32, (NA_QROWS * GRID_W, PAIR), 1)
    for hp in range(NA_HEADS * HEAD_DIM // PAIR):
        heads = (2 * hp, 2 * hp + 1)
        lanes = slice(hp * PAIR, (hp + 1) * PAIR)
        ctx_k = jnp.concatenate([ck_ref[0, 0, h].astype(BF) for h in heads], axis=1)
        ctx_v = jnp.concatenate([cv_ref[0, 0, h].astype(BF) for h in heads], axis=1)

        def body(qi, carry, heads=heads, lanes=lanes, ctx_k=ctx_k, ctx_v=ctx_v):
            r0 = qi * NA_QROWS
            ws = jnp.clip(r0 - NA_ROWS // 2, 0, rows - NA_WIN_ROWS)
            pat = jnp.where(qi == 0, 0, jnp.where(qi == groups - 1, 2, 1))
            rq = pl.ds(pl.multiple_of(r0 * GRID_W, NA_QROWS * GRID_W), NA_QROWS * GRID_W)
            rk = pl.ds(pl.multiple_of(ws * GRID_W, GRID_W), NA_WIN_ROWS * GRID_W)
            q2 = q_ref[0, rq, lanes] * ATTN_SCALE
            k2 = k_ref[0, rk, lanes]
            v2 = v_ref[0, rk, lanes]
            outs = []
            for g, h in enumerate(heads):
                qg = _one_head(q2, lane, g)
                s = _dot_nt(qg, k2) + bt_ref[h, pat]
                s_ctx = _dot_nt(qg, ctx_k)
                m = jnp.maximum(jnp.max(s, axis=-1, keepdims=True), jnp.max(s_ctx, axis=-1, keepdims=True))
                p = jnp.exp(s - m)
                p_ctx = jnp.exp(s_ctx - m)
                den = jnp.sum(p, axis=-1, keepdims=True) + jnp.sum(p_ctx, axis=-1, keepdims=True)
                outs.append((_dot(p.astype(BF), v2) + _dot(p_ctx.astype(BF), ctx_v)) / den)
            o_ref[0, rq, lanes] = jnp.where(lane // HEAD_DIM == 0, outs[0], outs[1]).astype(BF)
            return carry

        lax.fori_loop(0, groups, body, 0)


def _na_attn_call(z, cache_k, cache_v, bias_table, layer):
    B, L, _ = z.shape
    P = cache_k.shape[3]
    assert L // GRID_W >= NA_WIN_ROWS + NA_QROWS and (L // GRID_W) % NA_QROWS == 0
    cspec = pl.BlockSpec((1, 1, NA_HEADS, P, HEAD_DIM), lambda b: (b, layer, 0, 0, 0))
    return pl.pallas_call(
        functools.partial(_na_attn_kernel, L=L),
        out_shape=jax.ShapeDtypeStruct((B, L, BRANCH_W), BF),
        grid=(B,),
        in_specs=[
            _zcol_spec(L, COL_DQ), _zcol_spec(L, COL_DK), _zcol_spec(L, COL_DV), cspec, cspec,
            pl.BlockSpec(bias_table.shape, lambda b: (0, 0, 0, 0), pipeline_mode=pl.Buffered(1)),
        ],
        out_specs=pl.BlockSpec((1, L, BRANCH_W), lambda b: (b, 0, 0)),
        compiler_params=_params("arbitrary"),
        name="neighbourhood_attention",
    )(z, z, z, cache_k, cache_v, bias_table)


def _merge_kernel(x_ref, g_ref, sc_ref, sh_ref, gate_ref, ba_ref, bb_ref, bc_ref, bd_ref,
                  wg0_ref, wg1_ref, wb_ref, wo_ref, o_ref):
    x = x_ref[0]
    h = _norm_mod(x, g_ref[...], sc_ref[0], sh_ref[0]).astype(BF)
    merged = jnp.zeros(x.shape, F32)
    per_ref = MIX_COLS // D_MODEL
    for n, br_ref in enumerate((ba_ref, bb_ref, bc_ref, bd_ref)):
        wg_ref = (wg0_ref, wg1_ref)[n // per_ref]
        c0 = (n % per_ref) * D_MODEL
        gate = jax.nn.sigmoid(_dot(h, wg_ref[0, :, c0:c0 + D_MODEL]))
        merged = merged + gate * _dot(br_ref[0], wb_ref[0, n])
    o_ref[0] = x + gate_ref[0] * _dot(merged.astype(BF), wo_ref[0])


def _merge_call(x, norm_g, mod, branches, w_in_bf, w_branch, w_out, layer, tm):
    B, L, _ = x.shape
    bm = mod.shape[0]
    xspec = pl.BlockSpec((1, tm, D_MODEL), lambda b, i: (b, i, 0))
    bspec = pl.BlockSpec((1, tm, BRANCH_W), lambda b, i: (b, i, 0))
    return pl.pallas_call(
        _merge_kernel,
        out_shape=jax.ShapeDtypeStruct((B, L, D_MODEL), F32),
        grid=(B, L // tm),
        in_specs=[
            xspec,
            pl.BlockSpec((1, D_MODEL), lambda b, i: (0, 0)),
            _mod_spec(bm, 1, 2), _mod_spec(bm, 0, 2), _mod_spec(bm, 2, 2),
            bspec, bspec, bspec, bspec,
            pl.BlockSpec((1, D_MODEL, MIX_COLS), lambda b, i: (layer, 0, 1)),
            pl.BlockSpec((1, D_MODEL, MIX_COLS), lambda b, i: (layer, 0, 2)),
            pl.BlockSpec((1, N_BRANCH, BRANCH_W, D_MODEL), lambda b, i: (layer, 0, 0, 0)),
            pl.BlockSpec((1, D_MODEL, D_MODEL), lambda b, i: (layer, 0, 0)),
        ],
        out_specs=xspec,
        compiler_params=_params("arbitrary", "arbitrary"),
        name="merge_out_proj",
    )(x, norm_g, mod, mod, mod, *branches, w_in_bf, w_in_bf, w_branch, w_out)


ROUTE_CHUNK = 256
GATHER_ROWS = 512
SIGN_BIT = 31
SEARCH_BITS = 4
SEARCH_TOP_SHIFT = ((SIGN_BIT - 1) // SEARCH_BITS) * SEARCH_BITS


def _prefix_excl(mask_ref, out_ref, tri, L):
    totals = [jnp.sum(mask_ref[:, c0:c0 + LANES], axis=-1, keepdims=True) for c0 in range(0, L, LANES)]
    run = jnp.zeros((N_EXPERTS, 1), F32)
    for i, c0 in enumerate(range(0, L, LANES)):
        out_ref[:, c0:c0 + LANES] = _dot(mask_ref[:, c0:c0 + LANES].astype(BF), tri) + run
        run = run + totals[i]


def _split3(v):
    hi = v.astype(BF)
    r = v - hi.astype(F32)
    mid = r.astype(BF)
    lo = (r - mid.astype(F32)).astype(BF)
    return hi, mid, lo


def _route_kernel(x_ref, g_ref, sc_ref, sh_ref, wh_ref, wl_ref, tri_ref,
                  h_ref, gc_ref, slot_ref, slot_t_ref, off_ref, lt_ref, msk_ref, pre_ref, *, L, cap):
    g = g_ref[...]
    sc = sc_ref[0]
    sh = sh_ref[0]
    wh = wh_ref[...]
    wl = wl_ref[...]
    rc = min(L, ROUTE_CHUNK)
    for r0 in range(0, L, rc):
        h = _norm_mod(x_ref[0, r0:r0 + rc, :], g, sc, sh)
        h_hi = h.astype(BF)
        h_ref[0, r0:r0 + rc, :] = h_hi
        h_lo = (h - h_hi.astype(F32)).astype(BF)
        lt_ref[:, r0:r0 + rc] = _dot_nt(wh, h_hi) + (_dot_nt(wh, h_lo) + _dot_nt(wl, h_hi))
    lt = lt_ref[...]
    e = jnp.exp(lt - jnp.max(lt, axis=0, keepdims=True))
    aff = e / jnp.sum(e, axis=0, keepdims=True)
    thr = jnp.zeros((N_EXPERTS, 1), jnp.int32)
    for shift in range(SEARCH_TOP_SHIFT, -1, -SEARCH_BITS):
        n_digits = min(1 << SEARCH_BITS, 1 << (SIGN_BIT - shift))
        digit = jnp.zeros((N_EXPERTS, 1), jnp.int32)
        for d in range(1, n_digits):
            cand = pltpu.bitcast(thr | (d << shift), F32)
            cnt = jnp.sum((aff >= cand).astype(F32), axis=-1, keepdims=True)
            digit = digit + (cnt >= cap).astype(jnp.int32)
        thr = thr | (digit << shift)
    thr_f = pltpu.bitcast(thr, F32)
    gt = aff > thr_f
    eq = aff == thr_f
    need = cap - jnp.sum(gt.astype(F32), axis=-1, keepdims=True)
    tri = tri_ref[...]
    msk_ref[...] = eq.astype(F32)
    _prefix_excl(msk_ref, pre_ref, tri, L)
    sel = gt | (eq & (pre_ref[...] < need))
    msk_ref[...] = sel.astype(F32)
    _prefix_excl(msk_ref, pre_ref, tri, L)
    off_ref[0] = jnp.concatenate([pre_ref[:, c:c + 1] for c in range(0, L, min(L, COMBINE_TL))], axis=1)
    slot = jnp.where(sel, pre_ref[...], -1.0)
    slot_ref[0] = slot
    slot_t_ref[0] = slot.T
    pieces = [p.astype(F32) for p in _split3(jnp.where(sel, aff, 0.0))]
    pad = jnp.zeros((LANES - len(pieces) * N_EXPERTS, L), F32)
    gc_ref[0] = jnp.concatenate(pieces + [pad], axis=0).T.astype(BF)


def _route_call(x, norm_g, mod, wr_hi, wr_lo):
    B, L, _ = x.shape
    bm = mod.shape[0]
    cap = EC_CAPACITY * L // N_EXPERTS
    n_tiles = L // min(L, COMBINE_TL)
    tri = jnp.asarray(np.triu(np.ones((LANES, LANES)), 1).astype(ml_dtypes.bfloat16))
    wspec = pl.BlockSpec((N_EXPERTS, D_MODEL), lambda b: (0, 0))
    return pl.pallas_call(
        functools.partial(_route_kernel, L=L, cap=cap),
        out_shape=[
            jax.ShapeDtypeStruct((B, L, D_MODEL), BF),
            jax.ShapeDtypeStruct((B, L, LANES), BF),
            jax.ShapeDtypeStruct((B, N_EXPERTS, L), F32),
            jax.ShapeDtypeStruct((B, L, N_EXPERTS), F32),
            jax.ShapeDtypeStruct((B, N_EXPERTS, n_tiles), F32),
        ],
        grid=(B,),
        in_specs=[
            pl.BlockSpec((1, L, D_MODEL), lambda b: (b, 0, 0)),
            pl.BlockSpec((1, D_MODEL), lambda b: (0, 0)),
            _mod_spec(bm, 4, 1), _mod_spec(bm, 3, 1),
            wspec, wspec,
            pl.BlockSpec((LANES, LANES), lambda b: (0, 0)),
        ],
        out_specs=[
            pl.BlockSpec((1, L, D_MODEL), lambda b: (b, 0, 0)),
            pl.BlockSpec((1, L, LANES), lambda b: (b, 0, 0)),
            pl.BlockSpec((1, N_EXPERTS, L), lambda b: (b, 0, 0)),
            pl.BlockSpec((1, L, N_EXPERTS), lambda b: (b, 0, 0)),
            pl.BlockSpec((1, N_EXPERTS, n_tiles), lambda b: (b, 0, 0)),
        ],
        scratch_shapes=[
            pltpu.VMEM((N_EXPERTS, L), F32),
            pltpu.VMEM((N_EXPERTS, L), F32),
            pltpu.VMEM((N_EXPERTS, L), F32),
        ],
        compiler_params=_params("arbitrary"),
        name="ec_route",
    )(x, norm_g, mod, mod, wr_hi, wr_lo, tri)


COMBINE_TL = 256
COMBINE_WIN = 64
BF_TILE = 16


def _list_windows(off_ref, cap):
    n_tiles = pl.num_programs(1)
    base = (pl.program_id(0) * n_tiles + pl.program_id(1)) * N_EXPERTS
    starts = []
    fits = None
    for ex in range(N_EXPERTS):
        first = off_ref[base + ex]
        end = jnp.where(pl.program_id(1) == n_tiles - 1, cap, off_ref[base + N_EXPERTS + ex])
        start = jnp.minimum((first // BF_TILE) * BF_TILE, cap - COMBINE_WIN)
        ok = end <= start + COMBINE_WIN
        fits = ok if fits is None else (fits & ok)
        starts.append(start)
    return starts, fits


def _gather_kernel(off_ref, slot_ref, h_ref, gc_ref, xs_ref, gs_ref, *, cap):
    h = h_ref[0]
    gc = gc_ref[0]
    tl = h.shape[0]

    @pl.when(pl.program_id(1) == 0)
    def _():
        xs_ref[...] = jnp.zeros(xs_ref.shape, BF)
        gs_ref[...] = jnp.zeros(gs_ref.shape, F32)

    def gather_all():
        ge = max(1, min(N_EXPERTS, GATHER_ROWS // cap))
        slot_iota = lax.broadcasted_iota(jnp.int32, (cap, tl), 0).astype(F32)
        for e0 in range(0, N_EXPERTS, ge):
            onehot = jnp.concatenate(
                [jnp.where(slot_ref[0, ex:ex + 1, :] == slot_iota, 1.0, 0.0).astype(BF)
                 for ex in range(e0, e0 + ge)], axis=0)
            rows = _dot(onehot, h).astype(BF)
            gates = _dot(onehot, gc)
            for j in range(ge):
                xs_ref[e0 + j, 0] += rows[j * cap:(j + 1) * cap]
                gs_ref[e0 + j, 0] += gates[j * cap:(j + 1) * cap]

    if cap <= COMBINE_WIN:
        gather_all()
        return

    starts, fits = _list_windows(off_ref, cap)

    @pl.when(fits)
    def _():
        win_iota = lax.broadcasted_iota(jnp.int32, (COMBINE_WIN, tl), 0).astype(F32)
        onehot = jnp.concatenate(
            [jnp.where(slot_ref[0, ex:ex + 1, :] - starts[ex].astype(F32) == win_iota, 1.0, 0.0).astype(BF)
             for ex in range(N_EXPERTS)], axis=0)
        rows = _dot(onehot, h).astype(BF)
        gates = _dot(onehot, gc)
        for ex in range(N_EXPERTS):
            win = pl.ds(pl.multiple_of(starts[ex], BF_TILE), COMBINE_WIN)
            piece = slice(ex * COMBINE_WIN, (ex + 1) * COMBINE_WIN)
            xs_ref[ex, 0, win, :] += rows[piece]
            gs_ref[ex, 0, win, :] += gates[piece]

    @pl.when(jnp.logical_not(fits))
    def _():
        gather_all()


def _gather_call(offsets, slot, h2, gate_cols):
    B, L, _ = h2.shape
    cap = EC_CAPACITY * L // N_EXPERTS
    tl = min(L, COMBINE_TL)
    assert cap <= COMBINE_WIN or (cap % BF_TILE == 0 and COMBINE_WIN % BF_TILE == 0)
    return pl.pallas_call(
        functools.partial(_gather_kernel, cap=cap),
        out_shape=[
            jax.ShapeDtypeStruct((N_EXPERTS, B, cap, D_MODEL), BF),
            jax.ShapeDtypeStruct((N_EXPERTS, B, cap, LANES), F32),
        ],
        grid_spec=pltpu.PrefetchScalarGridSpec(
            num_scalar_prefetch=1,
            grid=(B, L // tl),
            in_specs=[
                pl.BlockSpec((1, N_EXPERTS, tl), lambda b, i, off: (b, 0, i)),
                pl.BlockSpec((1, tl, D_MODEL), lambda b, i, off: (b, i, 0)),
                pl.BlockSpec((1, tl, LANES), lambda b, i, off: (b, i, 0)),
            ],
            out_specs=[
                pl.BlockSpec((N_EXPERTS, 1, cap, D_MODEL), lambda b, i, off: (0, b, 0, 0)),
                pl.BlockSpec((N_EXPERTS, 1, cap, LANES), lambda b, i, off: (0, b, 0, 0)),
            ],
        ),
        compiler_params=_params("arbitrary", "arbitrary"),
        name="ec_gather",
    )(offsets, slot, h2, gate_cols)


EXPERT_TR = 512


def _expert_kernel(xa_ref, ga_ref, xb_ref, gb_ref, wg_ref, wu_ref, wd_ref, ya_ref, yb_ref,
                   wgb_ref, wub_ref, wdb_ref, *, tiles_a):
    ex = pl.program_id(0)
    j = pl.program_id(1)

    @pl.when(j == 0)
    def _():
        wgb_ref[...] = wg_ref[0, 0].astype(BF)
        wub_ref[...] = wu_ref[0, 0].astype(BF)
        wdb_ref[...] = wd_ref[0, 0].astype(BF)

    def ffn(x_ref, g_ref, y_ref):
        x = x_ref[0]
        a = _dot(x, wgb_ref[...])
        u = _dot(x, wub_ref[...])
        y = _dot((_silu(a) * u).astype(BF), wdb_ref[...])
        gates = g_ref[0]
        lane = lax.broadcasted_iota(jnp.int32, gates.shape, 1)
        gate = jnp.sum(jnp.where(lane % N_EXPERTS == ex, gates, 0.0), axis=-1, keepdims=True)
        y_ref[0] = (y * gate).astype(BF)

    @pl.when(j < tiles_a)
    def _():
        ffn(xa_ref, ga_ref, ya_ref)

    @pl.when(j >= tiles_a)
    def _():
        ffn(xb_ref, gb_ref, yb_ref)


def _expert_call(xs_a, gs_a, xs_b, gs_b, w_gate, w_up, w_down, layer):
    E, ra, _ = xs_a.shape
    rb = xs_b.shape[1]
    tr = math.gcd(math.gcd(ra, rb), EXPERT_TR)
    tiles_a, tiles_b = ra // tr, rb // tr
    amap = lambda e, j: (e, jnp.minimum(j, tiles_a - 1), 0)
    bmap = lambda e, j: (e, jnp.maximum(j - tiles_a, 0), 0)
    wspec_in = pl.BlockSpec((1, 1, D_MODEL, EXPERT_FF), lambda e, j: (layer, e, 0, 0))
    wspec_out = pl.BlockSpec((1, 1, EXPERT_FF, D_MODEL), lambda e, j: (layer, e, 0, 0))
    return pl.pallas_call(
        functools.partial(_expert_kernel, tiles_a=tiles_a),
        out_shape=[jax.ShapeDtypeStruct((E, ra, D_MODEL), BF), jax.ShapeDtypeStruct((E, rb, D_MODEL), BF)],
        grid=(E, tiles_a + tiles_b),
        in_specs=[
            pl.BlockSpec((1, tr, D_MODEL), amap), pl.BlockSpec((1, tr, LANES), amap),
            pl.BlockSpec((1, tr, D_MODEL), bmap), pl.BlockSpec((1, tr, LANES), bmap),
            wspec_in, wspec_in, wspec_out,
        ],
        out_specs=[pl.BlockSpec((1, tr, D_MODEL), amap), pl.BlockSpec((1, tr, D_MODEL), bmap)],
        scratch_shapes=[
            pltpu.VMEM((D_MODEL, EXPERT_FF), BF),
            pltpu.VMEM((D_MODEL, EXPERT_FF), BF),
            pltpu.VMEM((EXPERT_FF, D_MODEL), BF),
        ],
        compiler_params=_params("arbitrary", "arbitrary"),
        name="ec_experts",
    )(xs_a, gs_a, xs_b, gs_b, w_gate, w_up, w_down)


def _combine_kernel(off_ref, st_ref, y_ref, x_ref, gate_ref, fg_ref, o_ref, *, cap, final):
    st = st_ref[0]
    tl = st.shape[0]

    def finish(scattered):
        v = x_ref[0] + gate_ref[0] * scattered
        if final:
            v = v * lax.rsqrt(jnp.mean(v * v, axis=-1, keepdims=True) + EPS) * fg_ref[...]
        o_ref[0] = v

    def onehot(width, starts):
        n = N_EXPERTS * width
        owner = lax.broadcasted_iota(jnp.int32, (N_EXPERTS, n), 1) // width
        spread = jnp.where(owner == lax.broadcasted_iota(jnp.int32, (N_EXPERTS, n), 0), 1.0, 0.0).astype(BF)
        slot = _dot(st.astype(BF), spread)
        lane = lax.broadcasted_iota(jnp.int32, (1, n), 1)
        want = (lane % width).astype(F32)
        if starts is not None:
            for ex in range(N_EXPERTS):
                want = want + jnp.where(lane // width == ex, starts[ex].astype(F32), 0.0)
        return jnp.where(slot == want, 1.0, 0.0).astype(BF)

    def scatter_all():
        finish(_dot(onehot(cap, None), y_ref[:, 0].reshape(N_EXPERTS * cap, D_MODEL)))

    if cap <= COMBINE_WIN:
        scatter_all()
        return

    starts, fits = _list_windows(off_ref, cap)

    @pl.when(fits)
    def _():
        rows = [y_ref[ex, 0, pl.ds(pl.multiple_of(starts[ex], BF_TILE), COMBINE_WIN), :]
                for ex in range(N_EXPERTS)]
        finish(_dot(onehot(COMBINE_WIN, starts), jnp.concatenate(rows, axis=0)))

    @pl.when(jnp.logical_not(fits))
    def _():
        scatter_all()


def _combine_call(slot_t, offsets, y, x, mod, final_g, final):
    B, L, _ = x.shape
    bm = mod.shape[0]
    cap = y.shape[2]
    tl = min(L, COMBINE_TL)
    assert cap <= COMBINE_WIN or (cap % BF_TILE == 0 and COMBINE_WIN % BF_TILE == 0)
    assert cap <= 256
    gate_map =(lambda b, i, off: (b, 0, 5)) if bm > 1 else (lambda b, i, off: (0, 0, 5))
    return pl.pallas_call(
        functools.partial(_combine_kernel, cap=cap, final=final),
        out_shape=jax.ShapeDtypeStruct((B, L, D_MODEL), F32),
        grid_spec=pltpu.PrefetchScalarGridSpec(
            num_scalar_prefetch=1,
            grid=(B, L // tl),
            in_specs=[
                pl.BlockSpec((1, tl, N_EXPERTS), lambda b, i, off: (b, i, 0)),
                pl.BlockSpec((N_EXPERTS, 1, cap, D_MODEL), lambda b, i, off: (0, b, 0, 0)),
                pl.BlockSpec((1, tl, D_MODEL), lambda b, i, off: (b, i, 0)),
                pl.BlockSpec((1, 1, D_MODEL), gate_map),
                pl.BlockSpec((1, D_MODEL), lambda b, i, off: (0, 0)),
            ],
            out_specs=pl.BlockSpec((1, tl, D_MODEL), lambda b, i, off: (b, i, 0)),
        ),
        compiler_params=_params("arbitrary", "arbitrary"),
        name="ec_combine",
    )(offsets, slot_t, y, x, mod, final_g)


def _mixer_half(x, mod, lw, ctx, tm):
    latent = ctx is not None
    layer = lw["layer"]
    res = _inproj_call(x, lw["norm1_g"], mod, lw["w_in"], layer, tm, rope=latent, want_kv=not latent)
    z = res[0]
    br_a = _conv_call(z, lw["conv_w"], lw["conv_b"], lw["conv_ln_g"], lw["conv_ln_b"])
    br_b = _fourier_call(z, *lw["dft"][z.shape[1]])
    if latent:
        br_c = _win_attn_call(z, ctx["win_k"], ctx["win_v"], lw["win_sink"], layer)
        br_d = _na_attn_call(z, ctx["nat_k"], ctx["nat_v"], lw["bias_table"], layer)
    else:
        br_c, br_d = _ctx_attn_call(z, lw["win_sink"])
    x = _merge_call(x, lw["norm1_g"], mod, (br_a, br_b, br_c, br_d),
                    lw["w_in"], lw["w_branch"], lw["w_out"], layer, tm)
    return x, (tuple(res[1:]) if not latent else None)


def kernel(x_prompt, x_sample, cache_win_k, cache_win_v, cache_nat_k, cache_nat_v, c, c_ctx,
           norm1_g, norm2_g, w_ada, b_ada, w_in, conv_w, conv_b, conv_ln_g, conv_ln_b,
           win_sink, na_bias, w_branch, w_out, w_router, w_e_gate, w_e_up, w_e_down, final_norm_g):
    n_dec = c.shape[0]
    cond = jnp.zeros((ADA_ROWS, D_MODEL), F32).at[0].set(c_ctx).at[1:1 + n_dec].set(c)
    mods = _ada_call(cond, w_ada, b_ada)
    final_g = final_norm_g.reshape(1, D_MODEL)
    w_in_bf = _cast_call(w_in)
    w_branch_bf = _cast_call(w_branch)
    w_out_bf = _cast_call(w_out)
    dft = {L: _dft_tables(L) for L in (x_prompt.shape[1], x_sample.shape[1])}
    ctx = {"win_k": cache_win_k, "win_v": cache_win_v, "nat_k": cache_nat_k, "nat_v": cache_nat_v}

    xp, xs = x_prompt, x_sample
    kvs = []
    for l in range(DEPTH):
        wr_t = w_router[l].T
        wr_hi = wr_t.astype(BF)
        wr_lo = (wr_t - wr_hi.astype(F32)).astype(BF)
        lw = {
            "layer": l,
            "norm1_g": norm1_g[l].reshape(1, D_MODEL),
            "w_in": w_in_bf, "w_branch": w_branch_bf, "w_out": w_out_bf, "dft": dft,
            "conv_w": conv_w[l], "conv_b": conv_b[l],
            "conv_ln_g": conv_ln_g[l], "conv_ln_b": conv_ln_b[l],
            "win_sink": win_sink[l],
            "bias_table": _na_bias_call(na_bias[l], x_sample.shape[1] // GRID_W),
        }
        norm2 = norm2_g[l].reshape(1, D_MODEL)
        final = l == DEPTH - 1
        mod_p = mods[l, 0:1].reshape(1, 1, 6 * D_MODEL)
        mod_s = mods[l, 1:1 + n_dec].reshape(n_dec, 1, 6 * D_MODEL)
        xp, kv = _mixer_half(xp, mod_p, lw, None, tm=256)
        kvs.append(kv)
        xs, _ = _mixer_half(xs, mod_s, lw, ctx, tm=512)
        smem = lambda off: jnp.pad(off.transpose(0, 2, 1).astype(jnp.int32).reshape(-1), (0, N_EXPERTS))
        h_p, gc_p, slot_p, slot_t_p, off_p = _route_call(xp, norm2, mod_p, wr_hi, wr_lo)
        h_s, gc_s, slot_s, slot_t_s, off_s = _route_call(xs, norm2, mod_s, wr_hi, wr_lo)
        off_p, off_s = smem(off_p), smem(off_s)
        rows_p, gates_p = _gather_call(off_p, slot_p, h_p, gc_p)
        rows_s, gates_s = _gather_call(off_s, slot_s, h_s, gc_s)
        flat = lambda a: a.reshape(N_EXPERTS, a.shape[1] * a.shape[2], a.shape[3])
        y_p, y_s = _expert_call(flat(rows_p), flat(gates_p), flat(rows_s), flat(gates_s),
                                w_e_gate, w_e_up, w_e_down, l)
        xp = _combine_call(slot_t_p, off_p, y_p.reshape(rows_p.shape), xp, mod_p, final_g, final)
        xs = _combine_call(slot_t_s, off_s, y_s.reshape(rows_s.shape), xs, mod_s, final_g, final)

    caches = tuple(jnp.stack([kv[n] for kv in kvs], axis=1) for n in range(len(KV_HEADS)))
    return (xp, xs) + caches
```

```python
import functools
import math

import ml_dtypes
import numpy as np
import jax
import jax.numpy as jnp
from jax import lax
from jax.experimental import pallas as pl
from jax.experimental.pallas import tpu as pltpu

D_MODEL = 1024
DEPTH = 2
GRID_W = 64
BRANCH_W = 256
HEAD_DIM = 64
N_BRANCH = 4
CONV_K = 31
FNET_GROUPS = 4
FNET_GDIM = BRANCH_W // FNET_GROUPS
WIN_Q_HEADS = 4
WIN_KV_HEADS = 2
WIN_BLOCK = 128
NA_HEADS = 4
NA_ROWS = 8
NA_COLS = 16
ROPE_THETA = 10000.0
ATTN_SCALE = HEAD_DIM ** -0.5
N_EXPERTS = 16
EXPERT_FF = 1024
EC_CAPACITY = 2
EPS = 1e-6
NEG = -1e30

MIX_COLS = 2048
GATE_COLS = N_BRANCH * D_MODEL
IN_COLS = MIX_COLS + GATE_COLS
COL_A, COL_F, COL_CQ, COL_CKV, COL_DQ, COL_DK, COL_DV = 0, 512, 768, 1024, 1280, 1536, 1792
ROPE_LO, ROPE_HI = COL_CQ, COL_CKV + WIN_KV_HEADS * HEAD_DIM
KV_COLS = (COL_CKV, COL_CKV + WIN_KV_HEADS * HEAD_DIM, COL_DK, COL_DV)
KV_HEADS = (WIN_KV_HEADS, WIN_KV_HEADS, NA_HEADS, NA_HEADS)

LANES = 128
SUBLANES = 8
VMEM_LIMIT = 56 * 1024 * 1024

BF = jnp.bfloat16
F32 = jnp.float32


def _params(*sem):
    return pltpu.CompilerParams(dimension_semantics=sem, vmem_limit_bytes=VMEM_LIMIT)


def _dot(a, b):
    return jnp.dot(a, b, preferred_element_type=F32)


def _dot_nt(a, b):
    return lax.dot_general(a, b, (((1,), (1,)), ((), ())), preferred_element_type=F32)


def _norm_mod(x, g, sc, sh):
    y = x * lax.rsqrt(jnp.mean(x * x, axis=-1, keepdims=True) + EPS)
    return (y * g) * (1.0 + sc) + sh


def _silu(x):
    return x * jax.nn.sigmoid(x)


CAST_BYTES = 4 * 1024 * 1024


def _cast_kernel(x_ref, o_ref):
    o_ref[...] = x_ref[...].astype(BF)


def _cast_call(x):
    shape = x.shape
    x2 = x.reshape(-1, shape[-1])
    rows, cols = x2.shape
    tr = min(rows, 1 << (max(16, CAST_BYTES // (4 * cols)).bit_length() - 1))
    assert rows % tr == 0
    out = pl.pallas_call(
        _cast_kernel,
        out_shape=jax.ShapeDtypeStruct((rows, cols), BF),
        grid=(rows // tr,),
        in_specs=[pl.BlockSpec((tr, cols), lambda i: (i, 0))],
        out_specs=pl.BlockSpec((tr, cols), lambda i: (i, 0)),
        compiler_params=_params("arbitrary"),
        name="cast_bf16",
    )(x2)
    return out.reshape(shape)


ADA_ROWS = 16
ADA_TN = 1536


def _ada_kernel(c_ref, w_ref, b_ref, o_ref):
    s = _silu(c_ref[...]).astype(BF)
    o_ref[0] = _dot(s, w_ref[0].astype(BF)) + b_ref[0]


def _ada_call(cond, w_ada, b_ada):
    n = 6 * D_MODEL
    return pl.pallas_call(
        _ada_kernel,
        out_shape=jax.ShapeDtypeStruct((DEPTH, ADA_ROWS, n), F32),
        grid=(DEPTH, n // ADA_TN),
        in_specs=[
            pl.BlockSpec((ADA_ROWS, D_MODEL), lambda l, j: (0, 0)),
            pl.BlockSpec((1, D_MODEL, ADA_TN), lambda l, j: (l, 0, j)),
            pl.BlockSpec((1, 1, ADA_TN), lambda l, j: (l, 0, j)),
        ],
        out_specs=pl.BlockSpec((1, ADA_ROWS, ADA_TN), lambda l, j: (l, 0, j)),
        compiler_params=_params("arbitrary", "arbitrary"),
        name="ada_mod",
    )(cond, w_ada, b_ada.reshape(DEPTH, 1, n))


def _mod_spec(bm, chunk, ngrid):
    if ngrid == 1:
        imap = (lambda b: (b, 0, chunk)) if bm > 1 else (lambda b: (0, 0, chunk))
    else:
        imap = (lambda b, i: (b, 0, chunk)) if bm > 1 else (lambda b, i: (0, 0, chunk))
    return pl.BlockSpec((1, 1, D_MODEL), imap)


def _inproj_kernel(*refs, rope, want_kv):
    x_ref, g_ref, sc_ref, sh_ref, w_ref = refs[:5]
    pos = 5
    if rope:
        cos_ref, sin_ref = refs[pos:pos + 2]
        pos += 2
    z_ref = refs[pos]
    h = _norm_mod(x_ref[0], g_ref[...], sc_ref[0], sh_ref[0]).astype(BF)
    z = _dot(h, w_ref[0])
    if want_kv:
        for kv_ref, c0, heads in zip(refs[pos + 1:pos + 5], KV_COLS, KV_HEADS):
            for hd in range(heads):
                kv_ref[0, hd] = z[:, c0 + hd * HEAD_DIM:c0 + (hd + 1) * HEAD_DIM]
    if rope:
        z_ref[0, :, 0:ROPE_LO] = z[:, 0:ROPE_LO].astype(BF)
        lane = lax.broadcasted_iota(jnp.int32, (z.shape[0], LANES), 1)
        first_half = (lane % 32) < 16
        cos = cos_ref[...]
        sin = sin_ref[...]
        for c0 in range(ROPE_LO, ROPE_HI, LANES):
            xc = z[:, c0:c0 + LANES]
            partner = jnp.where(first_half, pltpu.roll(xc, LANES - 16, 1), pltpu.roll(xc, 16, 1))
            z_ref[0, :, c0:c0 + LANES] = (xc * cos + partner * sin).astype(BF)
        z_ref[0, :, ROPE_HI:MIX_COLS] = z[:, ROPE_HI:MIX_COLS].astype(BF)
    else:
        z_ref[0] = z.astype(BF)


def _rope_tables(L):
    t = np.arange(L)
    half = HEAD_DIM // 4
    freqs = 1.0 / (ROPE_THETA ** (np.arange(half, dtype=np.float64) / half))
    ang_r = (t // GRID_W).astype(np.float64)[:, None] * freqs[None, :]
    ang_c = (t % GRID_W).astype(np.float64)[:, None] * freqs[None, :]
    cos_h = np.concatenate([np.cos(ang_r), np.cos(ang_r), np.cos(ang_c), np.cos(ang_c)], axis=1)
    sin_h = np.concatenate([-np.sin(ang_r), np.sin(ang_r), -np.sin(ang_c), np.sin(ang_c)], axis=1)
    reps = LANES // HEAD_DIM
    return (jnp.asarray(np.tile(cos_h, (1, reps)), F32), jnp.asarray(np.tile(sin_h, (1, reps)), F32))


def _inproj_call(x, norm_g, mod, w_in_bf, layer, tm, rope, want_kv):
    B, L, _ = x.shape
    bm = mod.shape[0]
    in_specs = [
        pl.BlockSpec((1, tm, D_MODEL), lambda b, i: (b, i, 0)),
        pl.BlockSpec((1, D_MODEL), lambda b, i: (0, 0)),
        _mod_spec(bm, 1, 2),
        _mod_spec(bm, 0, 2),
        pl.BlockSpec((1, D_MODEL, MIX_COLS), lambda b, i: (layer, 0, 0)),
    ]
    args = [x, norm_g, mod, mod, w_in_bf]
    if rope:
        cos, sin = _rope_tables(L)
        in_specs += [pl.BlockSpec((tm, LANES), lambda b, i: (i, 0))] * 2
        args += [cos, sin]
    out_shape = [jax.ShapeDtypeStruct((B, L, MIX_COLS), BF)]
    out_specs = [pl.BlockSpec((1, tm, MIX_COLS), lambda b, i: (b, i, 0))]
    if want_kv:
        for heads in KV_HEADS:
            out_shape.append(jax.ShapeDtypeStruct((B, heads, L, HEAD_DIM), F32))
            out_specs.append(pl.BlockSpec((1, heads, tm, HEAD_DIM), lambda b, i: (b, 0, i, 0)))
    return pl.pallas_call(
        functools.partial(_inproj_kernel, rope=rope, want_kv=want_kv),
        out_shape=out_shape,
        grid=(B, L // tm),
        in_specs=in_specs,
        out_specs=out_specs,
        compiler_params=_params("arbitrary", "arbitrary"),
        name="in_proj",
    )(*args)


CONV_PAD = 16
CONV_TR = 64
CONV_CHUNK = 256


def _conv_kernel(a_ref, w_ref, cb_ref, lg_ref, lb_ref, o_ref, p_ref, u_ref, wb_ref, *, L):
    for k in range(CONV_K):
        wb_ref[k] = jnp.broadcast_to(w_ref[k:k + 1, :], (SUBLANES, BRANCH_W))
    zeros = jnp.zeros((CONV_PAD, BRANCH_W), F32)
    p_ref[0:CONV_PAD, :] = zeros
    p_ref[L + CONV_PAD:L + 2 * CONV_PAD, :] = zeros
    for r0 in range(0, L, CONV_CHUNK):
        a = a_ref[0, r0:r0 + CONV_CHUNK, :].astype(F32)
        p_ref[CONV_PAD + r0:CONV_PAD + r0 + CONV_CHUNK, :] = a[:, :BRANCH_W] * jax.nn.sigmoid(a[:, BRANCH_W:])
    n_u = L + 2 * CONV_PAD - SUBLANES
    for s in range(SUBLANES):
        u_ref[s] = p_ref[s:s + n_u, :]
    cb = cb_ref[...]

    def body(i, carry):
        base = pl.multiple_of(i * CONV_TR, CONV_TR)
        groups = (CONV_TR // SUBLANES, SUBLANES, BRANCH_W)
        acc = jnp.zeros(groups, F32) + cb
        for k in range(CONV_K):
            q, s = divmod(k + CONV_PAD - CONV_K // 2, SUBLANES)
            u = u_ref[s, pl.ds(base + SUBLANES * q, CONV_TR), :]
            acc = acc + wb_ref[k] * u.reshape(groups)
        p_ref[pl.ds(base, CONV_TR), :] = acc.reshape(CONV_TR, BRANCH_W)
        return carry

    lax.fori_loop(0, L // CONV_TR, body, 0)
    lg = lg_ref[...]
    lb = lb_ref[...]
    for r0 in range(0, L, CONV_CHUNK):
        acc = p_ref[r0:r0 + CONV_CHUNK, :]
        mu = jnp.mean(acc, axis=-1, keepdims=True)
        d = acc - mu
        var = jnp.mean(d * d, axis=-1, keepdims=True)
        y = d * lax.rsqrt(var + EPS) * lg + lb
        o_ref[0, r0:r0 + CONV_CHUNK, :] = _silu(y).astype(BF)


def _conv_call(z, conv_w, conv_b, ln_g, ln_b):
    B, L, _ = z.shape
    n_u = L + 2 * CONV_PAD - SUBLANES
    vec = lambda b: (0, 0)
    return pl.pallas_call(
        functools.partial(_conv_kernel, L=L),
        out_shape=jax.ShapeDtypeStruct((B, L, BRANCH_W), BF),
        grid=(B,),
        in_specs=[
            pl.BlockSpec((1, L, 2 * BRANCH_W), lambda b: (b, 0, COL_A // (2 * BRANCH_W))),
            pl.BlockSpec((CONV_K, BRANCH_W), vec),
            pl.BlockSpec((1, BRANCH_W), vec),
            pl.BlockSpec((1, BRANCH_W), vec),
            pl.BlockSpec((1, BRANCH_W), vec),
        ],
        out_specs=pl.BlockSpec((1, L, BRANCH_W), lambda b: (b, 0, 0)),
        scratch_shapes=[
            pltpu.VMEM((L + 2 * CONV_PAD, BRANCH_W), F32),
            pltpu.VMEM((SUBLANES, n_u, BRANCH_W), F32),
            pltpu.VMEM((CONV_K, SUBLANES, BRANCH_W), F32),
        ],
        compiler_params=_params("arbitrary"),
        name="conv_module",
    )(z, conv_w, conv_b.reshape(1, -1), ln_g.reshape(1, -1), ln_b.reshape(1, -1))


def _dft_tables(L):
    c = np.arange(FNET_GDIM)
    ang_c = 2.0 * np.pi * ((c[:, None] * c[None, :]) % FNET_GDIM) / FNET_GDIM
    cc = np.zeros((BRANCH_W, BRANCH_W))
    sc = np.zeros((BRANCH_W, BRANCH_W))
    for g in range(FNET_GROUPS):
        sl = slice(g * FNET_GDIM, (g + 1) * FNET_GDIM)
        cc[sl, sl] = np.cos(ang_c)
        sc[sl, sl] = np.sin(ang_c)
    ccs = np.concatenate([cc, sc], axis=1)
    t = np.arange(L)
    ang_l = 2.0 * np.pi * ((t[:, None] * t[None, :]) % L) / L
    csl = np.concatenate([np.cos(ang_l), -np.sin(ang_l)], axis=1)
    return _cast_call(jnp.asarray(ccs, F32)), _cast_call(jnp.asarray(csl, F32))


FOURIER_TR = 512


def _fourier_kernel(f_ref, ccs_ref, csl_ref, o_ref, pq_ref, *, L):
    scale = float(1.0 / np.sqrt(L * FNET_GDIM))
    tr = min(L, FOURIER_TR)
    for r0 in range(0, L, tr):
        pq = _dot(f_ref[0, r0:r0 + tr, :], ccs_ref[...])
        pq_ref[r0:r0 + tr, :] = pq[:, :BRANCH_W].astype(BF)
        pq_ref[L + r0:L + r0 + tr, :] = pq[:, BRANCH_W:].astype(BF)
    for r0 in range(0, L, tr):
        o = _dot(csl_ref[r0:r0 + tr, :], pq_ref[...])
        o_ref[0, r0:r0 + tr, :] = (o * scale).astype(BF)


def _fourier_call(z, ccs, csl):
    B, L, _ = z.shape
    return pl.pallas_call(
        functools.partial(_fourier_kernel, L=L),
        out_shape=jax.ShapeDtypeStruct((B, L, BRANCH_W), BF),
        grid=(B,),
        in_specs=[
            pl.BlockSpec((1, L, BRANCH_W), lambda b: (b, 0, COL_F // BRANCH_W)),
            pl.BlockSpec((BRANCH_W, 2 * BRANCH_W), lambda b: (0, 0)),
            pl.BlockSpec((L, 2 * L), lambda b: (0, 0), pipeline_mode=pl.Buffered(1)),
        ],
        out_specs=pl.BlockSpec((1, L, BRANCH_W), lambda b: (b, 0, 0)),
        scratch_shapes=[pltpu.VMEM((2 * L, BRANCH_W), BF)],
        compiler_params=_params("arbitrary"),
        name="fourier_mix",
    )(z, ccs, csl)


def _head(ref, rows, h):
    return ref[0, rows, h * HEAD_DIM:(h + 1) * HEAD_DIM]


def _ctx_attn_kernel(sink_ref, qc_ref, kvc_ref, qd_ref, kd_ref, vd_ref, oc_ref, od_ref):
    rows = slice(None)
    group = WIN_Q_HEADS // WIN_KV_HEADS
    for h in range(WIN_Q_HEADS):
        kvh = h // group
        s = _dot_nt(_head(qc_ref, rows, h), _head(kvc_ref, rows, kvh)) * ATTN_SCALE
        sink = sink_ref[h]
        m = jnp.maximum(jnp.max(s, axis=-1, keepdims=True), sink)
        p = jnp.exp(s - m)
        den = jnp.sum(p, axis=-1, keepdims=True) + jnp.exp(sink - m)
        o = _dot(p.astype(BF), _head(kvc_ref, rows, WIN_KV_HEADS + kvh)) / den
        oc_ref[0, :, h * HEAD_DIM:(h + 1) * HEAD_DIM] = o.astype(BF)
    for h in range(NA_HEADS):
        s = _dot_nt(_head(qd_ref, rows, h), _head(kd_ref, rows, h)) * ATTN_SCALE
        m = jnp.max(s, axis=-1, keepdims=True)
        p = jnp.exp(s - m)
        den = jnp.sum(p, axis=-1, keepdims=True)
        o = _dot(p.astype(BF), _head(vd_ref, rows, h)) / den
        od_ref[0, :, h * HEAD_DIM:(h + 1) * HEAD_DIM] = o.astype(BF)


def _zcol_spec(L, col):
    return pl.BlockSpec((1, L, BRANCH_W), lambda b: (b, 0, col // BRANCH_W))


def _ctx_attn_call(z, sink):
    B, L, _ = z.shape
    out = jax.ShapeDtypeStruct((B, L, BRANCH_W), BF)
    ospec = pl.BlockSpec((1, L, BRANCH_W), lambda b: (b, 0, 0))
    return pl.pallas_call(
        _ctx_attn_kernel,
        out_shape=[out, out],
        grid=(B,),
        in_specs=[
            pl.BlockSpec(memory_space=pltpu.SMEM),
            _zcol_spec(L, COL_CQ), _zcol_spec(L, COL_CKV),
            _zcol_spec(L, COL_DQ), _zcol_spec(L, COL_DK), _zcol_spec(L, COL_DV),
        ],
        out_specs=[ospec, ospec],
        compiler_params=_params("arbitrary"),
        name="ctx_attention",
    )(sink, z, z, z, z, z)


PAIR = 2 * HEAD_DIM


def _one_head(x, lane, g):
    return jnp.where(lane // HEAD_DIM == g, x, jnp.zeros_like(x))


def _win_attn_kernel(sink_ref, q_ref, kv_ref, ck_ref, cv_ref, o_ref, *, L):
    nb = L // WIN_BLOCK
    group = WIN_Q_HEADS // WIN_KV_HEADS
    m_rows = group * WIN_BLOCK
    assert WIN_BLOCK == LANES
    row = lax.broadcasted_iota(jnp.int32, (m_rows, WIN_BLOCK), 0)
    a = row % WIN_BLOCK
    j = lax.broadcasted_iota(jnp.int32, (m_rows, WIN_BLOCK), 1)
    row1 = lax.broadcasted_iota(jnp.int32, (m_rows, 1), 0)
    for kvh in range(WIN_KV_HEADS):
        ctx_k = ck_ref[0, 0, kvh].astype(BF)
        ctx_v = cv_ref[0, 0, kvh].astype(BF)
        sink = jnp.zeros((m_rows, 1), F32)
        for g in range(group):
            sink = jnp.where(row1 // WIN_BLOCK == g, sink_ref[kvh * group + g], sink)

        def body(n, carry, kvh=kvh, ctx_k=ctx_k, ctx_v=ctx_v, sink=sink):
            r0 = pl.multiple_of(n * WIN_BLOCK, WIN_BLOCK)
            q = jnp.concatenate(
                [_head(q_ref, pl.ds(r0, WIN_BLOCK), kvh * group + g) for g in range(group)], axis=0)
            q = q * ATTN_SCALE
            s_ctx = _dot_nt(q, ctx_k)
            m_el = jnp.maximum(s_ctx[:, :LANES], s_ctx[:, LANES:])
            scores = []
            vals = []
            for dj in (-1, 0, 1):
                nk = jnp.clip(n + dj, 0, nb - 1)
                rk = pl.multiple_of(nk * WIN_BLOCK, WIN_BLOCK)
                s = _dot_nt(q, _head(kv_ref, pl.ds(rk, WIN_BLOCK), kvh))
                if dj == -1:
                    s = jnp.where((j >= a) & (n >= 1), s, NEG)
                elif dj == 1:
                    s = jnp.where((j <= a) & (n <= nb - 2), s, NEG)
                m_el = jnp.maximum(m_el, s)
                scores.append(s)
                vals.append(_head(kv_ref, pl.ds(rk, WIN_BLOCK), WIN_KV_HEADS + kvh))
            m = jnp.maximum(jnp.max(m_el, axis=-1, keepdims=True), sink)
            p_ctx = jnp.exp(s_ctx - m)
            den_el = p_ctx[:, :LANES] + p_ctx[:, LANES:]
            o = _dot(p_ctx.astype(BF), ctx_v)
            for s, v in zip(scores, vals):
                p = jnp.exp(s - m)
                den_el = den_el + p
                o = o + _dot(p.astype(BF), v)
            den = jnp.sum(den_el, axis=-1, keepdims=True) + jnp.exp(sink - m)
            o = (o / den).astype(BF)
            for g in range(group):
                h = kvh * group + g
                o_ref[0, pl.ds(r0, WIN_BLOCK), h * HEAD_DIM:(h + 1) * HEAD_DIM] = (
                    o[g * WIN_BLOCK:(g + 1) * WIN_BLOCK])
            return carry

        lax.fori_loop(0, nb, body, 0, unroll=2)


def _win_attn_call(z, cache_k, cache_v, sink, layer):
    B, L, _ = z.shape
    P = cache_k.shape[3]
    assert P == 2 * LANES
    cspec = pl.BlockSpec((1, 1, WIN_KV_HEADS, P, HEAD_DIM), lambda b: (b, layer, 0, 0, 0))
    return pl.pallas_call(
        functools.partial(_win_attn_kernel, L=L),
        out_shape=jax.ShapeDtypeStruct((B, L, BRANCH_W), BF),
        grid=(B,),
        in_specs=[
            pl.BlockSpec(memory_space=pltpu.SMEM),
            _zcol_spec(L, COL_CQ), _zcol_spec(L, COL_CKV), cspec, cspec,
        ],
        out_specs=pl.BlockSpec((1, L, BRANCH_W), lambda b: (b, 0, 0)),
        compiler_params=_params("arbitrary"),
        name="window_attention",
    )(sink, z, z, cache_k, cache_v)


N_ROW_OFF = 2 * NA_ROWS - 1
N_COL_OFF = 2 * NA_COLS - 1


NA_QROWS = 4
NA_WIN_ROWS = NA_ROWS + NA_QROWS
NA_PATTERNS = 3


def _na_window_start(r0, rows):
    return min(max(r0 - NA_ROWS // 2, 0), rows - NA_WIN_ROWS)


def _na_pattern_blocks(p, rows):
    r0 = (0, 2 * NA_QROWS, rows - NA_QROWS)[p]
    ws = _na_window_start(r0, rows)
    blocks = {}
    for i in range(NA_QROWS):
        r = r0 + i
        start = min(max(r - NA_ROWS // 2, 0), rows - NA_ROWS)
        for kk in range(NA_WIN_ROWS):
            rel = ws + kk - start
            blocks[i, kk] = ws + kk - r + (NA_ROWS - 1) if 0 <= rel < NA_ROWS else None
    return blocks


def _na_bias_kernel(rb_ref, o_ref, *, rows):
    h = pl.program_id(0)
    cq = lax.broadcasted_iota(jnp.int32, (GRID_W, GRID_W), 0)
    ck = lax.broadcasted_iota(jnp.int32, (GRID_W, GRID_W), 1)
    col_start = jnp.clip(cq - NA_COLS // 2, 0, GRID_W - NA_COLS)
    col_ok = (ck >= col_start) & (ck < col_start + NA_COLS)
    col_off = jnp.clip(ck - cq + (NA_COLS - 1), 0, N_COL_OFF - 1)
    tiles = []
    for ro in range(N_ROW_OFF):
        t = jnp.zeros((GRID_W, GRID_W), F32)
        for co in range(N_COL_OFF):
            t = jnp.where(col_off == co, rb_ref[(h * N_ROW_OFF + ro) * N_COL_OFF + co], t)
        tiles.append(jnp.where(col_ok, t, NEG))
    masked = jnp.full((GRID_W, GRID_W), NEG, F32)
    for p in range(NA_PATTERNS):
        for (i, kk), ro in _na_pattern_blocks(p, rows).items():
            o_ref[0, p, i * GRID_W:(i + 1) * GRID_W, kk * GRID_W:(kk + 1) * GRID_W] = (
                masked if ro is None else tiles[ro])


def _na_bias_call(na_bias, rows):
    shape = (NA_HEADS, NA_PATTERNS, NA_QROWS * GRID_W, NA_WIN_ROWS * GRID_W)
    return pl.pallas_call(
        functools.partial(_na_bias_kernel, rows=rows),
        out_shape=jax.ShapeDtypeStruct(shape, F32),
        grid=(NA_HEADS,),
        in_specs=[pl.BlockSpec(memory_space=pltpu.SMEM)],
        out_specs=pl.BlockSpec((1,) + shape[1:], lambda h: (h, 0, 0, 0)),
        compiler_params=_params("arbitrary"),
        name="na_bias_table",
    )(na_bias.reshape(-1))


def _na_attn_kernel(q_ref, k_ref, v_ref, ck_ref, cv_ref, bt_ref, o_ref, *, L):
    rows = L // GRID_W
    groups = rows // NA_QROWS
    lane = lax.broadcasted_iota(jnp.int32, (NA_QROWS * GRID_W, PAIR), 1)
    for hp in range(NA_HEADS * HEAD_DIM // PAIR):
        heads = (2 * hp, 2 * hp + 1)
        lanes = slice(hp * PAIR, (hp + 1) * PAIR)
        ctx_k = jnp.concatenate([ck_ref[0, 0, h].astype(BF) for h in heads], axis=1)
        ctx_v = jnp.concatenate([cv_ref[0, 0, h].astype(BF) for h in heads], axis=1)

        def body(qi, carry, heads=heads, lanes=lanes, ctx_k=ctx_k, ctx_v=ctx_v):
            r0 = qi * NA_QROWS
            ws = jnp.clip(r0 - NA_ROWS // 2, 0, rows - NA_WIN_ROWS)
            pat = jnp.where(qi == 0, 0, jnp.where(qi == groups - 1, 2, 1))
            rq = pl.ds(pl.multiple_of(r0 * GRID_W, NA_QROWS * GRID_W), NA_QROWS * GRID_W)
            rk = pl.ds(pl.multiple_of(ws * GRID_W, GRID_W), NA_WIN_ROWS * GRID_W)
            q2 = q_ref[0, rq, lanes] * ATTN_SCALE
            k2 = k_ref[0, rk, lanes]
            v2 = v_ref[0, rk, lanes]
            outs = []
            for g, h in enumerate(heads):
                qg = _one_head(q2, lane, g)
                s = _dot_nt(qg, k2) + bt_ref[h, pat]
                s_ctx = _dot_nt(qg, ctx_k)
                m = jnp.maximum(jnp.max(s, axis=-1, keepdims=True), jnp.max(s_ctx, axis=-1, keepdims=True))
                p = jnp.exp(s - m)
                p_ctx = jnp.exp(s_ctx - m)
                den = jnp.sum(p, axis=-1, keepdims=True) + jnp.sum(p_ctx, axis=-1, keepdims=True)
                outs.append((_dot(p.astype(BF), v2) + _dot(p_ctx.astype(BF), ctx_v)) / den)
            o_ref[0, rq, lanes] = jnp.where(lane // HEAD_DIM == 0, outs[0], outs[1]).astype(BF)
            return carry

        lax.fori_loop(0, groups, body, 0)


def _na_attn_call(z, cache_k, cache_v, bias_table, layer):
    B, L, _ = z.shape
    P = cache_k.shape[3]
    assert L // GRID_W >= NA_WIN_ROWS + NA_QROWS and (L // GRID_W) % NA_QROWS == 0
    cspec = pl.BlockSpec((1, 1, NA_HEADS, P, HEAD_DIM), lambda b: (b, layer, 0, 0, 0))
    return pl.pallas_call(
        functools.partial(_na_attn_kernel, L=L),
        out_shape=jax.ShapeDtypeStruct((B, L, BRANCH_W), BF),
        grid=(B,),
        in_specs=[
            _zcol_spec(L, COL_DQ), _zcol_spec(L, COL_DK), _zcol_spec(L, COL_DV), cspec, cspec,
            pl.BlockSpec(bias_table.shape, lambda b: (0, 0, 0, 0), pipeline_mode=pl.Buffered(1)),
        ],
        out_specs=pl.BlockSpec((1, L, BRANCH_W), lambda b: (b, 0, 0)),
        compiler_params=_params("arbitrary"),
        name="neighbourhood_attention",
    )(z, z, z, cache_k, cache_v, bias_table)


def _merge_kernel(x_ref, g_ref, sc_ref, sh_ref, gate_ref, ba_ref, bb_ref, bc_ref, bd_ref,
                  wg0_ref, wg1_ref, wb_ref, wo_ref, o_ref):
    x = x_ref[0]
    h = _norm_mod(x, g_ref[...], sc_ref[0], sh_ref[0]).astype(BF)
    merged = jnp.zeros(x.shape, F32)
    per_ref = MIX_COLS // D_MODEL
    for n, br_ref in enumerate((ba_ref, bb_ref, bc_ref, bd_ref)):
        wg_ref = (wg0_ref, wg1_ref)[n // per_ref]
        c0 = (n % per_ref) * D_MODEL
        gate = jax.nn.sigmoid(_dot(h, wg_ref[0, :, c0:c0 + D_MODEL]))
        merged = merged + gate * _dot(br_ref[0], wb_ref[0, n])
    o_ref[0] = x + gate_ref[0] * _dot(merged.astype(BF), wo_ref[0])


def _merge_call(x, norm_g, mod, branches, w_in_bf, w_branch, w_out, layer, tm):
    B, L, _ = x.shape
    bm = mod.shape[0]
    xspec = pl.BlockSpec((1, tm, D_MODEL), lambda b, i: (b, i, 0))
    bspec = pl.BlockSpec((1, tm, BRANCH_W), lambda b, i: (b, i, 0))
    return pl.pallas_call(
        _merge_kernel,
        out_shape=jax.ShapeDtypeStruct((B, L, D_MODEL), F32),
        grid=(B, L // tm),
        in_specs=[
            xspec,
            pl.BlockSpec((1, D_MODEL), lambda b, i: (0, 0)),
            _mod_spec(bm, 1, 2), _mod_spec(bm, 0, 2), _mod_spec(bm, 2, 2),
            bspec, bspec, bspec, bspec,
            pl.BlockSpec((1, D_MODEL, MIX_COLS), lambda b, i: (layer, 0, 1)),
            pl.BlockSpec((1, D_MODEL, MIX_COLS), lambda b, i: (layer, 0, 2)),
            pl.BlockSpec((1, N_BRANCH, BRANCH_W, D_MODEL), lambda b, i: (layer, 0, 0, 0)),
            pl.BlockSpec((1, D_MODEL, D_MODEL), lambda b, i: (layer, 0, 0)),
        ],
        out_specs=xspec,
        compiler_params=_params("arbitrary", "arbitrary"),
        name="merge_out_proj",
    )(x, norm_g, mod, mod, mod, *branches, w_in_bf, w_in_bf, w_branch, w_out)


ROUTE_CHUNK = 256
GATHER_ROWS = 512
SIGN_BIT = 31
SEARCH_BITS = 4
SEARCH_TOP_SHIFT = ((SIGN_BIT - 1) // SEARCH_BITS) * SEARCH_BITS


def _prefix_excl(mask_ref, out_ref, tri, L):
    totals = [jnp.sum(mask_ref[:, c0:c0 + LANES], axis=-1, keepdims=True) for c0 in range(0, L, LANES)]
    run = jnp.zeros((N_EXPERTS, 1), F32)
    for i, c0 in enumerate(range(0, L, LANES)):
        out_ref[:, c0:c0 + LANES] = _dot(mask_ref[:, c0:c0 + LANES].astype(BF), tri) + run
        run = run + totals[i]


def _split3(v):
    hi = v.astype(BF)
    r = v - hi.astype(F32)
    mid = r.astype(BF)
    lo = (r - mid.astype(F32)).astype(BF)
    return hi, mid, lo


def _route_kernel(x_ref, g_ref, sc_ref, sh_ref, wh_ref, wl_ref, tri_ref,
                  h_ref, gc_ref, slot_ref, slot_t_ref, off_ref, lt_ref, msk_ref, pre_ref, *, L, cap):
    g = g_ref[...]
    sc = sc_ref[0]
    sh = sh_ref[0]
    wh = wh_ref[...]
    wl = wl_ref[...]
    rc = min(L, ROUTE_CHUNK)
    for r0 in range(0, L, rc):
        h = _norm_mod(x_ref[0, r0:r0 + rc, :], g, sc, sh)
        h_hi = h.astype(BF)
        h_ref[0, r0:r0 + rc, :] = h_hi
        h_lo = (h - h_hi.astype(F32)).astype(BF)
        lt_ref[:, r0:r0 + rc] = _dot_nt(wh, h_hi) + (_dot_nt(wh, h_lo) + _dot_nt(wl, h_hi))
    lt = lt_ref[...]
    e = jnp.exp(lt - jnp.max(lt, axis=0, keepdims=True))
    aff = e / jnp.sum(e, axis=0, keepdims=True)
    thr = jnp.zeros((N_EXPERTS, 1), jnp.int32)
    for shift in range(SEARCH_TOP_SHIFT, -1, -SEARCH_BITS):
        n_digits = min(1 << SEARCH_BITS, 1 << (SIGN_BIT - shift))
        digit = jnp.zeros((N_EXPERTS, 1), jnp.int32)
        for d in range(1, n_digits):
            cand = pltpu.bitcast(thr | (d << shift), F32)
            cnt = jnp.sum((aff >= cand).astype(F32), axis=-1, keepdims=True)
            digit = digit + (cnt >= cap).astype(jnp.int32)
        thr = thr | (digit << shift)
    thr_f = pltpu.bitcast(thr, F32)
    gt = aff > thr_f
    eq = aff == thr_f
    need = cap - jnp.sum(gt.astype(F32), axis=-1, keepdims=True)
    tri = tri_ref[...]
    msk_ref[...] = eq.astype(F32)
    _prefix_excl(msk_ref, pre_ref, tri, L)
    sel = gt | (eq & (pre_ref[...] < need))
    msk_ref[...] = sel.astype(F32)
    _prefix_excl(msk_ref, pre_ref, tri, L)
    off_ref[0] = jnp.concatenate([pre_ref[:, c:c + 1] for c in range(0, L, min(L, COMBINE_TL))], axis=1)
    slot = jnp.where(sel, pre_ref[...], -1.0)
    slot_ref[0] = slot
    slot_t_ref[0] = slot.T
    pieces = [p.astype(F32) for p in _split3(jnp.where(sel, aff, 0.0))]
    pad = jnp.zeros((LANES - len(pieces) * N_EXPERTS, L), F32)
    gc_ref[0] = jnp.concatenate(pieces + [pad], axis=0).T.astype(BF)


def _route_call(x, norm_g, mod, wr_hi, wr_lo):
    B, L, _ = x.shape
    bm = mod.shape[0]
    cap = EC_CAPACITY * L // N_EXPERTS
    n_tiles = L // min(L, COMBINE_TL)
    tri = jnp.asarray(np.triu(np.ones((LANES, LANES)), 1).astype(ml_dtypes.bfloat16))
    wspec = pl.BlockSpec((N_EXPERTS, D_MODEL), lambda b: (0, 0))
    return pl.pallas_call(
        functools.partial(_route_kernel, L=L, cap=cap),
        out_shape=[
            jax.ShapeDtypeStruct((B, L, D_MODEL), BF),
            jax.ShapeDtypeStruct((B, L, LANES), BF),
            jax.ShapeDtypeStruct((B, N_EXPERTS, L), F32),
            jax.ShapeDtypeStruct((B, L, N_EXPERTS), F32),
            jax.ShapeDtypeStruct((B, N_EXPERTS, n_tiles), F32),
        ],
        grid=(B,),
        in_specs=[
            pl.BlockSpec((1, L, D_MODEL), lambda b: (b, 0, 0)),
            pl.BlockSpec((1, D_MODEL), lambda b: (0, 0)),
            _mod_spec(bm, 4, 1), _mod_spec(bm, 3, 1),
            wspec, wspec,
            pl.BlockSpec((LANES, LANES), lambda b: (0, 0)),
        ],
        out_specs=[
            pl.BlockSpec((1, L, D_MODEL), lambda b: (b, 0, 0)),
            pl.BlockSpec((1, L, LANES), lambda b: (b, 0, 0)),
            pl.BlockSpec((1, N_EXPERTS, L), lambda b: (b, 0, 0)),
            pl.BlockSpec((1, L, N_EXPERTS), lambda b: (b, 0, 0)),
            pl.BlockSpec((1, N_EXPERTS, n_tiles), lambda b: (b, 0, 0)),
        ],
        scratch_shapes=[
            pltpu.VMEM((N_EXPERTS, L), F32),
            pltpu.VMEM((N_EXPERTS, L), F32),
            pltpu.VMEM((N_EXPERTS, L), F32),
        ],
        compiler_params=_params("arbitrary"),
        name="ec_route",
    )(x, norm_g, mod, mod, wr_hi, wr_lo, tri)


COMBINE_TL = 256
COMBINE_WIN = 64
BF_TILE = 16


def _list_windows(off_ref, cap):
    n_tiles = pl.num_programs(1)
    base = (pl.program_id(0) * n_tiles + pl.program_id(1)) * N_EXPERTS
    starts = []
    fits = None
    for ex in range(N_EXPERTS):
        first = off_ref[base + ex]
        end = jnp.where(pl.program_id(1) == n_tiles - 1, cap, off_ref[base + N_EXPERTS + ex])
        start = jnp.minimum((first // BF_TILE) * BF_TILE, cap - COMBINE_WIN)
        ok = end <= start + COMBINE_WIN
        fits = ok if fits is None else (fits & ok)
        starts.append(start)
    return starts, fits


def _gather_kernel(off_ref, slot_ref, h_ref, gc_ref, xs_ref, gs_ref, *, cap):
    h = h_ref[0]
    gc = gc_ref[0]
    tl = h.shape[0]

    @pl.when(pl.program_id(1) == 0)
    def _():
        xs_ref[...] = jnp.zeros(xs_ref.shape, BF)
        gs_ref[...] = jnp.zeros(gs_ref.shape, F32)

    def gather_all():
        ge = max(1, min(N_EXPERTS, GATHER_ROWS // cap))
        slot_iota = lax.broadcasted_iota(jnp.int32, (cap, tl), 0).astype(F32)
        for e0 in range(0, N_EXPERTS, ge):
            onehot = jnp.concatenate(
                [jnp.where(slot_ref[0, ex:ex + 1, :] == slot_iota, 1.0, 0.0).astype(BF)
                 for ex in range(e0, e0 + ge)], axis=0)
            rows = _dot(onehot, h).astype(BF)
            gates = _dot(onehot, gc)
            for j in range(ge):
                xs_ref[e0 + j, 0] += rows[j * cap:(j + 1) * cap]
                gs_ref[e0 + j, 0] += gates[j * cap:(j + 1) * cap]

    if cap <= COMBINE_WIN:
        gather_all()
        return

    starts, fits = _list_windows(off_ref, cap)

    @pl.when(fits)
    def _():
        win_iota = lax.broadcasted_iota(jnp.int32, (COMBINE_WIN, tl), 0).astype(F32)
        onehot = jnp.concatenate(
            [jnp.where(slot_ref[0, ex:ex + 1, :] - starts[ex].astype(F32) == win_iota, 1.0, 0.0).astype(BF)
             for ex in range(N_EXPERTS)], axis=0)
        rows = _dot(onehot, h).astype(BF)
        gates = _dot(onehot, gc)
        for ex in range(N_EXPERTS):
            win = pl.ds(pl.multiple_of(starts[ex], BF_TILE), COMBINE_WIN)
            piece = slice(ex * COMBINE_WIN, (ex + 1) * COMBINE_WIN)
            xs_ref[ex, 0, win, :] += rows[piece]
            gs_ref[ex, 0, win, :] += gates[piece]

    @pl.when(jnp.logical_not(fits))
    def _():
        gather_all()


def _gather_call(offsets, slot, h2, gate_cols):
    B, L, _ = h2.shape
    cap = EC_CAPACITY * L // N_EXPERTS
    tl = min(L, COMBINE_TL)
    assert cap <= COMBINE_WIN or (cap % BF_TILE == 0 and COMBINE_WIN % BF_TILE == 0)
    return pl.pallas_call(
        functools.partial(_gather_kernel, cap=cap),
        out_shape=[
            jax.ShapeDtypeStruct((N_EXPERTS, B, cap, D_MODEL), BF),
            jax.ShapeDtypeStruct((N_EXPERTS, B, cap, LANES), F32),
        ],
        grid_spec=pltpu.PrefetchScalarGridSpec(
            num_scalar_prefetch=1,
            grid=(B, L // tl),
            in_specs=[
                pl.BlockSpec((1, N_EXPERTS, tl), lambda b, i, off: (b, 0, i)),
                pl.BlockSpec((1, tl, D_MODEL), lambda b, i, off: (b, i, 0)),
                pl.BlockSpec((1, tl, LANES), lambda b, i, off: (b, i, 0)),
            ],
            out_specs=[
                pl.BlockSpec((N_EXPERTS, 1, cap, D_MODEL), lambda b, i, off: (0, b, 0, 0)),
                pl.BlockSpec((N_EXPERTS, 1, cap, LANES), lambda b, i, off: (0, b, 0, 0)),
            ],
        ),
        compiler_params=_params("arbitrary", "arbitrary"),
        name="ec_gather",
    )(offsets, slot, h2, gate_cols)


EXPERT_TR = 512


def _expert_kernel(xa_ref, ga_ref, xb_ref, gb_ref, wg_ref, wu_ref, wd_ref, ya_ref, yb_ref,
                   wgb_ref, wub_ref, wdb_ref, *, tiles_a):
    ex = pl.program_id(0)
    j = pl.program_id(1)

    @pl.when(j == 0)
    def _():
        wgb_ref[...] = wg_ref[0, 0].astype(BF)
        wub_ref[...] = wu_ref[0, 0].astype(BF)
        wdb_ref[...] = wd_ref[0, 0].astype(BF)

    def ffn(x_ref, g_ref, y_ref):
        x = x_ref[0]
        a = _dot(x, wgb_ref[...])
        u = _dot(x, wub_ref[...])
        y = _dot((_silu(a) * u).astype(BF), wdb_ref[...])
        gates = g_ref[0]
        lane = lax.broadcasted_iota(jnp.int32, gates.shape, 1)
        gate = jnp.sum(jnp.where(lane % N_EXPERTS == ex, gates, 0.0), axis=-1, keepdims=True)
        y_ref[0] = (y * gate).astype(BF)

    @pl.when(j < tiles_a)
    def _():
        ffn(xa_ref, ga_ref, ya_ref)

    @pl.when(j >= tiles_a)
    def _():
        ffn(xb_ref, gb_ref, yb_ref)


def _expert_call(xs_a, gs_a, xs_b, gs_b, w_gate, w_up, w_down, layer):
    E, ra, _ = xs_a.shape
    rb = xs_b.shape[1]
    tr = math.gcd(math.gcd(ra, rb), EXPERT_TR)
    tiles_a, tiles_b = ra // tr, rb // tr
    amap = lambda e, j: (e, jnp.minimum(j, tiles_a - 1), 0)
    bmap = lambda e, j: (e, jnp.maximum(j - tiles_a, 0), 0)
    wspec_in = pl.BlockSpec((1, 1, D_MODEL, EXPERT_FF), lambda e, j: (layer, e, 0, 0))
    wspec_out = pl.BlockSpec((1, 1, EXPERT_FF, D_MODEL), lambda e, j: (layer, e, 0, 0))
    return pl.pallas_call(
        functools.partial(_expert_kernel, tiles_a=tiles_a),
        out_shape=[jax.ShapeDtypeStruct((E, ra, D_MODEL), BF), jax.ShapeDtypeStruct((E, rb, D_MODEL), BF)],
        grid=(E, tiles_a + tiles_b),
        in_specs=[
            pl.BlockSpec((1, tr, D_MODEL), amap), pl.BlockSpec((1, tr, LANES), amap),
            pl.BlockSpec((1, tr, D_MODEL), bmap), pl.BlockSpec((1, tr, LANES), bmap),
            wspec_in, wspec_in, wspec_out,
        ],
        out_specs=[pl.BlockSpec((1, tr, D_MODEL), amap), pl.BlockSpec((1, tr, D_MODEL), bmap)],
        scratch_shapes=[
            pltpu.VMEM((D_MODEL, EXPERT_FF), BF),
            pltpu.VMEM((D_MODEL, EXPERT_FF), BF),
            pltpu.VMEM((EXPERT_FF, D_MODEL), BF),
        ],
        compiler_params=_params("arbitrary", "arbitrary"),
        name="ec_experts",
    )(xs_a, gs_a, xs_b, gs_b, w_gate, w_up, w_down)


def _combine_kernel(off_ref, st_ref, y_hbm, x_ref, gate_ref, fg_ref, o_ref, y_buf, y_sem, *, cap, final):
    st = st_ref[0]
    tl = st.shape[0]

    b = pl.program_id(0)
    buf = b % 2

    def y_copy(batch, into):
        return pltpu.make_async_copy(y_hbm.at[:, batch], y_buf.at[into], y_sem.at[into])

    @pl.when(pl.program_id(1) == 0)
    def _():
        @pl.when(b == 0)
        def _():
            y_copy(0, 0).start()

        y_copy(b, buf).wait()

        @pl.when(b + 1 < pl.num_programs(0))
        def _():
            y_copy(b + 1, 1 - buf).start()

    y_ref = y_buf.at[buf]

    def finish(scattered):
        v = x_ref[0] + gate_ref[0] * scattered
        if final:
            v = v * lax.rsqrt(jnp.mean(v * v, axis=-1, keepdims=True) + EPS) * fg_ref[...]
        o_ref[0] = v

    def onehot(width, starts):
        n = N_EXPERTS * width
        owner = lax.broadcasted_iota(jnp.int32, (N_EXPERTS, n), 1) // width
        spread = jnp.where(owner == lax.broadcasted_iota(jnp.int32, (N_EXPERTS, n), 0), 1.0, 0.0).astype(BF)
        slot = _dot(st.astype(BF), spread)
        lane = lax.broadcasted_iota(jnp.int32, (1, n), 1)
        want = (lane % width).astype(F32)
        if starts is not None:
            for ex in range(N_EXPERTS):
                want = want + jnp.where(lane // width == ex, starts[ex].astype(F32), 0.0)
        return jnp.where(slot == want, 1.0, 0.0).astype(BF)

    def scatter_all():
        finish(_dot(onehot(cap, None), y_ref[...].reshape(N_EXPERTS * cap, D_MODEL)))

    if cap <= COMBINE_WIN:
        scatter_all()
        return

    starts, fits = _list_windows(off_ref, cap)

    @pl.when(fits)
    def _():
        rows = [y_ref[ex, pl.ds(pl.multiple_of(starts[ex], BF_TILE), COMBINE_WIN), :]
                for ex in range(N_EXPERTS)]
        finish(_dot(onehot(COMBINE_WIN, starts), jnp.concatenate(rows, axis=0)))

    @pl.when(jnp.logical_not(fits))
    def _():
        scatter_all()


def _combine_call(slot_t, offsets, y, x, mod, final_g, final):
    B, L, _ = x.shape
    bm = mod.shape[0]
    cap = y.shape[2]
    tl = min(L, COMBINE_TL)
    assert cap <= COMBINE_WIN or (cap % BF_TILE == 0 and COMBINE_WIN % BF_TILE == 0)
    assert cap <= 256
    gate_map =(lambda b, i, off: (b, 0, 5)) if bm > 1 else (lambda b, i, off: (0, 0, 5))
    return pl.pallas_call(
        functools.partial(_combine_kernel, cap=cap, final=final),
        out_shape=jax.ShapeDtypeStruct((B, L, D_MODEL), F32),
        grid_spec=pltpu.PrefetchScalarGridSpec(
            num_scalar_prefetch=1,
            grid=(B, L // tl),
            in_specs=[
                pl.BlockSpec((1, tl, N_EXPERTS), lambda b, i, off: (b, i, 0)),
                pl.BlockSpec(memory_space=pl.ANY),
                pl.BlockSpec((1, tl, D_MODEL), lambda b, i, off: (b, i, 0)),
                pl.BlockSpec((1, 1, D_MODEL), gate_map),
                pl.BlockSpec((1, D_MODEL), lambda b, i, off: (0, 0)),
            ],
            out_specs=pl.BlockSpec((1, tl, D_MODEL), lambda b, i, off: (b, i, 0)),
            scratch_shapes=[
                pltpu.VMEM((2, N_EXPERTS, cap, D_MODEL), BF),
                pltpu.SemaphoreType.DMA((2,)),
            ],
        ),
        compiler_params=_params("arbitrary", "arbitrary"),
        name="ec_combine",
    )(offsets, slot_t, y, x, mod, final_g)


def _mixer_half(x, mod, lw, ctx, tm):
    latent = ctx is not None
    layer = lw["layer"]
    res = _inproj_call(x, lw["norm1_g"], mod, lw["w_in"], layer, tm, rope=latent, want_kv=not latent)
    z = res[0]
    br_a = _conv_call(z, lw["conv_w"], lw["conv_b"], lw["conv_ln_g"], lw["conv_ln_b"])
    br_b = _fourier_call(z, *lw["dft"][z.shape[1]])
    if latent:
        br_c = _win_attn_call(z, ctx["win_k"], ctx["win_v"], lw["win_sink"], layer)
        br_d = _na_attn_call(z, ctx["nat_k"], ctx["nat_v"], lw["bias_table"], layer)
    else:
        br_c, br_d = _ctx_attn_call(z, lw["win_sink"])
    x = _merge_call(x, lw["norm1_g"], mod, (br_a, br_b, br_c, br_d),
                    lw["w_in"], lw["w_branch"], lw["w_out"], layer, tm)
    return x, (tuple(res[1:]) if not latent else None)


def kernel(x_prompt, x_sample, cache_win_k, cache_win_v, cache_nat_k, cache_nat_v, c, c_ctx,
           norm1_g, norm2_g, w_ada, b_ada, w_in, conv_w, conv_b, conv_ln_g, conv_ln_b,
           win_sink, na_bias, w_branch, w_out, w_router, w_e_gate, w_e_up, w_e_down, final_norm_g):
    n_dec = c.shape[0]
    cond = jnp.zeros((ADA_ROWS, D_MODEL), F32).at[0].set(c_ctx).at[1:1 + n_dec].set(c)
    mods = _ada_call(cond, w_ada, b_ada)
    final_g = final_norm_g.reshape(1, D_MODEL)
    w_in_bf = _cast_call(w_in)
    w_branch_bf = _cast_call(w_branch)
    w_out_bf = _cast_call(w_out)
    dft = {L: _dft_tables(L) for L in (x_prompt.shape[1], x_sample.shape[1])}
    ctx = {"win_k": cache_win_k, "win_v": cache_win_v, "nat_k": cache_nat_k, "nat_v": cache_nat_v}

    xp, xs = x_prompt, x_sample
    kvs = []
    for l in range(DEPTH):
        wr_t = w_router[l].T
        wr_hi = wr_t.astype(BF)
        wr_lo = (wr_t - wr_hi.astype(F32)).astype(BF)
        lw = {
            "layer": l,
            "norm1_g": norm1_g[l].reshape(1, D_MODEL),
            "w_in": w_in_bf, "w_branch": w_branch_bf, "w_out": w_out_bf, "dft": dft,
            "conv_w": conv_w[l], "conv_b": conv_b[l],
            "conv_ln_g": conv_ln_g[l], "conv_ln_b": conv_ln_b[l],
            "win_sink": win_sink[l],
            "bias_table": _na_bias_call(na_bias[l], x_sample.shape[1] // GRID_W),
        }
        norm2 = norm2_g[l].reshape(1, D_MODEL)
        final = l == DEPTH - 1
        mod_p = mods[l, 0:1].reshape(1, 1, 6 * D_MODEL)
        mod_s = mods[l, 1:1 + n_dec].reshape(n_dec, 1, 6 * D_MODEL)
        xp, kv = _mixer_half(xp, mod_p, lw, None, tm=256)
        kvs.append(kv)
        xs, _ = _mixer_half(xs, mod_s, lw, ctx, tm=512)
        smem = lambda off: jnp.pad(off.transpose(0, 2, 1).astype(jnp.int32).reshape(-1), (0, N_EXPERTS))
        h_p, gc_p, slot_p, slot_t_p, off_p = _route_call(xp, norm2, mod_p, wr_hi, wr_lo)
        h_s, gc_s, slot_s, slot_t_s, off_s = _route_call(xs, norm2, mod_s, wr_hi, wr_lo)
        off_p, off_s = smem(off_p), smem(off_s)
        rows_p, gates_p = _gather_call(off_p, slot_p, h_p, gc_p)
        rows_s, gates_s = _gather_call(off_s, slot_s, h_s, gc_s)
        flat = lambda a: a.reshape(N_EXPERTS, a.shape[1] * a.shape[2], a.shape[3])
        y_p, y_s = _expert_call(flat(rows_p), flat(gates_p), flat(rows_s), flat(gates_s),
                                w_e_gate, w_e_up, w_e_down, l)
        xp = _combine_call(slot_t_p, off_p, y_p.reshape(rows_p.shape), xp, mod_p, final_g, final)
        xs = _combine_call(slot_t_s, off_s, y_s.reshape(rows_s.shape), xs, mod_s, final_g, final)

    caches = tuple(jnp.stack([kv[n] for kv in kvs], axis=1) for n in range(len(KV_HEADS)))
    return (xp, xs) + caches
```

```python
import functools
import math

import ml_dtypes
import numpy as np
import jax
import jax.numpy as jnp
from jax import lax
from jax.experimental import pallas as pl
from jax.experimental.pallas import tpu as pltpu

D_MODEL = 1024
DEPTH = 2
GRID_W = 64
BRANCH_W = 256
HEAD_DIM = 64
N_BRANCH = 4
CONV_K = 31
FNET_GROUPS = 4
FNET_GDIM = BRANCH_W // FNET_GROUPS
WIN_Q_HEADS = 4
WIN_KV_HEADS = 2
WIN_BLOCK = 128
NA_HEADS = 4
NA_ROWS = 8
NA_COLS = 16
ROPE_THETA = 10000.0
ATTN_SCALE = HEAD_DIM ** -0.5
N_EXPERTS = 16
EXPERT_FF = 1024
EC_CAPACITY = 2
EPS = 1e-6
NEG = -1e30

MIX_COLS = 2048
GATE_COLS = N_BRANCH * D_MODEL
IN_COLS = MIX_COLS + GATE_COLS
COL_A, COL_F, COL_CQ, COL_CKV, COL_DQ, COL_DK, COL_DV = 0, 512, 768, 1024, 1280, 1536, 1792
ROPE_LO, ROPE_HI = COL_CQ, COL_CKV + WIN_KV_HEADS * HEAD_DIM
KV_COLS = (COL_CKV, COL_CKV + WIN_KV_HEADS * HEAD_DIM, COL_DK, COL_DV)
KV_HEADS = (WIN_KV_HEADS, WIN_KV_HEADS, NA_HEADS, NA_HEADS)

LANES = 128
SUBLANES = 8
VMEM_LIMIT = 56 * 1024 * 1024

BF = jnp.bfloat16
F32 = jnp.float32


def _params(*sem):
    return pltpu.CompilerParams(dimension_semantics=sem, vmem_limit_bytes=VMEM_LIMIT)


def _dot(a, b):
    return jnp.dot(a, b, preferred_element_type=F32)


def _dot_nt(a, b):
    return lax.dot_general(a, b, (((1,), (1,)), ((), ())), preferred_element_type=F32)


def _norm_mod(x, g, sc, sh):
    y = x * lax.rsqrt(jnp.mean(x * x, axis=-1, keepdims=True) + EPS)
    return (y * g) * (1.0 + sc) + sh


def _silu(x):
    return x * jax.nn.sigmoid(x)


CAST_BYTES = 4 * 1024 * 1024


def _cast_kernel(x_ref, o_ref):
    o_ref[...] = x_ref[...].astype(BF)


def _cast_call(x):
    shape = x.shape
    x2 = x.reshape(-1, shape[-1])
    rows, cols = x2.shape
    tr = min(rows, 1 << (max(16, CAST_BYTES // (4 * cols)).bit_length() - 1))
    assert rows % tr == 0
    out = pl.pallas_call(
        _cast_kernel,
        out_shape=jax.ShapeDtypeStruct((rows, cols), BF),
        grid=(rows // tr,),
        in_specs=[pl.BlockSpec((tr, cols), lambda i: (i, 0))],
        out_specs=pl.BlockSpec((tr, cols), lambda i: (i, 0)),
        compiler_params=_params("arbitrary"),
        name="cast_bf16",
    )(x2)
    return out.reshape(shape)


ADA_ROWS = 16
ADA_TN = 1536


def _ada_kernel(c_ref, w_ref, b_ref, o_ref):
    s = _silu(c_ref[...]).astype(BF)
    o_ref[0] = _dot(s, w_ref[0].astype(BF)) + b_ref[0]


def _ada_call(cond, w_ada, b_ada):
    n = 6 * D_MODEL
    return pl.pallas_call(
        _ada_kernel,
        out_shape=jax.ShapeDtypeStruct((DEPTH, ADA_ROWS, n), F32),
        grid=(DEPTH, n // ADA_TN),
        in_specs=[
            pl.BlockSpec((ADA_ROWS, D_MODEL), lambda l, j: (0, 0)),
            pl.BlockSpec((1, D_MODEL, ADA_TN), lambda l, j: (l, 0, j)),
            pl.BlockSpec((1, 1, ADA_TN), lambda l, j: (l, 0, j)),
        ],
        out_specs=pl.BlockSpec((1, ADA_ROWS, ADA_TN), lambda l, j: (l, 0, j)),
        compiler_params=_params("arbitrary", "arbitrary"),
        name="ada_mod",
    )(cond, w_ada, b_ada.reshape(DEPTH, 1, n))


def _mod_spec(bm, chunk, ngrid):
    if ngrid == 1:
        imap = (lambda b: (b, 0, chunk)) if bm > 1 else (lambda b: (0, 0, chunk))
    else:
        imap = (lambda b, i: (b, 0, chunk)) if bm > 1 else (lambda b, i: (0, 0, chunk))
    return pl.BlockSpec((1, 1, D_MODEL), imap)


def _inproj_kernel(*refs, rope, kv_layer, kv_aliased):
    x_ref, g_ref, sc_ref, sh_ref, w_ref = refs[:5]
    pos = 5
    if rope:
        cos_ref, sin_ref = refs[pos:pos + 2]
        pos += 2
    if kv_aliased:
        pos += len(KV_HEADS)
    z_ref = refs[pos]
    h = _norm_mod(x_ref[0], g_ref[...], sc_ref[0], sh_ref[0]).astype(BF)
    z = _dot(h, w_ref[0])
    if kv_layer is not None:
        for kv_ref, c0, heads in zip(refs[pos + 1:pos + 1 + len(KV_HEADS)], KV_COLS, KV_HEADS):
            here = 0 if kv_aliased else kv_layer
            for hd in range(heads):
                kv_ref[0, here, hd] = z[:, c0 + hd * HEAD_DIM:c0 + (hd + 1) * HEAD_DIM]
            for other in range(kv_ref.shape[1]):
                if other != here:
                    kv_ref[0, other] = jnp.zeros(kv_ref.shape[2:], F32)
    if rope:
        z_ref[0, :, 0:ROPE_LO] = z[:, 0:ROPE_LO].astype(BF)
        lane = lax.broadcasted_iota(jnp.int32, (z.shape[0], LANES), 1)
        first_half = (lane % 32) < 16
        cos = cos_ref[...]
        sin = sin_ref[...]
        for c0 in range(ROPE_LO, ROPE_HI, LANES):
            xc = z[:, c0:c0 + LANES]
            partner = jnp.where(first_half, pltpu.roll(xc, LANES - 16, 1), pltpu.roll(xc, 16, 1))
            z_ref[0, :, c0:c0 + LANES] = (xc * cos + partner * sin).astype(BF)
        z_ref[0, :, ROPE_HI:MIX_COLS] = z[:, ROPE_HI:MIX_COLS].astype(BF)
    else:
        z_ref[0] = z.astype(BF)


def _rope_tables(L):
    t = np.arange(L)
    half = HEAD_DIM // 4
    freqs = 1.0 / (ROPE_THETA ** (np.arange(half, dtype=np.float64) / half))
    ang_r = (t // GRID_W).astype(np.float64)[:, None] * freqs[None, :]
    ang_c = (t % GRID_W).astype(np.float64)[:, None] * freqs[None, :]
    cos_h = np.concatenate([np.cos(ang_r), np.cos(ang_r), np.cos(ang_c), np.cos(ang_c)], axis=1)
    sin_h = np.concatenate([-np.sin(ang_r), np.sin(ang_r), -np.sin(ang_c), np.sin(ang_c)], axis=1)
    reps = LANES // HEAD_DIM
    return (jnp.asarray(np.tile(cos_h, (1, reps)), F32), jnp.asarray(np.tile(sin_h, (1, reps)), F32))


def _inproj_call(x, norm_g, mod, w_in_bf, layer, tm, rope, want_kv, kv_prev=None):
    B, L, _ = x.shape
    bm = mod.shape[0]
    in_specs = [
        pl.BlockSpec((1, tm, D_MODEL), lambda b, i: (b, i, 0)),
        pl.BlockSpec((1, D_MODEL), lambda b, i: (0, 0)),
        _mod_spec(bm, 1, 2),
        _mod_spec(bm, 0, 2),
        pl.BlockSpec((1, D_MODEL, MIX_COLS), lambda b, i: (layer, 0, 0)),
    ]
    args = [x, norm_g, mod, mod, w_in_bf]
    if rope:
        cos, sin = _rope_tables(L)
        in_specs += [pl.BlockSpec((tm, LANES), lambda b, i: (i, 0))] * 2
        args += [cos, sin]
    out_shape = [jax.ShapeDtypeStruct((B, L, MIX_COLS), BF)]
    out_specs = [pl.BlockSpec((1, tm, MIX_COLS), lambda b, i: (b, i, 0))]
    aliases = {}
    if want_kv:
        for n, heads in enumerate(KV_HEADS):
            out_shape.append(jax.ShapeDtypeStruct((B, DEPTH, heads, L, HEAD_DIM), F32))
            if kv_prev is None:
                out_specs.append(pl.BlockSpec((1, DEPTH, heads, tm, HEAD_DIM), lambda b, i: (b, 0, 0, i, 0)))
            else:
                out_specs.append(pl.BlockSpec((1, 1, heads, tm, HEAD_DIM), lambda b, i: (b, layer, 0, i, 0)))
                aliases[len(args)] = 1 + n
                in_specs.append(pl.BlockSpec(memory_space=pl.ANY))
                args.append(kv_prev[n])
    return pl.pallas_call(
        functools.partial(_inproj_kernel, rope=rope, kv_layer=layer if want_kv else None,
                          kv_aliased=kv_prev is not None),
        out_shape=out_shape,
        grid=(B, L // tm),
        in_specs=in_specs,
        out_specs=out_specs,
        input_output_aliases=aliases,
        compiler_params=_params("arbitrary", "arbitrary"),
        name="in_proj",
    )(*args)


CONV_PAD = 16
CONV_TR = 64
CONV_CHUNK = 256


def _conv_kernel(a_ref, w_ref, cb_ref, lg_ref, lb_ref, o_ref, p_ref, u_ref, wb_ref, *, L):
    for k in range(CONV_K):
        wb_ref[k] = jnp.broadcast_to(w_ref[k:k + 1, :], (SUBLANES, BRANCH_W))
    zeros = jnp.zeros((CONV_PAD, BRANCH_W), F32)
    p_ref[0:CONV_PAD, :] = zeros
    p_ref[L + CONV_PAD:L + 2 * CONV_PAD, :] = zeros
    for r0 in range(0, L, CONV_CHUNK):
        a = a_ref[0, r0:r0 + CONV_CHUNK, :].astype(F32)
        p_ref[CONV_PAD + r0:CONV_PAD + r0 + CONV_CHUNK, :] = a[:, :BRANCH_W] * jax.nn.sigmoid(a[:, BRANCH_W:])
    n_u = L + 2 * CONV_PAD - SUBLANES
    for s in range(SUBLANES):
        u_ref[s] = p_ref[s:s + n_u, :]
    cb = cb_ref[...]

    def body(i, carry):
        base = pl.multiple_of(i * CONV_TR, CONV_TR)
        groups = (CONV_TR // SUBLANES, SUBLANES, BRANCH_W)
        acc = jnp.zeros(groups, F32) + cb
        for k in range(CONV_K):
            q, s = divmod(k + CONV_PAD - CONV_K // 2, SUBLANES)
            u = u_ref[s, pl.ds(base + SUBLANES * q, CONV_TR), :]
            acc = acc + wb_ref[k] * u.reshape(groups)
        p_ref[pl.ds(base, CONV_TR), :] = acc.reshape(CONV_TR, BRANCH_W)
        return carry

    lax.fori_loop(0, L // CONV_TR, body, 0)
    lg = lg_ref[...]
    lb = lb_ref[...]
    for r0 in range(0, L, CONV_CHUNK):
        acc = p_ref[r0:r0 + CONV_CHUNK, :]
        mu = jnp.mean(acc, axis=-1, keepdims=True)
        d = acc - mu
        var = jnp.mean(d * d, axis=-1, keepdims=True)
        y = d * lax.rsqrt(var + EPS) * lg + lb
        o_ref[0, r0:r0 + CONV_CHUNK, :] = _silu(y).astype(BF)


def _conv_call(z, conv_w, conv_b, ln_g, ln_b):
    B, L, _ = z.shape
    n_u = L + 2 * CONV_PAD - SUBLANES
    vec = lambda b: (0, 0)
    return pl.pallas_call(
        functools.partial(_conv_kernel, L=L),
        out_shape=jax.ShapeDtypeStruct((B, L, BRANCH_W), BF),
        grid=(B,),
        in_specs=[
            pl.BlockSpec((1, L, 2 * BRANCH_W), lambda b: (b, 0, COL_A // (2 * BRANCH_W))),
            pl.BlockSpec((CONV_K, BRANCH_W), vec),
            pl.BlockSpec((1, BRANCH_W), vec),
            pl.BlockSpec((1, BRANCH_W), vec),
            pl.BlockSpec((1, BRANCH_W), vec),
        ],
        out_specs=pl.BlockSpec((1, L, BRANCH_W), lambda b: (b, 0, 0)),
        scratch_shapes=[
            pltpu.VMEM((L + 2 * CONV_PAD, BRANCH_W), F32),
            pltpu.VMEM((SUBLANES, n_u, BRANCH_W), F32),
            pltpu.VMEM((CONV_K, SUBLANES, BRANCH_W), F32),
        ],
        compiler_params=_params("arbitrary"),
        name="conv_module",
    )(z, conv_w, conv_b.reshape(1, -1), ln_g.reshape(1, -1), ln_b.reshape(1, -1))


def _dft_tables(L):
    c = np.arange(FNET_GDIM)
    ang_c = 2.0 * np.pi * ((c[:, None] * c[None, :]) % FNET_GDIM) / FNET_GDIM
    cc = np.zeros((BRANCH_W, BRANCH_W))
    sc = np.zeros((BRANCH_W, BRANCH_W))
    for g in range(FNET_GROUPS):
        sl = slice(g * FNET_GDIM, (g + 1) * FNET_GDIM)
        cc[sl, sl] = np.cos(ang_c)
        sc[sl, sl] = np.sin(ang_c)
    ccs = np.concatenate([cc, sc], axis=1)
    t = np.arange(L)
    ang_l = 2.0 * np.pi * ((t[:, None] * t[None, :]) % L) / L
    csl = np.concatenate([np.cos(ang_l), -np.sin(ang_l)], axis=1)
    return _cast_call(jnp.asarray(ccs, F32)), _cast_call(jnp.asarray(csl, F32))


FOURIER_TR = 512


def _fourier_kernel(f_ref, ccs_ref, csl_ref, o_ref, pq_ref, *, L):
    scale = float(1.0 / np.sqrt(L * FNET_GDIM))
    tr = min(L, FOURIER_TR)
    for r0 in range(0, L, tr):
        pq = _dot(f_ref[0, r0:r0 + tr, :], ccs_ref[...])
        pq_ref[r0:r0 + tr, :] = pq[:, :BRANCH_W].astype(BF)
        pq_ref[L + r0:L + r0 + tr, :] = pq[:, BRANCH_W:].astype(BF)
    for r0 in range(0, L, tr):
        o = _dot(csl_ref[r0:r0 + tr, :], pq_ref[...])
        o_ref[0, r0:r0 + tr, :] = (o * scale).astype(BF)


def _fourier_call(z, ccs, csl):
    B, L, _ = z.shape
    return pl.pallas_call(
        functools.partial(_fourier_kernel, L=L),
        out_shape=jax.ShapeDtypeStruct((B, L, BRANCH_W), BF),
        grid=(B,),
        in_specs=[
            pl.BlockSpec((1, L, BRANCH_W), lambda b: (b, 0, COL_F // BRANCH_W)),
            pl.BlockSpec((BRANCH_W, 2 * BRANCH_W), lambda b: (0, 0)),
            pl.BlockSpec((L, 2 * L), lambda b: (0, 0), pipeline_mode=pl.Buffered(1)),
        ],
        out_specs=pl.BlockSpec((1, L, BRANCH_W), lambda b: (b, 0, 0)),
        scratch_shapes=[pltpu.VMEM((2 * L, BRANCH_W), BF)],
        compiler_params=_params("arbitrary"),
        name="fourier_mix",
    )(z, ccs, csl)


def _head(ref, rows, h):
    return ref[0, rows, h * HEAD_DIM:(h + 1) * HEAD_DIM]


def _ctx_attn_kernel(sink_ref, qc_ref, kvc_ref, qd_ref, kd_ref, vd_ref, oc_ref, od_ref):
    rows = slice(None)
    group = WIN_Q_HEADS // WIN_KV_HEADS
    for h in range(WIN_Q_HEADS):
        kvh = h // group
        s = _dot_nt(_head(qc_ref, rows, h), _head(kvc_ref, rows, kvh)) * ATTN_SCALE
        sink = sink_ref[h]
        m = jnp.maximum(jnp.max(s, axis=-1, keepdims=True), sink)
        p = jnp.exp(s - m)
        den = jnp.sum(p, axis=-1, keepdims=True) + jnp.exp(sink - m)
        o = _dot(p.astype(BF), _head(kvc_ref, rows, WIN_KV_HEADS + kvh)) / den
        oc_ref[0, :, h * HEAD_DIM:(h + 1) * HEAD_DIM] = o.astype(BF)
    for h in range(NA_HEADS):
        s = _dot_nt(_head(qd_ref, rows, h), _head(kd_ref, rows, h)) * ATTN_SCALE
        m = jnp.max(s, axis=-1, keepdims=True)
        p = jnp.exp(s - m)
        den = jnp.sum(p, axis=-1, keepdims=True)
        o = _dot(p.astype(BF), _head(vd_ref, rows, h)) / den
        od_ref[0, :, h * HEAD_DIM:(h + 1) * HEAD_DIM] = o.astype(BF)


def _zcol_spec(L, col):
    return pl.BlockSpec((1, L, BRANCH_W), lambda b: (b, 0, col // BRANCH_W))


def _ctx_attn_call(z, sink):
    B, L, _ = z.shape
    out = jax.ShapeDtypeStruct((B, L, BRANCH_W), BF)
    ospec = pl.BlockSpec((1, L, BRANCH_W), lambda b: (b, 0, 0))
    return pl.pallas_call(
        _ctx_attn_kernel,
        out_shape=[out, out],
        grid=(B,),
        in_specs=[
            pl.BlockSpec(memory_space=pltpu.SMEM),
            _zcol_spec(L, COL_CQ), _zcol_spec(L, COL_CKV),
            _zcol_spec(L, COL_DQ), _zcol_spec(L, COL_DK), _zcol_spec(L, COL_DV),
        ],
        out_specs=[ospec, ospec],
        compiler_params=_params("arbitrary"),
        name="ctx_attention",
    )(sink, z, z, z, z, z)


PAIR = 2 * HEAD_DIM


def _one_head(x, lane, g):
    return jnp.where(lane // HEAD_DIM == g, x, jnp.zeros_like(x))


def _win_attn_kernel(sink_ref, q_ref, kv_ref, ck_ref, cv_ref, o_ref, *, L):
    nb = L // WIN_BLOCK
    group = WIN_Q_HEADS // WIN_KV_HEADS
    m_rows = group * WIN_BLOCK
    assert WIN_BLOCK == LANES
    row = lax.broadcasted_iota(jnp.int32, (m_rows, WIN_BLOCK), 0)
    a = row % WIN_BLOCK
    j = lax.broadcasted_iota(jnp.int32, (m_rows, WIN_BLOCK), 1)
    row1 = lax.broadcasted_iota(jnp.int32, (m_rows, 1), 0)
    for kvh in range(WIN_KV_HEADS):
        ctx_k = ck_ref[0, 0, kvh].astype(BF)
        ctx_v = cv_ref[0, 0, kvh].astype(BF)
        sink = jnp.zeros((m_rows, 1), F32)
        for g in range(group):
            sink = jnp.where(row1 // WIN_BLOCK == g, sink_ref[kvh * group + g], sink)

        def body(n, carry, kvh=kvh, ctx_k=ctx_k, ctx_v=ctx_v, sink=sink):
            r0 = pl.multiple_of(n * WIN_BLOCK, WIN_BLOCK)
            q = jnp.concatenate(
                [_head(q_ref, pl.ds(r0, WIN_BLOCK), kvh * group + g) for g in range(group)], axis=0)
            q = q * ATTN_SCALE
            s_ctx = _dot_nt(q, ctx_k)
            m_el = jnp.maximum(s_ctx[:, :LANES], s_ctx[:, LANES:])
            scores = []
            vals = []
            for dj in (-1, 0, 1):
                nk = jnp.clip(n + dj, 0, nb - 1)
                rk = pl.multiple_of(nk * WIN_BLOCK, WIN_BLOCK)
                s = _dot_nt(q, _head(kv_ref, pl.ds(rk, WIN_BLOCK), kvh))
                if dj == -1:
                    s = jnp.where((j >= a) & (n >= 1), s, NEG)
                elif dj == 1:
                    s = jnp.where((j <= a) & (n <= nb - 2), s, NEG)
                m_el = jnp.maximum(m_el, s)
                scores.append(s)
                vals.append(_head(kv_ref, pl.ds(rk, WIN_BLOCK), WIN_KV_HEADS + kvh))
            m = jnp.maximum(jnp.max(m_el, axis=-1, keepdims=True), sink)
            p_ctx = jnp.exp(s_ctx - m)
            den_el = p_ctx[:, :LANES] + p_ctx[:, LANES:]
            o = _dot(p_ctx.astype(BF), ctx_v)
            for s, v in zip(scores, vals):
                p = jnp.exp(s - m)
                den_el = den_el + p
                o = o + _dot(p.astype(BF), v)
            den = jnp.sum(den_el, axis=-1, keepdims=True) + jnp.exp(sink - m)
            o = (o / den).astype(BF)
            for g in range(group):
                h = kvh * group + g
                o_ref[0, pl.ds(r0, WIN_BLOCK), h * HEAD_DIM:(h + 1) * HEAD_DIM] = (
                    o[g * WIN_BLOCK:(g + 1) * WIN_BLOCK])
            return carry

        lax.fori_loop(0, nb, body, 0, unroll=2)


def _win_attn_call(z, cache_k, cache_v, sink, layer):
    B, L, _ = z.shape
    P = cache_k.shape[3]
    assert P == 2 * LANES
    cspec = pl.BlockSpec((1, 1, WIN_KV_HEADS, P, HEAD_DIM), lambda b: (b, layer, 0, 0, 0))
    return pl.pallas_call(
        functools.partial(_win_attn_kernel, L=L),
        out_shape=jax.ShapeDtypeStruct((B, L, BRANCH_W), BF),
        grid=(B,),
        in_specs=[
            pl.BlockSpec(memory_space=pltpu.SMEM),
            _zcol_spec(L, COL_CQ), _zcol_spec(L, COL_CKV), cspec, cspec,
        ],
        out_specs=pl.BlockSpec((1, L, BRANCH_W), lambda b: (b, 0, 0)),
        compiler_params=_params("arbitrary"),
        name="window_attention",
    )(sink, z, z, cache_k, cache_v)


N_ROW_OFF = 2 * NA_ROWS - 1
N_COL_OFF = 2 * NA_COLS - 1


NA_QROWS = 4
NA_WIN_ROWS = NA_ROWS + NA_QROWS
NA_PATTERNS = 3


def _na_window_start(r0, rows):
    return min(max(r0 - NA_ROWS // 2, 0), rows - NA_WIN_ROWS)


def _na_pattern_blocks(p, rows):
    r0 = (0, 2 * NA_QROWS, rows - NA_QROWS)[p]
    ws = _na_window_start(r0, rows)
    blocks = {}
    for i in range(NA_QROWS):
        r = r0 + i
        start = min(max(r - NA_ROWS // 2, 0), rows - NA_ROWS)
        for kk in range(NA_WIN_ROWS):
            rel = ws + kk - start
            blocks[i, kk] = ws + kk - r + (NA_ROWS - 1) if 0 <= rel < NA_ROWS else None
    return blocks


def _na_bias_kernel(rb_ref, o_ref, *, rows):
    h = pl.program_id(0)
    cq = lax.broadcasted_iota(jnp.int32, (GRID_W, GRID_W), 0)
    ck = lax.broadcasted_iota(jnp.int32, (GRID_W, GRID_W), 1)
    col_start = jnp.clip(cq - NA_COLS // 2, 0, GRID_W - NA_COLS)
    col_ok = (ck >= col_start) & (ck < col_start + NA_COLS)
    col_off = jnp.clip(ck - cq + (NA_COLS - 1), 0, N_COL_OFF - 1)
    tiles = []
    for ro in range(N_ROW_OFF):
        t = jnp.zeros((GRID_W, GRID_W), F32)
        for co in range(N_COL_OFF):
            t = jnp.where(col_off == co, rb_ref[(h * N_ROW_OFF + ro) * N_COL_OFF + co], t)
        tiles.append(jnp.where(col_ok, t, NEG))
    masked = jnp.full((GRID_W, GRID_W), NEG, F32)
    for p in range(NA_PATTERNS):
        for (i, kk), ro in _na_pattern_blocks(p, rows).items():
            o_ref[0, p, i * GRID_W:(i + 1) * GRID_W, kk * GRID_W:(kk + 1) * GRID_W] = (
                masked if ro is None else tiles[ro])


def _na_bias_call(na_bias, rows):
    shape = (NA_HEADS, NA_PATTERNS, NA_QROWS * GRID_W, NA_WIN_ROWS * GRID_W)
    return pl.pallas_call(
        functools.partial(_na_bias_kernel, rows=rows),
        out_shape=jax.ShapeDtypeStruct(shape, F32),
        grid=(NA_HEADS,),
        in_specs=[pl.BlockSpec(memory_space=pltpu.SMEM)],
        out_specs=pl.BlockSpec((1,) + shape[1:], lambda h: (h, 0, 0, 0)),
        compiler_params=_params("arbitrary"),
        name="na_bias_table",
    )(na_bias.reshape(-1))


def _na_attn_kernel(q_ref, k_ref, v_ref, ck_ref, cv_ref, bt_ref, o_ref, *, L):
    rows = L // GRID_W
    groups = rows // NA_QROWS
    lane = lax.broadcasted_iota(jnp.int32, (NA_QROWS * GRID_W, PAIR), 1)
    for hp in range(NA_HEADS * HEAD_DIM // PAIR):
        heads = (2 * hp, 2 * hp + 1)
        lanes = slice(hp * PAIR, (hp + 1) * PAIR)
        ctx_k = jnp.concatenate([ck_ref[0, 0, h].astype(BF) for h in heads], axis=1)
        ctx_v = jnp.concatenate([cv_ref[0, 0, h].astype(BF) for h in heads], axis=1)

        def body(qi, carry, heads=heads, lanes=lanes, ctx_k=ctx_k, ctx_v=ctx_v):
            r0 = qi * NA_QROWS
            ws = jnp.clip(r0 - NA_ROWS // 2, 0, rows - NA_WIN_ROWS)
            pat = jnp.where(qi == 0, 0, jnp.where(qi == groups - 1, 2, 1))
            rq = pl.ds(pl.multiple_of(r0 * GRID_W, NA_QROWS * GRID_W), NA_QROWS * GRID_W)
            rk = pl.ds(pl.multiple_of(ws * GRID_W, GRID_W), NA_WIN_ROWS * GRID_W)
            q2 = q_ref[0, rq, lanes] * ATTN_SCALE
            k2 = k_ref[0, rk, lanes]
            v2 = v_ref[0, rk, lanes]
            outs = []
            for g, h in enumerate(heads):
                qg = _one_head(q2, lane, g)
                s = _dot_nt(qg, k2) + bt_ref[h, pat]
                s_ctx = _dot_nt(qg, ctx_k)
                m = jnp.maximum(jnp.max(s, axis=-1, keepdims=True), jnp.max(s_ctx, axis=-1, keepdims=True))
                p = jnp.exp(s - m)
                p_ctx = jnp.exp(s_ctx - m)
                den = jnp.sum(p, axis=-1, keepdims=True) + jnp.sum(p_ctx, axis=-1, keepdims=True)
                outs.append((_dot(p.astype(BF), v2) + _dot(p_ctx.astype(BF), ctx_v)) / den)
            o_ref[0, rq, lanes] = jnp.where(lane // HEAD_DIM == 0, outs[0], outs[1]).astype(BF)
            return carry

        lax.fori_loop(0, groups, body, 0)


def _na_attn_call(z, cache_k, cache_v, bias_table, layer):
    B, L, _ = z.shape
    P = cache_k.shape[3]
    assert L // GRID_W >= NA_WIN_ROWS + NA_QROWS and (L // GRID_W) % NA_QROWS == 0
    cspec = pl.BlockSpec((1, 1, NA_HEADS, P, HEAD_DIM), lambda b: (b, layer, 0, 0, 0))
    return pl.pallas_call(
        functools.partial(_na_attn_kernel, L=L),
        out_shape=jax.ShapeDtypeStruct((B, L, BRANCH_W), BF),
        grid=(B,),
        in_specs=[
            _zcol_spec(L, COL_DQ), _zcol_spec(L, COL_DK), _zcol_spec(L, COL_DV), cspec, cspec,
            pl.BlockSpec(bias_table.shape, lambda b: (0, 0, 0, 0), pipeline_mode=pl.Buffered(1)),
        ],
        out_specs=pl.BlockSpec((1, L, BRANCH_W), lambda b: (b, 0, 0)),
        compiler_params=_params("arbitrary"),
        name="neighbourhood_attention",
    )(z, z, z, cache_k, cache_v, bias_table)


def _merge_kernel(x_ref, g_ref, sc_ref, sh_ref, gate_ref, ba_ref, bb_ref, bc_ref, bd_ref,
                  wg0_ref, wg1_ref, wb_ref, wo_ref, o_ref):
    x = x_ref[0]
    h = _norm_mod(x, g_ref[...], sc_ref[0], sh_ref[0]).astype(BF)
    merged = jnp.zeros(x.shape, F32)
    per_ref = MIX_COLS // D_MODEL
    for n, br_ref in enumerate((ba_ref, bb_ref, bc_ref, bd_ref)):
        wg_ref = (wg0_ref, wg1_ref)[n // per_ref]
        c0 = (n % per_ref) * D_MODEL
        gate = jax.nn.sigmoid(_dot(h, wg_ref[0, :, c0:c0 + D_MODEL]))
        merged = merged + gate * _dot(br_ref[0], wb_ref[0, n])
    o_ref[0] = x + gate_ref[0] * _dot(merged.astype(BF), wo_ref[0])


def _merge_call(x, norm_g, mod, branches, w_in_bf, w_branch, w_out, layer, tm):
    B, L, _ = x.shape
    bm = mod.shape[0]
    xspec = pl.BlockSpec((1, tm, D_MODEL), lambda b, i: (b, i, 0))
    bspec = pl.BlockSpec((1, tm, BRANCH_W), lambda b, i: (b, i, 0))
    return pl.pallas_call(
        _merge_kernel,
        out_shape=jax.ShapeDtypeStruct((B, L, D_MODEL), F32),
        grid=(B, L // tm),
        in_specs=[
            xspec,
            pl.BlockSpec((1, D_MODEL), lambda b, i: (0, 0)),
            _mod_spec(bm, 1, 2), _mod_spec(bm, 0, 2), _mod_spec(bm, 2, 2),
            bspec, bspec, bspec, bspec,
            pl.BlockSpec((1, D_MODEL, MIX_COLS), lambda b, i: (layer, 0, 1)),
            pl.BlockSpec((1, D_MODEL, MIX_COLS), lambda b, i: (layer, 0, 2)),
            pl.BlockSpec((1, N_BRANCH, BRANCH_W, D_MODEL), lambda b, i: (layer, 0, 0, 0)),
            pl.BlockSpec((1, D_MODEL, D_MODEL), lambda b, i: (layer, 0, 0)),
        ],
        out_specs=xspec,
        compiler_params=_params("arbitrary", "arbitrary"),
        name="merge_out_proj",
    )(x, norm_g, mod, mod, mod, *branches, w_in_bf, w_in_bf, w_branch, w_out)


ROUTE_CHUNK = 256
GATHER_ROWS = 512
SIGN_BIT = 31
SEARCH_BITS = 4
SEARCH_TOP_SHIFT = ((SIGN_BIT - 1) // SEARCH_BITS) * SEARCH_BITS


def _prefix_excl(mask_ref, out_ref, tri, L):
    totals = [jnp.sum(mask_ref[:, c0:c0 + LANES], axis=-1, keepdims=True) for c0 in range(0, L, LANES)]
    run = jnp.zeros((N_EXPERTS, 1), F32)
    for i, c0 in enumerate(range(0, L, LANES)):
        out_ref[:, c0:c0 + LANES] = _dot(mask_ref[:, c0:c0 + LANES].astype(BF), tri) + run
        run = run + totals[i]


def _route_kernel(x_ref, g_ref, sc_ref, sh_ref, wh_ref, wl_ref, tri_ref,
                  h_ref, gate_ref, slot_ref, slot_t_ref, off_ref, lt_ref, msk_ref, pre_ref, *, L, cap):
    g = g_ref[...]
    sc = sc_ref[0]
    sh = sh_ref[0]
    wh = wh_ref[...]
    wl = wl_ref[...]
    rc = min(L, ROUTE_CHUNK)
    for r0 in range(0, L, rc):
        h = _norm_mod(x_ref[0, r0:r0 + rc, :], g, sc, sh)
        h_hi = h.astype(BF)
        h_ref[0, r0:r0 + rc, :] = h_hi
        h_lo = (h - h_hi.astype(F32)).astype(BF)
        lt_ref[:, r0:r0 + rc] = _dot_nt(wh, h_hi) + (_dot_nt(wh, h_lo) + _dot_nt(wl, h_hi))
    lt = lt_ref[...]
    e = jnp.exp(lt - jnp.max(lt, axis=0, keepdims=True))
    aff = e / jnp.sum(e, axis=0, keepdims=True)
    thr = jnp.zeros((N_EXPERTS, 1), jnp.int32)
    for shift in range(SEARCH_TOP_SHIFT, -1, -SEARCH_BITS):
        n_digits = min(1 << SEARCH_BITS, 1 << (SIGN_BIT - shift))
        digit = jnp.zeros((N_EXPERTS, 1), jnp.int32)
        for d in range(1, n_digits):
            cand = pltpu.bitcast(thr | (d << shift), F32)
            cnt = jnp.sum((aff >= cand).astype(F32), axis=-1, keepdims=True)
            digit = digit + (cnt >= cap).astype(jnp.int32)
        thr = thr | (digit << shift)
    thr_f = pltpu.bitcast(thr, F32)
    gt = aff > thr_f
    eq = aff == thr_f
    need = cap - jnp.sum(gt.astype(F32), axis=-1, keepdims=True)
    tri = tri_ref[...]
    msk_ref[...] = eq.astype(F32)
    _prefix_excl(msk_ref, pre_ref, tri, L)
    sel = gt | (eq & (pre_ref[...] < need))
    msk_ref[...] = sel.astype(F32)
    _prefix_excl(msk_ref, pre_ref, tri, L)
    off_ref[0] = jnp.concatenate([pre_ref[:, c:c + 1] for c in range(0, L, min(L, COMBINE_TL))], axis=1)
    slot = jnp.where(sel, pre_ref[...], -1.0)
    slot_ref[0] = slot
    slot_t_ref[0] = slot.T
    gate_ref[0] = jnp.where(sel, aff, 0.0)


def _route_call(x, norm_g, mod, wr_hi, wr_lo):
    B, L, _ = x.shape
    bm = mod.shape[0]
    cap = EC_CAPACITY * L // N_EXPERTS
    n_tiles = L // min(L, COMBINE_TL)
    tri = jnp.asarray(np.triu(np.ones((LANES, LANES)), 1).astype(ml_dtypes.bfloat16))
    wspec = pl.BlockSpec((N_EXPERTS, D_MODEL), lambda b: (0, 0))
    return pl.pallas_call(
        functools.partial(_route_kernel, L=L, cap=cap),
        out_shape=[
            jax.ShapeDtypeStruct((B, L, D_MODEL), BF),
            jax.ShapeDtypeStruct((B, N_EXPERTS, L), F32),
            jax.ShapeDtypeStruct((B, N_EXPERTS, L), F32),
            jax.ShapeDtypeStruct((B, L, N_EXPERTS), F32),
            jax.ShapeDtypeStruct((B, N_EXPERTS, n_tiles), F32),
        ],
        grid=(B,),
        in_specs=[
            pl.BlockSpec((1, L, D_MODEL), lambda b: (b, 0, 0)),
            pl.BlockSpec((1, D_MODEL), lambda b: (0, 0)),
            _mod_spec(bm, 4, 1), _mod_spec(bm, 3, 1),
            wspec, wspec,
            pl.BlockSpec((LANES, LANES), lambda b: (0, 0)),
        ],
        out_specs=[
            pl.BlockSpec((1, L, D_MODEL), lambda b: (b, 0, 0)),
            pl.BlockSpec((1, N_EXPERTS, L), lambda b: (b, 0, 0)),
            pl.BlockSpec((1, N_EXPERTS, L), lambda b: (b, 0, 0)),
            pl.BlockSpec((1, L, N_EXPERTS), lambda b: (b, 0, 0)),
            pl.BlockSpec((1, N_EXPERTS, n_tiles), lambda b: (b, 0, 0)),
        ],
        scratch_shapes=[
            pltpu.VMEM((N_EXPERTS, L), F32),
            pltpu.VMEM((N_EXPERTS, L), F32),
            pltpu.VMEM((N_EXPERTS, L), F32),
        ],
        compiler_params=_params("arbitrary"),
        name="ec_route",
    )(x, norm_g, mod, mod, wr_hi, wr_lo, tri)


COMBINE_TL = 256
COMBINE_WIN = 64
BF_TILE = 16


def _list_windows(off_ref, cap):
    n_tiles = pl.num_programs(1)
    base = (pl.program_id(0) * n_tiles + pl.program_id(1)) * N_EXPERTS
    starts = []
    fits = None
    for ex in range(N_EXPERTS):
        first = off_ref[base + ex]
        end = jnp.where(pl.program_id(1) == n_tiles - 1, cap, off_ref[base + N_EXPERTS + ex])
        start = jnp.minimum((first // BF_TILE) * BF_TILE, cap - COMBINE_WIN)
        ok = end <= start + COMBINE_WIN
        fits = ok if fits is None else (fits & ok)
        starts.append(start)
    return starts, fits


def _gather_kernel(off_ref, slot_ref, gate_ref, h_ref, xs_ref, gs_ref, *, cap):
    h = h_ref[0]
    tl = h.shape[0]

    @pl.when(pl.program_id(1) == 0)
    def _():
        xs_ref[...] = jnp.zeros(xs_ref.shape, BF)
        gs_ref[...] = jnp.zeros(gs_ref.shape, F32)

    def select(ex, first, rows):
        pos = lax.broadcasted_iota(jnp.int32, (rows, tl), 0).astype(F32)
        hit = slot_ref[0, ex:ex + 1, :] - first == pos
        gate = jnp.sum(jnp.where(hit, gate_ref[0, ex:ex + 1, :], 0.0), axis=-1, keepdims=True)
        return jnp.where(hit, 1.0, 0.0).astype(BF), jnp.broadcast_to(gate, (rows, LANES))

    def gather_all():
        ge = max(1, min(N_EXPERTS, GATHER_ROWS // cap))
        for e0 in range(0, N_EXPERTS, ge):
            picked = [select(ex, 0.0, cap) for ex in range(e0, e0 + ge)]
            rows = _dot(jnp.concatenate([p[0] for p in picked], axis=0), h).astype(BF)
            for j in range(ge):
                xs_ref[e0 + j, 0] += rows[j * cap:(j + 1) * cap]
                gs_ref[e0 + j, 0] += picked[j][1]

    if cap <= COMBINE_WIN:
        gather_all()
        return

    starts, fits = _list_windows(off_ref, cap)

    @pl.when(fits)
    def _():
        picked = [select(ex, starts[ex].astype(F32), COMBINE_WIN) for ex in range(N_EXPERTS)]
        rows = _dot(jnp.concatenate([p[0] for p in picked], axis=0), h).astype(BF)
        for ex in range(N_EXPERTS):
            win = pl.ds(pl.multiple_of(starts[ex], BF_TILE), COMBINE_WIN)
            xs_ref[ex, 0, win, :] += rows[ex * COMBINE_WIN:(ex + 1) * COMBINE_WIN]
            gs_ref[ex, 0, win, :] += picked[ex][1]

    @pl.when(jnp.logical_not(fits))
    def _():
        gather_all()


def _gather_call(offsets, slot, gate_rows, h2):
    B, L, _ = h2.shape
    cap = EC_CAPACITY * L // N_EXPERTS
    tl = min(L, COMBINE_TL)
    assert cap <= COMBINE_WIN or (cap % BF_TILE == 0 and COMBINE_WIN % BF_TILE == 0)
    return pl.pallas_call(
        functools.partial(_gather_kernel, cap=cap),
        out_shape=[
            jax.ShapeDtypeStruct((N_EXPERTS, B, cap, D_MODEL), BF),
            jax.ShapeDtypeStruct((N_EXPERTS, B, cap, LANES), F32),
        ],
        grid_spec=pltpu.PrefetchScalarGridSpec(
            num_scalar_prefetch=1,
            grid=(B, L // tl),
            in_specs=[
                pl.BlockSpec((1, N_EXPERTS, tl), lambda b, i, off: (b, 0, i)),
                pl.BlockSpec((1, N_EXPERTS, tl), lambda b, i, off: (b, 0, i)),
                pl.BlockSpec((1, tl, D_MODEL), lambda b, i, off: (b, i, 0)),
            ],
            out_specs=[
                pl.BlockSpec((N_EXPERTS, 1, cap, D_MODEL), lambda b, i, off: (0, b, 0, 0)),
                pl.BlockSpec((N_EXPERTS, 1, cap, LANES), lambda b, i, off: (0, b, 0, 0)),
            ],
        ),
        compiler_params=_params("arbitrary", "arbitrary"),
        name="ec_gather",
    )(offsets, slot, gate_rows, h2)


EXPERT_TR = 512


def _expert_kernel(xa_ref, ga_ref, xb_ref, gb_ref, wg_ref, wu_ref, wd_ref, ya_ref, yb_ref,
                   wgb_ref, wub_ref, wdb_ref, *, tiles_a):
    j = pl.program_id(1)

    @pl.when(j == 0)
    def _():
        wgb_ref[...] = wg_ref[0, 0].astype(BF)
        wub_ref[...] = wu_ref[0, 0].astype(BF)
        wdb_ref[...] = wd_ref[0, 0].astype(BF)

    def ffn(x_ref, g_ref, y_ref):
        x = x_ref[0]
        a = _dot(x, wgb_ref[...])
        u = _dot(x, wub_ref[...])
        y = _dot((_silu(a) * u).astype(BF), wdb_ref[...])
        y_ref[0] = (y * g_ref[0, :, 0:1]).astype(BF)

    @pl.when(j < tiles_a)
    def _():
        ffn(xa_ref, ga_ref, ya_ref)

    @pl.when(j >= tiles_a)
    def _():
        ffn(xb_ref, gb_ref, yb_ref)


def _expert_call(xs_a, gs_a, xs_b, gs_b, w_gate, w_up, w_down, layer):
    E, ra, _ = xs_a.shape
    rb = xs_b.shape[1]
    tr = math.gcd(math.gcd(ra, rb), EXPERT_TR)
    tiles_a, tiles_b = ra // tr, rb // tr
    amap = lambda e, j: (e, jnp.minimum(j, tiles_a - 1), 0)
    bmap = lambda e, j: (e, jnp.maximum(j - tiles_a, 0), 0)
    wspec_in = pl.BlockSpec((1, 1, D_MODEL, EXPERT_FF), lambda e, j: (layer, e, 0, 0))
    wspec_out = pl.BlockSpec((1, 1, EXPERT_FF, D_MODEL), lambda e, j: (layer, e, 0, 0))
    return pl.pallas_call(
        functools.partial(_expert_kernel, tiles_a=tiles_a),
        out_shape=[jax.ShapeDtypeStruct((E, ra, D_MODEL), BF), jax.ShapeDtypeStruct((E, rb, D_MODEL), BF)],
        grid=(E, tiles_a + tiles_b),
        in_specs=[
            pl.BlockSpec((1, tr, D_MODEL), amap), pl.BlockSpec((1, tr, LANES), amap),
            pl.BlockSpec((1, tr, D_MODEL), bmap), pl.BlockSpec((1, tr, LANES), bmap),
            wspec_in, wspec_in, wspec_out,
        ],
        out_specs=[pl.BlockSpec((1, tr, D_MODEL), amap), pl.BlockSpec((1, tr, D_MODEL), bmap)],
        scratch_shapes=[
            pltpu.VMEM((D_MODEL, EXPERT_FF), BF),
            pltpu.VMEM((D_MODEL, EXPERT_FF), BF),
            pltpu.VMEM((EXPERT_FF, D_MODEL), BF),
        ],
        compiler_params=_params("arbitrary", "arbitrary"),
        name="ec_experts",
    )(xs_a, gs_a, xs_b, gs_b, w_gate, w_up, w_down)


def _combine_kernel(off_ref, st_ref, y_ref, x_ref, gate_ref, fg_ref, o_ref, *, cap, final):
    st = st_ref[0]
    tl = st.shape[0]

    def finish(scattered):
        v = x_ref[0] + gate_ref[0] * scattered
        if final:
            v = v * lax.rsqrt(jnp.mean(v * v, axis=-1, keepdims=True) + EPS) * fg_ref[...]
        o_ref[0] = v

    def onehot(width, starts):
        n = N_EXPERTS * width
        owner = lax.broadcasted_iota(jnp.int32, (N_EXPERTS, n), 1) // width
        spread = jnp.where(owner == lax.broadcasted_iota(jnp.int32, (N_EXPERTS, n), 0), 1.0, 0.0).astype(BF)
        slot = _dot(st.astype(BF), spread)
        lane = lax.broadcasted_iota(jnp.int32, (1, n), 1)
        want = (lane % width).astype(F32)
        if starts is not None:
            for ex in range(N_EXPERTS):
                want = want + jnp.where(lane // width == ex, starts[ex].astype(F32), 0.0)
        return jnp.where(slot == want, 1.0, 0.0).astype(BF)

    def scatter_all():
        finish(_dot(onehot(cap, None), y_ref[:, 0].reshape(N_EXPERTS * cap, D_MODEL)))

    if cap <= COMBINE_WIN:
        scatter_all()
        return

    starts, fits = _list_windows(off_ref, cap)

    @pl.when(fits)
    def _():
        rows = [y_ref[ex, 0, pl.ds(pl.multiple_of(starts[ex], BF_TILE), COMBINE_WIN), :]
                for ex in range(N_EXPERTS)]
        finish(_dot(onehot(COMBINE_WIN, starts), jnp.concatenate(rows, axis=0)))

    @pl.when(jnp.logical_not(fits))
    def _():
        scatter_all()


def _combine_call(slot_t, offsets, y, x, mod, final_g, final):
    B, L, _ = x.shape
    bm = mod.shape[0]
    cap = y.shape[2]
    tl = min(L, COMBINE_TL)
    assert cap <= COMBINE_WIN or (cap % BF_TILE == 0 and COMBINE_WIN % BF_TILE == 0)
    assert cap <= 256
    gate_map =(lambda b, i, off: (b, 0, 5)) if bm > 1 else (lambda b, i, off: (0, 0, 5))
    return pl.pallas_call(
        functools.partial(_combine_kernel, cap=cap, final=final),
        out_shape=jax.ShapeDtypeStruct((B, L, D_MODEL), F32),
        grid_spec=pltpu.PrefetchScalarGridSpec(
            num_scalar_prefetch=1,
            grid=(B, L // tl),
            in_specs=[
                pl.BlockSpec((1, tl, N_EXPERTS), lambda b, i, off: (b, i, 0)),
                pl.BlockSpec((N_EXPERTS, 1, cap, D_MODEL), lambda b, i, off: (0, b, 0, 0)),
                pl.BlockSpec((1, tl, D_MODEL), lambda b, i, off: (b, i, 0)),
                pl.BlockSpec((1, 1, D_MODEL), gate_map),
                pl.BlockSpec((1, D_MODEL), lambda b, i, off: (0, 0)),
            ],
            out_specs=pl.BlockSpec((1, tl, D_MODEL), lambda b, i, off: (b, i, 0)),
        ),
        compiler_params=_params("arbitrary", "arbitrary"),
        name="ec_combine",
    )(offsets, slot_t, y, x, mod, final_g)


def _mixer_half(x, mod, lw, ctx, tm, kv_prev=None):
    latent = ctx is not None
    layer = lw["layer"]
    res = _inproj_call(x, lw["norm1_g"], mod, lw["w_in"], layer, tm, rope=latent, want_kv=not latent,
                       kv_prev=kv_prev)
    z = res[0]
    br_a = _conv_call(z, lw["conv_w"], lw["conv_b"], lw["conv_ln_g"], lw["conv_ln_b"])
    br_b = _fourier_call(z, *lw["dft"][z.shape[1]])
    if latent:
        br_c = _win_attn_call(z, ctx["win_k"], ctx["win_v"], lw["win_sink"], layer)
        br_d = _na_attn_call(z, ctx["nat_k"], ctx["nat_v"], lw["bias_table"], layer)
    else:
        br_c, br_d = _ctx_attn_call(z, lw["win_sink"])
    x = _merge_call(x, lw["norm1_g"], mod, (br_a, br_b, br_c, br_d),
                    lw["w_in"], lw["w_branch"], lw["w_out"], layer, tm)
    return x, (tuple(res[1:]) if not latent else None)


def kernel(x_prompt, x_sample, cache_win_k, cache_win_v, cache_nat_k, cache_nat_v, c, c_ctx,
           norm1_g, norm2_g, w_ada, b_ada, w_in, conv_w, conv_b, conv_ln_g, conv_ln_b,
           win_sink, na_bias, w_branch, w_out, w_router, w_e_gate, w_e_up, w_e_down, final_norm_g):
    n_dec = c.shape[0]
    cond = jnp.zeros((ADA_ROWS, D_MODEL), F32).at[0].set(c_ctx).at[1:1 + n_dec].set(c)
    mods = _ada_call(cond, w_ada, b_ada)
    final_g = final_norm_g.reshape(1, D_MODEL)
    w_in_bf = _cast_call(w_in)
    w_branch_bf = _cast_call(w_branch)
    w_out_bf = _cast_call(w_out)
    dft = {L: _dft_tables(L) for L in (x_prompt.shape[1], x_sample.shape[1])}
    ctx = {"win_k": cache_win_k, "win_v": cache_win_v, "nat_k": cache_nat_k, "nat_v": cache_nat_v}

    xp, xs = x_prompt, x_sample
    caches = None
    for l in range(DEPTH):
        wr_t = w_router[l].T
        wr_hi = wr_t.astype(BF)
        wr_lo = (wr_t - wr_hi.astype(F32)).astype(BF)
        lw = {
            "layer": l,
            "norm1_g": norm1_g[l].reshape(1, D_MODEL),
            "w_in": w_in_bf, "w_branch": w_branch_bf, "w_out": w_out_bf, "dft": dft,
            "conv_w": conv_w[l], "conv_b": conv_b[l],
            "conv_ln_g": conv_ln_g[l], "conv_ln_b": conv_ln_b[l],
            "win_sink": win_sink[l],
            "bias_table": _na_bias_call(na_bias[l], x_sample.shape[1] // GRID_W),
        }
        norm2 = norm2_g[l].reshape(1, D_MODEL)
        final = l == DEPTH - 1
        mod_p = mods[l, 0:1].reshape(1, 1, 6 * D_MODEL)
        mod_s = mods[l, 1:1 + n_dec].reshape(n_dec, 1, 6 * D_MODEL)
        xp, caches = _mixer_half(xp, mod_p, lw, None, tm=256, kv_prev=caches)
        xs, _ = _mixer_half(xs, mod_s, lw, ctx, tm=512)
        smem = lambda off: jnp.pad(off.transpose(0, 2, 1).astype(jnp.int32).reshape(-1), (0, N_EXPERTS))
        h_p, g_p, slot_p, slot_t_p, off_p = _route_call(xp, norm2, mod_p, wr_hi, wr_lo)
        h_s, g_s, slot_s, slot_t_s, off_s = _route_call(xs, norm2, mod_s, wr_hi, wr_lo)
        off_p, off_s = smem(off_p), smem(off_s)
        rows_p, gates_p = _gather_call(off_p, slot_p, g_p, h_p)
        rows_s, gates_s = _gather_call(off_s, slot_s, g_s, h_s)
        flat = lambda a: a.reshape(N_EXPERTS, a.shape[1] * a.shape[2], a.shape[3])
        y_p, y_s = _expert_call(flat(rows_p), flat(gates_p), flat(rows_s), flat(gates_s),
                                w_e_gate, w_e_up, w_e_down, l)
        xp = _combine_call(slot_t_p, off_p, y_p.reshape(rows_p.shape), xp, mod_p, final_g, final)
        xs = _combine_call(slot_t_s, off_s, y_s.reshape(rows_s.shape), xs, mod_s, final_g, final)

    return (xp, xs) + caches
```

```python
import functools
import math

import ml_dtypes
import numpy as np
import jax
import jax.numpy as jnp
from jax import lax
from jax.experimental import pallas as pl
from jax.experimental.pallas import tpu as pltpu

D_MODEL = 1024
DEPTH = 2
GRID_W = 64
BRANCH_W = 256
HEAD_DIM = 64
N_BRANCH = 4
CONV_K = 31
FNET_GROUPS = 4
FNET_GDIM = BRANCH_W // FNET_GROUPS
WIN_Q_HEADS = 4
WIN_KV_HEADS = 2
WIN_BLOCK = 128
NA_HEADS = 4
NA_ROWS = 8
NA_COLS = 16
ROPE_THETA = 10000.0
ATTN_SCALE = HEAD_DIM ** -0.5
N_EXPERTS = 16
EXPERT_FF = 1024
EC_CAPACITY = 2
EPS = 1e-6
NEG = -1e30

MIX_COLS = 2048
GATE_COLS = N_BRANCH * D_MODEL
IN_COLS = MIX_COLS + GATE_COLS
COL_A, COL_F, COL_CQ, COL_CKV, COL_DQ, COL_DK, COL_DV = 0, 512, 768, 1024, 1280, 1536, 1792
ROPE_LO, ROPE_HI = COL_CQ, COL_CKV + WIN_KV_HEADS * HEAD_DIM
KV_COLS = (COL_CKV, COL_CKV + WIN_KV_HEADS * HEAD_DIM, COL_DK, COL_DV)
KV_HEADS = (WIN_KV_HEADS, WIN_KV_HEADS, NA_HEADS, NA_HEADS)

LANES = 128
SUBLANES = 8
VMEM_LIMIT = 56 * 1024 * 1024

BF = jnp.bfloat16
F32 = jnp.float32


def _params(*sem):
    return pltpu.CompilerParams(dimension_semantics=sem, vmem_limit_bytes=VMEM_LIMIT)


def _dot(a, b):
    return jnp.dot(a, b, preferred_element_type=F32)


def _dot_nt(a, b):
    return lax.dot_general(a, b, (((1,), (1,)), ((), ())), preferred_element_type=F32)


def _norm_mod(x, g, sc, sh):
    y = x * lax.rsqrt(jnp.mean(x * x, axis=-1, keepdims=True) + EPS)
    return (y * g) * (1.0 + sc) + sh


def _silu(x):
    return x * jax.nn.sigmoid(x)


CAST_BYTES = 4 * 1024 * 1024


def _cast_kernel(x_ref, o_ref):
    o_ref[...] = x_ref[...].astype(BF)


def _cast_call(x):
    shape = x.shape
    x2 = x.reshape(-1, shape[-1])
    rows, cols = x2.shape
    tr = min(rows, 1 << (max(16, CAST_BYTES // (4 * cols)).bit_length() - 1))
    assert rows % tr == 0
    out = pl.pallas_call(
        _cast_kernel,
        out_shape=jax.ShapeDtypeStruct((rows, cols), BF),
        grid=(rows // tr,),
        in_specs=[pl.BlockSpec((tr, cols), lambda i: (i, 0))],
        out_specs=pl.BlockSpec((tr, cols), lambda i: (i, 0)),
        compiler_params=_params("arbitrary"),
        name="cast_bf16",
    )(x2)
    return out.reshape(shape)


ADA_ROWS = 16
ADA_TN = 1536


def _ada_kernel(c_ref, w_ref, b_ref, o_ref):
    s = _silu(c_ref[...]).astype(BF)
    o_ref[0] = _dot(s, w_ref[0].astype(BF)) + b_ref[0]


def _ada_call(cond, w_ada, b_ada):
    n = 6 * D_MODEL
    return pl.pallas_call(
        _ada_kernel,
        out_shape=jax.ShapeDtypeStruct((DEPTH, ADA_ROWS, n), F32),
        grid=(DEPTH, n // ADA_TN),
        in_specs=[
            pl.BlockSpec((ADA_ROWS, D_MODEL), lambda l, j: (0, 0)),
            pl.BlockSpec((1, D_MODEL, ADA_TN), lambda l, j: (l, 0, j)),
            pl.BlockSpec((1, 1, ADA_TN), lambda l, j: (l, 0, j)),
        ],
        out_specs=pl.BlockSpec((1, ADA_ROWS, ADA_TN), lambda l, j: (l, 0, j)),
        compiler_params=_params("arbitrary", "arbitrary"),
        name="ada_mod",
    )(cond, w_ada, b_ada.reshape(DEPTH, 1, n))


def _mod_spec(bm, chunk, ngrid):
    if ngrid == 1:
        imap = (lambda b: (b, 0, chunk)) if bm > 1 else (lambda b: (0, 0, chunk))
    else:
        imap = (lambda b, i: (b, 0, chunk)) if bm > 1 else (lambda b, i: (0, 0, chunk))
    return pl.BlockSpec((1, 1, D_MODEL), imap)


def _inproj_kernel(*refs, rope, kv_layer, kv_aliased):
    x_ref, g_ref, sc_ref, sh_ref, w_ref = refs[:5]
    pos = 5
    if rope:
        cos_ref, sin_ref = refs[pos:pos + 2]
        pos += 2
    if kv_aliased:
        pos += len(KV_HEADS)
    z_ref = refs[pos]
    h = _norm_mod(x_ref[0], g_ref[...], sc_ref[0], sh_ref[0]).astype(BF)
    z = _dot(h, w_ref[0])
    if kv_layer is not None:
        for kv_ref, c0, heads in zip(refs[pos + 1:pos + 1 + len(KV_HEADS)], KV_COLS, KV_HEADS):
            here = 0 if kv_aliased else kv_layer
            for hd in range(heads):
                kv_ref[0, here, hd] = z[:, c0 + hd * HEAD_DIM:c0 + (hd + 1) * HEAD_DIM].T
            for other in range(kv_ref.shape[1]):
                if other != here:
                    kv_ref[0, other] = jnp.zeros(kv_ref.shape[2:], F32)
    if rope:
        z_ref[0, :, 0:ROPE_LO] = z[:, 0:ROPE_LO].astype(BF)
        lane = lax.broadcasted_iota(jnp.int32, (z.shape[0], LANES), 1)
        first_half = (lane % 32) < 16
        cos = cos_ref[...]
        sin = sin_ref[...]
        for c0 in range(ROPE_LO, ROPE_HI, LANES):
            xc = z[:, c0:c0 + LANES]
            partner = jnp.where(first_half, pltpu.roll(xc, LANES - 16, 1), pltpu.roll(xc, 16, 1))
            z_ref[0, :, c0:c0 + LANES] = (xc * cos + partner * sin).astype(BF)
        z_ref[0, :, ROPE_HI:MIX_COLS] = z[:, ROPE_HI:MIX_COLS].astype(BF)
    else:
        z_ref[0] = z.astype(BF)


def _rope_tables(L):
    t = np.arange(L)
    half = HEAD_DIM // 4
    freqs = 1.0 / (ROPE_THETA ** (np.arange(half, dtype=np.float64) / half))
    ang_r = (t // GRID_W).astype(np.float64)[:, None] * freqs[None, :]
    ang_c = (t % GRID_W).astype(np.float64)[:, None] * freqs[None, :]
    cos_h = np.concatenate([np.cos(ang_r), np.cos(ang_r), np.cos(ang_c), np.cos(ang_c)], axis=1)
    sin_h = np.concatenate([-np.sin(ang_r), np.sin(ang_r), -np.sin(ang_c), np.sin(ang_c)], axis=1)
    reps = LANES // HEAD_DIM
    return (jnp.asarray(np.tile(cos_h, (1, reps)), F32), jnp.asarray(np.tile(sin_h, (1, reps)), F32))


def _inproj_call(x, norm_g, mod, w_in_bf, layer, tm, rope, want_kv, kv_prev=None):
    B, L, _ = x.shape
    bm = mod.shape[0]
    in_specs = [
        pl.BlockSpec((1, tm, D_MODEL), lambda b, i: (b, i, 0)),
        pl.BlockSpec((1, D_MODEL), lambda b, i: (0, 0)),
        _mod_spec(bm, 1, 2),
        _mod_spec(bm, 0, 2),
        pl.BlockSpec((1, D_MODEL, MIX_COLS), lambda b, i: (layer, 0, 0)),
    ]
    args = [x, norm_g, mod, mod, w_in_bf]
    if rope:
        cos, sin = _rope_tables(L)
        in_specs += [pl.BlockSpec((tm, LANES), lambda b, i: (i, 0))] * 2
        args += [cos, sin]
    out_shape = [jax.ShapeDtypeStruct((B, L, MIX_COLS), BF)]
    out_specs = [pl.BlockSpec((1, tm, MIX_COLS), lambda b, i: (b, i, 0))]
    aliases = {}
    if want_kv:
        for n, heads in enumerate(KV_HEADS):
            out_shape.append(jax.ShapeDtypeStruct((B, DEPTH, heads, HEAD_DIM, L), F32))
            if kv_prev is None:
                out_specs.append(pl.BlockSpec((1, DEPTH, heads, HEAD_DIM, tm), lambda b, i: (b, 0, 0, 0, i)))
            else:
                out_specs.append(pl.BlockSpec((1, 1, heads, HEAD_DIM, tm), lambda b, i: (b, layer, 0, 0, i)))
                aliases[len(args)] = 1 + n
                in_specs.append(pl.BlockSpec(memory_space=pl.ANY))
                args.append(kv_prev[n])
    return pl.pallas_call(
        functools.partial(_inproj_kernel, rope=rope, kv_layer=layer if want_kv else None,
                          kv_aliased=kv_prev is not None),
        out_shape=out_shape,
        grid=(B, L // tm),
        in_specs=in_specs,
        out_specs=out_specs,
        input_output_aliases=aliases,
        compiler_params=_params("arbitrary", "arbitrary"),
        name="in_proj",
    )(*args)


CONV_PAD = 16
CONV_TR = 64
CONV_CHUNK = 256


def _conv_kernel(a_ref, w_ref, cb_ref, lg_ref, lb_ref, o_ref, p_ref, u_ref, wb_ref, *, L):
    for k in range(CONV_K):
        wb_ref[k] = jnp.broadcast_to(w_ref[k:k + 1, :], (SUBLANES, BRANCH_W))
    zeros = jnp.zeros((CONV_PAD, BRANCH_W), F32)
    p_ref[0:CONV_PAD, :] = zeros
    p_ref[L + CONV_PAD:L + 2 * CONV_PAD, :] = zeros
    for r0 in range(0, L, CONV_CHUNK):
        a = a_ref[0, r0:r0 + CONV_CHUNK, :].astype(F32)
        p_ref[CONV_PAD + r0:CONV_PAD + r0 + CONV_CHUNK, :] = a[:, :BRANCH_W] * jax.nn.sigmoid(a[:, BRANCH_W:])
    n_u = L + 2 * CONV_PAD - SUBLANES
    for s in range(SUBLANES):
        u_ref[s] = p_ref[s:s + n_u, :]
    cb = cb_ref[...]

    def body(i, carry):
        base = pl.multiple_of(i * CONV_TR, CONV_TR)
        groups = (CONV_TR // SUBLANES, SUBLANES, BRANCH_W)
        acc = jnp.zeros(groups, F32) + cb
        for k in range(CONV_K):
            q, s = divmod(k + CONV_PAD - CONV_K // 2, SUBLANES)
            u = u_ref[s, pl.ds(base + SUBLANES * q, CONV_TR), :]
            acc = acc + wb_ref[k] * u.reshape(groups)
        p_ref[pl.ds(base, CONV_TR), :] = acc.reshape(CONV_TR, BRANCH_W)
        return carry

    lax.fori_loop(0, L // CONV_TR, body, 0)
    lg = lg_ref[...]
    lb = lb_ref[...]
    for r0 in range(0, L, CONV_CHUNK):
        acc = p_ref[r0:r0 + CONV_CHUNK, :]
        mu = jnp.mean(acc, axis=-1, keepdims=True)
        d = acc - mu
        var = jnp.mean(d * d, axis=-1, keepdims=True)
        y = d * lax.rsqrt(var + EPS) * lg + lb
        o_ref[0, r0:r0 + CONV_CHUNK, :] = _silu(y).astype(BF)


def _conv_call(z, conv_w, conv_b, ln_g, ln_b):
    B, L, _ = z.shape
    n_u = L + 2 * CONV_PAD - SUBLANES
    vec = lambda b: (0, 0)
    return pl.pallas_call(
        functools.partial(_conv_kernel, L=L),
        out_shape=jax.ShapeDtypeStruct((B, L, BRANCH_W), BF),
        grid=(B,),
        in_specs=[
            pl.BlockSpec((1, L, 2 * BRANCH_W), lambda b: (b, 0, COL_A // (2 * BRANCH_W))),
            pl.BlockSpec((CONV_K, BRANCH_W), vec),
            pl.BlockSpec((1, BRANCH_W), vec),
            pl.BlockSpec((1, BRANCH_W), vec),
            pl.BlockSpec((1, BRANCH_W), vec),
        ],
        out_specs=pl.BlockSpec((1, L, BRANCH_W), lambda b: (b, 0, 0)),
        scratch_shapes=[
            pltpu.VMEM((L + 2 * CONV_PAD, BRANCH_W), F32),
            pltpu.VMEM((SUBLANES, n_u, BRANCH_W), F32),
            pltpu.VMEM((CONV_K, SUBLANES, BRANCH_W), F32),
        ],
        compiler_params=_params("arbitrary"),
        name="conv_module",
    )(z, conv_w, conv_b.reshape(1, -1), ln_g.reshape(1, -1), ln_b.reshape(1, -1))


def _dft_tables(L):
    c = np.arange(FNET_GDIM)
    ang_c = 2.0 * np.pi * ((c[:, None] * c[None, :]) % FNET_GDIM) / FNET_GDIM
    cc = np.zeros((BRANCH_W, BRANCH_W))
    sc = np.zeros((BRANCH_W, BRANCH_W))
    for g in range(FNET_GROUPS):
        sl = slice(g * FNET_GDIM, (g + 1) * FNET_GDIM)
        cc[sl, sl] = np.cos(ang_c)
        sc[sl, sl] = np.sin(ang_c)
    ccs = np.concatenate([cc, sc], axis=1)
    t = np.arange(L)
    ang_l = 2.0 * np.pi * ((t[:, None] * t[None, :]) % L) / L
    csl = np.concatenate([np.cos(ang_l), -np.sin(ang_l)], axis=1)
    return _cast_call(jnp.asarray(ccs, F32)), _cast_call(jnp.asarray(csl, F32))


FOURIER_TR = 512


def _fourier_kernel(f_ref, ccs_ref, csl_ref, o_ref, pq_ref, *, L):
    scale = float(1.0 / np.sqrt(L * FNET_GDIM))
    tr = min(L, FOURIER_TR)
    for r0 in range(0, L, tr):
        pq = _dot(f_ref[0, r0:r0 + tr, :], ccs_ref[...])
        pq_ref[r0:r0 + tr, :] = pq[:, :BRANCH_W].astype(BF)
        pq_ref[L + r0:L + r0 + tr, :] = pq[:, BRANCH_W:].astype(BF)
    for r0 in range(0, L, tr):
        o = _dot(csl_ref[r0:r0 + tr, :], pq_ref[...])
        o_ref[0, r0:r0 + tr, :] = (o * scale).astype(BF)


def _fourier_call(z, ccs, csl):
    B, L, _ = z.shape
    return pl.pallas_call(
        functools.partial(_fourier_kernel, L=L),
        out_shape=jax.ShapeDtypeStruct((B, L, BRANCH_W), BF),
        grid=(B,),
        in_specs=[
            pl.BlockSpec((1, L, BRANCH_W), lambda b: (b, 0, COL_F // BRANCH_W)),
            pl.BlockSpec((BRANCH_W, 2 * BRANCH_W), lambda b: (0, 0)),
            pl.BlockSpec((L, 2 * L), lambda b: (0, 0), pipeline_mode=pl.Buffered(1)),
        ],
        out_specs=pl.BlockSpec((1, L, BRANCH_W), lambda b: (b, 0, 0)),
        scratch_shapes=[pltpu.VMEM((2 * L, BRANCH_W), BF)],
        compiler_params=_params("arbitrary"),
        name="fourier_mix",
    )(z, ccs, csl)


def _head(ref, rows, h):
    return ref[0, rows, h * HEAD_DIM:(h + 1) * HEAD_DIM]


def _ctx_attn_kernel(sink_ref, qc_ref, kvc_ref, qd_ref, kd_ref, vd_ref, oc_ref, od_ref):
    rows = slice(None)
    group = WIN_Q_HEADS // WIN_KV_HEADS
    for h in range(WIN_Q_HEADS):
        kvh = h // group
        s = _dot_nt(_head(qc_ref, rows, h), _head(kvc_ref, rows, kvh)) * ATTN_SCALE
        sink = sink_ref[h]
        m = jnp.maximum(jnp.max(s, axis=-1, keepdims=True), sink)
        p = jnp.exp(s - m)
        den = jnp.sum(p, axis=-1, keepdims=True) + jnp.exp(sink - m)
        o = _dot(p.astype(BF), _head(kvc_ref, rows, WIN_KV_HEADS + kvh)) / den
        oc_ref[0, :, h * HEAD_DIM:(h + 1) * HEAD_DIM] = o.astype(BF)
    for h in range(NA_HEADS):
        s = _dot_nt(_head(qd_ref, rows, h), _head(kd_ref, rows, h)) * ATTN_SCALE
        m = jnp.max(s, axis=-1, keepdims=True)
        p = jnp.exp(s - m)
        den = jnp.sum(p, axis=-1, keepdims=True)
        o = _dot(p.astype(BF), _head(vd_ref, rows, h)) / den
        od_ref[0, :, h * HEAD_DIM:(h + 1) * HEAD_DIM] = o.astype(BF)


def _zcol_spec(L, col):
    return pl.BlockSpec((1, L, BRANCH_W), lambda b: (b, 0, col // BRANCH_W))


def _ctx_attn_call(z, sink):
    B, L, _ = z.shape
    out = jax.ShapeDtypeStruct((B, L, BRANCH_W), BF)
    ospec = pl.BlockSpec((1, L, BRANCH_W), lambda b: (b, 0, 0))
    return pl.pallas_call(
        _ctx_attn_kernel,
        out_shape=[out, out],
        grid=(B,),
        in_specs=[
            pl.BlockSpec(memory_space=pltpu.SMEM),
            _zcol_spec(L, COL_CQ), _zcol_spec(L, COL_CKV),
            _zcol_spec(L, COL_DQ), _zcol_spec(L, COL_DK), _zcol_spec(L, COL_DV),
        ],
        out_specs=[ospec, ospec],
        compiler_params=_params("arbitrary"),
        name="ctx_attention",
    )(sink, z, z, z, z, z)


PAIR = 2 * HEAD_DIM


def _one_head(x, lane, g):
    return jnp.where(lane // HEAD_DIM == g, x, jnp.zeros_like(x))


def _win_attn_kernel(sink_ref, q_ref, kv_ref, ck_ref, cv_ref, o_ref, *, L):
    nb = L // WIN_BLOCK
    group = WIN_Q_HEADS // WIN_KV_HEADS
    m_rows = group * WIN_BLOCK
    assert WIN_BLOCK == LANES
    row = lax.broadcasted_iota(jnp.int32, (m_rows, WIN_BLOCK), 0)
    a = row % WIN_BLOCK
    j = lax.broadcasted_iota(jnp.int32, (m_rows, WIN_BLOCK), 1)
    row1 = lax.broadcasted_iota(jnp.int32, (m_rows, 1), 0)
    for kvh in range(WIN_KV_HEADS):
        ctx_k = ck_ref[0, 0, kvh].astype(BF)
        ctx_v = cv_ref[0, 0, kvh].astype(BF)
        sink = jnp.zeros((m_rows, 1), F32)
        for g in range(group):
            sink = jnp.where(row1 // WIN_BLOCK == g, sink_ref[kvh * group + g], sink)

        def body(n, carry, kvh=kvh, ctx_k=ctx_k, ctx_v=ctx_v, sink=sink):
            r0 = pl.multiple_of(n * WIN_BLOCK, WIN_BLOCK)
            q = jnp.concatenate(
                [_head(q_ref, pl.ds(r0, WIN_BLOCK), kvh * group + g) for g in range(group)], axis=0)
            q = q * ATTN_SCALE
            s_ctx = _dot(q, ctx_k)
            m_el = jnp.maximum(s_ctx[:, :LANES], s_ctx[:, LANES:])
            scores = []
            vals = []
            for dj in (-1, 0, 1):
                nk = jnp.clip(n + dj, 0, nb - 1)
                rk = pl.multiple_of(nk * WIN_BLOCK, WIN_BLOCK)
                s = _dot_nt(q, _head(kv_ref, pl.ds(rk, WIN_BLOCK), kvh))
                if dj == -1:
                    s = jnp.where((j >= a) & (n >= 1), s, NEG)
                elif dj == 1:
                    s = jnp.where((j <= a) & (n <= nb - 2), s, NEG)
                m_el = jnp.maximum(m_el, s)
                scores.append(s)
                vals.append(_head(kv_ref, pl.ds(rk, WIN_BLOCK), WIN_KV_HEADS + kvh))
            m = jnp.maximum(jnp.max(m_el, axis=-1, keepdims=True), sink)
            p_ctx = jnp.exp(s_ctx - m)
            den_el = p_ctx[:, :LANES] + p_ctx[:, LANES:]
            o = _dot_nt(p_ctx.astype(BF), ctx_v)
            for s, v in zip(scores, vals):
                p = jnp.exp(s - m)
                den_el = den_el + p
                o = o + _dot(p.astype(BF), v)
            den = jnp.sum(den_el, axis=-1, keepdims=True) + jnp.exp(sink - m)
            o = (o / den).astype(BF)
            for g in range(group):
                h = kvh * group + g
                o_ref[0, pl.ds(r0, WIN_BLOCK), h * HEAD_DIM:(h + 1) * HEAD_DIM] = (
                    o[g * WIN_BLOCK:(g + 1) * WIN_BLOCK])
            return carry

        lax.fori_loop(0, nb, body, 0, unroll=2)


def _win_attn_call(z, cache_k, cache_v, sink, layer):
    B, L, _ = z.shape
    P = cache_k.shape[4]
    assert P == 2 * LANES
    cspec = pl.BlockSpec((1, 1, WIN_KV_HEADS, HEAD_DIM, P), lambda b: (b, layer, 0, 0, 0))
    return pl.pallas_call(
        functools.partial(_win_attn_kernel, L=L),
        out_shape=jax.ShapeDtypeStruct((B, L, BRANCH_W), BF),
        grid=(B,),
        in_specs=[
            pl.BlockSpec(memory_space=pltpu.SMEM),
            _zcol_spec(L, COL_CQ), _zcol_spec(L, COL_CKV), cspec, cspec,
        ],
        out_specs=pl.BlockSpec((1, L, BRANCH_W), lambda b: (b, 0, 0)),
        compiler_params=_params("arbitrary"),
        name="window_attention",
    )(sink, z, z, cache_k, cache_v)


N_ROW_OFF = 2 * NA_ROWS - 1
N_COL_OFF = 2 * NA_COLS - 1


NA_QROWS = 4
NA_WIN_ROWS = NA_ROWS + NA_QROWS
NA_PATTERNS = 3


def _na_window_start(r0, rows):
    return min(max(r0 - NA_ROWS // 2, 0), rows - NA_WIN_ROWS)


def _na_pattern_blocks(p, rows):
    r0 = (0, 2 * NA_QROWS, rows - NA_QROWS)[p]
    ws = _na_window_start(r0, rows)
    blocks = {}
    for i in range(NA_QROWS):
        r = r0 + i
        start = min(max(r - NA_ROWS // 2, 0), rows - NA_ROWS)
        for kk in range(NA_WIN_ROWS):
            rel = ws + kk - start
            blocks[i, kk] = ws + kk - r + (NA_ROWS - 1) if 0 <= rel < NA_ROWS else None
    return blocks


def _na_bias_kernel(rb_ref, o_ref, *, rows):
    h = pl.program_id(0)
    cq = lax.broadcasted_iota(jnp.int32, (GRID_W, GRID_W), 0)
    ck = lax.broadcasted_iota(jnp.int32, (GRID_W, GRID_W), 1)
    col_start = jnp.clip(cq - NA_COLS // 2, 0, GRID_W - NA_COLS)
    col_ok = (ck >= col_start) & (ck < col_start + NA_COLS)
    col_off = jnp.clip(ck - cq + (NA_COLS - 1), 0, N_COL_OFF - 1)
    tiles = []
    for ro in range(N_ROW_OFF):
        t = jnp.zeros((GRID_W, GRID_W), F32)
        for co in range(N_COL_OFF):
            t = jnp.where(col_off == co, rb_ref[(h * N_ROW_OFF + ro) * N_COL_OFF + co], t)
        tiles.append(jnp.where(col_ok, t, NEG))
    masked = jnp.full((GRID_W, GRID_W), NEG, F32)
    for p in range(NA_PATTERNS):
        for (i, kk), ro in _na_pattern_blocks(p, rows).items():
            o_ref[0, p, i * GRID_W:(i + 1) * GRID_W, kk * GRID_W:(kk + 1) * GRID_W] = (
                masked if ro is None else tiles[ro])


def _na_bias_call(na_bias, rows):
    shape = (NA_HEADS, NA_PATTERNS, NA_QROWS * GRID_W, NA_WIN_ROWS * GRID_W)
    return pl.pallas_call(
        functools.partial(_na_bias_kernel, rows=rows),
        out_shape=jax.ShapeDtypeStruct(shape, F32),
        grid=(NA_HEADS,),
        in_specs=[pl.BlockSpec(memory_space=pltpu.SMEM)],
        out_specs=pl.BlockSpec((1,) + shape[1:], lambda h: (h, 0, 0, 0)),
        compiler_params=_params("arbitrary"),
        name="na_bias_table",
    )(na_bias.reshape(-1))


def _na_attn_kernel(q_ref, k_ref, v_ref, ck_ref, cv_ref, bt_ref, o_ref, *, L):
    rows = L // GRID_W
    groups = rows // NA_QROWS
    lane = lax.broadcasted_iota(jnp.int32, (NA_QROWS * GRID_W, PAIR), 1)
    for hp in range(NA_HEADS * HEAD_DIM // PAIR):
        heads = (2 * hp, 2 * hp + 1)
        lanes = slice(hp * PAIR, (hp + 1) * PAIR)
        ctx_k = jnp.concatenate([ck_ref[0, 0, h].astype(BF) for h in heads], axis=0)
        ctx_v = jnp.concatenate([cv_ref[0, 0, h].astype(BF) for h in heads], axis=0)

        def body(qi, carry, heads=heads, lanes=lanes, ctx_k=ctx_k, ctx_v=ctx_v):
            r0 = qi * NA_QROWS
            ws = jnp.clip(r0 - NA_ROWS // 2, 0, rows - NA_WIN_ROWS)
            pat = jnp.where(qi == 0, 0, jnp.where(qi == groups - 1, 2, 1))
            rq = pl.ds(pl.multiple_of(r0 * GRID_W, NA_QROWS * GRID_W), NA_QROWS * GRID_W)
            rk = pl.ds(pl.multiple_of(ws * GRID_W, GRID_W), NA_WIN_ROWS * GRID_W)
            q2 = q_ref[0, rq, lanes] * ATTN_SCALE
            k2 = k_ref[0, rk, lanes]
            v2 = v_ref[0, rk, lanes]
            outs = []
            for g, h in enumerate(heads):
                qg = _one_head(q2, lane, g)
                s = _dot_nt(qg, k2) + bt_ref[h, pat]
                s_ctx = _dot(qg, ctx_k)
                m = jnp.maximum(jnp.max(s, axis=-1, keepdims=True), jnp.max(s_ctx, axis=-1, keepdims=True))
                p = jnp.exp(s - m)
                p_ctx = jnp.exp(s_ctx - m)
                den = jnp.sum(p, axis=-1, keepdims=True) + jnp.sum(p_ctx, axis=-1, keepdims=True)
                outs.append((_dot(p.astype(BF), v2) + _dot_nt(p_ctx.astype(BF), ctx_v)) / den)
            o_ref[0, rq, lanes] = jnp.where(lane // HEAD_DIM == 0, outs[0], outs[1]).astype(BF)
            return carry

        lax.fori_loop(0, groups, body, 0)


def _na_attn_call(z, cache_k, cache_v, bias_table, layer):
    B, L, _ = z.shape
    P = cache_k.shape[4]
    assert L // GRID_W >= NA_WIN_ROWS + NA_QROWS and (L // GRID_W) % NA_QROWS == 0
    cspec = pl.BlockSpec((1, 1, NA_HEADS, HEAD_DIM, P), lambda b: (b, layer, 0, 0, 0))
    return pl.pallas_call(
        functools.partial(_na_attn_kernel, L=L),
        out_shape=jax.ShapeDtypeStruct((B, L, BRANCH_W), BF),
        grid=(B,),
        in_specs=[
            _zcol_spec(L, COL_DQ), _zcol_spec(L, COL_DK), _zcol_spec(L, COL_DV), cspec, cspec,
            pl.BlockSpec(bias_table.shape, lambda b: (0, 0, 0, 0), pipeline_mode=pl.Buffered(1)),
        ],
        out_specs=pl.BlockSpec((1, L, BRANCH_W), lambda b: (b, 0, 0)),
        compiler_params=_params("arbitrary"),
        name="neighbourhood_attention",
    )(z, z, z, cache_k, cache_v, bias_table)


def _merge_kernel(x_ref, g_ref, sc_ref, sh_ref, gate_ref, ba_ref, bb_ref, bc_ref, bd_ref,
                  wg0_ref, wg1_ref, wb_ref, wo_ref, o_ref):
    x = x_ref[0]
    h = _norm_mod(x, g_ref[...], sc_ref[0], sh_ref[0]).astype(BF)
    merged = jnp.zeros(x.shape, F32)
    per_ref = MIX_COLS // D_MODEL
    for n, br_ref in enumerate((ba_ref, bb_ref, bc_ref, bd_ref)):
        wg_ref = (wg0_ref, wg1_ref)[n // per_ref]
        c0 = (n % per_ref) * D_MODEL
        gate = jax.nn.sigmoid(_dot(h, wg_ref[0, :, c0:c0 + D_MODEL]))
        merged = merged + gate * _dot(br_ref[0], wb_ref[0, n])
    o_ref[0] = x + gate_ref[0] * _dot(merged.astype(BF), wo_ref[0])


def _merge_call(x, norm_g, mod, branches, w_in_bf, w_branch, w_out, layer, tm):
    B, L, _ = x.shape
    bm = mod.shape[0]
    xspec = pl.BlockSpec((1, tm, D_MODEL), lambda b, i: (b, i, 0))
    bspec = pl.BlockSpec((1, tm, BRANCH_W), lambda b, i: (b, i, 0))
    return pl.pallas_call(
        _merge_kernel,
        out_shape=jax.ShapeDtypeStruct((B, L, D_MODEL), F32),
        grid=(B, L // tm),
        in_specs=[
            xspec,
            pl.BlockSpec((1, D_MODEL), lambda b, i: (0, 0)),
            _mod_spec(bm, 1, 2), _mod_spec(bm, 0, 2), _mod_spec(bm, 2, 2),
            bspec, bspec, bspec, bspec,
            pl.BlockSpec((1, D_MODEL, MIX_COLS), lambda b, i: (layer, 0, 1)),
            pl.BlockSpec((1, D_MODEL, MIX_COLS), lambda b, i: (layer, 0, 2)),
            pl.BlockSpec((1, N_BRANCH, BRANCH_W, D_MODEL), lambda b, i: (layer, 0, 0, 0)),
            pl.BlockSpec((1, D_MODEL, D_MODEL), lambda b, i: (layer, 0, 0)),
        ],
        out_specs=xspec,
        compiler_params=_params("arbitrary", "arbitrary"),
        name="merge_out_proj",
    )(x, norm_g, mod, mod, mod, *branches, w_in_bf, w_in_bf, w_branch, w_out)


ROUTE_CHUNK = 256
GATHER_ROWS = 512
SIGN_BIT = 31
SEARCH_BITS = 4
SEARCH_TOP_SHIFT = ((SIGN_BIT - 1) // SEARCH_BITS) * SEARCH_BITS


def _prefix_excl(mask_ref, out_ref, tri, L):
    totals = [jnp.sum(mask_ref[:, c0:c0 + LANES], axis=-1, keepdims=True) for c0 in range(0, L, LANES)]
    run = jnp.zeros((N_EXPERTS, 1), F32)
    for i, c0 in enumerate(range(0, L, LANES)):
        out_ref[:, c0:c0 + LANES] = _dot(mask_ref[:, c0:c0 + LANES].astype(BF), tri) + run
        run = run + totals[i]


def _route_kernel(x_ref, g_ref, sc_ref, sh_ref, wh_ref, wl_ref, tri_ref,
                  h_ref, gate_ref, slot_ref, slot_t_ref, off_ref, lt_ref, msk_ref, pre_ref, *, L, cap):
    g = g_ref[...]
    sc = sc_ref[0]
    sh = sh_ref[0]
    wh = wh_ref[...]
    wl = wl_ref[...]
    rc = min(L, ROUTE_CHUNK)
    for r0 in range(0, L, rc):
        h = _norm_mod(x_ref[0, r0:r0 + rc, :], g, sc, sh)
        h_hi = h.astype(BF)
        h_ref[0, r0:r0 + rc, :] = h_hi
        h_lo = (h - h_hi.astype(F32)).astype(BF)
        lt_ref[:, r0:r0 + rc] = _dot_nt(wh, h_hi) + (_dot_nt(wh, h_lo) + _dot_nt(wl, h_hi))
    lt = lt_ref[...]
    e = jnp.exp(lt - jnp.max(lt, axis=0, keepdims=True))
    aff = e / jnp.sum(e, axis=0, keepdims=True)
    thr = jnp.zeros((N_EXPERTS, 1), jnp.int32)
    for shift in range(SEARCH_TOP_SHIFT, -1, -SEARCH_BITS):
        n_digits = min(1 << SEARCH_BITS, 1 << (SIGN_BIT - shift))
        digit = jnp.zeros((N_EXPERTS, 1), jnp.int32)
        for d in range(1, n_digits):
            cand = pltpu.bitcast(thr | (d << shift), F32)
            cnt = jnp.sum((aff >= cand).astype(F32), axis=-1, keepdims=True)
            digit = digit + (cnt >= cap).astype(jnp.int32)
        thr = thr | (digit << shift)
    thr_f = pltpu.bitcast(thr, F32)
    gt = aff > thr_f
    eq = aff == thr_f
    need = cap - jnp.sum(gt.astype(F32), axis=-1, keepdims=True)
    tri = tri_ref[...]
    msk_ref[...] = eq.astype(F32)
    _prefix_excl(msk_ref, pre_ref, tri, L)
    sel = gt | (eq & (pre_ref[...] < need))
    msk_ref[...] = sel.astype(F32)
    _prefix_excl(msk_ref, pre_ref, tri, L)
    off_ref[0] = jnp.concatenate([pre_ref[:, c:c + 1] for c in range(0, L, min(L, COMBINE_TL))], axis=1)
    slot = jnp.where(sel, pre_ref[...], -1.0)
    slot_ref[0] = slot
    slot_t_ref[0] = slot.T
    gate_ref[0] = jnp.where(sel, aff, 0.0)


def _route_call(x, norm_g, mod, wr_hi, wr_lo):
    B, L, _ = x.shape
    bm = mod.shape[0]
    cap = EC_CAPACITY * L // N_EXPERTS
    n_tiles = L // min(L, COMBINE_TL)
    tri = jnp.asarray(np.triu(np.ones((LANES, LANES)), 1).astype(ml_dtypes.bfloat16))
    wspec = pl.BlockSpec((N_EXPERTS, D_MODEL), lambda b: (0, 0))
    return pl.pallas_call(
        functools.partial(_route_kernel, L=L, cap=cap),
        out_shape=[
            jax.ShapeDtypeStruct((B, L, D_MODEL), BF),
            jax.ShapeDtypeStruct((B, N_EXPERTS, L), F32),
            jax.ShapeDtypeStruct((B, N_EXPERTS, L), F32),
            jax.ShapeDtypeStruct((B, L, N_EXPERTS), F32),
            jax.ShapeDtypeStruct((B, N_EXPERTS, n_tiles), F32),
        ],
        grid=(B,),
        in_specs=[
            pl.BlockSpec((1, L, D_MODEL), lambda b: (b, 0, 0)),
            pl.BlockSpec((1, D_MODEL), lambda b: (0, 0)),
            _mod_spec(bm, 4, 1), _mod_spec(bm, 3, 1),
            wspec, wspec,
            pl.BlockSpec((LANES, LANES), lambda b: (0, 0)),
        ],
        out_specs=[
            pl.BlockSpec((1, L, D_MODEL), lambda b: (b, 0, 0)),
            pl.BlockSpec((1, N_EXPERTS, L), lambda b: (b, 0, 0)),
            pl.BlockSpec((1, N_EXPERTS, L), lambda b: (b, 0, 0)),
            pl.BlockSpec((1, L, N_EXPERTS), lambda b: (b, 0, 0)),
            pl.BlockSpec((1, N_EXPERTS, n_tiles), lambda b: (b, 0, 0)),
        ],
        scratch_shapes=[
            pltpu.VMEM((N_EXPERTS, L), F32),
            pltpu.VMEM((N_EXPERTS, L), F32),
            pltpu.VMEM((N_EXPERTS, L), F32),
        ],
        compiler_params=_params("arbitrary"),
        name="ec_route",
    )(x, norm_g, mod, mod, wr_hi, wr_lo, tri)


COMBINE_TL = 256
COMBINE_WIN = 64
BF_TILE = 16


def _list_windows(off_ref, cap):
    n_tiles = pl.num_programs(1)
    base = (pl.program_id(0) * n_tiles + pl.program_id(1)) * N_EXPERTS
    starts = []
    fits = None
    for ex in range(N_EXPERTS):
        first = off_ref[base + ex]
        end = jnp.where(pl.program_id(1) == n_tiles - 1, cap, off_ref[base + N_EXPERTS + ex])
        start = jnp.minimum((first // BF_TILE) * BF_TILE, cap - COMBINE_WIN)
        ok = end <= start + COMBINE_WIN
        fits = ok if fits is None else (fits & ok)
        starts.append(start)
    return starts, fits


def _gather_kernel(off_ref, slot_ref, gate_ref, h_ref, xs_ref, gs_ref, *, cap):
    h = h_ref[0]
    tl = h.shape[0]

    @pl.when(pl.program_id(1) == 0)
    def _():
        xs_ref[...] = jnp.zeros(xs_ref.shape, BF)
        gs_ref[...] = jnp.zeros(gs_ref.shape, F32)

    def select(ex, first, rows):
        pos = lax.broadcasted_iota(jnp.int32, (rows, tl), 0).astype(F32)
        hit = slot_ref[0, ex:ex + 1, :] - first == pos
        gate = jnp.sum(jnp.where(hit, gate_ref[0, ex:ex + 1, :], 0.0), axis=-1, keepdims=True)
        return jnp.where(hit, 1.0, 0.0).astype(BF), jnp.broadcast_to(gate, (rows, LANES))

    def gather_all():
        ge = max(1, min(N_EXPERTS, GATHER_ROWS // cap))
        for e0 in range(0, N_EXPERTS, ge):
            picked = [select(ex, 0.0, cap) for ex in range(e0, e0 + ge)]
            rows = _dot(jnp.concatenate([p[0] for p in picked], axis=0), h).astype(BF)
            for j in range(ge):
                xs_ref[e0 + j, 0] += rows[j * cap:(j + 1) * cap]
                gs_ref[e0 + j, 0] += picked[j][1]

    if cap <= COMBINE_WIN:
        gather_all()
        return

    starts, fits = _list_windows(off_ref, cap)

    @pl.when(fits)
    def _():
        picked = [select(ex, starts[ex].astype(F32), COMBINE_WIN) for ex in range(N_EXPERTS)]
        rows = _dot(jnp.concatenate([p[0] for p in picked], axis=0), h).astype(BF)
        for ex in range(N_EXPERTS):
            win = pl.ds(pl.multiple_of(starts[ex], BF_TILE), COMBINE_WIN)
            xs_ref[ex, 0, win, :] += rows[ex * COMBINE_WIN:(ex + 1) * COMBINE_WIN]
            gs_ref[ex, 0, win, :] += picked[ex][1]

    @pl.when(jnp.logical_not(fits))
    def _():
        gather_all()


def _gather_call(offsets, slot, gate_rows, h2):
    B, L, _ = h2.shape
    cap = EC_CAPACITY * L // N_EXPERTS
    tl = min(L, COMBINE_TL)
    assert cap <= COMBINE_WIN or (cap % BF_TILE == 0 and COMBINE_WIN % BF_TILE == 0)
    return pl.pallas_call(
        functools.partial(_gather_kernel, cap=cap),
        out_shape=[
            jax.ShapeDtypeStruct((N_EXPERTS, B, cap, D_MODEL), BF),
            jax.ShapeDtypeStruct((N_EXPERTS, B, cap, LANES), F32),
        ],
        grid_spec=pltpu.PrefetchScalarGridSpec(
            num_scalar_prefetch=1,
            grid=(B, L // tl),
            in_specs=[
                pl.BlockSpec((1, N_EXPERTS, tl), lambda b, i, off: (b, 0, i)),
                pl.BlockSpec((1, N_EXPERTS, tl), lambda b, i, off: (b, 0, i)),
                pl.BlockSpec((1, tl, D_MODEL), lambda b, i, off: (b, i, 0)),
            ],
            out_specs=[
                pl.BlockSpec((N_EXPERTS, 1, cap, D_MODEL), lambda b, i, off: (0, b, 0, 0)),
                pl.BlockSpec((N_EXPERTS, 1, cap, LANES), lambda b, i, off: (0, b, 0, 0)),
            ],
        ),
        compiler_params=_params("arbitrary", "arbitrary"),
        name="ec_gather",
    )(offsets, slot, gate_rows, h2)


EXPERT_TR = 512


def _expert_kernel(xa_ref, ga_ref, xb_ref, gb_ref, wg_ref, wu_ref, wd_ref, ya_ref, yb_ref,
                   wgb_ref, wub_ref, wdb_ref, *, tiles_a):
    j = pl.program_id(1)

    @pl.when(j == 0)
    def _():
        wgb_ref[...] = wg_ref[0, 0].astype(BF)
        wub_ref[...] = wu_ref[0, 0].astype(BF)
        wdb_ref[...] = wd_ref[0, 0].astype(BF)

    def ffn(x_ref, g_ref, y_ref):
        x = x_ref[0]
        a = _dot(x, wgb_ref[...])
        u = _dot(x, wub_ref[...])
        y = _dot((_silu(a) * u).astype(BF), wdb_ref[...])
        y_ref[0] = (y * g_ref[0, :, 0:1]).astype(BF)

    @pl.when(j < tiles_a)
    def _():
        ffn(xa_ref, ga_ref, ya_ref)

    @pl.when(j >= tiles_a)
    def _():
        ffn(xb_ref, gb_ref, yb_ref)


def _expert_call(xs_a, gs_a, xs_b, gs_b, w_gate, w_up, w_down, layer):
    E, ra, _ = xs_a.shape
    rb = xs_b.shape[1]
    tr = math.gcd(math.gcd(ra, rb), EXPERT_TR)
    tiles_a, tiles_b = ra // tr, rb // tr
    amap = lambda e, j: (e, jnp.minimum(j, tiles_a - 1), 0)
    bmap = lambda e, j: (e, jnp.maximum(j - tiles_a, 0), 0)
    wspec_in = pl.BlockSpec((1, 1, D_MODEL, EXPERT_FF), lambda e, j: (layer, e, 0, 0))
    wspec_out = pl.BlockSpec((1, 1, EXPERT_FF, D_MODEL), lambda e, j: (layer, e, 0, 0))
    return pl.pallas_call(
        functools.partial(_expert_kernel, tiles_a=tiles_a),
        out_shape=[jax.ShapeDtypeStruct((E, ra, D_MODEL), BF), jax.ShapeDtypeStruct((E, rb, D_MODEL), BF)],
        grid=(E, tiles_a + tiles_b),
        in_specs=[
            pl.BlockSpec((1, tr, D_MODEL), amap), pl.BlockSpec((1, tr, LANES), amap),
            pl.BlockSpec((1, tr, D_MODEL), bmap), pl.BlockSpec((1, tr, LANES), bmap),
            wspec_in, wspec_in, wspec_out,
        ],
        out_specs=[pl.BlockSpec((1, tr, D_MODEL), amap), pl.BlockSpec((1, tr, D_MODEL), bmap)],
        scratch_shapes=[
            pltpu.VMEM((D_MODEL, EXPERT_FF), BF),
            pltpu.VMEM((D_MODEL, EXPERT_FF), BF),
            pltpu.VMEM((EXPERT_FF, D_MODEL), BF),
        ],
        compiler_params=_params("arbitrary", "arbitrary"),
        name="ec_experts",
    )(xs_a, gs_a, xs_b, gs_b, w_gate, w_up, w_down)


def _combine_kernel(off_ref, st_ref, y_ref, x_ref, gate_ref, fg_ref, o_ref, *, cap, final):
    st = st_ref[0]
    tl = st.shape[0]

    def finish(scattered):
        v = x_ref[0] + gate_ref[0] * scattered
        if final:
            v = v * lax.rsqrt(jnp.mean(v * v, axis=-1, keepdims=True) + EPS) * fg_ref[...]
        o_ref[0] = v

    def onehot(width, starts):
        n = N_EXPERTS * width
        owner = lax.broadcasted_iota(jnp.int32, (N_EXPERTS, n), 1) // width
        spread = jnp.where(owner == lax.broadcasted_iota(jnp.int32, (N_EXPERTS, n), 0), 1.0, 0.0).astype(BF)
        slot = _dot(st.astype(BF), spread)
        lane = lax.broadcasted_iota(jnp.int32, (1, n), 1)
        want = (lane % width).astype(F32)
        if starts is not None:
            for ex in range(N_EXPERTS):
                want = want + jnp.where(lane // width == ex, starts[ex].astype(F32), 0.0)
        return jnp.where(slot == want, 1.0, 0.0).astype(BF)

    def scatter_all():
        finish(_dot(onehot(cap, None), y_ref[:, 0].reshape(N_EXPERTS * cap, D_MODEL)))

    if cap <= COMBINE_WIN:
        scatter_all()
        return

    starts, fits = _list_windows(off_ref, cap)

    @pl.when(fits)
    def _():
        rows = [y_ref[ex, 0, pl.ds(pl.multiple_of(starts[ex], BF_TILE), COMBINE_WIN), :]
                for ex in range(N_EXPERTS)]
        finish(_dot(onehot(COMBINE_WIN, starts), jnp.concatenate(rows, axis=0)))

    @pl.when(jnp.logical_not(fits))
    def _():
        scatter_all()


def _combine_call(slot_t, offsets, y, x, mod, final_g, final):
    B, L, _ = x.shape
    bm = mod.shape[0]
    cap = y.shape[2]
    tl = min(L, COMBINE_TL)
    assert cap <= COMBINE_WIN or (cap % BF_TILE == 0 and COMBINE_WIN % BF_TILE == 0)
    assert cap <= 256
    gate_map =(lambda b, i, off: (b, 0, 5)) if bm > 1 else (lambda b, i, off: (0, 0, 5))
    return pl.pallas_call(
        functools.partial(_combine_kernel, cap=cap, final=final),
        out_shape=jax.ShapeDtypeStruct((B, L, D_MODEL), F32),
        grid_spec=pltpu.PrefetchScalarGridSpec(
            num_scalar_prefetch=1,
            grid=(B, L // tl),
            in_specs=[
                pl.BlockSpec((1, tl, N_EXPERTS), lambda b, i, off: (b, i, 0)),
                pl.BlockSpec((N_EXPERTS, 1, cap, D_MODEL), lambda b, i, off: (0, b, 0, 0)),
                pl.BlockSpec((1, tl, D_MODEL), lambda b, i, off: (b, i, 0)),
                pl.BlockSpec((1, 1, D_MODEL), gate_map),
                pl.BlockSpec((1, D_MODEL), lambda b, i, off: (0, 0)),
            ],
            out_specs=pl.BlockSpec((1, tl, D_MODEL), lambda b, i, off: (b, i, 0)),
        ),
        compiler_params=_params("arbitrary", "arbitrary"),
        name="ec_combine",
    )(offsets, slot_t, y, x, mod, final_g)


def _mixer_half(x, mod, lw, ctx, tm, kv_prev=None):
    latent = ctx is not None
    layer = lw["layer"]
    res = _inproj_call(x, lw["norm1_g"], mod, lw["w_in"], layer, tm, rope=latent, want_kv=not latent,
                       kv_prev=kv_prev)
    z = res[0]
    br_a = _conv_call(z, lw["conv_w"], lw["conv_b"], lw["conv_ln_g"], lw["conv_ln_b"])
    br_b = _fourier_call(z, *lw["dft"][z.shape[1]])
    if latent:
        br_c = _win_attn_call(z, ctx["win_k"], ctx["win_v"], lw["win_sink"], layer)
        br_d = _na_attn_call(z, ctx["nat_k"], ctx["nat_v"], lw["bias_table"], layer)
    else:
        br_c, br_d = _ctx_attn_call(z, lw["win_sink"])
    x = _merge_call(x, lw["norm1_g"], mod, (br_a, br_b, br_c, br_d),
                    lw["w_in"], lw["w_branch"], lw["w_out"], layer, tm)
    return x, (tuple(res[1:]) if not latent else None)


def kernel(x_prompt, x_sample, cache_win_k, cache_win_v, cache_nat_k, cache_nat_v, c, c_ctx,
           norm1_g, norm2_g, w_ada, b_ada, w_in, conv_w, conv_b, conv_ln_g, conv_ln_b,
           win_sink, na_bias, w_branch, w_out, w_router, w_e_gate, w_e_up, w_e_down, final_norm_g):
    n_dec = c.shape[0]
    cond = jnp.zeros((ADA_ROWS, D_MODEL), F32).at[0].set(c_ctx).at[1:1 + n_dec].set(c)
    mods = _ada_call(cond, w_ada, b_ada)
    final_g = final_norm_g.reshape(1, D_MODEL)
    w_in_bf = _cast_call(w_in)
    w_branch_bf = _cast_call(w_branch)
    w_out_bf = _cast_call(w_out)
    dft = {L: _dft_tables(L) for L in (x_prompt.shape[1], x_sample.shape[1])}
    dim_major = lambda a: jnp.swapaxes(a, -1, -2)
    ctx = {"win_k": dim_major(cache_win_k), "win_v": dim_major(cache_win_v),
           "nat_k": dim_major(cache_nat_k), "nat_v": dim_major(cache_nat_v)}

    xp, xs = x_prompt, x_sample
    caches = None
    for l in range(DEPTH):
        wr_t = w_router[l].T
        wr_hi = wr_t.astype(BF)
        wr_lo = (wr_t - wr_hi.astype(F32)).astype(BF)
        lw = {
            "layer": l,
            "norm1_g": norm1_g[l].reshape(1, D_MODEL),
            "w_in": w_in_bf, "w_branch": w_branch_bf, "w_out": w_out_bf, "dft": dft,
            "conv_w": conv_w[l], "conv_b": conv_b[l],
            "conv_ln_g": conv_ln_g[l], "conv_ln_b": conv_ln_b[l],
            "win_sink": win_sink[l],
            "bias_table": _na_bias_call(na_bias[l], x_sample.shape[1] // GRID_W),
        }
        norm2 = norm2_g[l].reshape(1, D_MODEL)
        final = l == DEPTH - 1
        mod_p = mods[l, 0:1].reshape(1, 1, 6 * D_MODEL)
        mod_s = mods[l, 1:1 + n_dec].reshape(n_dec, 1, 6 * D_MODEL)
        xp, caches = _mixer_half(xp, mod_p, lw, None, tm=256, kv_prev=caches)
        xs, _ = _mixer_half(xs, mod_s, lw, ctx, tm=512)
        smem = lambda off: jnp.pad(off.transpose(0, 2, 1).astype(jnp.int32).reshape(-1), (0, N_EXPERTS))
        h_p, g_p, slot_p, slot_t_p, off_p = _route_call(xp, norm2, mod_p, wr_hi, wr_lo)
        h_s, g_s, slot_s, slot_t_s, off_s = _route_call(xs, norm2, mod_s, wr_hi, wr_lo)
        off_p, off_s = smem(off_p), smem(off_s)
        rows_p, gates_p = _gather_call(off_p, slot_p, g_p, h_p)
        rows_s, gates_s = _gather_call(off_s, slot_s, g_s, h_s)
        flat = lambda a: a.reshape(N_EXPERTS, a.shape[1] * a.shape[2], a.shape[3])
        y_p, y_s = _expert_call(flat(rows_p), flat(gates_p), flat(rows_s), flat(gates_s),
                                w_e_gate, w_e_up, w_e_down, l)
        xp = _combine_call(slot_t_p, off_p, y_p.reshape(rows_p.shape), xp, mod_p, final_g, final)
        xs = _combine_call(slot_t_s, off_s, y_s.reshape(rows_s.shape), xs, mod_s, final_g, final)

    return (xp, xs) + tuple(dim_major(c) for c in caches)
```

```python
import functools
import math

import ml_dtypes
import numpy as np
import jax
import jax.numpy as jnp
from jax import lax
from jax.experimental import pallas as pl
from jax.experimental.pallas import tpu as pltpu

D_MODEL = 1024
DEPTH = 2
GRID_W = 64
BRANCH_W = 256
HEAD_DIM = 64
N_BRANCH = 4
CONV_K = 31
FNET_GROUPS = 4
FNET_GDIM = BRANCH_W // FNET_GROUPS
WIN_Q_HEADS = 4
WIN_KV_HEADS = 2
WIN_BLOCK = 128
NA_HEADS = 4
NA_ROWS = 8
NA_COLS = 16
ROPE_THETA = 10000.0
ATTN_SCALE = HEAD_DIM ** -0.5
N_EXPERTS = 16
EXPERT_FF = 1024
EC_CAPACITY = 2
EPS = 1e-6
NEG = -1e30

MIX_COLS = 2048
GATE_COLS = N_BRANCH * D_MODEL
IN_COLS = MIX_COLS + GATE_COLS
COL_A, COL_F, COL_CQ, COL_CKV, COL_DQ, COL_DK, COL_DV = 0, 512, 768, 1024, 1280, 1536, 1792
ROPE_LO, ROPE_HI = COL_CQ, COL_CKV + WIN_KV_HEADS * HEAD_DIM
KV_COLS = (COL_CKV, COL_CKV + WIN_KV_HEADS * HEAD_DIM, COL_DK, COL_DV)
KV_HEADS = (WIN_KV_HEADS, WIN_KV_HEADS, NA_HEADS, NA_HEADS)

LANES = 128
SUBLANES = 8
VMEM_LIMIT = 56 * 1024 * 1024

BF = jnp.bfloat16
F32 = jnp.float32


def _params(*sem):
    return pltpu.CompilerParams(dimension_semantics=sem, vmem_limit_bytes=VMEM_LIMIT)


def _dot(a, b):
    return jnp.dot(a, b, preferred_element_type=F32)


def _dot_nt(a, b):
    return lax.dot_general(a, b, (((1,), (1,)), ((), ())), preferred_element_type=F32)


def _norm_mod(x, g, sc, sh):
    y = x * lax.rsqrt(jnp.mean(x * x, axis=-1, keepdims=True) + EPS)
    return (y * g) * (1.0 + sc) + sh


def _silu(x):
    return x * jax.nn.sigmoid(x)


CAST_BYTES = 4 * 1024 * 1024


def _cast_kernel(x_ref, o_ref):
    o_ref[...] = x_ref[...].astype(BF)


def _cast_call(x):
    shape = x.shape
    x2 = x.reshape(-1, shape[-1])
    rows, cols = x2.shape
    tr = min(rows, 1 << (max(16, CAST_BYTES // (4 * cols)).bit_length() - 1))
    assert rows % tr == 0
    out = pl.pallas_call(
        _cast_kernel,
        out_shape=jax.ShapeDtypeStruct((rows, cols), BF),
        grid=(rows // tr,),
        in_specs=[pl.BlockSpec((tr, cols), lambda i: (i, 0))],
        out_specs=pl.BlockSpec((tr, cols), lambda i: (i, 0)),
        compiler_params=_params("arbitrary"),
        name="cast_bf16",
    )(x2)
    return out.reshape(shape)


ADA_ROWS = 16
ADA_TN = 1536


def _ada_kernel(c_ref, w_ref, b_ref, o_ref):
    s = _silu(c_ref[...]).astype(BF)
    o_ref[0] = _dot(s, w_ref[0].astype(BF)) + b_ref[0]


def _ada_call(cond, w_ada, b_ada):
    n = 6 * D_MODEL
    return pl.pallas_call(
        _ada_kernel,
        out_shape=jax.ShapeDtypeStruct((DEPTH, ADA_ROWS, n), F32),
        grid=(DEPTH, n // ADA_TN),
        in_specs=[
            pl.BlockSpec((ADA_ROWS, D_MODEL), lambda l, j: (0, 0)),
            pl.BlockSpec((1, D_MODEL, ADA_TN), lambda l, j: (l, 0, j)),
            pl.BlockSpec((1, 1, ADA_TN), lambda l, j: (l, 0, j)),
        ],
        out_specs=pl.BlockSpec((1, ADA_ROWS, ADA_TN), lambda l, j: (l, 0, j)),
        compiler_params=_params("arbitrary", "arbitrary"),
        name="ada_mod",
    )(cond, w_ada, b_ada.reshape(DEPTH, 1, n))


def _mod_spec(bm, chunk, ngrid):
    if ngrid == 1:
        imap = (lambda b: (b, 0, chunk)) if bm > 1 else (lambda b: (0, 0, chunk))
    else:
        imap = (lambda b, i: (b, 0, chunk)) if bm > 1 else (lambda b, i: (0, 0, chunk))
    return pl.BlockSpec((1, 1, D_MODEL), imap)


def _inproj_kernel(*refs, rope, kv_layer, kv_aliased):
    x_ref, g_ref, sc_ref, sh_ref, w_ref = refs[:5]
    pos = 5
    if rope:
        cos_ref, sin_ref = refs[pos:pos + 2]
        pos += 2
    if kv_aliased:
        pos += len(KV_HEADS)
    z_ref = refs[pos]
    h = _norm_mod(x_ref[0], g_ref[...], sc_ref[0], sh_ref[0]).astype(BF)
    z = _dot(h, w_ref[0])
    if kv_layer is not None:
        for kv_ref, c0, heads in zip(refs[pos + 1:pos + 1 + len(KV_HEADS)], KV_COLS, KV_HEADS):
            here = 0 if kv_aliased else kv_layer
            for hd in range(heads):
                kv_ref[0, here, hd] = z[:, c0 + hd * HEAD_DIM:c0 + (hd + 1) * HEAD_DIM].T
            for other in range(kv_ref.shape[1]):
                if other != here:
                    kv_ref[0, other] = jnp.zeros(kv_ref.shape[2:], F32)
    if rope:
        z_ref[0, :, 0:ROPE_LO] = z[:, 0:ROPE_LO].astype(BF)
        lane = lax.broadcasted_iota(jnp.int32, (z.shape[0], LANES), 1)
        first_half = (lane % 32) < 16
        cos = cos_ref[...]
        sin = sin_ref[...]
        for c0 in range(ROPE_LO, ROPE_HI, LANES):
            xc = z[:, c0:c0 + LANES]
            partner = jnp.where(first_half, pltpu.roll(xc, LANES - 16, 1), pltpu.roll(xc, 16, 1))
            z_ref[0, :, c0:c0 + LANES] = (xc * cos + partner * sin).astype(BF)
        z_ref[0, :, ROPE_HI:MIX_COLS] = z[:, ROPE_HI:MIX_COLS].astype(BF)
    else:
        z_ref[0] = z.astype(BF)


def _rope_tables(L):
    t = np.arange(L)
    half = HEAD_DIM // 4
    freqs = 1.0 / (ROPE_THETA ** (np.arange(half, dtype=np.float64) / half))
    ang_r = (t // GRID_W).astype(np.float64)[:, None] * freqs[None, :]
    ang_c = (t % GRID_W).astype(np.float64)[:, None] * freqs[None, :]
    cos_h = np.concatenate([np.cos(ang_r), np.cos(ang_r), np.cos(ang_c), np.cos(ang_c)], axis=1)
    sin_h = np.concatenate([-np.sin(ang_r), np.sin(ang_r), -np.sin(ang_c), np.sin(ang_c)], axis=1)
    reps = LANES // HEAD_DIM
    return (jnp.asarray(np.tile(cos_h, (1, reps)), F32), jnp.asarray(np.tile(sin_h, (1, reps)), F32))


def _inproj_call(x, norm_g, mod, w_in_bf, layer, tm, rope, want_kv, kv_prev=None):
    B, L, _ = x.shape
    bm = mod.shape[0]
    in_specs = [
        pl.BlockSpec((1, tm, D_MODEL), lambda b, i: (b, i, 0)),
        pl.BlockSpec((1, D_MODEL), lambda b, i: (0, 0)),
        _mod_spec(bm, 1, 2),
        _mod_spec(bm, 0, 2),
        pl.BlockSpec((1, D_MODEL, MIX_COLS), lambda b, i: (layer, 0, 0)),
    ]
    args = [x, norm_g, mod, mod, w_in_bf]
    if rope:
        cos, sin = _rope_tables(L)
        in_specs += [pl.BlockSpec((tm, LANES), lambda b, i: (i, 0))] * 2
        args += [cos, sin]
    out_shape = [jax.ShapeDtypeStruct((B, L, MIX_COLS), BF)]
    out_specs = [pl.BlockSpec((1, tm, MIX_COLS), lambda b, i: (b, i, 0))]
    aliases = {}
    if want_kv:
        for n, heads in enumerate(KV_HEADS):
            out_shape.append(jax.ShapeDtypeStruct((B, DEPTH, heads, HEAD_DIM, L), F32))
            if kv_prev is None:
                out_specs.append(pl.BlockSpec((1, DEPTH, heads, HEAD_DIM, tm), lambda b, i: (b, 0, 0, 0, i)))
            else:
                out_specs.append(pl.BlockSpec((1, 1, heads, HEAD_DIM, tm), lambda b, i: (b, layer, 0, 0, i)))
                aliases[len(args)] = 1 + n
                in_specs.append(pl.BlockSpec(memory_space=pl.ANY))
                args.append(kv_prev[n])
    return pl.pallas_call(
        functools.partial(_inproj_kernel, rope=rope, kv_layer=layer if want_kv else None,
                          kv_aliased=kv_prev is not None),
        out_shape=out_shape,
        grid=(B, L // tm),
        in_specs=in_specs,
        out_specs=out_specs,
        input_output_aliases=aliases,
        compiler_params=_params("arbitrary", "arbitrary"),
        name="in_proj",
    )(*args)


CONV_PAD = 16
CONV_TR = 64
CONV_CHUNK = 256


def _conv_kernel(a_ref, w_ref, cb_ref, lg_ref, lb_ref, o_ref, p_ref, u_ref, wb_ref, *, L):
    for k in range(CONV_K):
        wb_ref[k] = jnp.broadcast_to(w_ref[k:k + 1, :], (SUBLANES, BRANCH_W))
    zeros = jnp.zeros((CONV_PAD, BRANCH_W), F32)
    p_ref[0:CONV_PAD, :] = zeros
    p_ref[L + CONV_PAD:L + 2 * CONV_PAD, :] = zeros
    for r0 in range(0, L, CONV_CHUNK):
        a = a_ref[0, r0:r0 + CONV_CHUNK, :].astype(F32)
        p_ref[CONV_PAD + r0:CONV_PAD + r0 + CONV_CHUNK, :] = a[:, :BRANCH_W] * jax.nn.sigmoid(a[:, BRANCH_W:])
    n_u = L + 2 * CONV_PAD - SUBLANES
    for s in range(SUBLANES):
        u_ref[s] = p_ref[s:s + n_u, :]
    cb = cb_ref[...]

    def body(i, carry):
        base = pl.multiple_of(i * CONV_TR, CONV_TR)
        groups = (CONV_TR // SUBLANES, SUBLANES, BRANCH_W)
        acc = jnp.zeros(groups, F32) + cb
        for k in range(CONV_K):
            q, s = divmod(k + CONV_PAD - CONV_K // 2, SUBLANES)
            u = u_ref[s, pl.ds(base + SUBLANES * q, CONV_TR), :]
            acc = acc + wb_ref[k] * u.reshape(groups)
        p_ref[pl.ds(base, CONV_TR), :] = acc.reshape(CONV_TR, BRANCH_W)
        return carry

    lax.fori_loop(0, L // CONV_TR, body, 0)
    lg = lg_ref[...]
    lb = lb_ref[...]
    for r0 in range(0, L, CONV_CHUNK):
        acc = p_ref[r0:r0 + CONV_CHUNK, :]
        mu = jnp.mean(acc, axis=-1, keepdims=True)
        d = acc - mu
        var = jnp.mean(d * d, axis=-1, keepdims=True)
        y = d * lax.rsqrt(var + EPS) * lg + lb
        o_ref[0, r0:r0 + CONV_CHUNK, :] = _silu(y).astype(BF)


def _conv_call(z, conv_w, conv_b, ln_g, ln_b):
    B, L, _ = z.shape
    n_u = L + 2 * CONV_PAD - SUBLANES
    vec = lambda b: (0, 0)
    return pl.pallas_call(
        functools.partial(_conv_kernel, L=L),
        out_shape=jax.ShapeDtypeStruct((B, L, BRANCH_W), BF),
        grid=(B,),
        in_specs=[
            pl.BlockSpec((1, L, 2 * BRANCH_W), lambda b: (b, 0, COL_A // (2 * BRANCH_W))),
            pl.BlockSpec((CONV_K, BRANCH_W), vec),
            pl.BlockSpec((1, BRANCH_W), vec),
            pl.BlockSpec((1, BRANCH_W), vec),
            pl.BlockSpec((1, BRANCH_W), vec),
        ],
        out_specs=pl.BlockSpec((1, L, BRANCH_W), lambda b: (b, 0, 0)),
        scratch_shapes=[
            pltpu.VMEM((L + 2 * CONV_PAD, BRANCH_W), F32),
            pltpu.VMEM((SUBLANES, n_u, BRANCH_W), F32),
            pltpu.VMEM((CONV_K, SUBLANES, BRANCH_W), F32),
        ],
        compiler_params=_params("arbitrary"),
        name="conv_module",
    )(z, conv_w, conv_b.reshape(1, -1), ln_g.reshape(1, -1), ln_b.reshape(1, -1))


def _dft_tables(L):
    c = np.arange(FNET_GDIM)
    ang_c = 2.0 * np.pi * ((c[:, None] * c[None, :]) % FNET_GDIM) / FNET_GDIM
    cc = np.zeros((BRANCH_W, BRANCH_W))
    sc = np.zeros((BRANCH_W, BRANCH_W))
    for g in range(FNET_GROUPS):
        sl = slice(g * FNET_GDIM, (g + 1) * FNET_GDIM)
        cc[sl, sl] = np.cos(ang_c)
        sc[sl, sl] = np.sin(ang_c)
    ccs = np.concatenate([cc, sc], axis=1)
    f = np.arange(L)
    t = np.arange(L // 2)
    ang_l = 2.0 * np.pi * ((f[:, None] * t[None, :]) % L) / L
    csl = np.concatenate([np.cos(ang_l), -np.sin(ang_l)], axis=1)
    return _cast_call(jnp.asarray(ccs, F32)), _cast_call(jnp.asarray(csl, F32))


FOURIER_TR = 512
FOURIER_PAD = 8


def _fourier_kernel(f_ref, ccs_ref, csl_ref, o_ref, pq_ref, fold_ref, *, L):
    scale = float(1.0 / np.sqrt(L * FNET_GDIM))
    half = L // 2
    tr = min(L, FOURIER_TR)
    for r0 in range(0, L, tr):
        pq_ref[r0:r0 + tr, :] = _dot(f_ref[0, r0:r0 + tr, :], ccs_ref[...])
    pq_ref[L:L + FOURIER_PAD, :] = jnp.zeros((FOURIER_PAD, 2 * BRANCH_W), F32)
    blk = min(half, LANES)
    ri = lax.broadcasted_iota(jnp.int32, (blk, blk), 0)
    ci = lax.broadcasted_iota(jnp.int32, (blk, blk), 1)
    exchange = jnp.where(ri + ci == blk - 1, 1.0, 0.0).astype(BF)
    for i in range(half // blk):
        lo = L - blk * (i + 1) + 1
        partner = _dot(exchange, pq_ref[lo:lo + blk, :].astype(BF))
        own = pq_ref[blk * i:blk * (i + 1), :]
        fold_ref[blk * i:blk * (i + 1), :] = (own[:, :BRANCH_W] + partner[:, :BRANCH_W]).astype(BF)
        fold_ref[half + blk * i:half + blk * (i + 1), :] = (own[:, BRANCH_W:] - partner[:, BRANCH_W:]).astype(BF)
    mid = pq_ref[half:half + 1, 0:BRANCH_W]
    for r0 in range(0, L, tr):
        o = _dot(csl_ref[r0:r0 + tr, :], fold_ref[...])
        row = lax.broadcasted_iota(jnp.int32, (tr, 1), 0) + r0
        sign = (1 - 2 * (row % 2)).astype(F32)
        o_ref[0, r0:r0 + tr, :] = ((o + sign * mid) * scale).astype(BF)


def _fourier_call(z, ccs, csl):
    B, L, _ = z.shape
    return pl.pallas_call(
        functools.partial(_fourier_kernel, L=L),
        out_shape=jax.ShapeDtypeStruct((B, L, BRANCH_W), BF),
        grid=(B,),
        in_specs=[
            pl.BlockSpec((1, L, BRANCH_W), lambda b: (b, 0, COL_F // BRANCH_W)),
            pl.BlockSpec((BRANCH_W, 2 * BRANCH_W), lambda b: (0, 0)),
            pl.BlockSpec((L, L), lambda b: (0, 0), pipeline_mode=pl.Buffered(1)),
        ],
        out_specs=pl.BlockSpec((1, L, BRANCH_W), lambda b: (b, 0, 0)),
        scratch_shapes=[
            pltpu.VMEM((L + FOURIER_PAD, 2 * BRANCH_W), F32),
            pltpu.VMEM((L, BRANCH_W), BF),
        ],
        compiler_params=_params("arbitrary"),
        name="fourier_mix",
    )(z, ccs, csl)


def _head(ref, rows, h):
    return ref[0, rows, h * HEAD_DIM:(h + 1) * HEAD_DIM]


def _ctx_attn_kernel(sink_ref, qc_ref, kvc_ref, qd_ref, kd_ref, vd_ref, oc_ref, od_ref):
    rows = slice(None)
    group = WIN_Q_HEADS // WIN_KV_HEADS
    for h in range(WIN_Q_HEADS):
        kvh = h // group
        s = _dot_nt(_head(qc_ref, rows, h), _head(kvc_ref, rows, kvh)) * ATTN_SCALE
        sink = sink_ref[h]
        m = jnp.maximum(jnp.max(s, axis=-1, keepdims=True), sink)
        p = jnp.exp(s - m)
        den = jnp.sum(p, axis=-1, keepdims=True) + jnp.exp(sink - m)
        o = _dot(p.astype(BF), _head(kvc_ref, rows, WIN_KV_HEADS + kvh)) / den
        oc_ref[0, :, h * HEAD_DIM:(h + 1) * HEAD_DIM] = o.astype(BF)
    for h in range(NA_HEADS):
        s = _dot_nt(_head(qd_ref, rows, h), _head(kd_ref, rows, h)) * ATTN_SCALE
        m = jnp.max(s, axis=-1, keepdims=True)
        p = jnp.exp(s - m)
        den = jnp.sum(p, axis=-1, keepdims=True)
        o = _dot(p.astype(BF), _head(vd_ref, rows, h)) / den
        od_ref[0, :, h * HEAD_DIM:(h + 1) * HEAD_DIM] = o.astype(BF)


def _zcol_spec(L, col):
    return pl.BlockSpec((1, L, BRANCH_W), lambda b: (b, 0, col // BRANCH_W))


def _ctx_attn_call(z, sink):
    B, L, _ = z.shape
    out = jax.ShapeDtypeStruct((B, L, BRANCH_W), BF)
    ospec = pl.BlockSpec((1, L, BRANCH_W), lambda b: (b, 0, 0))
    return pl.pallas_call(
        _ctx_attn_kernel,
        out_shape=[out, out],
        grid=(B,),
        in_specs=[
            pl.BlockSpec(memory_space=pltpu.SMEM),
            _zcol_spec(L, COL_CQ), _zcol_spec(L, COL_CKV),
            _zcol_spec(L, COL_DQ), _zcol_spec(L, COL_DK), _zcol_spec(L, COL_DV),
        ],
        out_specs=[ospec, ospec],
        compiler_params=_params("arbitrary"),
        name="ctx_attention",
    )(sink, z, z, z, z, z)


PAIR = 2 * HEAD_DIM


def _one_head(x, lane, g):
    return jnp.where(lane // HEAD_DIM == g, x, jnp.zeros_like(x))


def _win_attn_kernel(sink_ref, q_ref, kv_ref, ck_ref, cv_ref, o_ref, *, L):
    nb = L // WIN_BLOCK
    group = WIN_Q_HEADS // WIN_KV_HEADS
    m_rows = group * WIN_BLOCK
    assert WIN_BLOCK == LANES
    row = lax.broadcasted_iota(jnp.int32, (m_rows, WIN_BLOCK), 0)
    a = row % WIN_BLOCK
    j = lax.broadcasted_iota(jnp.int32, (m_rows, WIN_BLOCK), 1)
    row1 = lax.broadcasted_iota(jnp.int32, (m_rows, 1), 0)
    for kvh in range(WIN_KV_HEADS):
        ctx_k = ck_ref[0, 0, kvh].astype(BF)
        ctx_v = cv_ref[0, 0, kvh].astype(BF)
        sink = jnp.zeros((m_rows, 1), F32)
        for g in range(group):
            sink = jnp.where(row1 // WIN_BLOCK == g, sink_ref[kvh * group + g], sink)

        def body(n, carry, kvh=kvh, ctx_k=ctx_k, ctx_v=ctx_v, sink=sink):
            r0 = pl.multiple_of(n * WIN_BLOCK, WIN_BLOCK)
            q = jnp.concatenate(
                [_head(q_ref, pl.ds(r0, WIN_BLOCK), kvh * group + g) for g in range(group)], axis=0)
            q = q * ATTN_SCALE
            s_ctx = _dot(q, ctx_k)
            m_el = jnp.maximum(s_ctx[:, :LANES], s_ctx[:, LANES:])
            scores = []
            vals = []
            for dj in (-1, 0, 1):
                nk = jnp.clip(n + dj, 0, nb - 1)
                rk = pl.multiple_of(nk * WIN_BLOCK, WIN_BLOCK)
                s = _dot_nt(q, _head(kv_ref, pl.ds(rk, WIN_BLOCK), kvh))
                if dj == -1:
                    s = jnp.where((j >= a) & (n >= 1), s, NEG)
                elif dj == 1:
                    s = jnp.where((j <= a) & (n <= nb - 2), s, NEG)
                m_el = jnp.maximum(m_el, s)
                scores.append(s)
                vals.append(_head(kv_ref, pl.ds(rk, WIN_BLOCK), WIN_KV_HEADS + kvh))
            m = jnp.maximum(jnp.max(m_el, axis=-1, keepdims=True), sink)
            p_ctx = jnp.exp(s_ctx - m)
            den_el = p_ctx[:, :LANES] + p_ctx[:, LANES:]
            o = _dot_nt(p_ctx.astype(BF), ctx_v)
            for s, v in zip(scores, vals):
                p = jnp.exp(s - m)
                den_el = den_el + p
                o = o + _dot(p.astype(BF), v)
            den = jnp.sum(den_el, axis=-1, keepdims=True) + jnp.exp(sink - m)
            o = (o / den).astype(BF)
            for g in range(group):
                h = kvh * group + g
                o_ref[0, pl.ds(r0, WIN_BLOCK), h * HEAD_DIM:(h + 1) * HEAD_DIM] = (
                    o[g * WIN_BLOCK:(g + 1) * WIN_BLOCK])
            return carry

        lax.fori_loop(0, nb, body, 0, unroll=2)


def _win_attn_call(z, cache_k, cache_v, sink, layer):
    B, L, _ = z.shape
    P = cache_k.shape[4]
    assert P == 2 * LANES
    cspec = pl.BlockSpec((1, 1, WIN_KV_HEADS, HEAD_DIM, P), lambda b: (b, layer, 0, 0, 0))
    return pl.pallas_call(
        functools.partial(_win_attn_kernel, L=L),
        out_shape=jax.ShapeDtypeStruct((B, L, BRANCH_W), BF),
        grid=(B,),
        in_specs=[
            pl.BlockSpec(memory_space=pltpu.SMEM),
            _zcol_spec(L, COL_CQ), _zcol_spec(L, COL_CKV), cspec, cspec,
        ],
        out_specs=pl.BlockSpec((1, L, BRANCH_W), lambda b: (b, 0, 0)),
        compiler_params=_params("arbitrary"),
        name="window_attention",
    )(sink, z, z, cache_k, cache_v)


N_ROW_OFF = 2 * NA_ROWS - 1
N_COL_OFF = 2 * NA_COLS - 1


NA_QROWS = 4
NA_WIN_ROWS = NA_ROWS + NA_QROWS
NA_PATTERNS = 3


def _na_window_start(r0, rows):
    return min(max(r0 - NA_ROWS // 2, 0), rows - NA_WIN_ROWS)


def _na_pattern_blocks(p, rows):
    r0 = (0, 2 * NA_QROWS, rows - NA_QROWS)[p]
    ws = _na_window_start(r0, rows)
    blocks = {}
    for i in range(NA_QROWS):
        r = r0 + i
        start = min(max(r - NA_ROWS // 2, 0), rows - NA_ROWS)
        for kk in range(NA_WIN_ROWS):
            rel = ws + kk - start
            blocks[i, kk] = ws + kk - r + (NA_ROWS - 1) if 0 <= rel < NA_ROWS else None
    return blocks


def _na_bias_kernel(rb_ref, o_ref, *, rows):
    h = pl.program_id(0)
    cq = lax.broadcasted_iota(jnp.int32, (GRID_W, GRID_W), 0)
    ck = lax.broadcasted_iota(jnp.int32, (GRID_W, GRID_W), 1)
    col_start = jnp.clip(cq - NA_COLS // 2, 0, GRID_W - NA_COLS)
    col_ok = (ck >= col_start) & (ck < col_start + NA_COLS)
    col_off = jnp.clip(ck - cq + (NA_COLS - 1), 0, N_COL_OFF - 1)
    tiles = []
    for ro in range(N_ROW_OFF):
        t = jnp.zeros((GRID_W, GRID_W), F32)
        for co in range(N_COL_OFF):
            t = jnp.where(col_off == co, rb_ref[(h * N_ROW_OFF + ro) * N_COL_OFF + co], t)
        tiles.append(jnp.where(col_ok, t, NEG))
    masked = jnp.full((GRID_W, GRID_W), NEG, F32)
    for p in range(NA_PATTERNS):
        for (i, kk), ro in _na_pattern_blocks(p, rows).items():
            o_ref[0, p, i * GRID_W:(i + 1) * GRID_W, kk * GRID_W:(kk + 1) * GRID_W] = (
                masked if ro is None else tiles[ro])


def _na_bias_call(na_bias, rows):
    shape = (NA_HEADS, NA_PATTERNS, NA_QROWS * GRID_W, NA_WIN_ROWS * GRID_W)
    return pl.pallas_call(
        functools.partial(_na_bias_kernel, rows=rows),
        out_shape=jax.ShapeDtypeStruct(shape, F32),
        grid=(NA_HEADS,),
        in_specs=[pl.BlockSpec(memory_space=pltpu.SMEM)],
        out_specs=pl.BlockSpec((1,) + shape[1:], lambda h: (h, 0, 0, 0)),
        compiler_params=_params("arbitrary"),
        name="na_bias_table",
    )(na_bias.reshape(-1))


def _na_attn_kernel(q_ref, k_ref, v_ref, ck_ref, cv_ref, bt_ref, o_ref, *, L):
    rows = L // GRID_W
    groups = rows // NA_QROWS
    lane = lax.broadcasted_iota(jnp.int32, (NA_QROWS * GRID_W, PAIR), 1)
    for hp in range(NA_HEADS * HEAD_DIM // PAIR):
        heads = (2 * hp, 2 * hp + 1)
        lanes = slice(hp * PAIR, (hp + 1) * PAIR)
        ctx_k = jnp.concatenate([ck_ref[0, 0, h].astype(BF) for h in heads], axis=0)
        ctx_v = jnp.concatenate([cv_ref[0, 0, h].astype(BF) for h in heads], axis=0)

        def body(qi, carry, heads=heads, lanes=lanes, ctx_k=ctx_k, ctx_v=ctx_v):
            r0 = qi * NA_QROWS
            ws = jnp.clip(r0 - NA_ROWS // 2, 0, rows - NA_WIN_ROWS)
            pat = jnp.where(qi == 0, 0, jnp.where(qi == groups - 1, 2, 1))
            rq = pl.ds(pl.multiple_of(r0 * GRID_W, NA_QROWS * GRID_W), NA_QROWS * GRID_W)
            rk = pl.ds(pl.multiple_of(ws * GRID_W, GRID_W), NA_WIN_ROWS * GRID_W)
            q2 = q_ref[0, rq, lanes] * ATTN_SCALE
            k2 = k_ref[0, rk, lanes]
            v2 = v_ref[0, rk, lanes]
            outs = []
            for g, h in enumerate(heads):
                qg = _one_head(q2, lane, g)
                s = _dot_nt(qg, k2) + bt_ref[h, pat]
                s_ctx = _dot(qg, ctx_k)
                m = jnp.maximum(jnp.max(s, axis=-1, keepdims=True), jnp.max(s_ctx, axis=-1, keepdims=True))
                p = jnp.exp(s - m)
                p_ctx = jnp.exp(s_ctx - m)
                den = jnp.sum(p, axis=-1, keepdims=True) + jnp.sum(p_ctx, axis=-1, keepdims=True)
                outs.append((_dot(p.astype(BF), v2) + _dot_nt(p_ctx.astype(BF), ctx_v)) / den)
            o_ref[0, rq, lanes] = jnp.where(lane // HEAD_DIM == 0, outs[0], outs[1]).astype(BF)
            return carry

        lax.fori_loop(0, groups, body, 0)


def _na_attn_call(z, cache_k, cache_v, bias_table, layer):
    B, L, _ = z.shape
    P = cache_k.shape[4]
    assert L // GRID_W >= NA_WIN_ROWS + NA_QROWS and (L // GRID_W) % NA_QROWS == 0
    cspec = pl.BlockSpec((1, 1, NA_HEADS, HEAD_DIM, P), lambda b: (b, layer, 0, 0, 0))
    return pl.pallas_call(
        functools.partial(_na_attn_kernel, L=L),
        out_shape=jax.ShapeDtypeStruct((B, L, BRANCH_W), BF),
        grid=(B,),
        in_specs=[
            _zcol_spec(L, COL_DQ), _zcol_spec(L, COL_DK), _zcol_spec(L, COL_DV), cspec, cspec,
            pl.BlockSpec(bias_table.shape, lambda b: (0, 0, 0, 0), pipeline_mode=pl.Buffered(1)),
        ],
        out_specs=pl.BlockSpec((1, L, BRANCH_W), lambda b: (b, 0, 0)),
        compiler_params=_params("arbitrary"),
        name="neighbourhood_attention",
    )(z, z, z, cache_k, cache_v, bias_table)


def _merge_kernel(x_ref, g_ref, sc_ref, sh_ref, gate_ref, ba_ref, bb_ref, bc_ref, bd_ref,
                  wg0_ref, wg1_ref, wb_ref, wo_ref, o_ref):
    x = x_ref[0]
    h = _norm_mod(x, g_ref[...], sc_ref[0], sh_ref[0]).astype(BF)
    merged = jnp.zeros(x.shape, F32)
    per_ref = MIX_COLS // D_MODEL
    for n, br_ref in enumerate((ba_ref, bb_ref, bc_ref, bd_ref)):
        wg_ref = (wg0_ref, wg1_ref)[n // per_ref]
        c0 = (n % per_ref) * D_MODEL
        gate = jax.nn.sigmoid(_dot(h, wg_ref[0, :, c0:c0 + D_MODEL]))
        merged = merged + gate * _dot(br_ref[0], wb_ref[0, n])
    o_ref[0] = x + gate_ref[0] * _dot(merged.astype(BF), wo_ref[0])


def _merge_call(x, norm_g, mod, branches, w_in_bf, w_branch, w_out, layer, tm):
    B, L, _ = x.shape
    bm = mod.shape[0]
    xspec = pl.BlockSpec((1, tm, D_MODEL), lambda b, i: (b, i, 0))
    bspec = pl.BlockSpec((1, tm, BRANCH_W), lambda b, i: (b, i, 0))
    return pl.pallas_call(
        _merge_kernel,
        out_shape=jax.ShapeDtypeStruct((B, L, D_MODEL), F32),
        grid=(B, L // tm),
        in_specs=[
            xspec,
            pl.BlockSpec((1, D_MODEL), lambda b, i: (0, 0)),
            _mod_spec(bm, 1, 2), _mod_spec(bm, 0, 2), _mod_spec(bm, 2, 2),
            bspec, bspec, bspec, bspec,
            pl.BlockSpec((1, D_MODEL, MIX_COLS), lambda b, i: (layer, 0, 1)),
            pl.BlockSpec((1, D_MODEL, MIX_COLS), lambda b, i: (layer, 0, 2)),
            pl.BlockSpec((1, N_BRANCH, BRANCH_W, D_MODEL), lambda b, i: (layer, 0, 0, 0)),
            pl.BlockSpec((1, D_MODEL, D_MODEL), lambda b, i: (layer, 0, 0)),
        ],
        out_specs=xspec,
        compiler_params=_params("arbitrary", "arbitrary"),
        name="merge_out_proj",
    )(x, norm_g, mod, mod, mod, *branches, w_in_bf, w_in_bf, w_branch, w_out)


ROUTE_CHUNK = 256
GATHER_ROWS = 512
SIGN_BIT = 31
SEARCH_BITS = 4
SEARCH_TOP_SHIFT = ((SIGN_BIT - 1) // SEARCH_BITS) * SEARCH_BITS


def _prefix_excl(mask_ref, out_ref, tri, L):
    totals = [jnp.sum(mask_ref[:, c0:c0 + LANES], axis=-1, keepdims=True) for c0 in range(0, L, LANES)]
    run = jnp.zeros((N_EXPERTS, 1), F32)
    for i, c0 in enumerate(range(0, L, LANES)):
        out_ref[:, c0:c0 + LANES] = _dot(mask_ref[:, c0:c0 + LANES].astype(BF), tri) + run
        run = run + totals[i]


def _route_kernel(x_ref, g_ref, sc_ref, sh_ref, wh_ref, wl_ref, tri_ref,
                  h_ref, gate_ref, slot_ref, slot_t_ref, off_ref, lt_ref, msk_ref, pre_ref, *, L, cap):
    g = g_ref[...]
    sc = sc_ref[0]
    sh = sh_ref[0]
    wh = wh_ref[...]
    wl = wl_ref[...]
    rc = min(L, ROUTE_CHUNK)
    for r0 in range(0, L, rc):
        h = _norm_mod(x_ref[0, r0:r0 + rc, :], g, sc, sh)
        h_hi = h.astype(BF)
        h_ref[0, r0:r0 + rc, :] = h_hi
        h_lo = (h - h_hi.astype(F32)).astype(BF)
        lt_ref[:, r0:r0 + rc] = _dot_nt(wh, h_hi) + (_dot_nt(wh, h_lo) + _dot_nt(wl, h_hi))
    lt = lt_ref[...]
    e = jnp.exp(lt - jnp.max(lt, axis=0, keepdims=True))
    aff = e / jnp.sum(e, axis=0, keepdims=True)
    thr = jnp.zeros((N_EXPERTS, 1), jnp.int32)
    for shift in range(SEARCH_TOP_SHIFT, -1, -SEARCH_BITS):
        n_digits = min(1 << SEARCH_BITS, 1 << (SIGN_BIT - shift))
        digit = jnp.zeros((N_EXPERTS, 1), jnp.int32)
        for d in range(1, n_digits):
            cand = pltpu.bitcast(thr | (d << shift), F32)
            cnt = jnp.sum((aff >= cand).astype(F32), axis=-1, keepdims=True)
            digit = digit + (cnt >= cap).astype(jnp.int32)
        thr = thr | (digit << shift)
    thr_f = pltpu.bitcast(thr, F32)
    gt = aff > thr_f
    eq = aff == thr_f
    need = cap - jnp.sum(gt.astype(F32), axis=-1, keepdims=True)
    tri = tri_ref[...]
    msk_ref[...] = eq.astype(F32)
    _prefix_excl(msk_ref, pre_ref, tri, L)
    sel = gt | (eq & (pre_ref[...] < need))
    msk_ref[...] = sel.astype(F32)
    _prefix_excl(msk_ref, pre_ref, tri, L)
    off_ref[0] = jnp.concatenate([pre_ref[:, c:c + 1] for c in range(0, L, min(L, COMBINE_TL))], axis=1)
    slot = jnp.where(sel, pre_ref[...], -1.0)
    slot_ref[0] = slot
    slot_t_ref[0] = slot.T
    gate_ref[0] = jnp.where(sel, aff, 0.0)


def _route_call(x, norm_g, mod, wr_hi, wr_lo):
    B, L, _ = x.shape
    bm = mod.shape[0]
    cap = EC_CAPACITY * L // N_EXPERTS
    n_tiles = L // min(L, COMBINE_TL)
    tri = jnp.asarray(np.triu(np.ones((LANES, LANES)), 1).astype(ml_dtypes.bfloat16))
    wspec = pl.BlockSpec((N_EXPERTS, D_MODEL), lambda b: (0, 0))
    return pl.pallas_call(
        functools.partial(_route_kernel, L=L, cap=cap),
        out_shape=[
            jax.ShapeDtypeStruct((B, L, D_MODEL), BF),
            jax.ShapeDtypeStruct((B, N_EXPERTS, L), F32),
            jax.ShapeDtypeStruct((B, N_EXPERTS, L), F32),
            jax.ShapeDtypeStruct((B, L, N_EXPERTS), F32),
            jax.ShapeDtypeStruct((B, N_EXPERTS, n_tiles), F32),
        ],
        grid=(B,),
        in_specs=[
            pl.BlockSpec((1, L, D_MODEL), lambda b: (b, 0, 0)),
            pl.BlockSpec((1, D_MODEL), lambda b: (0, 0)),
            _mod_spec(bm, 4, 1), _mod_spec(bm, 3, 1),
            wspec, wspec,
            pl.BlockSpec((LANES, LANES), lambda b: (0, 0)),
        ],
        out_specs=[
            pl.BlockSpec((1, L, D_MODEL), lambda b: (b, 0, 0)),
            pl.BlockSpec((1, N_EXPERTS, L), lambda b: (b, 0, 0)),
            pl.BlockSpec((1, N_EXPERTS, L), lambda b: (b, 0, 0)),
            pl.BlockSpec((1, L, N_EXPERTS), lambda b: (b, 0, 0)),
            pl.BlockSpec((1, N_EXPERTS, n_tiles), lambda b: (b, 0, 0)),
        ],
        scratch_shapes=[
            pltpu.VMEM((N_EXPERTS, L), F32),
            pltpu.VMEM((N_EXPERTS, L), F32),
            pltpu.VMEM((N_EXPERTS, L), F32),
        ],
        compiler_params=_params("arbitrary"),
        name="ec_route",
    )(x, norm_g, mod, mod, wr_hi, wr_lo, tri)


COMBINE_TL = 256
COMBINE_WIN = 64
BF_TILE = 16


def _list_windows(off_ref, cap):
    n_tiles = pl.num_programs(1)
    base = (pl.program_id(0) * n_tiles + pl.program_id(1)) * N_EXPERTS
    starts = []
    fits = None
    for ex in range(N_EXPERTS):
        first = off_ref[base + ex]
        end = jnp.where(pl.program_id(1) == n_tiles - 1, cap, off_ref[base + N_EXPERTS + ex])
        start = jnp.minimum((first // BF_TILE) * BF_TILE, cap - COMBINE_WIN)
        ok = end <= start + COMBINE_WIN
        fits = ok if fits is None else (fits & ok)
        starts.append(start)
    return starts, fits


def _gather_kernel(off_ref, slot_ref, gate_ref, h_ref, xs_ref, gs_ref, *, cap):
    h = h_ref[0]
    tl = h.shape[0]

    @pl.when(pl.program_id(1) == 0)
    def _():
        xs_ref[...] = jnp.zeros(xs_ref.shape, BF)
        gs_ref[...] = jnp.zeros(gs_ref.shape, F32)

    def select(ex, first, rows):
        pos = lax.broadcasted_iota(jnp.int32, (rows, tl), 0).astype(F32)
        hit = slot_ref[0, ex:ex + 1, :] - first == pos
        gate = jnp.sum(jnp.where(hit, gate_ref[0, ex:ex + 1, :], 0.0), axis=-1, keepdims=True)
        return jnp.where(hit, 1.0, 0.0).astype(BF), jnp.broadcast_to(gate, (rows, LANES))

    def gather_all():
        ge = max(1, min(N_EXPERTS, GATHER_ROWS // cap))
        for e0 in range(0, N_EXPERTS, ge):
            picked = [select(ex, 0.0, cap) for ex in range(e0, e0 + ge)]
            rows = _dot(jnp.concatenate([p[0] for p in picked], axis=0), h).astype(BF)
            for j in range(ge):
                xs_ref[e0 + j, 0] += rows[j * cap:(j + 1) * cap]
                gs_ref[e0 + j, 0] += picked[j][1]

    if cap <= COMBINE_WIN:
        gather_all()
        return

    starts, fits = _list_windows(off_ref, cap)

    @pl.when(fits)
    def _():
        picked = [select(ex, starts[ex].astype(F32), COMBINE_WIN) for ex in range(N_EXPERTS)]
        rows = _dot(jnp.concatenate([p[0] for p in picked], axis=0), h).astype(BF)
        for ex in range(N_EXPERTS):
            win = pl.ds(pl.multiple_of(starts[ex], BF_TILE), COMBINE_WIN)
            xs_ref[ex, 0, win, :] += rows[ex * COMBINE_WIN:(ex + 1) * COMBINE_WIN]
            gs_ref[ex, 0, win, :] += picked[ex][1]

    @pl.when(jnp.logical_not(fits))
    def _():
        gather_all()


def _gather_call(offsets, slot, gate_rows, h2):
    B, L, _ = h2.shape
    cap = EC_CAPACITY * L // N_EXPERTS
    tl = min(L, COMBINE_TL)
    assert cap <= COMBINE_WIN or (cap % BF_TILE == 0 and COMBINE_WIN % BF_TILE == 0)
    return pl.pallas_call(
        functools.partial(_gather_kernel, cap=cap),
        out_shape=[
            jax.ShapeDtypeStruct((N_EXPERTS, B, cap, D_MODEL), BF),
            jax.ShapeDtypeStruct((N_EXPERTS, B, cap, LANES), F32),
        ],
        grid_spec=pltpu.PrefetchScalarGridSpec(
            num_scalar_prefetch=1,
            grid=(B, L // tl),
            in_specs=[
                pl.BlockSpec((1, N_EXPERTS, tl), lambda b, i, off: (b, 0, i)),
                pl.BlockSpec((1, N_EXPERTS, tl), lambda b, i, off: (b, 0, i)),
                pl.BlockSpec((1, tl, D_MODEL), lambda b, i, off: (b, i, 0)),
            ],
            out_specs=[
                pl.BlockSpec((N_EXPERTS, 1, cap, D_MODEL), lambda b, i, off: (0, b, 0, 0)),
                pl.BlockSpec((N_EXPERTS, 1, cap, LANES), lambda b, i, off: (0, b, 0, 0)),
            ],
        ),
        compiler_params=_params("arbitrary", "arbitrary"),
        name="ec_gather",
    )(offsets, slot, gate_rows, h2)


EXPERT_TR = 512


def _expert_kernel(xa_ref, ga_ref, xb_ref, gb_ref, wg_ref, wu_ref, wd_ref, ya_ref, yb_ref,
                   wgb_ref, wub_ref, wdb_ref, *, tiles_a):
    j = pl.program_id(1)

    @pl.when(j == 0)
    def _():
        wgb_ref[...] = wg_ref[0, 0].astype(BF)
        wub_ref[...] = wu_ref[0, 0].astype(BF)
        wdb_ref[...] = wd_ref[0, 0].astype(BF)

    def ffn(x_ref, g_ref, y_ref):
        x = x_ref[0]
        a = _dot(x, wgb_ref[...])
        u = _dot(x, wub_ref[...])
        y = _dot((_silu(a) * u).astype(BF), wdb_ref[...])
        y_ref[0] = (y * g_ref[0, :, 0:1]).astype(BF)

    @pl.when(j < tiles_a)
    def _():
        ffn(xa_ref, ga_ref, ya_ref)

    @pl.when(j >= tiles_a)
    def _():
        ffn(xb_ref, gb_ref, yb_ref)


def _expert_call(xs_a, gs_a, xs_b, gs_b, w_gate, w_up, w_down, layer):
    E, ra, _ = xs_a.shape
    rb = xs_b.shape[1]
    tr = math.gcd(math.gcd(ra, rb), EXPERT_TR)
    tiles_a, tiles_b = ra // tr, rb // tr
    amap = lambda e, j: (e, jnp.minimum(j, tiles_a - 1), 0)
    bmap = lambda e, j: (e, jnp.maximum(j - tiles_a, 0), 0)
    wspec_in = pl.BlockSpec((1, 1, D_MODEL, EXPERT_FF), lambda e, j: (layer, e, 0, 0))
    wspec_out = pl.BlockSpec((1, 1, EXPERT_FF, D_MODEL), lambda e, j: (layer, e, 0, 0))
    return pl.pallas_call(
        functools.partial(_expert_kernel, tiles_a=tiles_a),
        out_shape=[jax.ShapeDtypeStruct((E, ra, D_MODEL), BF), jax.ShapeDtypeStruct((E, rb, D_MODEL), BF)],
        grid=(E, tiles_a + tiles_b),
        in_specs=[
            pl.BlockSpec((1, tr, D_MODEL), amap), pl.BlockSpec((1, tr, LANES), amap),
            pl.BlockSpec((1, tr, D_MODEL), bmap), pl.BlockSpec((1, tr, LANES), bmap),
            wspec_in, wspec_in, wspec_out,
        ],
        out_specs=[pl.BlockSpec((1, tr, D_MODEL), amap), pl.BlockSpec((1, tr, D_MODEL), bmap)],
        scratch_shapes=[
            pltpu.VMEM((D_MODEL, EXPERT_FF), BF),
            pltpu.VMEM((D_MODEL, EXPERT_FF), BF),
            pltpu.VMEM((EXPERT_FF, D_MODEL), BF),
        ],
        compiler_params=_params("arbitrary", "arbitrary"),
        name="ec_experts",
    )(xs_a, gs_a, xs_b, gs_b, w_gate, w_up, w_down)


def _combine_kernel(off_ref, st_ref, y_ref, x_ref, gate_ref, fg_ref, o_ref, *, cap, final):
    st = st_ref[0]
    tl = st.shape[0]

    def finish(scattered):
        v = x_ref[0] + gate_ref[0] * scattered
        if final:
            v = v * lax.rsqrt(jnp.mean(v * v, axis=-1, keepdims=True) + EPS) * fg_ref[...]
        o_ref[0] = v

    def onehot(width, starts):
        n = N_EXPERTS * width
        owner = lax.broadcasted_iota(jnp.int32, (N_EXPERTS, n), 1) // width
        spread = jnp.where(owner == lax.broadcasted_iota(jnp.int32, (N_EXPERTS, n), 0), 1.0, 0.0).astype(BF)
        slot = _dot(st.astype(BF), spread)
        lane = lax.broadcasted_iota(jnp.int32, (1, n), 1)
        want = (lane % width).astype(F32)
        if starts is not None:
            for ex in range(N_EXPERTS):
                want = want + jnp.where(lane // width == ex, starts[ex].astype(F32), 0.0)
        return jnp.where(slot == want, 1.0, 0.0).astype(BF)

    def scatter_all():
        finish(_dot(onehot(cap, None), y_ref[:, 0].reshape(N_EXPERTS * cap, D_MODEL)))

    if cap <= COMBINE_WIN:
        scatter_all()
        return

    starts, fits = _list_windows(off_ref, cap)

    @pl.when(fits)
    def _():
        rows = [y_ref[ex, 0, pl.ds(pl.multiple_of(starts[ex], BF_TILE), COMBINE_WIN), :]
                for ex in range(N_EXPERTS)]
        finish(_dot(onehot(COMBINE_WIN, starts), jnp.concatenate(rows, axis=0)))

    @pl.when(jnp.logical_not(fits))
    def _():
        scatter_all()


def _combine_call(slot_t, offsets, y, x, mod, final_g, final):
    B, L, _ = x.shape
    bm = mod.shape[0]
    cap = y.shape[2]
    tl = min(L, COMBINE_TL)
    assert cap <= COMBINE_WIN or (cap % BF_TILE == 0 and COMBINE_WIN % BF_TILE == 0)
    assert cap <= 256
    gate_map =(lambda b, i, off: (b, 0, 5)) if bm > 1 else (lambda b, i, off: (0, 0, 5))
    return pl.pallas_call(
        functools.partial(_combine_kernel, cap=cap, final=final),
        out_shape=jax.ShapeDtypeStruct((B, L, D_MODEL), F32),
        grid_spec=pltpu.PrefetchScalarGridSpec(
            num_scalar_prefetch=1,
            grid=(B, L // tl),
            in_specs=[
                pl.BlockSpec((1, tl, N_EXPERTS), lambda b, i, off: (b, i, 0)),
                pl.BlockSpec((N_EXPERTS, 1, cap, D_MODEL), lambda b, i, off: (0, b, 0, 0)),
                pl.BlockSpec((1, tl, D_MODEL), lambda b, i, off: (b, i, 0)),
                pl.BlockSpec((1, 1, D_MODEL), gate_map),
                pl.BlockSpec((1, D_MODEL), lambda b, i, off: (0, 0)),
            ],
            out_specs=pl.BlockSpec((1, tl, D_MODEL), lambda b, i, off: (b, i, 0)),
        ),
        compiler_params=_params("arbitrary", "arbitrary"),
        name="ec_combine",
    )(offsets, slot_t, y, x, mod, final_g)


def _mixer_half(x, mod, lw, ctx, tm, kv_prev=None):
    latent = ctx is not None
    layer = lw["layer"]
    res = _inproj_call(x, lw["norm1_g"], mod, lw["w_in"], layer, tm, rope=latent, want_kv=not latent,
                       kv_prev=kv_prev)
    z = res[0]
    br_a = _conv_call(z, lw["conv_w"], lw["conv_b"], lw["conv_ln_g"], lw["conv_ln_b"])
    br_b = _fourier_call(z, *lw["dft"][z.shape[1]])
    if latent:
        br_c = _win_attn_call(z, ctx["win_k"], ctx["win_v"], lw["win_sink"], layer)
        br_d = _na_attn_call(z, ctx["nat_k"], ctx["nat_v"], lw["bias_table"], layer)
    else:
        br_c, br_d = _ctx_attn_call(z, lw["win_sink"])
    x = _merge_call(x, lw["norm1_g"], mod, (br_a, br_b, br_c, br_d),
                    lw["w_in"], lw["w_branch"], lw["w_out"], layer, tm)
    return x, (tuple(res[1:]) if not latent else None)


def kernel(x_prompt, x_sample, cache_win_k, cache_win_v, cache_nat_k, cache_nat_v, c, c_ctx,
           norm1_g, norm2_g, w_ada, b_ada, w_in, conv_w, conv_b, conv_ln_g, conv_ln_b,
           win_sink, na_bias, w_branch, w_out, w_router, w_e_gate, w_e_up, w_e_down, final_norm_g):
    n_dec = c.shape[0]
    cond = jnp.zeros((ADA_ROWS, D_MODEL), F32).at[0].set(c_ctx).at[1:1 + n_dec].set(c)
    mods = _ada_call(cond, w_ada, b_ada)
    final_g = final_norm_g.reshape(1, D_MODEL)
    w_in_bf = _cast_call(w_in)
    w_branch_bf = _cast_call(w_branch)
    w_out_bf = _cast_call(w_out)
    dft = {L: _dft_tables(L) for L in (x_prompt.shape[1], x_sample.shape[1])}
    dim_major = lambda a: jnp.swapaxes(a, -1, -2)
    ctx = {"win_k": dim_major(cache_win_k), "win_v": dim_major(cache_win_v),
           "nat_k": dim_major(cache_nat_k), "nat_v": dim_major(cache_nat_v)}

    xp, xs = x_prompt, x_sample
    caches = None
    for l in range(DEPTH):
        wr_t = w_router[l].T
        wr_hi = wr_t.astype(BF)
        wr_lo = (wr_t - wr_hi.astype(F32)).astype(BF)
        lw = {
            "layer": l,
            "norm1_g": norm1_g[l].reshape(1, D_MODEL),
            "w_in": w_in_bf, "w_branch": w_branch_bf, "w_out": w_out_bf, "dft": dft,
            "conv_w": conv_w[l], "conv_b": conv_b[l],
            "conv_ln_g": conv_ln_g[l], "conv_ln_b": conv_ln_b[l],
            "win_sink": win_sink[l],
            "bias_table": _na_bias_call(na_bias[l], x_sample.shape[1] // GRID_W),
        }
        norm2 = norm2_g[l].reshape(1, D_MODEL)
        final = l == DEPTH - 1
        mod_p = mods[l, 0:1].reshape(1, 1, 6 * D_MODEL)
        mod_s = mods[l, 1:1 + n_dec].reshape(n_dec, 1, 6 * D_MODEL)
        xp, caches = _mixer_half(xp, mod_p, lw, None, tm=256, kv_prev=caches)
        xs, _ = _mixer_half(xs, mod_s, lw, ctx, tm=512)
        smem = lambda off: jnp.pad(off.transpose(0, 2, 1).astype(jnp.int32).reshape(-1), (0, N_EXPERTS))
        h_p, g_p, slot_p, slot_t_p, off_p = _route_call(xp, norm2, mod_p, wr_hi, wr_lo)
        h_s, g_s, slot_s, slot_t_s, off_s = _route_call(xs, norm2, mod_s, wr_hi, wr_lo)
        off_p, off_s = smem(off_p), smem(off_s)
        rows_p, gates_p = _gather_call(off_p, slot_p, g_p, h_p)
        rows_s, gates_s = _gather_call(off_s, slot_s, g_s, h_s)
        flat = lambda a: a.reshape(N_EXPERTS, a.shape[1] * a.shape[2], a.shape[3])
        y_p, y_s = _expert_call(flat(rows_p), flat(gates_p), flat(rows_s), flat(gates_s),
                                w_e_gate, w_e_up, w_e_down, l)
        xp = _combine_call(slot_t_p, off_p, y_p.reshape(rows_p.shape), xp, mod_p, final_g, final)
        xs = _combine_call(slot_t_s, off_s, y_s.reshape(rows_s.shape), xs, mod_s, final_g, final)

    return (xp, xs) + tuple(dim_major(c) for c in caches)
```

```python
import functools
import math

import ml_dtypes
import numpy as np
import jax
import jax.numpy as jnp
from jax import lax
from jax.experimental import pallas as pl
from jax.experimental.pallas import tpu as pltpu

D_MODEL = 1024
DEPTH = 2
GRID_W = 64
BRANCH_W = 256
HEAD_DIM = 64
N_BRANCH = 4
CONV_K = 31
FNET_GROUPS = 4
FNET_GDIM = BRANCH_W // FNET_GROUPS
WIN_Q_HEADS = 4
WIN_KV_HEADS = 2
WIN_BLOCK = 128
NA_HEADS = 4
NA_ROWS = 8
NA_COLS = 16
ROPE_THETA = 10000.0
ATTN_SCALE = HEAD_DIM ** -0.5
N_EXPERTS = 16
EXPERT_FF = 1024
EC_CAPACITY = 2
EPS = 1e-6
NEG = -1e30

MIX_COLS = 2048
GATE_COLS = N_BRANCH * D_MODEL
IN_COLS = MIX_COLS + GATE_COLS
COL_A, COL_F, COL_CQ, COL_CKV, COL_DQ, COL_DK, COL_DV = 0, 512, 768, 1024, 1280, 1536, 1792
ROPE_LO, ROPE_HI = COL_CQ, COL_CKV + WIN_KV_HEADS * HEAD_DIM
KV_COLS = (COL_CKV, COL_CKV + WIN_KV_HEADS * HEAD_DIM, COL_DK, COL_DV)
KV_HEADS = (WIN_KV_HEADS, WIN_KV_HEADS, NA_HEADS, NA_HEADS)

LANES = 128
SUBLANES = 8
VMEM_LIMIT = 56 * 1024 * 1024

BF = jnp.bfloat16
F32 = jnp.float32


def _params(*sem):
    return pltpu.CompilerParams(dimension_semantics=sem, vmem_limit_bytes=VMEM_LIMIT)


def _dot(a, b):
    return jnp.dot(a, b, preferred_element_type=F32)


def _dot_nt(a, b):
    return lax.dot_general(a, b, (((1,), (1,)), ((), ())), preferred_element_type=F32)


def _norm_mod(x, g, sc, sh):
    y = x * lax.rsqrt(jnp.mean(x * x, axis=-1, keepdims=True) + EPS)
    return (y * g) * (1.0 + sc) + sh


def _silu(x):
    return x * jax.nn.sigmoid(x)


CAST_BYTES = 4 * 1024 * 1024


def _cast_kernel(x_ref, o_ref):
    o_ref[...] = x_ref[...].astype(BF)


def _cast_call(x):
    shape = x.shape
    x2 = x.reshape(-1, shape[-1])
    rows, cols = x2.shape
    tr = min(rows, 1 << (max(16, CAST_BYTES // (4 * cols)).bit_length() - 1))
    assert rows % tr == 0
    out = pl.pallas_call(
        _cast_kernel,
        out_shape=jax.ShapeDtypeStruct((rows, cols), BF),
        grid=(rows // tr,),
        in_specs=[pl.BlockSpec((tr, cols), lambda i: (i, 0))],
        out_specs=pl.BlockSpec((tr, cols), lambda i: (i, 0)),
        compiler_params=_params("arbitrary"),
        name="cast_bf16",
    )(x2)
    return out.reshape(shape)


ADA_ROWS = 16
ADA_TN = 1536


def _ada_kernel(c_ref, w_ref, b_ref, o_ref):
    s = _silu(c_ref[...]).astype(BF)
    o_ref[0] = _dot(s, w_ref[0].astype(BF)) + b_ref[0]


def _ada_call(cond, w_ada, b_ada):
    n = 6 * D_MODEL
    return pl.pallas_call(
        _ada_kernel,
        out_shape=jax.ShapeDtypeStruct((DEPTH, ADA_ROWS, n), F32),
        grid=(DEPTH, n // ADA_TN),
        in_specs=[
            pl.BlockSpec((ADA_ROWS, D_MODEL), lambda l, j: (0, 0)),
            pl.BlockSpec((1, D_MODEL, ADA_TN), lambda l, j: (l, 0, j)),
            pl.BlockSpec((1, 1, ADA_TN), lambda l, j: (l, 0, j)),
        ],
        out_specs=pl.BlockSpec((1, ADA_ROWS, ADA_TN), lambda l, j: (l, 0, j)),
        compiler_params=_params("arbitrary", "arbitrary"),
        name="ada_mod",
    )(cond, w_ada, b_ada.reshape(DEPTH, 1, n))


def _mod_spec(bm, chunk, ngrid):
    if ngrid == 1:
        imap = (lambda b: (b, 0, chunk)) if bm > 1 else (lambda b: (0, 0, chunk))
    else:
        imap = (lambda b, i: (b, 0, chunk)) if bm > 1 else (lambda b, i: (0, 0, chunk))
    return pl.BlockSpec((1, 1, D_MODEL), imap)


def _inproj_kernel(*refs, rope, kv_layer, kv_aliased):
    x_ref, g_ref, sc_ref, sh_ref, w_ref = refs[:5]
    pos = 5
    if rope:
        cos_ref, sin_ref = refs[pos:pos + 2]
        pos += 2
    if kv_aliased:
        pos += len(KV_HEADS)
    z_ref = refs[pos]
    h = _norm_mod(x_ref[0], g_ref[...], sc_ref[0], sh_ref[0]).astype(BF)
    z = _dot(h, w_ref[0])
    if kv_layer is not None:
        for kv_ref, c0, heads in zip(refs[pos + 1:pos + 1 + len(KV_HEADS)], KV_COLS, KV_HEADS):
            here = 0 if kv_aliased else kv_layer
            for hd in range(heads):
                kv_ref[0, here, hd] = z[:, c0 + hd * HEAD_DIM:c0 + (hd + 1) * HEAD_DIM].T
            for other in range(kv_ref.shape[1]):
                if other != here:
                    kv_ref[0, other] = jnp.zeros(kv_ref.shape[2:], F32)
    if rope:
        z_ref[0, :, 0:ROPE_LO] = z[:, 0:ROPE_LO].astype(BF)
        lane = lax.broadcasted_iota(jnp.int32, (z.shape[0], LANES), 1)
        first_half = (lane % 32) < 16
        cos = cos_ref[...]
        sin = sin_ref[...]
        for c0 in range(ROPE_LO, ROPE_HI, LANES):
            xc = z[:, c0:c0 + LANES]
            partner = jnp.where(first_half, pltpu.roll(xc, LANES - 16, 1), pltpu.roll(xc, 16, 1))
            z_ref[0, :, c0:c0 + LANES] = (xc * cos + partner * sin).astype(BF)
        z_ref[0, :, ROPE_HI:MIX_COLS] = z[:, ROPE_HI:MIX_COLS].astype(BF)
    else:
        z_ref[0] = z.astype(BF)


def _rope_tables(L):
    t = np.arange(L)
    half = HEAD_DIM // 4
    freqs = 1.0 / (ROPE_THETA ** (np.arange(half, dtype=np.float64) / half))
    ang_r = (t // GRID_W).astype(np.float64)[:, None] * freqs[None, :]
    ang_c = (t % GRID_W).astype(np.float64)[:, None] * freqs[None, :]
    cos_h = np.concatenate([np.cos(ang_r), np.cos(ang_r), np.cos(ang_c), np.cos(ang_c)], axis=1)
    sin_h = np.concatenate([-np.sin(ang_r), np.sin(ang_r), -np.sin(ang_c), np.sin(ang_c)], axis=1)
    reps = LANES // HEAD_DIM
    return (jnp.asarray(np.tile(cos_h, (1, reps)), F32), jnp.asarray(np.tile(sin_h, (1, reps)), F32))


def _inproj_call(x, norm_g, mod, w_in_bf, layer, tm, rope, want_kv, kv_prev=None):
    B, L, _ = x.shape
    bm = mod.shape[0]
    in_specs = [
        pl.BlockSpec((1, tm, D_MODEL), lambda b, i: (b, i, 0)),
        pl.BlockSpec((1, D_MODEL), lambda b, i: (0, 0)),
        _mod_spec(bm, 1, 2),
        _mod_spec(bm, 0, 2),
        pl.BlockSpec((1, D_MODEL, MIX_COLS), lambda b, i: (layer, 0, 0)),
    ]
    args = [x, norm_g, mod, mod, w_in_bf]
    if rope:
        cos, sin = _rope_tables(L)
        in_specs += [pl.BlockSpec((tm, LANES), lambda b, i: (i, 0))] * 2
        args += [cos, sin]
    out_shape = [jax.ShapeDtypeStruct((B, L, MIX_COLS), BF)]
    out_specs = [pl.BlockSpec((1, tm, MIX_COLS), lambda b, i: (b, i, 0))]
    aliases = {}
    if want_kv:
        for n, heads in enumerate(KV_HEADS):
            out_shape.append(jax.ShapeDtypeStruct((B, DEPTH, heads, HEAD_DIM, L), F32))
            if kv_prev is None:
                out_specs.append(pl.BlockSpec((1, DEPTH, heads, HEAD_DIM, tm), lambda b, i: (b, 0, 0, 0, i)))
            else:
                out_specs.append(pl.BlockSpec((1, 1, heads, HEAD_DIM, tm), lambda b, i: (b, layer, 0, 0, i)))
                aliases[len(args)] = 1 + n
                in_specs.append(pl.BlockSpec(memory_space=pl.ANY))
                args.append(kv_prev[n])
    return pl.pallas_call(
        functools.partial(_inproj_kernel, rope=rope, kv_layer=layer if want_kv else None,
                          kv_aliased=kv_prev is not None),
        out_shape=out_shape,
        grid=(B, L // tm),
        in_specs=in_specs,
        out_specs=out_specs,
        input_output_aliases=aliases,
        compiler_params=_params("arbitrary", "arbitrary"),
        name="in_proj",
    )(*args)


CONV_PAD = 16
CONV_TR = 64
CONV_CHUNK = 256


def _conv_kernel(a_ref, w_ref, cb_ref, lg_ref, lb_ref, o_ref, p_ref, u_ref, wb_ref, *, L):
    for k in range(CONV_K):
        wb_ref[k] = jnp.broadcast_to(w_ref[k:k + 1, :], (SUBLANES, BRANCH_W))
    zeros = jnp.zeros((CONV_PAD, BRANCH_W), F32)
    p_ref[0:CONV_PAD, :] = zeros
    p_ref[L + CONV_PAD:L + 2 * CONV_PAD, :] = zeros
    for r0 in range(0, L, CONV_CHUNK):
        a = a_ref[0, r0:r0 + CONV_CHUNK, :].astype(F32)
        p_ref[CONV_PAD + r0:CONV_PAD + r0 + CONV_CHUNK, :] = a[:, :BRANCH_W] * jax.nn.sigmoid(a[:, BRANCH_W:])
    n_u = L + 2 * CONV_PAD - SUBLANES
    for s in range(SUBLANES):
        u_ref[s] = p_ref[s:s + n_u, :]
    cb = cb_ref[...]

    def body(i, carry):
        base = pl.multiple_of(i * CONV_TR, CONV_TR)
        groups = (CONV_TR // SUBLANES, SUBLANES, BRANCH_W)
        acc = jnp.zeros(groups, F32) + cb
        for k in range(CONV_K):
            q, s = divmod(k + CONV_PAD - CONV_K // 2, SUBLANES)
            u = u_ref[s, pl.ds(base + SUBLANES * q, CONV_TR), :]
            acc = acc + wb_ref[k] * u.reshape(groups)
        p_ref[pl.ds(base, CONV_TR), :] = acc.reshape(CONV_TR, BRANCH_W)
        return carry

    lax.fori_loop(0, L // CONV_TR, body, 0)
    lg = lg_ref[...]
    lb = lb_ref[...]
    for r0 in range(0, L, CONV_CHUNK):
        acc = p_ref[r0:r0 + CONV_CHUNK, :]
        mu = jnp.mean(acc, axis=-1, keepdims=True)
        d = acc - mu
        var = jnp.mean(d * d, axis=-1, keepdims=True)
        y = d * lax.rsqrt(var + EPS) * lg + lb
        o_ref[0, r0:r0 + CONV_CHUNK, :] = _silu(y).astype(BF)


def _conv_call(z, conv_w, conv_b, ln_g, ln_b):
    B, L, _ = z.shape
    n_u = L + 2 * CONV_PAD - SUBLANES
    vec = lambda b: (0, 0)
    return pl.pallas_call(
        functools.partial(_conv_kernel, L=L),
        out_shape=jax.ShapeDtypeStruct((B, L, BRANCH_W), BF),
        grid=(B,),
        in_specs=[
            pl.BlockSpec((1, L, 2 * BRANCH_W), lambda b: (b, 0, COL_A // (2 * BRANCH_W))),
            pl.BlockSpec((CONV_K, BRANCH_W), vec),
            pl.BlockSpec((1, BRANCH_W), vec),
            pl.BlockSpec((1, BRANCH_W), vec),
            pl.BlockSpec((1, BRANCH_W), vec),
        ],
        out_specs=pl.BlockSpec((1, L, BRANCH_W), lambda b: (b, 0, 0)),
        scratch_shapes=[
            pltpu.VMEM((L + 2 * CONV_PAD, BRANCH_W), F32),
            pltpu.VMEM((SUBLANES, n_u, BRANCH_W), F32),
            pltpu.VMEM((CONV_K, SUBLANES, BRANCH_W), F32),
        ],
        compiler_params=_params("arbitrary"),
        name="conv_module",
    )(z, conv_w, conv_b.reshape(1, -1), ln_g.reshape(1, -1), ln_b.reshape(1, -1))


def _dft_tables(L):
    c = np.arange(FNET_GDIM)
    ang_c = 2.0 * np.pi * ((c[:, None] * c[None, :]) % FNET_GDIM) / FNET_GDIM
    cc = np.zeros((BRANCH_W, BRANCH_W))
    sc = np.zeros((BRANCH_W, BRANCH_W))
    for g in range(FNET_GROUPS):
        sl = slice(g * FNET_GDIM, (g + 1) * FNET_GDIM)
        cc[sl, sl] = np.cos(ang_c)
        sc[sl, sl] = np.sin(ang_c)
    ccs = np.concatenate([cc, sc], axis=1)
    f = np.arange(L)
    t = np.arange(L // 2)
    ang_l = 2.0 * np.pi * ((f[:, None] * t[None, :]) % L) / L
    csl = np.concatenate([np.cos(ang_l), -np.sin(ang_l)], axis=1)
    return _cast_call(jnp.asarray(ccs, F32)), _cast_call(jnp.asarray(csl, F32))


FOURIER_TR = 512
FOURIER_PAD = 8


def _fourier_kernel(f_ref, ccs_ref, csl_ref, o_ref, pq_ref, fold_ref, *, L):
    scale = float(1.0 / np.sqrt(L * FNET_GDIM))
    half = L // 2
    tr = min(L, FOURIER_TR)
    for r0 in range(0, L, tr):
        pq_ref[r0:r0 + tr, :] = _dot(f_ref[0, r0:r0 + tr, :], ccs_ref[...])
    pq_ref[L:L + FOURIER_PAD, :] = jnp.zeros((FOURIER_PAD, 2 * BRANCH_W), F32)
    blk = min(half, LANES)
    ri = lax.broadcasted_iota(jnp.int32, (blk, blk), 0)
    ci = lax.broadcasted_iota(jnp.int32, (blk, blk), 1)
    exchange = jnp.where(ri + ci == blk - 1, 1.0, 0.0).astype(BF)
    for i in range(half // blk):
        lo = L - blk * (i + 1) + 1
        partner = _dot(exchange, pq_ref[lo:lo + blk, :].astype(BF))
        own = pq_ref[blk * i:blk * (i + 1), :]
        fold_ref[blk * i:blk * (i + 1), :] = (own[:, :BRANCH_W] + partner[:, :BRANCH_W]).astype(BF)
        fold_ref[half + blk * i:half + blk * (i + 1), :] = (own[:, BRANCH_W:] - partner[:, BRANCH_W:]).astype(BF)
    mid = pq_ref[half:half + 1, 0:BRANCH_W]
    for r0 in range(0, L, tr):
        o = _dot(csl_ref[r0:r0 + tr, :], fold_ref[...])
        row = lax.broadcasted_iota(jnp.int32, (tr, 1), 0) + r0
        sign = (1 - 2 * (row % 2)).astype(F32)
        o_ref[0, r0:r0 + tr, :] = ((o + sign * mid) * scale).astype(BF)


def _fourier_call(z, ccs, csl):
    B, L, _ = z.shape
    return pl.pallas_call(
        functools.partial(_fourier_kernel, L=L),
        out_shape=jax.ShapeDtypeStruct((B, L, BRANCH_W), BF),
        grid=(B,),
        in_specs=[
            pl.BlockSpec((1, L, BRANCH_W), lambda b: (b, 0, COL_F // BRANCH_W)),
            pl.BlockSpec((BRANCH_W, 2 * BRANCH_W), lambda b: (0, 0)),
            pl.BlockSpec((L, L), lambda b: (0, 0), pipeline_mode=pl.Buffered(1)),
        ],
        out_specs=pl.BlockSpec((1, L, BRANCH_W), lambda b: (b, 0, 0)),
        scratch_shapes=[
            pltpu.VMEM((L + FOURIER_PAD, 2 * BRANCH_W), F32),
            pltpu.VMEM((L, BRANCH_W), BF),
        ],
        compiler_params=_params("arbitrary"),
        name="fourier_mix",
    )(z, ccs, csl)


def _head(ref, rows, h):
    return ref[0, rows, h * HEAD_DIM:(h + 1) * HEAD_DIM]


def _all_heads_attention(q, k, v, sinks):
    L, width = q.shape
    heads = width // HEAD_DIM
    lane = lax.broadcasted_iota(jnp.int32, (L, width), 1)
    q = q * ATTN_SCALE
    stacked = jnp.concatenate([jnp.where(lane // HEAD_DIM == h, q, jnp.zeros_like(q)) for h in range(heads)],
                              axis=0)
    s = _dot_nt(stacked, k)
    m = jnp.max(s, axis=-1, keepdims=True)
    if sinks is not None:
        row = lax.broadcasted_iota(jnp.int32, (heads * L, 1), 0)
        sink = jnp.zeros((heads * L, 1), F32)
        for h in range(heads):
            sink = jnp.where(row // L == h, sinks[h], sink)
        m = jnp.maximum(m, sink)
    p = jnp.exp(s - m)
    den = jnp.sum(p, axis=-1, keepdims=True)
    if sinks is not None:
        den = den + jnp.exp(sink - m)
    o = _dot(p.astype(BF), v) / den
    out = jnp.zeros((L, width), F32)
    for h in range(heads):
        out = jnp.where(lane // HEAD_DIM == h, o[h * L:(h + 1) * L], out)
    return out.astype(BF)


def _ctx_attn_kernel(sink_ref, qc_ref, kvc_ref, qd_ref, kd_ref, vd_ref, oc_ref, od_ref):
    rows = slice(None)
    group = WIN_Q_HEADS // WIN_KV_HEADS
    k_c = jnp.concatenate([_head(kvc_ref, rows, h // group) for h in range(WIN_Q_HEADS)], axis=1)
    v_c = jnp.concatenate([_head(kvc_ref, rows, WIN_KV_HEADS + h // group) for h in range(WIN_Q_HEADS)], axis=1)
    oc_ref[0] = _all_heads_attention(qc_ref[0], k_c, v_c, [sink_ref[h] for h in range(WIN_Q_HEADS)])
    od_ref[0] = _all_heads_attention(qd_ref[0], kd_ref[0], vd_ref[0], None)


def _zcol_spec(L, col):
    return pl.BlockSpec((1, L, BRANCH_W), lambda b: (b, 0, col // BRANCH_W))


def _ctx_attn_call(z, sink):
    B, L, _ = z.shape
    out = jax.ShapeDtypeStruct((B, L, BRANCH_W), BF)
    ospec = pl.BlockSpec((1, L, BRANCH_W), lambda b: (b, 0, 0))
    return pl.pallas_call(
        _ctx_attn_kernel,
        out_shape=[out, out],
        grid=(B,),
        in_specs=[
            pl.BlockSpec(memory_space=pltpu.SMEM),
            _zcol_spec(L, COL_CQ), _zcol_spec(L, COL_CKV),
            _zcol_spec(L, COL_DQ), _zcol_spec(L, COL_DK), _zcol_spec(L, COL_DV),
        ],
        out_specs=[ospec, ospec],
        compiler_params=_params("arbitrary"),
        name="ctx_attention",
    )(sink, z, z, z, z, z)


PAIR = 2 * HEAD_DIM


def _one_head(x, lane, g):
    return jnp.where(lane // HEAD_DIM == g, x, jnp.zeros_like(x))


def _win_attn_kernel(sink_ref, q_ref, kv_ref, ck_ref, cv_ref, o_ref, *, L):
    nb = L // WIN_BLOCK
    group = WIN_Q_HEADS // WIN_KV_HEADS
    m_rows = group * WIN_BLOCK
    assert WIN_BLOCK == LANES
    row = lax.broadcasted_iota(jnp.int32, (m_rows, WIN_BLOCK), 0)
    a = row % WIN_BLOCK
    j = lax.broadcasted_iota(jnp.int32, (m_rows, WIN_BLOCK), 1)
    row1 = lax.broadcasted_iota(jnp.int32, (m_rows, 1), 0)
    for kvh in range(WIN_KV_HEADS):
        ctx_k = ck_ref[0, 0, kvh].astype(BF)
        ctx_v = cv_ref[0, 0, kvh].astype(BF)
        sink = jnp.zeros((m_rows, 1), F32)
        for g in range(group):
            sink = jnp.where(row1 // WIN_BLOCK == g, sink_ref[kvh * group + g], sink)

        def body(n, carry, kvh=kvh, ctx_k=ctx_k, ctx_v=ctx_v, sink=sink):
            r0 = pl.multiple_of(n * WIN_BLOCK, WIN_BLOCK)
            q = jnp.concatenate(
                [_head(q_ref, pl.ds(r0, WIN_BLOCK), kvh * group + g) for g in range(group)], axis=0)
            q = q * ATTN_SCALE
            s_ctx = _dot(q, ctx_k)
            m_el = jnp.maximum(s_ctx[:, :LANES], s_ctx[:, LANES:])
            scores = []
            vals = []
            for dj in (-1, 0, 1):
                nk = jnp.clip(n + dj, 0, nb - 1)
                rk = pl.multiple_of(nk * WIN_BLOCK, WIN_BLOCK)
                s = _dot_nt(q, _head(kv_ref, pl.ds(rk, WIN_BLOCK), kvh))
                if dj == -1:
                    s = jnp.where((j >= a) & (n >= 1), s, NEG)
                elif dj == 1:
                    s = jnp.where((j <= a) & (n <= nb - 2), s, NEG)
                m_el = jnp.maximum(m_el, s)
                scores.append(s)
                vals.append(_head(kv_ref, pl.ds(rk, WIN_BLOCK), WIN_KV_HEADS + kvh))
            m = jnp.maximum(jnp.max(m_el, axis=-1, keepdims=True), sink)
            p_ctx = jnp.exp(s_ctx - m)
            den_el = p_ctx[:, :LANES] + p_ctx[:, LANES:]
            o = _dot_nt(p_ctx.astype(BF), ctx_v)
            for s, v in zip(scores, vals):
                p = jnp.exp(s - m)
                den_el = den_el + p
                o = o + _dot(p.astype(BF), v)
            den = jnp.sum(den_el, axis=-1, keepdims=True) + jnp.exp(sink - m)
            o = (o / den).astype(BF)
            for g in range(group):
                h = kvh * group + g
                o_ref[0, pl.ds(r0, WIN_BLOCK), h * HEAD_DIM:(h + 1) * HEAD_DIM] = (
                    o[g * WIN_BLOCK:(g + 1) * WIN_BLOCK])
            return carry

        lax.fori_loop(0, nb, body, 0, unroll=2)


def _win_attn_call(z, cache_k, cache_v, sink, layer):
    B, L, _ = z.shape
    P = cache_k.shape[4]
    assert P == 2 * LANES
    cspec = pl.BlockSpec((1, 1, WIN_KV_HEADS, HEAD_DIM, P), lambda b: (b, layer, 0, 0, 0))
    return pl.pallas_call(
        functools.partial(_win_attn_kernel, L=L),
        out_shape=jax.ShapeDtypeStruct((B, L, BRANCH_W), BF),
        grid=(B,),
        in_specs=[
            pl.BlockSpec(memory_space=pltpu.SMEM),
            _zcol_spec(L, COL_CQ), _zcol_spec(L, COL_CKV), cspec, cspec,
        ],
        out_specs=pl.BlockSpec((1, L, BRANCH_W), lambda b: (b, 0, 0)),
        compiler_params=_params("arbitrary"),
        name="window_attention",
    )(sink, z, z, cache_k, cache_v)


N_ROW_OFF = 2 * NA_ROWS - 1
N_COL_OFF = 2 * NA_COLS - 1


NA_QROWS = 4
NA_WIN_ROWS = NA_ROWS + NA_QROWS


def _na_window_start(r0, rows):
    return min(max(r0 - NA_ROWS // 2, 0), rows - NA_WIN_ROWS)


def _na_group_blocks(r0, rows):
    ws = _na_window_start(r0, rows)
    blocks = {}
    for i in range(NA_QROWS):
        r = r0 + i
        start = min(max(r - NA_ROWS // 2, 0), rows - NA_ROWS)
        for kk in range(NA_WIN_ROWS):
            rel = ws + kk - start
            blocks[i, kk] = ws + kk - r + (NA_ROWS - 1) if 0 <= rel < NA_ROWS else None
    return blocks


def _na_patterns(rows):
    layouts, first = [], []
    for qi in range(rows // NA_QROWS):
        blocks = _na_group_blocks(qi * NA_QROWS, rows)
        if not layouts or blocks != layouts[-1]:
            assert blocks not in layouts
            layouts.append(blocks)
            first.append(qi)
    return layouts, first


def _na_bias_kernel(rb_ref, o_ref, *, rows):
    h = pl.program_id(0)
    cq = lax.broadcasted_iota(jnp.int32, (GRID_W, GRID_W), 0)
    ck = lax.broadcasted_iota(jnp.int32, (GRID_W, GRID_W), 1)
    col_start = jnp.clip(cq - NA_COLS // 2, 0, GRID_W - NA_COLS)
    col_ok = (ck >= col_start) & (ck < col_start + NA_COLS)
    col_off = jnp.clip(ck - cq + (NA_COLS - 1), 0, N_COL_OFF - 1)
    tiles = []
    for ro in range(N_ROW_OFF):
        t = jnp.zeros((GRID_W, GRID_W), F32)
        for co in range(N_COL_OFF):
            t = jnp.where(col_off == co, rb_ref[(h * N_ROW_OFF + ro) * N_COL_OFF + co], t)
        tiles.append(jnp.where(col_ok, t, NEG))
    masked = jnp.full((GRID_W, GRID_W), NEG, F32)
    for p, blocks in enumerate(_na_patterns(rows)[0]):
        for (i, kk), ro in blocks.items():
            o_ref[0, p, i * GRID_W:(i + 1) * GRID_W, kk * GRID_W:(kk + 1) * GRID_W] = (
                masked if ro is None else tiles[ro])


def _na_bias_call(na_bias, rows):
    shape = (NA_HEADS, len(_na_patterns(rows)[0]), NA_QROWS * GRID_W, NA_WIN_ROWS * GRID_W)
    return pl.pallas_call(
        functools.partial(_na_bias_kernel, rows=rows),
        out_shape=jax.ShapeDtypeStruct(shape, F32),
        grid=(NA_HEADS,),
        in_specs=[pl.BlockSpec(memory_space=pltpu.SMEM)],
        out_specs=pl.BlockSpec((1,) + shape[1:], lambda h: (h, 0, 0, 0)),
        compiler_params=_params("arbitrary"),
        name="na_bias_table",
    )(na_bias.reshape(-1))


def _na_attn_kernel(q_ref, k_ref, v_ref, ck_ref, cv_ref, bt_ref, o_ref, *, L):
    rows = L // GRID_W
    groups = rows // NA_QROWS
    pattern_starts = _na_patterns(rows)[1]
    lane = lax.broadcasted_iota(jnp.int32, (NA_QROWS * GRID_W, PAIR), 1)
    for hp in range(NA_HEADS * HEAD_DIM // PAIR):
        heads = (2 * hp, 2 * hp + 1)
        lanes = slice(hp * PAIR, (hp + 1) * PAIR)
        ctx_k = jnp.concatenate([ck_ref[0, 0, h].astype(BF) for h in heads], axis=0)
        ctx_v = jnp.concatenate([cv_ref[0, 0, h].astype(BF) for h in heads], axis=0)

        def body(qi, carry, heads=heads, lanes=lanes, ctx_k=ctx_k, ctx_v=ctx_v):
            r0 = qi * NA_QROWS
            ws = jnp.clip(r0 - NA_ROWS // 2, 0, rows - NA_WIN_ROWS)
            pat = sum((qi >= first).astype(jnp.int32) for first in pattern_starts[1:])
            rq = pl.ds(pl.multiple_of(r0 * GRID_W, NA_QROWS * GRID_W), NA_QROWS * GRID_W)
            rk = pl.ds(pl.multiple_of(ws * GRID_W, GRID_W), NA_WIN_ROWS * GRID_W)
            q2 = q_ref[0, rq, lanes] * ATTN_SCALE
            k2 = k_ref[0, rk, lanes]
            v2 = v_ref[0, rk, lanes]
            outs = []
            for g, h in enumerate(heads):
                qg = _one_head(q2, lane, g)
                s = _dot_nt(qg, k2) + bt_ref[h, pat]
                s_ctx = _dot(qg, ctx_k)
                m = jnp.maximum(jnp.max(s, axis=-1, keepdims=True), jnp.max(s_ctx, axis=-1, keepdims=True))
                p = jnp.exp(s - m)
                p_ctx = jnp.exp(s_ctx - m)
                den = jnp.sum(p, axis=-1, keepdims=True) + jnp.sum(p_ctx, axis=-1, keepdims=True)
                outs.append((_dot(p.astype(BF), v2) + _dot_nt(p_ctx.astype(BF), ctx_v)) / den)
            o_ref[0, rq, lanes] = jnp.where(lane // HEAD_DIM == 0, outs[0], outs[1]).astype(BF)
            return carry

        lax.fori_loop(0, groups, body, 0)


def _na_attn_call(z, cache_k, cache_v, bias_table, layer):
    B, L, _ = z.shape
    P = cache_k.shape[4]
    assert L // GRID_W >= NA_WIN_ROWS + NA_QROWS and (L // GRID_W) % NA_QROWS == 0
    cspec = pl.BlockSpec((1, 1, NA_HEADS, HEAD_DIM, P), lambda b: (b, layer, 0, 0, 0))
    return pl.pallas_call(
        functools.partial(_na_attn_kernel, L=L),
        out_shape=jax.ShapeDtypeStruct((B, L, BRANCH_W), BF),
        grid=(B,),
        in_specs=[
            _zcol_spec(L, COL_DQ), _zcol_spec(L, COL_DK), _zcol_spec(L, COL_DV), cspec, cspec,
            pl.BlockSpec(bias_table.shape, lambda b: (0, 0, 0, 0), pipeline_mode=pl.Buffered(1)),
        ],
        out_specs=pl.BlockSpec((1, L, BRANCH_W), lambda b: (b, 0, 0)),
        compiler_params=_params("arbitrary"),
        name="neighbourhood_attention",
    )(z, z, z, cache_k, cache_v, bias_table)


def _merge_kernel(x_ref, g_ref, sc_ref, sh_ref, gate_ref, ba_ref, bb_ref, bc_ref, bd_ref,
                  wg0_ref, wg1_ref, wb_ref, wo_ref, o_ref):
    x = x_ref[0]
    h = _norm_mod(x, g_ref[...], sc_ref[0], sh_ref[0]).astype(BF)
    merged = jnp.zeros(x.shape, F32)
    per_ref = MIX_COLS // D_MODEL
    for n, br_ref in enumerate((ba_ref, bb_ref, bc_ref, bd_ref)):
        wg_ref = (wg0_ref, wg1_ref)[n // per_ref]
        c0 = (n % per_ref) * D_MODEL
        gate = jax.nn.sigmoid(_dot(h, wg_ref[0, :, c0:c0 + D_MODEL]))
        merged = merged + gate * _dot(br_ref[0], wb_ref[0, n])
    o_ref[0] = x + gate_ref[0] * _dot(merged.astype(BF), wo_ref[0])


def _merge_call(x, norm_g, mod, branches, w_in_bf, w_branch, w_out, layer, tm):
    B, L, _ = x.shape
    bm = mod.shape[0]
    xspec = pl.BlockSpec((1, tm, D_MODEL), lambda b, i: (b, i, 0))
    bspec = pl.BlockSpec((1, tm, BRANCH_W), lambda b, i: (b, i, 0))
    return pl.pallas_call(
        _merge_kernel,
        out_shape=jax.ShapeDtypeStruct((B, L, D_MODEL), F32),
        grid=(B, L // tm),
        in_specs=[
            xspec,
            pl.BlockSpec((1, D_MODEL), lambda b, i: (0, 0)),
            _mod_spec(bm, 1, 2), _mod_spec(bm, 0, 2), _mod_spec(bm, 2, 2),
            bspec, bspec, bspec, bspec,
            pl.BlockSpec((1, D_MODEL, MIX_COLS), lambda b, i: (layer, 0, 1)),
            pl.BlockSpec((1, D_MODEL, MIX_COLS), lambda b, i: (layer, 0, 2)),
            pl.BlockSpec((1, N_BRANCH, BRANCH_W, D_MODEL), lambda b, i: (layer, 0, 0, 0)),
            pl.BlockSpec((1, D_MODEL, D_MODEL), lambda b, i: (layer, 0, 0)),
        ],
        out_specs=xspec,
        compiler_params=_params("arbitrary", "arbitrary"),
        name="merge_out_proj",
    )(x, norm_g, mod, mod, mod, *branches, w_in_bf, w_in_bf, w_branch, w_out)


ROUTE_CHUNK = 256
GATHER_ROWS = 512
SIGN_BIT = 31
SEARCH_BITS = 4
SEARCH_TOP_SHIFT = ((SIGN_BIT - 1) // SEARCH_BITS) * SEARCH_BITS


def _prefix_excl(mask_ref, out_ref, tri, L):
    totals = [jnp.sum(mask_ref[:, c0:c0 + LANES], axis=-1, keepdims=True) for c0 in range(0, L, LANES)]
    run = jnp.zeros((N_EXPERTS, 1), F32)
    for i, c0 in enumerate(range(0, L, LANES)):
        out_ref[:, c0:c0 + LANES] = _dot(mask_ref[:, c0:c0 + LANES].astype(BF), tri) + run
        run = run + totals[i]


def _route_kernel(x_ref, g_ref, sc_ref, sh_ref, wh_ref, wl_ref, tri_ref,
                  h_ref, gate_ref, slot_ref, slot_t_ref, off_ref, lt_ref, msk_ref, pre_ref, *, L, cap):
    g = g_ref[...]
    sc = sc_ref[0]
    sh = sh_ref[0]
    wh = wh_ref[...]
    wl = wl_ref[...]
    rc = min(L, ROUTE_CHUNK)
    for r0 in range(0, L, rc):
        h = _norm_mod(x_ref[0, r0:r0 + rc, :], g, sc, sh)
        h_hi = h.astype(BF)
        h_ref[0, r0:r0 + rc, :] = h_hi
        h_lo = (h - h_hi.astype(F32)).astype(BF)
        lt_ref[:, r0:r0 + rc] = _dot_nt(wh, h_hi) + (_dot_nt(wh, h_lo) + _dot_nt(wl, h_hi))
    lt = lt_ref[...]
    e = jnp.exp(lt - jnp.max(lt, axis=0, keepdims=True))
    aff = e / jnp.sum(e, axis=0, keepdims=True)
    thr = jnp.zeros((N_EXPERTS, 1), jnp.int32)
    for shift in range(SEARCH_TOP_SHIFT, -1, -SEARCH_BITS):
        n_digits = min(1 << SEARCH_BITS, 1 << (SIGN_BIT - shift))
        digit = jnp.zeros((N_EXPERTS, 1), jnp.int32)
        for d in range(1, n_digits):
            cand = pltpu.bitcast(thr | (d << shift), F32)
            cnt = jnp.sum((aff >= cand).astype(F32), axis=-1, keepdims=True)
            digit = digit + (cnt >= cap).astype(jnp.int32)
        thr = thr | (digit << shift)
    thr_f = pltpu.bitcast(thr, F32)
    gt = aff > thr_f
    eq = aff == thr_f
    need = cap - jnp.sum(gt.astype(F32), axis=-1, keepdims=True)
    tri = tri_ref[...]
    msk_ref[...] = eq.astype(F32)
    _prefix_excl(msk_ref, pre_ref, tri, L)
    sel = gt | (eq & (pre_ref[...] < need))
    msk_ref[...] = sel.astype(F32)
    _prefix_excl(msk_ref, pre_ref, tri, L)
    off_ref[0] = jnp.concatenate([pre_ref[:, c:c + 1] for c in range(0, L, min(L, COMBINE_TL))], axis=1)
    slot = jnp.where(sel, pre_ref[...], -1.0)
    slot_ref[0] = slot
    slot_t_ref[0] = slot.T
    gate_ref[0] = jnp.where(sel, aff, 0.0)


def _route_call(x, norm_g, mod, wr_hi, wr_lo):
    B, L, _ = x.shape
    bm = mod.shape[0]
    cap = EC_CAPACITY * L // N_EXPERTS
    n_tiles = L // min(L, COMBINE_TL)
    tri = jnp.asarray(np.triu(np.ones((LANES, LANES)), 1).astype(ml_dtypes.bfloat16))
    wspec = pl.BlockSpec((N_EXPERTS, D_MODEL), lambda b: (0, 0))
    return pl.pallas_call(
        functools.partial(_route_kernel, L=L, cap=cap),
        out_shape=[
            jax.ShapeDtypeStruct((B, L, D_MODEL), BF),
            jax.ShapeDtypeStruct((B, N_EXPERTS, L), F32),
            jax.ShapeDtypeStruct((B, N_EXPERTS, L), F32),
            jax.ShapeDtypeStruct((B, L, N_EXPERTS), F32),
            jax.ShapeDtypeStruct((B, N_EXPERTS, n_tiles), F32),
        ],
        grid=(B,),
        in_specs=[
            pl.BlockSpec((1, L, D_MODEL), lambda b: (b, 0, 0)),
            pl.BlockSpec((1, D_MODEL), lambda b: (0, 0)),
            _mod_spec(bm, 4, 1), _mod_spec(bm, 3, 1),
            wspec, wspec,
            pl.BlockSpec((LANES, LANES), lambda b: (0, 0)),
        ],
        out_specs=[
            pl.BlockSpec((1, L, D_MODEL), lambda b: (b, 0, 0)),
            pl.BlockSpec((1, N_EXPERTS, L), lambda b: (b, 0, 0)),
            pl.BlockSpec((1, N_EXPERTS, L), lambda b: (b, 0, 0)),
            pl.BlockSpec((1, L, N_EXPERTS), lambda b: (b, 0, 0)),
            pl.BlockSpec((1, N_EXPERTS, n_tiles), lambda b: (b, 0, 0)),
        ],
        scratch_shapes=[
            pltpu.VMEM((N_EXPERTS, L), F32),
            pltpu.VMEM((N_EXPERTS, L), F32),
            pltpu.VMEM((N_EXPERTS, L), F32),
        ],
        compiler_params=_params("arbitrary"),
        name="ec_route",
    )(x, norm_g, mod, mod, wr_hi, wr_lo, tri)


COMBINE_TL = 256
COMBINE_WIN = 64
BF_TILE = 16


def _list_windows(off_ref, cap):
    n_tiles = pl.num_programs(1)
    base = (pl.program_id(0) * n_tiles + pl.program_id(1)) * N_EXPERTS
    starts = []
    fits = None
    for ex in range(N_EXPERTS):
        first = off_ref[base + ex]
        end = jnp.where(pl.program_id(1) == n_tiles - 1, cap, off_ref[base + N_EXPERTS + ex])
        start = jnp.minimum((first // BF_TILE) * BF_TILE, cap - COMBINE_WIN)
        ok = end <= start + COMBINE_WIN
        fits = ok if fits is None else (fits & ok)
        starts.append(start)
    return starts, fits


def _gather_kernel(off_ref, slot_ref, gate_ref, h_ref, xs_ref, gs_ref, *, cap):
    h = h_ref[0]
    tl = h.shape[0]

    @pl.when(pl.program_id(1) == 0)
    def _():
        xs_ref[...] = jnp.zeros(xs_ref.shape, BF)
        gs_ref[...] = jnp.zeros(gs_ref.shape, F32)

    def select(ex, first, rows):
        pos = lax.broadcasted_iota(jnp.int32, (rows, tl), 0).astype(F32)
        hit = slot_ref[0, ex:ex + 1, :] - first == pos
        gate = jnp.sum(jnp.where(hit, gate_ref[0, ex:ex + 1, :], 0.0), axis=-1, keepdims=True)
        return jnp.where(hit, 1.0, 0.0).astype(BF), jnp.broadcast_to(gate, (rows, LANES))

    def gather_all():
        ge = max(1, min(N_EXPERTS, GATHER_ROWS // cap))
        for e0 in range(0, N_EXPERTS, ge):
            picked = [select(ex, 0.0, cap) for ex in range(e0, e0 + ge)]
            rows = _dot(jnp.concatenate([p[0] for p in picked], axis=0), h).astype(BF)
            for j in range(ge):
                xs_ref[e0 + j, 0] += rows[j * cap:(j + 1) * cap]
                gs_ref[e0 + j, 0] += picked[j][1]

    if cap <= COMBINE_WIN:
        gather_all()
        return

    starts, fits = _list_windows(off_ref, cap)

    @pl.when(fits)
    def _():
        picked = [select(ex, starts[ex].astype(F32), COMBINE_WIN) for ex in range(N_EXPERTS)]
        rows = _dot(jnp.concatenate([p[0] for p in picked], axis=0), h).astype(BF)
        for ex in range(N_EXPERTS):
            win = pl.ds(pl.multiple_of(starts[ex], BF_TILE), COMBINE_WIN)
            xs_ref[ex, 0, win, :] += rows[ex * COMBINE_WIN:(ex + 1) * COMBINE_WIN]
            gs_ref[ex, 0, win, :] += picked[ex][1]

    @pl.when(jnp.logical_not(fits))
    def _():
        gather_all()


def _gather_call(offsets, slot, gate_rows, h2):
    B, L, _ = h2.shape
    cap = EC_CAPACITY * L // N_EXPERTS
    tl = min(L, COMBINE_TL)
    assert cap <= COMBINE_WIN or (cap % BF_TILE == 0 and COMBINE_WIN % BF_TILE == 0)
    return pl.pallas_call(
        functools.partial(_gather_kernel, cap=cap),
        out_shape=[
            jax.ShapeDtypeStruct((N_EXPERTS, B, cap, D_MODEL), BF),
            jax.ShapeDtypeStruct((N_EXPERTS, B, cap, LANES), F32),
        ],
        grid_spec=pltpu.PrefetchScalarGridSpec(
            num_scalar_prefetch=1,
            grid=(B, L // tl),
            in_specs=[
                pl.BlockSpec((1, N_EXPERTS, tl), lambda b, i, off: (b, 0, i)),
                pl.BlockSpec((1, N_EXPERTS, tl), lambda b, i, off: (b, 0, i)),
                pl.BlockSpec((1, tl, D_MODEL), lambda b, i, off: (b, i, 0)),
            ],
            out_specs=[
                pl.BlockSpec((N_EXPERTS, 1, cap, D_MODEL), lambda b, i, off: (0, b, 0, 0)),
                pl.BlockSpec((N_EXPERTS, 1, cap, LANES), lambda b, i, off: (0, b, 0, 0)),
            ],
        ),
        compiler_params=_params("arbitrary", "arbitrary"),
        name="ec_gather",
    )(offsets, slot, gate_rows, h2)


EXPERT_TR = 512


def _expert_kernel(xa_ref, ga_ref, xb_ref, gb_ref, wg_ref, wu_ref, wd_ref, ya_ref, yb_ref,
                   wgb_ref, wub_ref, wdb_ref, *, tiles_a):
    j = pl.program_id(1)

    @pl.when(j == 0)
    def _():
        wgb_ref[...] = wg_ref[0, 0].astype(BF)
        wub_ref[...] = wu_ref[0, 0].astype(BF)
        wdb_ref[...] = wd_ref[0, 0].astype(BF)

    def ffn(x_ref, g_ref, y_ref):
        x = x_ref[0]
        a = _dot(x, wgb_ref[...])
        u = _dot(x, wub_ref[...])
        y = _dot((_silu(a) * u).astype(BF), wdb_ref[...])
        y_ref[0] = (y * g_ref[0, :, 0:1]).astype(BF)

    @pl.when(j < tiles_a)
    def _():
        ffn(xa_ref, ga_ref, ya_ref)

    @pl.when(j >= tiles_a)
    def _():
        ffn(xb_ref, gb_ref, yb_ref)


def _expert_call(xs_a, gs_a, xs_b, gs_b, w_gate, w_up, w_down, layer):
    E, ra, _ = xs_a.shape
    rb = xs_b.shape[1]
    tr = math.gcd(math.gcd(ra, rb), EXPERT_TR)
    tiles_a, tiles_b = ra // tr, rb // tr
    amap = lambda e, j: (e, jnp.minimum(j, tiles_a - 1), 0)
    bmap = lambda e, j: (e, jnp.maximum(j - tiles_a, 0), 0)
    wspec_in = pl.BlockSpec((1, 1, D_MODEL, EXPERT_FF), lambda e, j: (layer, e, 0, 0))
    wspec_out = pl.BlockSpec((1, 1, EXPERT_FF, D_MODEL), lambda e, j: (layer, e, 0, 0))
    return pl.pallas_call(
        functools.partial(_expert_kernel, tiles_a=tiles_a),
        out_shape=[jax.ShapeDtypeStruct((E, ra, D_MODEL), BF), jax.ShapeDtypeStruct((E, rb, D_MODEL), BF)],
        grid=(E, tiles_a + tiles_b),
        in_specs=[
            pl.BlockSpec((1, tr, D_MODEL), amap), pl.BlockSpec((1, tr, LANES), amap),
            pl.BlockSpec((1, tr, D_MODEL), bmap), pl.BlockSpec((1, tr, LANES), bmap),
            wspec_in, wspec_in, wspec_out,
        ],
        out_specs=[pl.BlockSpec((1, tr, D_MODEL), amap), pl.BlockSpec((1, tr, D_MODEL), bmap)],
        scratch_shapes=[
            pltpu.VMEM((D_MODEL, EXPERT_FF), BF),
            pltpu.VMEM((D_MODEL, EXPERT_FF), BF),
            pltpu.VMEM((EXPERT_FF, D_MODEL), BF),
        ],
        compiler_params=_params("arbitrary", "arbitrary"),
        name="ec_experts",
    )(xs_a, gs_a, xs_b, gs_b, w_gate, w_up, w_down)


def _combine_kernel(off_ref, st_ref, y_ref, x_ref, gate_ref, fg_ref, o_ref, *, cap, final):
    st = st_ref[0]
    tl = st.shape[0]

    def finish(scattered):
        v = x_ref[0] + gate_ref[0] * scattered
        if final:
            v = v * lax.rsqrt(jnp.mean(v * v, axis=-1, keepdims=True) + EPS) * fg_ref[...]
        o_ref[0] = v

    def onehot(width, starts):
        n = N_EXPERTS * width
        owner = lax.broadcasted_iota(jnp.int32, (N_EXPERTS, n), 1) // width
        spread = jnp.where(owner == lax.broadcasted_iota(jnp.int32, (N_EXPERTS, n), 0), 1.0, 0.0).astype(BF)
        slot = _dot(st.astype(BF), spread)
        lane = lax.broadcasted_iota(jnp.int32, (1, n), 1)
        want = (lane % width).astype(F32)
        if starts is not None:
            for ex in range(N_EXPERTS):
                want = want + jnp.where(lane // width == ex, starts[ex].astype(F32), 0.0)
        return jnp.where(slot == want, 1.0, 0.0).astype(BF)

    def scatter_all():
        finish(_dot(onehot(cap, None), y_ref[:, 0].reshape(N_EXPERTS * cap, D_MODEL)))

    if cap <= COMBINE_WIN:
        scatter_all()
        return

    starts, fits = _list_windows(off_ref, cap)

    @pl.when(fits)
    def _():
        rows = [y_ref[ex, 0, pl.ds(pl.multiple_of(starts[ex], BF_TILE), COMBINE_WIN), :]
                for ex in range(N_EXPERTS)]
        finish(_dot(onehot(COMBINE_WIN, starts), jnp.concatenate(rows, axis=0)))

    @pl.when(jnp.logical_not(fits))
    def _():
        scatter_all()


def _combine_call(slot_t, offsets, y, x, mod, final_g, final):
    B, L, _ = x.shape
    bm = mod.shape[0]
    cap = y.shape[2]
    tl = min(L, COMBINE_TL)
    assert cap <= COMBINE_WIN or (cap % BF_TILE == 0 and COMBINE_WIN % BF_TILE == 0)
    assert cap <= 256
    gate_map =(lambda b, i, off: (b, 0, 5)) if bm > 1 else (lambda b, i, off: (0, 0, 5))
    return pl.pallas_call(
        functools.partial(_combine_kernel, cap=cap, final=final),
        out_shape=jax.ShapeDtypeStruct((B, L, D_MODEL), F32),
        grid_spec=pltpu.PrefetchScalarGridSpec(
            num_scalar_prefetch=1,
            grid=(B, L // tl),
            in_specs=[
                pl.BlockSpec((1, tl, N_EXPERTS), lambda b, i, off: (b, i, 0)),
                pl.BlockSpec((N_EXPERTS, 1, cap, D_MODEL), lambda b, i, off: (0, b, 0, 0)),
                pl.BlockSpec((1, tl, D_MODEL), lambda b, i, off: (b, i, 0)),
                pl.BlockSpec((1, 1, D_MODEL), gate_map),
                pl.BlockSpec((1, D_MODEL), lambda b, i, off: (0, 0)),
            ],
            out_specs=pl.BlockSpec((1, tl, D_MODEL), lambda b, i, off: (b, i, 0)),
        ),
        compiler_params=_params("arbitrary", "arbitrary"),
        name="ec_combine",
    )(offsets, slot_t, y, x, mod, final_g)


def _mixer_half(x, mod, lw, ctx, tm, kv_prev=None):
    latent = ctx is not None
    layer = lw["layer"]
    res = _inproj_call(x, lw["norm1_g"], mod, lw["w_in"], layer, tm, rope=latent, want_kv=not latent,
                       kv_prev=kv_prev)
    z = res[0]
    br_a = _conv_call(z, lw["conv_w"], lw["conv_b"], lw["conv_ln_g"], lw["conv_ln_b"])
    br_b = _fourier_call(z, *lw["dft"][z.shape[1]])
    if latent:
        br_c = _win_attn_call(z, ctx["win_k"], ctx["win_v"], lw["win_sink"], layer)
        br_d = _na_attn_call(z, ctx["nat_k"], ctx["nat_v"], lw["bias_table"], layer)
    else:
        br_c, br_d = _ctx_attn_call(z, lw["win_sink"])
    x = _merge_call(x, lw["norm1_g"], mod, (br_a, br_b, br_c, br_d),
                    lw["w_in"], lw["w_branch"], lw["w_out"], layer, tm)
    return x, (tuple(res[1:]) if not latent else None)


def kernel(x_prompt, x_sample, cache_win_k, cache_win_v, cache_nat_k, cache_nat_v, c, c_ctx,
           norm1_g, norm2_g, w_ada, b_ada, w_in, conv_w, conv_b, conv_ln_g, conv_ln_b,
           win_sink, na_bias, w_branch, w_out, w_router, w_e_gate, w_e_up, w_e_down, final_norm_g):
    n_dec = c.shape[0]
    cond = jnp.zeros((ADA_ROWS, D_MODEL), F32).at[0].set(c_ctx).at[1:1 + n_dec].set(c)
    mods = _ada_call(cond, w_ada, b_ada)
    final_g = final_norm_g.reshape(1, D_MODEL)
    w_in_bf = _cast_call(w_in)
    w_branch_bf = _cast_call(w_branch)
    w_out_bf = _cast_call(w_out)
    dft = {L: _dft_tables(L) for L in (x_prompt.shape[1], x_sample.shape[1])}
    dim_major = lambda a: jnp.swapaxes(a, -1, -2)
    ctx = {"win_k": dim_major(cache_win_k), "win_v": dim_major(cache_win_v),
           "nat_k": dim_major(cache_nat_k), "nat_v": dim_major(cache_nat_v)}

    xp, xs = x_prompt, x_sample
    caches = None
    for l in range(DEPTH):
        wr_t = w_router[l].T
        wr_hi = wr_t.astype(BF)
        wr_lo = (wr_t - wr_hi.astype(F32)).astype(BF)
        lw = {
            "layer": l,
            "norm1_g": norm1_g[l].reshape(1, D_MODEL),
            "w_in": w_in_bf, "w_branch": w_branch_bf, "w_out": w_out_bf, "dft": dft,
            "conv_w": conv_w[l], "conv_b": conv_b[l],
            "conv_ln_g": conv_ln_g[l], "conv_ln_b": conv_ln_b[l],
            "win_sink": win_sink[l],
            "bias_table": _na_bias_call(na_bias[l], x_sample.shape[1] // GRID_W),
        }
        norm2 = norm2_g[l].reshape(1, D_MODEL)
        final = l == DEPTH - 1
        mod_p = mods[l, 0:1].reshape(1, 1, 6 * D_MODEL)
        mod_s = mods[l, 1:1 + n_dec].reshape(n_dec, 1, 6 * D_MODEL)
        xp, caches = _mixer_half(xp, mod_p, lw, None, tm=256, kv_prev=caches)
        xs, _ = _mixer_half(xs, mod_s, lw, ctx, tm=512)
        smem = lambda off: jnp.pad(off.transpose(0, 2, 1).astype(jnp.int32).reshape(-1), (0, N_EXPERTS))
        h_p, g_p, slot_p, slot_t_p, off_p = _route_call(xp, norm2, mod_p, wr_hi, wr_lo)
        h_s, g_s, slot_s, slot_t_s, off_s = _route_call(xs, norm2, mod_s, wr_hi, wr_lo)
        off_p, off_s = smem(off_p), smem(off_s)
        rows_p, gates_p = _gather_call(off_p, slot_p, g_p, h_p)
        rows_s, gates_s = _gather_call(off_s, slot_s, g_s, h_s)
        flat = lambda a: a.reshape(N_EXPERTS, a.shape[1] * a.shape[2], a.shape[3])
        y_p, y_s = _expert_call(flat(rows_p), flat(gates_p), flat(rows_s), flat(gates_s),
                                w_e_gate, w_e_up, w_e_down, l)
        xp = _combine_call(slot_t_p, off_p, y_p.reshape(rows_p.shape), xp, mod_p, final_g, final)
        xs = _combine_call(slot_t_s, off_s, y_s.reshape(rows_s.shape), xs, mod_s, final_g, final)

    return (xp, xs) + tuple(dim_major(c) for c in caches)
```

```python
import functools
import math

import ml_dtypes
import numpy as np
import jax
import jax.numpy as jnp
from jax import lax
from jax.experimental import pallas as pl
from jax.experimental.pallas import tpu as pltpu

D_MODEL = 1024
DEPTH = 2
GRID_W = 64
BRANCH_W = 256
HEAD_DIM = 64
N_BRANCH = 4
CONV_K = 31
FNET_GROUPS = 4
FNET_GDIM = BRANCH_W // FNET_GROUPS
WIN_Q_HEADS = 4
WIN_KV_HEADS = 2
WIN_BLOCK = 128
NA_HEADS = 4
NA_ROWS = 8
NA_COLS = 16
ROPE_THETA = 10000.0
ATTN_SCALE = HEAD_DIM ** -0.5
N_EXPERTS = 16
EXPERT_FF = 1024
EC_CAPACITY = 2
EPS = 1e-6
NEG = -1e30

MIX_COLS = 2048
GATE_COLS = N_BRANCH * D_MODEL
IN_COLS = MIX_COLS + GATE_COLS
COL_A, COL_F, COL_CQ, COL_CKV, COL_DQ, COL_DK, COL_DV = 0, 512, 768, 1024, 1280, 1536, 1792
ROPE_LO, ROPE_HI = COL_CQ, COL_CKV + WIN_KV_HEADS * HEAD_DIM
KV_COLS = (COL_CKV, COL_CKV + WIN_KV_HEADS * HEAD_DIM, COL_DK, COL_DV)
KV_HEADS = (WIN_KV_HEADS, WIN_KV_HEADS, NA_HEADS, NA_HEADS)

LANES = 128
SUBLANES = 8
VMEM_LIMIT = 56 * 1024 * 1024

BF = jnp.bfloat16
F32 = jnp.float32


def _params(*sem):
    return pltpu.CompilerParams(dimension_semantics=sem, vmem_limit_bytes=VMEM_LIMIT)


def _dot(a, b):
    return jnp.dot(a, b, preferred_element_type=F32)


def _dot_nt(a, b):
    return lax.dot_general(a, b, (((1,), (1,)), ((), ())), preferred_element_type=F32)


def _norm_mod(x, g, sc, sh):
    y = x * lax.rsqrt(jnp.mean(x * x, axis=-1, keepdims=True) + EPS)
    return (y * g) * (1.0 + sc) + sh


def _silu(x):
    return x * jax.nn.sigmoid(x)


CAST_BYTES = 4 * 1024 * 1024


def _cast_kernel(x_ref, o_ref):
    o_ref[...] = x_ref[...].astype(BF)


def _cast_call(x):
    shape = x.shape
    x2 = x.reshape(-1, shape[-1])
    rows, cols = x2.shape
    tr = min(rows, 1 << (max(16, CAST_BYTES // (4 * cols)).bit_length() - 1))
    assert rows % tr == 0
    out = pl.pallas_call(
        _cast_kernel,
        out_shape=jax.ShapeDtypeStruct((rows, cols), BF),
        grid=(rows // tr,),
        in_specs=[pl.BlockSpec((tr, cols), lambda i: (i, 0))],
        out_specs=pl.BlockSpec((tr, cols), lambda i: (i, 0)),
        compiler_params=_params("arbitrary"),
        name="cast_bf16",
    )(x2)
    return out.reshape(shape)


ADA_ROWS = 16
ADA_TN = 1536


def _ada_kernel(c_ref, w_ref, b_ref, o_ref):
    s = _silu(c_ref[...]).astype(BF)
    o_ref[0] = _dot(s, w_ref[0].astype(BF)) + b_ref[0]


def _ada_call(cond, w_ada, b_ada):
    n = 6 * D_MODEL
    return pl.pallas_call(
        _ada_kernel,
        out_shape=jax.ShapeDtypeStruct((DEPTH, ADA_ROWS, n), F32),
        grid=(DEPTH, n // ADA_TN),
        in_specs=[
            pl.BlockSpec((ADA_ROWS, D_MODEL), lambda l, j: (0, 0)),
            pl.BlockSpec((1, D_MODEL, ADA_TN), lambda l, j: (l, 0, j)),
            pl.BlockSpec((1, 1, ADA_TN), lambda l, j: (l, 0, j)),
        ],
        out_specs=pl.BlockSpec((1, ADA_ROWS, ADA_TN), lambda l, j: (l, 0, j)),
        compiler_params=_params("arbitrary", "arbitrary"),
        name="ada_mod",
    )(cond, w_ada, b_ada.reshape(DEPTH, 1, n))


def _mod_spec(bm, chunk, ngrid):
    if ngrid == 1:
        imap = (lambda b: (b, 0, chunk)) if bm > 1 else (lambda b: (0, 0, chunk))
    else:
        imap = (lambda b, i: (b, 0, chunk)) if bm > 1 else (lambda b, i: (0, 0, chunk))
    return pl.BlockSpec((1, 1, D_MODEL), imap)


def _inproj_kernel(*refs, rope, kv_layer, kv_aliased):
    x_ref, g_ref, sc_ref, sh_ref, w_ref = refs[:5]
    pos = 5
    if rope:
        cos_ref, sin_ref = refs[pos:pos + 2]
        pos += 2
    if kv_aliased:
        pos += len(KV_HEADS)
    z_ref = refs[pos]
    h = _norm_mod(x_ref[0], g_ref[...], sc_ref[0], sh_ref[0]).astype(BF)
    z = _dot(h, w_ref[0])
    if kv_layer is not None:
        for kv_ref, c0, heads in zip(refs[pos + 1:pos + 1 + len(KV_HEADS)], KV_COLS, KV_HEADS):
            here = 0 if kv_aliased else kv_layer
            for hd in range(heads):
                kv_ref[0, here, hd] = z[:, c0 + hd * HEAD_DIM:c0 + (hd + 1) * HEAD_DIM].T
            for other in range(kv_ref.shape[1]):
                if other != here:
                    kv_ref[0, other] = jnp.zeros(kv_ref.shape[2:], F32)
    if rope:
        z_ref[0, :, 0:ROPE_LO] = z[:, 0:ROPE_LO].astype(BF)
        lane = lax.broadcasted_iota(jnp.int32, (z.shape[0], LANES), 1)
        first_half = (lane % 32) < 16
        cos = cos_ref[...]
        sin = sin_ref[...]
        for c0 in range(ROPE_LO, ROPE_HI, LANES):
            xc = z[:, c0:c0 + LANES]
            partner = jnp.where(first_half, pltpu.roll(xc, LANES - 16, 1), pltpu.roll(xc, 16, 1))
            z_ref[0, :, c0:c0 + LANES] = (xc * cos + partner * sin).astype(BF)
        z_ref[0, :, ROPE_HI:MIX_COLS] = z[:, ROPE_HI:MIX_COLS].astype(BF)
    else:
        z_ref[0] = z.astype(BF)


def _rope_tables(L):
    t = np.arange(L)
    half = HEAD_DIM // 4
    freqs = 1.0 / (ROPE_THETA ** (np.arange(half, dtype=np.float64) / half))
    ang_r = (t // GRID_W).astype(np.float64)[:, None] * freqs[None, :]
    ang_c = (t % GRID_W).astype(np.float64)[:, None] * freqs[None, :]
    cos_h = np.concatenate([np.cos(ang_r), np.cos(ang_r), np.cos(ang_c), np.cos(ang_c)], axis=1)
    sin_h = np.concatenate([-np.sin(ang_r), np.sin(ang_r), -np.sin(ang_c), np.sin(ang_c)], axis=1)
    reps = LANES // HEAD_DIM
    return (jnp.asarray(np.tile(cos_h, (1, reps)), F32), jnp.asarray(np.tile(sin_h, (1, reps)), F32))


def _inproj_call(x, norm_g, mod, w_in_bf, layer, tm, rope, want_kv, kv_prev=None):
    B, L, _ = x.shape
    bm = mod.shape[0]
    in_specs = [
        pl.BlockSpec((1, tm, D_MODEL), lambda b, i: (b, i, 0)),
        pl.BlockSpec((1, D_MODEL), lambda b, i: (0, 0)),
        _mod_spec(bm, 1, 2),
        _mod_spec(bm, 0, 2),
        pl.BlockSpec((1, D_MODEL, MIX_COLS), lambda b, i: (layer, 0, 0)),
    ]
    args = [x, norm_g, mod, mod, w_in_bf]
    if rope:
        cos, sin = _rope_tables(L)
        in_specs += [pl.BlockSpec((tm, LANES), lambda b, i: (i, 0))] * 2
        args += [cos, sin]
    out_shape = [jax.ShapeDtypeStruct((B, L, MIX_COLS), BF)]
    out_specs = [pl.BlockSpec((1, tm, MIX_COLS), lambda b, i: (b, i, 0))]
    aliases = {}
    if want_kv:
        for n, heads in enumerate(KV_HEADS):
            out_shape.append(jax.ShapeDtypeStruct((B, DEPTH, heads, HEAD_DIM, L), F32))
            if kv_prev is None:
                out_specs.append(pl.BlockSpec((1, DEPTH, heads, HEAD_DIM, tm), lambda b, i: (b, 0, 0, 0, i)))
            else:
                out_specs.append(pl.BlockSpec((1, 1, heads, HEAD_DIM, tm), lambda b, i: (b, layer, 0, 0, i)))
                aliases[len(args)] = 1 + n
                in_specs.append(pl.BlockSpec(memory_space=pl.ANY))
                args.append(kv_prev[n])
    return pl.pallas_call(
        functools.partial(_inproj_kernel, rope=rope, kv_layer=layer if want_kv else None,
                          kv_aliased=kv_prev is not None),
        out_shape=out_shape,
        grid=(B, L // tm),
        in_specs=in_specs,
        out_specs=out_specs,
        input_output_aliases=aliases,
        compiler_params=_params("arbitrary", "arbitrary"),
        name="in_proj",
    )(*args)


CONV_PAD = 16
CONV_TR = 64
CONV_CHUNK = 256


def _conv_kernel(a_ref, w_ref, cb_ref, lg_ref, lb_ref, o_ref, p_ref, u_ref, wb_ref, *, L):
    for k in range(CONV_K):
        wb_ref[k] = jnp.broadcast_to(w_ref[k:k + 1, :], (SUBLANES, BRANCH_W))
    zeros = jnp.zeros((CONV_PAD, BRANCH_W), F32)
    p_ref[0:CONV_PAD, :] = zeros
    p_ref[L + CONV_PAD:L + 2 * CONV_PAD, :] = zeros
    for r0 in range(0, L, CONV_CHUNK):
        a = a_ref[0, r0:r0 + CONV_CHUNK, :].astype(F32)
        p_ref[CONV_PAD + r0:CONV_PAD + r0 + CONV_CHUNK, :] = a[:, :BRANCH_W] * jax.nn.sigmoid(a[:, BRANCH_W:])
    n_u = L + 2 * CONV_PAD - SUBLANES
    for s in range(SUBLANES):
        u_ref[s] = p_ref[s:s + n_u, :]
    cb = cb_ref[...]

    def body(i, carry):
        base = pl.multiple_of(i * CONV_TR, CONV_TR)
        groups = (CONV_TR // SUBLANES, SUBLANES, BRANCH_W)
        acc = jnp.zeros(groups, F32) + cb
        for k in range(CONV_K):
            q, s = divmod(k + CONV_PAD - CONV_K // 2, SUBLANES)
            u = u_ref[s, pl.ds(base + SUBLANES * q, CONV_TR), :]
            acc = acc + wb_ref[k] * u.reshape(groups)
        p_ref[pl.ds(base, CONV_TR), :] = acc.reshape(CONV_TR, BRANCH_W)
        return carry

    lax.fori_loop(0, L // CONV_TR, body, 0)
    lg = lg_ref[...]
    lb = lb_ref[...]
    for r0 in range(0, L, CONV_CHUNK):
        acc = p_ref[r0:r0 + CONV_CHUNK, :]
        mu = jnp.mean(acc, axis=-1, keepdims=True)
        d = acc - mu
        var = jnp.mean(d * d, axis=-1, keepdims=True)
        y = d * lax.rsqrt(var + EPS) * lg + lb
        o_ref[0, r0:r0 + CONV_CHUNK, :] = _silu(y).astype(BF)


def _conv_call(z, conv_w, conv_b, ln_g, ln_b):
    B, L, _ = z.shape
    n_u = L + 2 * CONV_PAD - SUBLANES
    vec = lambda b: (0, 0)
    return pl.pallas_call(
        functools.partial(_conv_kernel, L=L),
        out_shape=jax.ShapeDtypeStruct((B, L, BRANCH_W), BF),
        grid=(B,),
        in_specs=[
            pl.BlockSpec((1, L, 2 * BRANCH_W), lambda b: (b, 0, COL_A // (2 * BRANCH_W))),
            pl.BlockSpec((CONV_K, BRANCH_W), vec),
            pl.BlockSpec((1, BRANCH_W), vec),
            pl.BlockSpec((1, BRANCH_W), vec),
            pl.BlockSpec((1, BRANCH_W), vec),
        ],
        out_specs=pl.BlockSpec((1, L, BRANCH_W), lambda b: (b, 0, 0)),
        scratch_shapes=[
            pltpu.VMEM((L + 2 * CONV_PAD, BRANCH_W), F32),
            pltpu.VMEM((SUBLANES, n_u, BRANCH_W), F32),
            pltpu.VMEM((CONV_K, SUBLANES, BRANCH_W), F32),
        ],
        compiler_params=_params("arbitrary"),
        name="conv_module",
    )(z, conv_w, conv_b.reshape(1, -1), ln_g.reshape(1, -1), ln_b.reshape(1, -1))


def _dft_tables(L):
    c = np.arange(FNET_GDIM)
    ang_c = 2.0 * np.pi * ((c[:, None] * c[None, :]) % FNET_GDIM) / FNET_GDIM
    cc = np.zeros((BRANCH_W, BRANCH_W))
    sc = np.zeros((BRANCH_W, BRANCH_W))
    for g in range(FNET_GROUPS):
        sl = slice(g * FNET_GDIM, (g + 1) * FNET_GDIM)
        cc[sl, sl] = np.cos(ang_c)
        sc[sl, sl] = np.sin(ang_c)
    ccs = np.concatenate([cc, sc], axis=1)
    f = np.arange(L)
    t = np.arange(L // 2)
    ang_l = 2.0 * np.pi * ((f[:, None] * t[None, :]) % L) / L
    csl = np.concatenate([np.cos(ang_l), -np.sin(ang_l)], axis=1)
    return _cast_call(jnp.asarray(ccs, F32)), _cast_call(jnp.asarray(csl, F32))


FOURIER_TR = 512
FOURIER_PAD = 8


def _fourier_kernel(f_ref, ccs_ref, csl_ref, o_ref, pq_ref, fold_ref, *, L):
    scale = float(1.0 / np.sqrt(L * FNET_GDIM))
    half = L // 2
    tr = min(L, FOURIER_TR)
    for r0 in range(0, L, tr):
        pq_ref[r0:r0 + tr, :] = _dot(f_ref[0, r0:r0 + tr, :], ccs_ref[...])
    pq_ref[L:L + FOURIER_PAD, :] = jnp.zeros((FOURIER_PAD, 2 * BRANCH_W), F32)
    blk = min(half, LANES)
    ri = lax.broadcasted_iota(jnp.int32, (blk, blk), 0)
    ci = lax.broadcasted_iota(jnp.int32, (blk, blk), 1)
    exchange = jnp.where(ri + ci == blk - 1, 1.0, 0.0).astype(BF)
    for i in range(half // blk):
        lo = L - blk * (i + 1) + 1
        partner = _dot(exchange, pq_ref[lo:lo + blk, :].astype(BF))
        own = pq_ref[blk * i:blk * (i + 1), :]
        fold_ref[blk * i:blk * (i + 1), :] = (own[:, :BRANCH_W] + partner[:, :BRANCH_W]).astype(BF)
        fold_ref[half + blk * i:half + blk * (i + 1), :] = (own[:, BRANCH_W:] - partner[:, BRANCH_W:]).astype(BF)
    mid = pq_ref[half:half + 1, 0:BRANCH_W]
    for r0 in range(0, L, tr):
        o = _dot(csl_ref[r0:r0 + tr, :], fold_ref[...])
        row = lax.broadcasted_iota(jnp.int32, (tr, 1), 0) + r0
        sign = (1 - 2 * (row % 2)).astype(F32)
        o_ref[0, r0:r0 + tr, :] = ((o + sign * mid) * scale).astype(BF)


def _fourier_call(z, ccs, csl):
    B, L, _ = z.shape
    return pl.pallas_call(
        functools.partial(_fourier_kernel, L=L),
        out_shape=jax.ShapeDtypeStruct((B, L, BRANCH_W), BF),
        grid=(B,),
        in_specs=[
            pl.BlockSpec((1, L, BRANCH_W), lambda b: (b, 0, COL_F // BRANCH_W)),
            pl.BlockSpec((BRANCH_W, 2 * BRANCH_W), lambda b: (0, 0)),
            pl.BlockSpec((L, L), lambda b: (0, 0), pipeline_mode=pl.Buffered(1)),
        ],
        out_specs=pl.BlockSpec((1, L, BRANCH_W), lambda b: (b, 0, 0)),
        scratch_shapes=[
            pltpu.VMEM((L + FOURIER_PAD, 2 * BRANCH_W), F32),
            pltpu.VMEM((L, BRANCH_W), BF),
        ],
        compiler_params=_params("arbitrary"),
        name="fourier_mix",
    )(z, ccs, csl)


def _head(ref, rows, h):
    return ref[0, rows, h * HEAD_DIM:(h + 1) * HEAD_DIM]


def _all_heads_attention(q, k, v, sinks):
    L, width = q.shape
    heads = width // HEAD_DIM
    lane = lax.broadcasted_iota(jnp.int32, (L, width), 1)
    q = q * ATTN_SCALE
    stacked = jnp.concatenate([jnp.where(lane // HEAD_DIM == h, q, jnp.zeros_like(q)) for h in range(heads)],
                              axis=0)
    s = _dot_nt(stacked, k)
    m = jnp.max(s, axis=-1, keepdims=True)
    if sinks is not None:
        row = lax.broadcasted_iota(jnp.int32, (heads * L, 1), 0)
        sink = jnp.zeros((heads * L, 1), F32)
        for h in range(heads):
            sink = jnp.where(row // L == h, sinks[h], sink)
        m = jnp.maximum(m, sink)
    p = jnp.exp(s - m)
    den = jnp.sum(p, axis=-1, keepdims=True)
    if sinks is not None:
        den = den + jnp.exp(sink - m)
    o = _dot(p.astype(BF), v) / den
    out = jnp.zeros((L, width), F32)
    for h in range(heads):
        out = jnp.where(lane // HEAD_DIM == h, o[h * L:(h + 1) * L], out)
    return out.astype(BF)


def _ctx_attn_kernel(sink_ref, qc_ref, kvc_ref, qd_ref, kd_ref, vd_ref, oc_ref, od_ref):
    rows = slice(None)
    group = WIN_Q_HEADS // WIN_KV_HEADS
    k_c = jnp.concatenate([_head(kvc_ref, rows, h // group) for h in range(WIN_Q_HEADS)], axis=1)
    v_c = jnp.concatenate([_head(kvc_ref, rows, WIN_KV_HEADS + h // group) for h in range(WIN_Q_HEADS)], axis=1)
    oc_ref[0] = _all_heads_attention(qc_ref[0], k_c, v_c, [sink_ref[h] for h in range(WIN_Q_HEADS)])
    od_ref[0] = _all_heads_attention(qd_ref[0], kd_ref[0], vd_ref[0], None)


def _zcol_spec(L, col):
    return pl.BlockSpec((1, L, BRANCH_W), lambda b: (b, 0, col // BRANCH_W))


def _ctx_attn_call(z, sink):
    B, L, _ = z.shape
    out = jax.ShapeDtypeStruct((B, L, BRANCH_W), BF)
    ospec = pl.BlockSpec((1, L, BRANCH_W), lambda b: (b, 0, 0))
    return pl.pallas_call(
        _ctx_attn_kernel,
        out_shape=[out, out],
        grid=(B,),
        in_specs=[
            pl.BlockSpec(memory_space=pltpu.SMEM),
            _zcol_spec(L, COL_CQ), _zcol_spec(L, COL_CKV),
            _zcol_spec(L, COL_DQ), _zcol_spec(L, COL_DK), _zcol_spec(L, COL_DV),
        ],
        out_specs=[ospec, ospec],
        compiler_params=_params("arbitrary"),
        name="ctx_attention",
    )(sink, z, z, z, z, z)


PAIR = 2 * HEAD_DIM


def _one_head(x, lane, g):
    return jnp.where(lane // HEAD_DIM == g, x, jnp.zeros_like(x))


def _win_attn_kernel(sink_ref, q_ref, kv_ref, ck_ref, cv_ref, o_ref, *, L):
    nb = L // WIN_BLOCK
    group = WIN_Q_HEADS // WIN_KV_HEADS
    m_rows = group * WIN_BLOCK
    assert WIN_BLOCK == LANES
    row = lax.broadcasted_iota(jnp.int32, (m_rows, WIN_BLOCK), 0)
    a = row % WIN_BLOCK
    j = lax.broadcasted_iota(jnp.int32, (m_rows, WIN_BLOCK), 1)
    row1 = lax.broadcasted_iota(jnp.int32, (m_rows, 1), 0)
    for kvh in range(WIN_KV_HEADS):
        ctx_k = ck_ref[0, 0, kvh].astype(BF)
        ctx_v = cv_ref[0, 0, kvh].astype(BF)
        sink = jnp.zeros((m_rows, 1), F32)
        for g in range(group):
            sink = jnp.where(row1 // WIN_BLOCK == g, sink_ref[kvh * group + g], sink)

        def body(n, carry, kvh=kvh, ctx_k=ctx_k, ctx_v=ctx_v, sink=sink):
            r0 = pl.multiple_of(n * WIN_BLOCK, WIN_BLOCK)
            q = jnp.concatenate(
                [_head(q_ref, pl.ds(r0, WIN_BLOCK), kvh * group + g) for g in range(group)], axis=0)
            q = q * ATTN_SCALE
            s_ctx = _dot(q, ctx_k)
            m_el = jnp.maximum(s_ctx[:, :LANES], s_ctx[:, LANES:])
            scores = []
            vals = []
            for dj in (-1, 0, 1):
                nk = jnp.clip(n + dj, 0, nb - 1)
                rk = pl.multiple_of(nk * WIN_BLOCK, WIN_BLOCK)
                s = _dot_nt(q, _head(kv_ref, pl.ds(rk, WIN_BLOCK), kvh))
                if dj == -1:
                    s = jnp.where((j >= a) & (n >= 1), s, NEG)
                elif dj == 1:
                    s = jnp.where((j <= a) & (n <= nb - 2), s, NEG)
                m_el = jnp.maximum(m_el, s)
                scores.append(s)
                vals.append(_head(kv_ref, pl.ds(rk, WIN_BLOCK), WIN_KV_HEADS + kvh))
            m = jnp.maximum(jnp.max(m_el, axis=-1, keepdims=True), sink)
            p_ctx = jnp.exp(s_ctx - m)
            den_el = p_ctx[:, :LANES] + p_ctx[:, LANES:]
            o = _dot_nt(p_ctx.astype(BF), ctx_v)
            for s, v in zip(scores, vals):
                p = jnp.exp(s - m)
                den_el = den_el + p
                o = o + _dot(p.astype(BF), v)
            den = jnp.sum(den_el, axis=-1, keepdims=True) + jnp.exp(sink - m)
            o = (o / den).astype(BF)
            for g in range(group):
                h = kvh * group + g
                o_ref[0, pl.ds(r0, WIN_BLOCK), h * HEAD_DIM:(h + 1) * HEAD_DIM] = (
                    o[g * WIN_BLOCK:(g + 1) * WIN_BLOCK])
            return carry

        lax.fori_loop(0, nb, body, 0, unroll=4)


def _win_attn_call(z, cache_k, cache_v, sink, layer):
    B, L, _ = z.shape
    P = cache_k.shape[4]
    assert P == 2 * LANES
    cspec = pl.BlockSpec((1, 1, WIN_KV_HEADS, HEAD_DIM, P), lambda b: (b, layer, 0, 0, 0))
    return pl.pallas_call(
        functools.partial(_win_attn_kernel, L=L),
        out_shape=jax.ShapeDtypeStruct((B, L, BRANCH_W), BF),
        grid=(B,),
        in_specs=[
            pl.BlockSpec(memory_space=pltpu.SMEM),
            _zcol_spec(L, COL_CQ), _zcol_spec(L, COL_CKV), cspec, cspec,
        ],
        out_specs=pl.BlockSpec((1, L, BRANCH_W), lambda b: (b, 0, 0)),
        compiler_params=_params("arbitrary"),
        name="window_attention",
    )(sink, z, z, cache_k, cache_v)


N_ROW_OFF = 2 * NA_ROWS - 1
N_COL_OFF = 2 * NA_COLS - 1


NA_QROWS = 4
NA_WIN_ROWS = NA_ROWS + NA_QROWS


def _na_window_start(r0, rows):
    return min(max(r0 - NA_ROWS // 2, 0), rows - NA_WIN_ROWS)


def _na_group_blocks(r0, rows):
    ws = _na_window_start(r0, rows)
    blocks = {}
    for i in range(NA_QROWS):
        r = r0 + i
        start = min(max(r - NA_ROWS // 2, 0), rows - NA_ROWS)
        for kk in range(NA_WIN_ROWS):
            rel = ws + kk - start
            blocks[i, kk] = ws + kk - r + (NA_ROWS - 1) if 0 <= rel < NA_ROWS else None
    return blocks


def _na_patterns(rows):
    layouts, first = [], []
    for qi in range(rows // NA_QROWS):
        blocks = _na_group_blocks(qi * NA_QROWS, rows)
        if not layouts or blocks != layouts[-1]:
            assert blocks not in layouts
            layouts.append(blocks)
            first.append(qi)
    return layouts, first


def _na_bias_kernel(rb_ref, o_ref, *, rows):
    h = pl.program_id(0)
    cq = lax.broadcasted_iota(jnp.int32, (GRID_W, GRID_W), 0)
    ck = lax.broadcasted_iota(jnp.int32, (GRID_W, GRID_W), 1)
    col_start = jnp.clip(cq - NA_COLS // 2, 0, GRID_W - NA_COLS)
    col_ok = (ck >= col_start) & (ck < col_start + NA_COLS)
    col_off = jnp.clip(ck - cq + (NA_COLS - 1), 0, N_COL_OFF - 1)
    tiles = []
    for ro in range(N_ROW_OFF):
        t = jnp.zeros((GRID_W, GRID_W), F32)
        for co in range(N_COL_OFF):
            t = jnp.where(col_off == co, rb_ref[(h * N_ROW_OFF + ro) * N_COL_OFF + co], t)
        tiles.append(jnp.where(col_ok, t, NEG))
    masked = jnp.full((GRID_W, GRID_W), NEG, F32)
    for p, blocks in enumerate(_na_patterns(rows)[0]):
        for (i, kk), ro in blocks.items():
            o_ref[0, p, i * GRID_W:(i + 1) * GRID_W, kk * GRID_W:(kk + 1) * GRID_W] = (
                masked if ro is None else tiles[ro])


def _na_bias_call(na_bias, rows):
    shape = (NA_HEADS, len(_na_patterns(rows)[0]), NA_QROWS * GRID_W, NA_WIN_ROWS * GRID_W)
    return pl.pallas_call(
        functools.partial(_na_bias_kernel, rows=rows),
        out_shape=jax.ShapeDtypeStruct(shape, F32),
        grid=(NA_HEADS,),
        in_specs=[pl.BlockSpec(memory_space=pltpu.SMEM)],
        out_specs=pl.BlockSpec((1,) + shape[1:], lambda h: (h, 0, 0, 0)),
        compiler_params=_params("arbitrary"),
        name="na_bias_table",
    )(na_bias.reshape(-1))


def _na_attn_kernel(q_ref, k_ref, v_ref, ck_ref, cv_ref, bt_ref, o_ref, *, L):
    rows = L // GRID_W
    groups = rows // NA_QROWS
    pattern_starts = _na_patterns(rows)[1]
    lane = lax.broadcasted_iota(jnp.int32, (NA_QROWS * GRID_W, PAIR), 1)
    for hp in range(NA_HEADS * HEAD_DIM // PAIR):
        heads = (2 * hp, 2 * hp + 1)
        lanes = slice(hp * PAIR, (hp + 1) * PAIR)
        ctx_k = jnp.concatenate([ck_ref[0, 0, h].astype(BF) for h in heads], axis=0)
        ctx_v = jnp.concatenate([cv_ref[0, 0, h].astype(BF) for h in heads], axis=0)

        def body(qi, carry, heads=heads, lanes=lanes, ctx_k=ctx_k, ctx_v=ctx_v):
            r0 = qi * NA_QROWS
            ws = jnp.clip(r0 - NA_ROWS // 2, 0, rows - NA_WIN_ROWS)
            pat = sum(jnp.where(qi >= first, 1, 0) for first in pattern_starts[1:])
            rq = pl.ds(pl.multiple_of(r0 * GRID_W, NA_QROWS * GRID_W), NA_QROWS * GRID_W)
            rk = pl.ds(pl.multiple_of(ws * GRID_W, GRID_W), NA_WIN_ROWS * GRID_W)
            q2 = q_ref[0, rq, lanes] * ATTN_SCALE
            k2 = k_ref[0, rk, lanes]
            v2 = v_ref[0, rk, lanes]
            outs = []
            for g, h in enumerate(heads):
                qg = _one_head(q2, lane, g)
                s = _dot_nt(qg, k2) + bt_ref[h, pat]
                s_ctx = _dot(qg, ctx_k)
                m = jnp.maximum(jnp.max(s, axis=-1, keepdims=True), jnp.max(s_ctx, axis=-1, keepdims=True))
                p = jnp.exp(s - m)
                p_ctx = jnp.exp(s_ctx - m)
                den = jnp.sum(p, axis=-1, keepdims=True) + jnp.sum(p_ctx, axis=-1, keepdims=True)
                outs.append((_dot(p.astype(BF), v2) + _dot_nt(p_ctx.astype(BF), ctx_v)) / den)
            o_ref[0, rq, lanes] = jnp.where(lane // HEAD_DIM == 0, outs[0], outs[1]).astype(BF)
            return carry

        lax.fori_loop(0, groups, body, 0, unroll=4)


def _na_attn_call(z, cache_k, cache_v, bias_table, layer):
    B, L, _ = z.shape
    P = cache_k.shape[4]
    assert L // GRID_W >= NA_WIN_ROWS + NA_QROWS and (L // GRID_W) % NA_QROWS == 0
    cspec = pl.BlockSpec((1, 1, NA_HEADS, HEAD_DIM, P), lambda b: (b, layer, 0, 0, 0))
    return pl.pallas_call(
        functools.partial(_na_attn_kernel, L=L),
        out_shape=jax.ShapeDtypeStruct((B, L, BRANCH_W), BF),
        grid=(B,),
        in_specs=[
            _zcol_spec(L, COL_DQ), _zcol_spec(L, COL_DK), _zcol_spec(L, COL_DV), cspec, cspec,
            pl.BlockSpec(bias_table.shape, lambda b: (0, 0, 0, 0), pipeline_mode=pl.Buffered(1)),
        ],
        out_specs=pl.BlockSpec((1, L, BRANCH_W), lambda b: (b, 0, 0)),
        compiler_params=_params("arbitrary"),
        name="neighbourhood_attention",
    )(z, z, z, cache_k, cache_v, bias_table)


def _merge_kernel(x_ref, g_ref, sc_ref, sh_ref, gate_ref, ba_ref, bb_ref, bc_ref, bd_ref,
                  wg0_ref, wg1_ref, wb_ref, wo_ref, o_ref):
    x = x_ref[0]
    h = _norm_mod(x, g_ref[...], sc_ref[0], sh_ref[0]).astype(BF)
    merged = jnp.zeros(x.shape, F32)
    per_ref = MIX_COLS // D_MODEL
    for n, br_ref in enumerate((ba_ref, bb_ref, bc_ref, bd_ref)):
        wg_ref = (wg0_ref, wg1_ref)[n // per_ref]
        c0 = (n % per_ref) * D_MODEL
        gate = jax.nn.sigmoid(_dot(h, wg_ref[0, :, c0:c0 + D_MODEL]))
        merged = merged + gate * _dot(br_ref[0], wb_ref[0, n])
    o_ref[0] = x + gate_ref[0] * _dot(merged.astype(BF), wo_ref[0])


def _merge_call(x, norm_g, mod, branches, w_in_bf, w_branch, w_out, layer, tm):
    B, L, _ = x.shape
    bm = mod.shape[0]
    xspec = pl.BlockSpec((1, tm, D_MODEL), lambda b, i: (b, i, 0))
    bspec = pl.BlockSpec((1, tm, BRANCH_W), lambda b, i: (b, i, 0))
    return pl.pallas_call(
        _merge_kernel,
        out_shape=jax.ShapeDtypeStruct((B, L, D_MODEL), F32),
        grid=(B, L // tm),
        in_specs=[
            xspec,
            pl.BlockSpec((1, D_MODEL), lambda b, i: (0, 0)),
            _mod_spec(bm, 1, 2), _mod_spec(bm, 0, 2), _mod_spec(bm, 2, 2),
            bspec, bspec, bspec, bspec,
            pl.BlockSpec((1, D_MODEL, MIX_COLS), lambda b, i: (layer, 0, 1)),
            pl.BlockSpec((1, D_MODEL, MIX_COLS), lambda b, i: (layer, 0, 2)),
            pl.BlockSpec((1, N_BRANCH, BRANCH_W, D_MODEL), lambda b, i: (layer, 0, 0, 0)),
            pl.BlockSpec((1, D_MODEL, D_MODEL), lambda b, i: (layer, 0, 0)),
        ],
        out_specs=xspec,
        compiler_params=_params("arbitrary", "arbitrary"),
        name="merge_out_proj",
    )(x, norm_g, mod, mod, mod, *branches, w_in_bf, w_in_bf, w_branch, w_out)


ROUTE_CHUNK = 256
GATHER_ROWS = 512
SIGN_BIT = 31
SEARCH_BITS = 4
SEARCH_TOP_SHIFT = ((SIGN_BIT - 1) // SEARCH_BITS) * SEARCH_BITS


def _prefix_excl(mask_ref, out_ref, tri, L):
    totals = [jnp.sum(mask_ref[:, c0:c0 + LANES], axis=-1, keepdims=True) for c0 in range(0, L, LANES)]
    run = jnp.zeros((N_EXPERTS, 1), F32)
    for i, c0 in enumerate(range(0, L, LANES)):
        out_ref[:, c0:c0 + LANES] = _dot(mask_ref[:, c0:c0 + LANES].astype(BF), tri) + run
        run = run + totals[i]


def _route_kernel(x_ref, g_ref, sc_ref, sh_ref, wh_ref, wl_ref, tri_ref,
                  h_ref, gate_ref, slot_ref, slot_t_ref, off_ref, lt_ref, msk_ref, pre_ref, *, L, cap):
    g = g_ref[...]
    sc = sc_ref[0]
    sh = sh_ref[0]
    wh = wh_ref[...]
    wl = wl_ref[...]
    rc = min(L, ROUTE_CHUNK)
    for r0 in range(0, L, rc):
        h = _norm_mod(x_ref[0, r0:r0 + rc, :], g, sc, sh)
        h_hi = h.astype(BF)
        h_ref[0, r0:r0 + rc, :] = h_hi
        h_lo = (h - h_hi.astype(F32)).astype(BF)
        lt_ref[:, r0:r0 + rc] = _dot_nt(wh, h_hi) + (_dot_nt(wh, h_lo) + _dot_nt(wl, h_hi))
    lt = lt_ref[...]
    e = jnp.exp(lt - jnp.max(lt, axis=0, keepdims=True))
    aff = e / jnp.sum(e, axis=0, keepdims=True)
    thr = jnp.zeros((N_EXPERTS, 1), jnp.int32)
    for shift in range(SEARCH_TOP_SHIFT, -1, -SEARCH_BITS):
        n_digits = min(1 << SEARCH_BITS, 1 << (SIGN_BIT - shift))
        digit = jnp.zeros((N_EXPERTS, 1), jnp.int32)
        for d in range(1, n_digits):
            cand = pltpu.bitcast(thr | (d << shift), F32)
            cnt = jnp.sum((aff >= cand).astype(F32), axis=-1, keepdims=True)
            digit = digit + (cnt >= cap).astype(jnp.int32)
        thr = thr | (digit << shift)
    thr_f = pltpu.bitcast(thr, F32)
    gt = aff > thr_f
    eq = aff == thr_f
    need = cap - jnp.sum(gt.astype(F32), axis=-1, keepdims=True)
    tri = tri_ref[...]
    msk_ref[...] = eq.astype(F32)
    _prefix_excl(msk_ref, pre_ref, tri, L)
    sel = gt | (eq & (pre_ref[...] < need))
    msk_ref[...] = sel.astype(F32)
    _prefix_excl(msk_ref, pre_ref, tri, L)
    off_ref[0] = jnp.concatenate([pre_ref[:, c:c + 1] for c in range(0, L, min(L, COMBINE_TL))], axis=1)
    slot = jnp.where(sel, pre_ref[...], -1.0)
    slot_ref[0] = slot
    slot_t_ref[0] = slot.T
    gate_ref[0] = jnp.where(sel, aff, 0.0)


def _route_call(x, norm_g, mod, wr_hi, wr_lo):
    B, L, _ = x.shape
    bm = mod.shape[0]
    cap = EC_CAPACITY * L // N_EXPERTS
    n_tiles = L // min(L, COMBINE_TL)
    tri = jnp.asarray(np.triu(np.ones((LANES, LANES)), 1).astype(ml_dtypes.bfloat16))
    wspec = pl.BlockSpec((N_EXPERTS, D_MODEL), lambda b: (0, 0))
    return pl.pallas_call(
        functools.partial(_route_kernel, L=L, cap=cap),
        out_shape=[
            jax.ShapeDtypeStruct((B, L, D_MODEL), BF),
            jax.ShapeDtypeStruct((B, N_EXPERTS, L), F32),
            jax.ShapeDtypeStruct((B, N_EXPERTS, L), F32),
            jax.ShapeDtypeStruct((B, L, N_EXPERTS), F32),
            jax.ShapeDtypeStruct((B, N_EXPERTS, n_tiles), F32),
        ],
        grid=(B,),
        in_specs=[
            pl.BlockSpec((1, L, D_MODEL), lambda b: (b, 0, 0)),
            pl.BlockSpec((1, D_MODEL), lambda b: (0, 0)),
            _mod_spec(bm, 4, 1), _mod_spec(bm, 3, 1),
            wspec, wspec,
            pl.BlockSpec((LANES, LANES), lambda b: (0, 0)),
        ],
        out_specs=[
            pl.BlockSpec((1, L, D_MODEL), lambda b: (b, 0, 0)),
            pl.BlockSpec((1, N_EXPERTS, L), lambda b: (b, 0, 0)),
            pl.BlockSpec((1, N_EXPERTS, L), lambda b: (b, 0, 0)),
            pl.BlockSpec((1, L, N_EXPERTS), lambda b: (b, 0, 0)),
            pl.BlockSpec((1, N_EXPERTS, n_tiles), lambda b: (b, 0, 0)),
        ],
        scratch_shapes=[
            pltpu.VMEM((N_EXPERTS, L), F32),
            pltpu.VMEM((N_EXPERTS, L), F32),
            pltpu.VMEM((N_EXPERTS, L), F32),
        ],
        compiler_params=_params("arbitrary"),
        name="ec_route",
    )(x, norm_g, mod, mod, wr_hi, wr_lo, tri)


COMBINE_TL = 256
COMBINE_WIN = 64
BF_TILE = 16


def _list_windows(off_ref, cap):
    n_tiles = pl.num_programs(1)
    base = (pl.program_id(0) * n_tiles + pl.program_id(1)) * N_EXPERTS
    starts = []
    fits = None
    for ex in range(N_EXPERTS):
        first = off_ref[base + ex]
        end = jnp.where(pl.program_id(1) == n_tiles - 1, cap, off_ref[base + N_EXPERTS + ex])
        start = jnp.minimum((first // BF_TILE) * BF_TILE, cap - COMBINE_WIN)
        ok = end <= start + COMBINE_WIN
        fits = ok if fits is None else (fits & ok)
        starts.append(start)
    return starts, fits


def _gather_kernel(off_ref, slot_ref, gate_ref, h_ref, xs_ref, gs_ref, *, cap):
    h = h_ref[0]
    tl = h.shape[0]

    @pl.when(pl.program_id(1) == 0)
    def _():
        xs_ref[...] = jnp.zeros(xs_ref.shape, BF)
        gs_ref[...] = jnp.zeros(gs_ref.shape, F32)

    def select(ex, first, rows):
        pos = lax.broadcasted_iota(jnp.int32, (rows, tl), 0).astype(F32)
        hit = slot_ref[0, ex:ex + 1, :] - first == pos
        gate = jnp.sum(jnp.where(hit, gate_ref[0, ex:ex + 1, :], 0.0), axis=-1, keepdims=True)
        return jnp.where(hit, 1.0, 0.0).astype(BF), jnp.broadcast_to(gate, (rows, LANES))

    def gather_all():
        ge = max(1, min(N_EXPERTS, GATHER_ROWS // cap))
        for e0 in range(0, N_EXPERTS, ge):
            picked = [select(ex, 0.0, cap) for ex in range(e0, e0 + ge)]
            rows = _dot(jnp.concatenate([p[0] for p in picked], axis=0), h).astype(BF)
            for j in range(ge):
                xs_ref[e0 + j, 0] += rows[j * cap:(j + 1) * cap]
                gs_ref[e0 + j, 0] += picked[j][1]

    if cap <= COMBINE_WIN:
        gather_all()
        return

    starts, fits = _list_windows(off_ref, cap)

    @pl.when(fits)
    def _():
        picked = [select(ex, starts[ex].astype(F32), COMBINE_WIN) for ex in range(N_EXPERTS)]
        rows = _dot(jnp.concatenate([p[0] for p in picked], axis=0), h).astype(BF)
        for ex in range(N_EXPERTS):
            win = pl.ds(pl.multiple_of(starts[ex], BF_TILE), COMBINE_WIN)
            xs_ref[ex, 0, win, :] += rows[ex * COMBINE_WIN:(ex + 1) * COMBINE_WIN]
            gs_ref[ex, 0, win, :] += picked[ex][1]

    @pl.when(jnp.logical_not(fits))
    def _():
        gather_all()


def _gather_call(offsets, slot, gate_rows, h2):
    B, L, _ = h2.shape
    cap = EC_CAPACITY * L // N_EXPERTS
    tl = min(L, COMBINE_TL)
    assert cap <= COMBINE_WIN or (cap % BF_TILE == 0 and COMBINE_WIN % BF_TILE == 0)
    return pl.pallas_call(
        functools.partial(_gather_kernel, cap=cap),
        out_shape=[
            jax.ShapeDtypeStruct((N_EXPERTS, B, cap, D_MODEL), BF),
            jax.ShapeDtypeStruct((N_EXPERTS, B, cap, LANES), F32),
        ],
        grid_spec=pltpu.PrefetchScalarGridSpec(
            num_scalar_prefetch=1,
            grid=(B, L // tl),
            in_specs=[
                pl.BlockSpec((1, N_EXPERTS, tl), lambda b, i, off: (b, 0, i)),
                pl.BlockSpec((1, N_EXPERTS, tl), lambda b, i, off: (b, 0, i)),
                pl.BlockSpec((1, tl, D_MODEL), lambda b, i, off: (b, i, 0)),
            ],
            out_specs=[
                pl.BlockSpec((N_EXPERTS, 1, cap, D_MODEL), lambda b, i, off: (0, b, 0, 0)),
                pl.BlockSpec((N_EXPERTS, 1, cap, LANES), lambda b, i, off: (0, b, 0, 0)),
            ],
        ),
        compiler_params=_params("arbitrary", "arbitrary"),
        name="ec_gather",
    )(offsets, slot, gate_rows, h2)


EXPERT_TR = 512


def _expert_kernel(xa_ref, ga_ref, xb_ref, gb_ref, wg_ref, wu_ref, wd_ref, ya_ref, yb_ref,
                   wgb_ref, wub_ref, wdb_ref, *, tiles_a):
    j = pl.program_id(1)

    @pl.when(j == 0)
    def _():
        wgb_ref[...] = wg_ref[0, 0].astype(BF)
        wub_ref[...] = wu_ref[0, 0].astype(BF)
        wdb_ref[...] = wd_ref[0, 0].astype(BF)

    def ffn(x_ref, g_ref, y_ref):
        x = x_ref[0]
        a = _dot(x, wgb_ref[...])
        u = _dot(x, wub_ref[...])
        y = _dot((_silu(a) * u).astype(BF), wdb_ref[...])
        y_ref[0] = (y * g_ref[0, :, 0:1]).astype(BF)

    @pl.when(j < tiles_a)
    def _():
        ffn(xa_ref, ga_ref, ya_ref)

    @pl.when(j >= tiles_a)
    def _():
        ffn(xb_ref, gb_ref, yb_ref)


def _expert_call(xs_a, gs_a, xs_b, gs_b, w_gate, w_up, w_down, layer):
    E, ra, _ = xs_a.shape
    rb = xs_b.shape[1]
    tr = math.gcd(math.gcd(ra, rb), EXPERT_TR)
    tiles_a, tiles_b = ra // tr, rb // tr
    amap = lambda e, j: (e, jnp.minimum(j, tiles_a - 1), 0)
    bmap = lambda e, j: (e, jnp.maximum(j - tiles_a, 0), 0)
    wspec_in = pl.BlockSpec((1, 1, D_MODEL, EXPERT_FF), lambda e, j: (layer, e, 0, 0))
    wspec_out = pl.BlockSpec((1, 1, EXPERT_FF, D_MODEL), lambda e, j: (layer, e, 0, 0))
    return pl.pallas_call(
        functools.partial(_expert_kernel, tiles_a=tiles_a),
        out_shape=[jax.ShapeDtypeStruct((E, ra, D_MODEL), BF), jax.ShapeDtypeStruct((E, rb, D_MODEL), BF)],
        grid=(E, tiles_a + tiles_b),
        in_specs=[
            pl.BlockSpec((1, tr, D_MODEL), amap), pl.BlockSpec((1, tr, LANES), amap),
            pl.BlockSpec((1, tr, D_MODEL), bmap), pl.BlockSpec((1, tr, LANES), bmap),
            wspec_in, wspec_in, wspec_out,
        ],
        out_specs=[pl.BlockSpec((1, tr, D_MODEL), amap), pl.BlockSpec((1, tr, D_MODEL), bmap)],
        scratch_shapes=[
            pltpu.VMEM((D_MODEL, EXPERT_FF), BF),
            pltpu.VMEM((D_MODEL, EXPERT_FF), BF),
            pltpu.VMEM((EXPERT_FF, D_MODEL), BF),
        ],
        compiler_params=_params("arbitrary", "arbitrary"),
        name="ec_experts",
    )(xs_a, gs_a, xs_b, gs_b, w_gate, w_up, w_down)


def _combine_kernel(off_ref, st_ref, y_ref, x_ref, gate_ref, fg_ref, o_ref, *, cap, final):
    st = st_ref[0]
    tl = st.shape[0]

    def finish(scattered):
        v = x_ref[0] + gate_ref[0] * scattered
        if final:
            v = v * lax.rsqrt(jnp.mean(v * v, axis=-1, keepdims=True) + EPS) * fg_ref[...]
        o_ref[0] = v

    def onehot(width, starts):
        n = N_EXPERTS * width
        owner = lax.broadcasted_iota(jnp.int32, (N_EXPERTS, n), 1) // width
        spread = jnp.where(owner == lax.broadcasted_iota(jnp.int32, (N_EXPERTS, n), 0), 1.0, 0.0).astype(BF)
        slot = _dot(st.astype(BF), spread)
        lane = lax.broadcasted_iota(jnp.int32, (1, n), 1)
        want = (lane % width).astype(F32)
        if starts is not None:
            for ex in range(N_EXPERTS):
                want = want + jnp.where(lane // width == ex, starts[ex].astype(F32), 0.0)
        return jnp.where(slot == want, 1.0, 0.0).astype(BF)

    def scatter_all():
        finish(_dot(onehot(cap, None), y_ref[:, 0].reshape(N_EXPERTS * cap, D_MODEL)))

    if cap <= COMBINE_WIN:
        scatter_all()
        return

    starts, fits = _list_windows(off_ref, cap)

    @pl.when(fits)
    def _():
        rows = [y_ref[ex, 0, pl.ds(pl.multiple_of(starts[ex], BF_TILE), COMBINE_WIN), :]
                for ex in range(N_EXPERTS)]
        finish(_dot(onehot(COMBINE_WIN, starts), jnp.concatenate(rows, axis=0)))

    @pl.when(jnp.logical_not(fits))
    def _():
        scatter_all()


def _combine_call(slot_t, offsets, y, x, mod, final_g, final):
    B, L, _ = x.shape
    bm = mod.shape[0]
    cap = y.shape[2]
    tl = min(L, COMBINE_TL)
    assert cap <= COMBINE_WIN or (cap % BF_TILE == 0 and COMBINE_WIN % BF_TILE == 0)
    assert cap <= 256
    gate_map =(lambda b, i, off: (b, 0, 5)) if bm > 1 else (lambda b, i, off: (0, 0, 5))
    return pl.pallas_call(
        functools.partial(_combine_kernel, cap=cap, final=final),
        out_shape=jax.ShapeDtypeStruct((B, L, D_MODEL), F32),
        grid_spec=pltpu.PrefetchScalarGridSpec(
            num_scalar_prefetch=1,
            grid=(B, L // tl),
            in_specs=[
                pl.BlockSpec((1, tl, N_EXPERTS), lambda b, i, off: (b, i, 0)),
                pl.BlockSpec((N_EXPERTS, 1, cap, D_MODEL), lambda b, i, off: (0, b, 0, 0)),
                pl.BlockSpec((1, tl, D_MODEL), lambda b, i, off: (b, i, 0)),
                pl.BlockSpec((1, 1, D_MODEL), gate_map),
                pl.BlockSpec((1, D_MODEL), lambda b, i, off: (0, 0)),
            ],
            out_specs=pl.BlockSpec((1, tl, D_MODEL), lambda b, i, off: (b, i, 0)),
        ),
        compiler_params=_params("arbitrary", "arbitrary"),
        name="ec_combine",
    )(offsets, slot_t, y, x, mod, final_g)


def _mixer_half(x, mod, lw, ctx, tm, kv_prev=None):
    latent = ctx is not None
    layer = lw["layer"]
    res = _inproj_call(x, lw["norm1_g"], mod, lw["w_in"], layer, tm, rope=latent, want_kv=not latent,
                       kv_prev=kv_prev)
    z = res[0]
    br_a = _conv_call(z, lw["conv_w"], lw["conv_b"], lw["conv_ln_g"], lw["conv_ln_b"])
    br_b = _fourier_call(z, *lw["dft"][z.shape[1]])
    if latent:
        br_c = _win_attn_call(z, ctx["win_k"], ctx["win_v"], lw["win_sink"], layer)
        br_d = _na_attn_call(z, ctx["nat_k"], ctx["nat_v"], lw["bias_table"], layer)
    else:
        br_c, br_d = _ctx_attn_call(z, lw["win_sink"])
    x = _merge_call(x, lw["norm1_g"], mod, (br_a, br_b, br_c, br_d),
                    lw["w_in"], lw["w_branch"], lw["w_out"], layer, tm)
    return x, (tuple(res[1:]) if not latent else None)


def kernel(x_prompt, x_sample, cache_win_k, cache_win_v, cache_nat_k, cache_nat_v, c, c_ctx,
           norm1_g, norm2_g, w_ada, b_ada, w_in, conv_w, conv_b, conv_ln_g, conv_ln_b,
           win_sink, na_bias, w_branch, w_out, w_router, w_e_gate, w_e_up, w_e_down, final_norm_g):
    n_dec = c.shape[0]
    cond = jnp.zeros((ADA_ROWS, D_MODEL), F32).at[0].set(c_ctx).at[1:1 + n_dec].set(c)
    mods = _ada_call(cond, w_ada, b_ada)
    final_g = final_norm_g.reshape(1, D_MODEL)
    w_in_bf = _cast_call(w_in)
    w_branch_bf = _cast_call(w_branch)
    w_out_bf = _cast_call(w_out)
    dft = {L: _dft_tables(L) for L in (x_prompt.shape[1], x_sample.shape[1])}
    dim_major = lambda a: jnp.swapaxes(a, -1, -2)
    ctx = {"win_k": dim_major(cache_win_k), "win_v": dim_major(cache_win_v),
           "nat_k": dim_major(cache_nat_k), "nat_v": dim_major(cache_nat_v)}

    xp, xs = x_prompt, x_sample
    caches = None
    for l in range(DEPTH):
        wr_t = w_router[l].T
        wr_hi = wr_t.astype(BF)
        wr_lo = (wr_t - wr_hi.astype(F32)).astype(BF)
        lw = {
            "layer": l,
            "norm1_g": norm1_g[l].reshape(1, D_MODEL),
            "w_in": w_in_bf, "w_branch": w_branch_bf, "w_out": w_out_bf, "dft": dft,
            "conv_w": conv_w[l], "conv_b": conv_b[l],
            "conv_ln_g": conv_ln_g[l], "conv_ln_b": conv_ln_b[l],
            "win_sink": win_sink[l],
            "bias_table": _na_bias_call(na_bias[l], x_sample.shape[1] // GRID_W),
        }
        norm2 = norm2_g[l].reshape(1, D_MODEL)
        final = l == DEPTH - 1
        mod_p = mods[l, 0:1].reshape(1, 1, 6 * D_MODEL)
        mod_s = mods[l, 1:1 + n_dec].reshape(n_dec, 1, 6 * D_MODEL)
        xp, caches = _mixer_half(xp, mod_p, lw, None, tm=256, kv_prev=caches)
        xs, _ = _mixer_half(xs, mod_s, lw, ctx, tm=512)
        smem = lambda off: jnp.pad(off.transpose(0, 2, 1).astype(jnp.int32).reshape(-1), (0, N_EXPERTS))
        h_p, g_p, slot_p, slot_t_p, off_p = _route_call(xp, norm2, mod_p, wr_hi, wr_lo)
        h_s, g_s, slot_s, slot_t_s, off_s = _route_call(xs, norm2, mod_s, wr_hi, wr_lo)
        off_p, off_s = smem(off_p), smem(off_s)
        rows_p, gates_p = _gather_call(off_p, slot_p, g_p, h_p)
        rows_s, gates_s = _gather_call(off_s, slot_s, g_s, h_s)
        flat = lambda a: a.reshape(N_EXPERTS, a.shape[1] * a.shape[2], a.shape[3])
        y_p, y_s = _expert_call(flat(rows_p), flat(gates_p), flat(rows_s), flat(gates_s),
                                w_e_gate, w_e_up, w_e_down, l)
        xp = _combine_call(slot_t_p, off_p, y_p.reshape(rows_p.shape), xp, mod_p, final_g, final)
        xs = _combine_call(slot_t_s, off_s, y_s.reshape(rows_s.shape), xs, mod_s, final_g, final)

    return (xp, xs) + tuple(dim_major(c) for c in caches)
```

```python
import functools
import math

import ml_dtypes
import numpy as np
import jax
import jax.numpy as jnp
from jax import lax
from jax.experimental import pallas as pl
from jax.experimental.pallas import tpu as pltpu

D_MODEL = 1024
DEPTH = 2
GRID_W = 64
BRANCH_W = 256
HEAD_DIM = 64
N_BRANCH = 4
CONV_K = 31
FNET_GROUPS = 4
FNET_GDIM = BRANCH_W // FNET_GROUPS
WIN_Q_HEADS = 4
WIN_KV_HEADS = 2
WIN_BLOCK = 128
NA_HEADS = 4
NA_ROWS = 8
NA_COLS = 16
ROPE_THETA = 10000.0
ATTN_SCALE = HEAD_DIM ** -0.5
N_EXPERTS = 16
EXPERT_FF = 1024
EC_CAPACITY = 2
EPS = 1e-6
NEG = -1e30

MIX_COLS = 2048
GATE_COLS = N_BRANCH * D_MODEL
IN_COLS = MIX_COLS + GATE_COLS
COL_A, COL_F, COL_CQ, COL_CKV, COL_DQ, COL_DK, COL_DV = 0, 512, 768, 1024, 1280, 1536, 1792
ROPE_LO, ROPE_HI = COL_CQ, COL_CKV + WIN_KV_HEADS * HEAD_DIM
KV_COLS = (COL_CKV, COL_CKV + WIN_KV_HEADS * HEAD_DIM, COL_DK, COL_DV)
KV_HEADS = (WIN_KV_HEADS, WIN_KV_HEADS, NA_HEADS, NA_HEADS)

LANES = 128
SUBLANES = 8
VMEM_LIMIT = 56 * 1024 * 1024

BF = jnp.bfloat16
F32 = jnp.float32


def _params(*sem):
    return pltpu.CompilerParams(dimension_semantics=sem, vmem_limit_bytes=VMEM_LIMIT)


def _dot(a, b):
    return jnp.dot(a, b, preferred_element_type=F32)


def _dot_nt(a, b):
    return lax.dot_general(a, b, (((1,), (1,)), ((), ())), preferred_element_type=F32)


def _norm_mod(x, g, sc, sh):
    y = x * lax.rsqrt(jnp.mean(x * x, axis=-1, keepdims=True) + EPS)
    return (y * g) * (1.0 + sc) + sh


def _silu(x):
    return x * jax.nn.sigmoid(x)


CAST_BYTES = 4 * 1024 * 1024


def _cast_kernel(x_ref, o_ref):
    o_ref[...] = x_ref[...].astype(BF)


def _cast_call(x):
    shape = x.shape
    x2 = x.reshape(-1, shape[-1])
    rows, cols = x2.shape
    tr = min(rows, 1 << (max(16, CAST_BYTES // (4 * cols)).bit_length() - 1))
    assert rows % tr == 0
    out = pl.pallas_call(
        _cast_kernel,
        out_shape=jax.ShapeDtypeStruct((rows, cols), BF),
        grid=(rows // tr,),
        in_specs=[pl.BlockSpec((tr, cols), lambda i: (i, 0))],
        out_specs=pl.BlockSpec((tr, cols), lambda i: (i, 0)),
        compiler_params=_params("arbitrary"),
        name="cast_bf16",
    )(x2)
    return out.reshape(shape)


ADA_ROWS = 16
ADA_TN = 1536


def _ada_kernel(c_ref, w_ref, b_ref, o_ref):
    s = _silu(c_ref[...]).astype(BF)
    o_ref[0] = _dot(s, w_ref[0].astype(BF)) + b_ref[0]


def _ada_call(cond, w_ada, b_ada):
    n = 6 * D_MODEL
    return pl.pallas_call(
        _ada_kernel,
        out_shape=jax.ShapeDtypeStruct((DEPTH, ADA_ROWS, n), F32),
        grid=(DEPTH, n // ADA_TN),
        in_specs=[
            pl.BlockSpec((ADA_ROWS, D_MODEL), lambda l, j: (0, 0)),
            pl.BlockSpec((1, D_MODEL, ADA_TN), lambda l, j: (l, 0, j)),
            pl.BlockSpec((1, 1, ADA_TN), lambda l, j: (l, 0, j)),
        ],
        out_specs=pl.BlockSpec((1, ADA_ROWS, ADA_TN), lambda l, j: (l, 0, j)),
        compiler_params=_params("arbitrary", "arbitrary"),
        name="ada_mod",
    )(cond, w_ada, b_ada.reshape(DEPTH, 1, n))


def _mod_spec(bm, chunk, ngrid):
    if ngrid == 1:
        imap = (lambda b: (b, 0, chunk)) if bm > 1 else (lambda b: (0, 0, chunk))
    else:
        imap = (lambda b, i: (b, 0, chunk)) if bm > 1 else (lambda b, i: (0, 0, chunk))
    return pl.BlockSpec((1, 1, D_MODEL), imap)


def _inproj_kernel(*refs, rope, kv_layer, kv_aliased):
    x_ref, g_ref, sc_ref, sh_ref, w_ref = refs[:5]
    pos = 5
    if rope:
        cos_ref, sin_ref = refs[pos:pos + 2]
        pos += 2
    if kv_aliased:
        pos += len(KV_HEADS)
    z_ref = refs[pos]
    h = _norm_mod(x_ref[0], g_ref[...], sc_ref[0], sh_ref[0]).astype(BF)
    z = _dot(h, w_ref[0])
    if kv_layer is not None:
        for kv_ref, c0, heads in zip(refs[pos + 1:pos + 1 + len(KV_HEADS)], KV_COLS, KV_HEADS):
            here = 0 if kv_aliased else kv_layer
            for hd in range(heads):
                kv_ref[0, here, hd] = z[:, c0 + hd * HEAD_DIM:c0 + (hd + 1) * HEAD_DIM].T
            for other in range(kv_ref.shape[1]):
                if other != here:
                    kv_ref[0, other] = jnp.zeros(kv_ref.shape[2:], F32)
    if rope:
        z_ref[0, :, 0:ROPE_LO] = z[:, 0:ROPE_LO].astype(BF)
        lane = lax.broadcasted_iota(jnp.int32, (z.shape[0], LANES), 1)
        first_half = (lane % 32) < 16
        cos = cos_ref[...]
        sin = sin_ref[...]
        for c0 in range(ROPE_LO, ROPE_HI, LANES):
            xc = z[:, c0:c0 + LANES]
            partner = jnp.where(first_half, pltpu.roll(xc, LANES - 16, 1), pltpu.roll(xc, 16, 1))
            z_ref[0, :, c0:c0 + LANES] = (xc * cos + partner * sin).astype(BF)
        z_ref[0, :, ROPE_HI:MIX_COLS] = z[:, ROPE_HI:MIX_COLS].astype(BF)
    else:
        z_ref[0] = z.astype(BF)


def _rope_tables(L):
    t = np.arange(L)
    half = HEAD_DIM // 4
    freqs = 1.0 / (ROPE_THETA ** (np.arange(half, dtype=np.float64) / half))
    ang_r = (t // GRID_W).astype(np.float64)[:, None] * freqs[None, :]
    ang_c = (t % GRID_W).astype(np.float64)[:, None] * freqs[None, :]
    cos_h = np.concatenate([np.cos(ang_r), np.cos(ang_r), np.cos(ang_c), np.cos(ang_c)], axis=1)
    sin_h = np.concatenate([-np.sin(ang_r), np.sin(ang_r), -np.sin(ang_c), np.sin(ang_c)], axis=1)
    reps = LANES // HEAD_DIM
    return (jnp.asarray(np.tile(cos_h, (1, reps)), F32), jnp.asarray(np.tile(sin_h, (1, reps)), F32))


def _inproj_call(x, norm_g, mod, w_in_bf, layer, tm, rope, want_kv, kv_prev=None):
    B, L, _ = x.shape
    bm = mod.shape[0]
    in_specs = [
        pl.BlockSpec((1, tm, D_MODEL), lambda b, i: (b, i, 0)),
        pl.BlockSpec((1, D_MODEL), lambda b, i: (0, 0)),
        _mod_spec(bm, 1, 2),
        _mod_spec(bm, 0, 2),
        pl.BlockSpec((1, D_MODEL, MIX_COLS), lambda b, i: (layer, 0, 0)),
    ]
    args = [x, norm_g, mod, mod, w_in_bf]
    if rope:
        cos, sin = _rope_tables(L)
        in_specs += [pl.BlockSpec((tm, LANES), lambda b, i: (i, 0))] * 2
        args += [cos, sin]
    out_shape = [jax.ShapeDtypeStruct((B, L, MIX_COLS), BF)]
    out_specs = [pl.BlockSpec((1, tm, MIX_COLS), lambda b, i: (b, i, 0))]
    aliases = {}
    if want_kv:
        for n, heads in enumerate(KV_HEADS):
            out_shape.append(jax.ShapeDtypeStruct((B, DEPTH, heads, HEAD_DIM, L), F32))
            if kv_prev is None:
                out_specs.append(pl.BlockSpec((1, DEPTH, heads, HEAD_DIM, tm), lambda b, i: (b, 0, 0, 0, i)))
            else:
                out_specs.append(pl.BlockSpec((1, 1, heads, HEAD_DIM, tm), lambda b, i: (b, layer, 0, 0, i)))
                aliases[len(args)] = 1 + n
                in_specs.append(pl.BlockSpec(memory_space=pl.ANY))
                args.append(kv_prev[n])
    return pl.pallas_call(
        functools.partial(_inproj_kernel, rope=rope, kv_layer=layer if want_kv else None,
                          kv_aliased=kv_prev is not None),
        out_shape=out_shape,
        grid=(B, L // tm),
        in_specs=in_specs,
        out_specs=out_specs,
        input_output_aliases=aliases,
        compiler_params=_params("arbitrary", "arbitrary"),
        name="in_proj",
    )(*args)


CONV_PAD = 16
CONV_TR = 64
CONV_CHUNK = 256


def _conv_kernel(a_ref, w_ref, cb_ref, lg_ref, lb_ref, o_ref, p_ref, u_ref, wb_ref, *, L):
    for k in range(CONV_K):
        wb_ref[k] = jnp.broadcast_to(w_ref[k:k + 1, :], (SUBLANES, BRANCH_W))
    zeros = jnp.zeros((CONV_PAD, BRANCH_W), F32)
    p_ref[0:CONV_PAD, :] = zeros
    p_ref[L + CONV_PAD:L + 2 * CONV_PAD, :] = zeros
    for r0 in range(0, L, CONV_CHUNK):
        a = a_ref[0, r0:r0 + CONV_CHUNK, :].astype(F32)
        p_ref[CONV_PAD + r0:CONV_PAD + r0 + CONV_CHUNK, :] = a[:, :BRANCH_W] * jax.nn.sigmoid(a[:, BRANCH_W:])
    n_u = L + 2 * CONV_PAD - SUBLANES
    for s in range(SUBLANES):
        u_ref[s] = p_ref[s:s + n_u, :]
    cb = cb_ref[...]

    def body(i, carry):
        base = pl.multiple_of(i * CONV_TR, CONV_TR)
        groups = (CONV_TR // SUBLANES, SUBLANES, BRANCH_W)
        acc = jnp.zeros(groups, F32) + cb
        for k in range(CONV_K):
            q, s = divmod(k + CONV_PAD - CONV_K // 2, SUBLANES)
            u = u_ref[s, pl.ds(base + SUBLANES * q, CONV_TR), :]
            acc = acc + wb_ref[k] * u.reshape(groups)
        p_ref[pl.ds(base, CONV_TR), :] = acc.reshape(CONV_TR, BRANCH_W)
        return carry

    lax.fori_loop(0, L // CONV_TR, body, 0)
    lg = lg_ref[...]
    lb = lb_ref[...]
    for r0 in range(0, L, CONV_CHUNK):
        acc = p_ref[r0:r0 + CONV_CHUNK, :]
        mu = jnp.mean(acc, axis=-1, keepdims=True)
        d = acc - mu
        var = jnp.mean(d * d, axis=-1, keepdims=True)
        y = d * lax.rsqrt(var + EPS) * lg + lb
        o_ref[0, r0:r0 + CONV_CHUNK, :] = _silu(y).astype(BF)


def _conv_call(z, conv_w, conv_b, ln_g, ln_b):
    B, L, _ = z.shape
    n_u = L + 2 * CONV_PAD - SUBLANES
    vec = lambda b: (0, 0)
    return pl.pallas_call(
        functools.partial(_conv_kernel, L=L),
        out_shape=jax.ShapeDtypeStruct((B, L, BRANCH_W), BF),
        grid=(B,),
        in_specs=[
            pl.BlockSpec((1, L, 2 * BRANCH_W), lambda b: (b, 0, COL_A // (2 * BRANCH_W))),
            pl.BlockSpec((CONV_K, BRANCH_W), vec),
            pl.BlockSpec((1, BRANCH_W), vec),
            pl.BlockSpec((1, BRANCH_W), vec),
            pl.BlockSpec((1, BRANCH_W), vec),
        ],
        out_specs=pl.BlockSpec((1, L, BRANCH_W), lambda b: (b, 0, 0)),
        scratch_shapes=[
            pltpu.VMEM((L + 2 * CONV_PAD, BRANCH_W), F32),
            pltpu.VMEM((SUBLANES, n_u, BRANCH_W), F32),
            pltpu.VMEM((CONV_K, SUBLANES, BRANCH_W), F32),
        ],
        compiler_params=_params("arbitrary"),
        name="conv_module",
    )(z, conv_w, conv_b.reshape(1, -1), ln_g.reshape(1, -1), ln_b.reshape(1, -1))


def _dft_tables(L):
    c = np.arange(FNET_GDIM)
    ang_c = 2.0 * np.pi * ((c[:, None] * c[None, :]) % FNET_GDIM) / FNET_GDIM
    cc = np.zeros((BRANCH_W, BRANCH_W))
    sc = np.zeros((BRANCH_W, BRANCH_W))
    for g in range(FNET_GROUPS):
        sl = slice(g * FNET_GDIM, (g + 1) * FNET_GDIM)
        cc[sl, sl] = np.cos(ang_c)
        sc[sl, sl] = np.sin(ang_c)
    ccs = np.concatenate([cc, sc], axis=1)
    f = np.arange(L)
    t = np.arange(L // 2)
    ang_l = 2.0 * np.pi * ((f[:, None] * t[None, :]) % L) / L
    csl = np.concatenate([np.cos(ang_l), -np.sin(ang_l)], axis=1)
    return _cast_call(jnp.asarray(ccs, F32)), _cast_call(jnp.asarray(csl, F32))


FOURIER_TR = 512
FOURIER_PAD = 8


def _fourier_kernel(f_ref, ccs_ref, csl_ref, o_ref, pq_ref, fold_ref, *, L):
    scale = float(1.0 / np.sqrt(L * FNET_GDIM))
    half = L // 2
    tr = min(L, FOURIER_TR)
    for r0 in range(0, L, tr):
        pq_ref[r0:r0 + tr, :] = _dot(f_ref[0, r0:r0 + tr, :], ccs_ref[...])
    pq_ref[L:L + FOURIER_PAD, :] = jnp.zeros((FOURIER_PAD, 2 * BRANCH_W), F32)
    blk = min(half, LANES)
    ri = lax.broadcasted_iota(jnp.int32, (blk, blk), 0)
    ci = lax.broadcasted_iota(jnp.int32, (blk, blk), 1)
    exchange = jnp.where(ri + ci == blk - 1, 1.0, 0.0).astype(BF)
    for i in range(half // blk):
        lo = L - blk * (i + 1) + 1
        partner = _dot(exchange, pq_ref[lo:lo + blk, :].astype(BF))
        own = pq_ref[blk * i:blk * (i + 1), :]
        fold_ref[blk * i:blk * (i + 1), :] = (own[:, :BRANCH_W] + partner[:, :BRANCH_W]).astype(BF)
        fold_ref[half + blk * i:half + blk * (i + 1), :] = (own[:, BRANCH_W:] - partner[:, BRANCH_W:]).astype(BF)
    mid = pq_ref[half:half + 1, 0:BRANCH_W]
    for r0 in range(0, L, tr):
        o = _dot(csl_ref[r0:r0 + tr, :], fold_ref[...])
        row = lax.broadcasted_iota(jnp.int32, (tr, 1), 0) + r0
        sign = (1 - 2 * (row % 2)).astype(F32)
        o_ref[0, r0:r0 + tr, :] = ((o + sign * mid) * scale).astype(BF)


def _fourier_call(z, ccs, csl):
    B, L, _ = z.shape
    return pl.pallas_call(
        functools.partial(_fourier_kernel, L=L),
        out_shape=jax.ShapeDtypeStruct((B, L, BRANCH_W), BF),
        grid=(B,),
        in_specs=[
            pl.BlockSpec((1, L, BRANCH_W), lambda b: (b, 0, COL_F // BRANCH_W)),
            pl.BlockSpec((BRANCH_W, 2 * BRANCH_W), lambda b: (0, 0)),
            pl.BlockSpec((L, L), lambda b: (0, 0), pipeline_mode=pl.Buffered(1)),
        ],
        out_specs=pl.BlockSpec((1, L, BRANCH_W), lambda b: (b, 0, 0)),
        scratch_shapes=[
            pltpu.VMEM((L + FOURIER_PAD, 2 * BRANCH_W), F32),
            pltpu.VMEM((L, BRANCH_W), BF),
        ],
        compiler_params=_params("arbitrary"),
        name="fourier_mix",
    )(z, ccs, csl)


def _head(ref, rows, h):
    return ref[0, rows, h * HEAD_DIM:(h + 1) * HEAD_DIM]


def _all_heads_attention(q, k, v, sinks):
    L, width = q.shape
    heads = width // HEAD_DIM
    lane = lax.broadcasted_iota(jnp.int32, (L, width), 1)
    q = q * ATTN_SCALE
    stacked = jnp.concatenate([jnp.where(lane // HEAD_DIM == h, q, jnp.zeros_like(q)) for h in range(heads)],
                              axis=0)
    s = _dot_nt(stacked, k)
    m = jnp.max(s, axis=-1, keepdims=True)
    if sinks is not None:
        row = lax.broadcasted_iota(jnp.int32, (heads * L, 1), 0)
        sink = jnp.zeros((heads * L, 1), F32)
        for h in range(heads):
            sink = jnp.where(row // L == h, sinks[h], sink)
        m = jnp.maximum(m, sink)
    p = jnp.exp(s - m)
    den = jnp.sum(p, axis=-1, keepdims=True)
    if sinks is not None:
        den = den + jnp.exp(sink - m)
    o = _dot(p.astype(BF), v) / den
    out = jnp.zeros((L, width), F32)
    for h in range(heads):
        out = jnp.where(lane // HEAD_DIM == h, o[h * L:(h + 1) * L], out)
    return out.astype(BF)


def _ctx_attn_kernel(sink_ref, qc_ref, kvc_ref, qd_ref, kd_ref, vd_ref, oc_ref, od_ref):
    rows = slice(None)
    group = WIN_Q_HEADS // WIN_KV_HEADS
    k_c = jnp.concatenate([_head(kvc_ref, rows, h // group) for h in range(WIN_Q_HEADS)], axis=1)
    v_c = jnp.concatenate([_head(kvc_ref, rows, WIN_KV_HEADS + h // group) for h in range(WIN_Q_HEADS)], axis=1)
    oc_ref[0] = _all_heads_attention(qc_ref[0], k_c, v_c, [sink_ref[h] for h in range(WIN_Q_HEADS)])
    od_ref[0] = _all_heads_attention(qd_ref[0], kd_ref[0], vd_ref[0], None)


def _zcol_spec(L, col):
    return pl.BlockSpec((1, L, BRANCH_W), lambda b: (b, 0, col // BRANCH_W))


def _ctx_attn_call(z, sink):
    B, L, _ = z.shape
    out = jax.ShapeDtypeStruct((B, L, BRANCH_W), BF)
    ospec = pl.BlockSpec((1, L, BRANCH_W), lambda b: (b, 0, 0))
    return pl.pallas_call(
        _ctx_attn_kernel,
        out_shape=[out, out],
        grid=(B,),
        in_specs=[
            pl.BlockSpec(memory_space=pltpu.SMEM),
            _zcol_spec(L, COL_CQ), _zcol_spec(L, COL_CKV),
            _zcol_spec(L, COL_DQ), _zcol_spec(L, COL_DK), _zcol_spec(L, COL_DV),
        ],
        out_specs=[ospec, ospec],
        compiler_params=_params("arbitrary"),
        name="ctx_attention",
    )(sink, z, z, z, z, z)


PAIR = 2 * HEAD_DIM


def _one_head(x, lane, g):
    return jnp.where(lane // HEAD_DIM == g, x, jnp.zeros_like(x))


def _win_attn_kernel(sink_ref, q_ref, kv_ref, ck_ref, cv_ref, o_ref, *, L):
    nb = L // WIN_BLOCK
    group = WIN_Q_HEADS // WIN_KV_HEADS
    m_rows = group * WIN_BLOCK
    assert WIN_BLOCK == LANES
    row = lax.broadcasted_iota(jnp.int32, (m_rows, WIN_BLOCK), 0)
    a = row % WIN_BLOCK
    j = lax.broadcasted_iota(jnp.int32, (m_rows, WIN_BLOCK), 1)
    row1 = lax.broadcasted_iota(jnp.int32, (m_rows, 1), 0)
    for kvh in range(WIN_KV_HEADS):
        ctx_k = ck_ref[0, 0, kvh].astype(BF)
        ctx_v = cv_ref[0, 0, kvh].astype(BF)
        sink = jnp.zeros((m_rows, 1), F32)
        for g in range(group):
            sink = jnp.where(row1 // WIN_BLOCK == g, sink_ref[kvh * group + g], sink)

        def body(n, carry, kvh=kvh, ctx_k=ctx_k, ctx_v=ctx_v, sink=sink):
            r0 = pl.multiple_of(n * WIN_BLOCK, WIN_BLOCK)
            q = jnp.concatenate(
                [_head(q_ref, pl.ds(r0, WIN_BLOCK), kvh * group + g) for g in range(group)], axis=0)
            q = q * ATTN_SCALE
            s_ctx = _dot(q, ctx_k)
            m_el = jnp.maximum(s_ctx[:, :LANES], s_ctx[:, LANES:])
            scores = []
            vals = []
            for dj in (-1, 0, 1):
                nk = jnp.clip(n + dj, 0, nb - 1)
                rk = pl.multiple_of(nk * WIN_BLOCK, WIN_BLOCK)
                s = _dot_nt(q, _head(kv_ref, pl.ds(rk, WIN_BLOCK), kvh))
                if dj == -1:
                    s = jnp.where((j >= a) & (n >= 1), s, NEG)
                elif dj == 1:
                    s = jnp.where((j <= a) & (n <= nb - 2), s, NEG)
                m_el = jnp.maximum(m_el, s)
                scores.append(s)
                vals.append(_head(kv_ref, pl.ds(rk, WIN_BLOCK), WIN_KV_HEADS + kvh))
            m = jnp.maximum(jnp.max(m_el, axis=-1, keepdims=True), sink)
            p_ctx = jnp.exp(s_ctx - m)
            den_el = p_ctx[:, :LANES] + p_ctx[:, LANES:]
            o = _dot_nt(p_ctx.astype(BF), ctx_v)
            for s, v in zip(scores, vals):
                p = jnp.exp(s - m)
                den_el = den_el + p
                o = o + _dot(p.astype(BF), v)
            den = jnp.sum(den_el, axis=-1, keepdims=True) + jnp.exp(sink - m)
            o = (o / den).astype(BF)
            for g in range(group):
                h = kvh * group + g
                o_ref[0, pl.ds(r0, WIN_BLOCK), h * HEAD_DIM:(h + 1) * HEAD_DIM] = (
                    o[g * WIN_BLOCK:(g + 1) * WIN_BLOCK])
            return carry

        lax.fori_loop(0, nb, body, 0, unroll=8)


def _win_attn_call(z, cache_k, cache_v, sink, layer):
    B, L, _ = z.shape
    P = cache_k.shape[4]
    assert P == 2 * LANES
    cspec = pl.BlockSpec((1, 1, WIN_KV_HEADS, HEAD_DIM, P), lambda b: (b, layer, 0, 0, 0))
    return pl.pallas_call(
        functools.partial(_win_attn_kernel, L=L),
        out_shape=jax.ShapeDtypeStruct((B, L, BRANCH_W), BF),
        grid=(B,),
        in_specs=[
            pl.BlockSpec(memory_space=pltpu.SMEM),
            _zcol_spec(L, COL_CQ), _zcol_spec(L, COL_CKV), cspec, cspec,
        ],
        out_specs=pl.BlockSpec((1, L, BRANCH_W), lambda b: (b, 0, 0)),
        compiler_params=_params("arbitrary"),
        name="window_attention",
    )(sink, z, z, cache_k, cache_v)


N_ROW_OFF = 2 * NA_ROWS - 1
N_COL_OFF = 2 * NA_COLS - 1


NA_QROWS = 4
NA_WIN_ROWS = NA_ROWS + NA_QROWS


def _na_window_start(r0, rows):
    return min(max(r0 - NA_ROWS // 2, 0), rows - NA_WIN_ROWS)


def _na_group_blocks(r0, rows):
    ws = _na_window_start(r0, rows)
    blocks = {}
    for i in range(NA_QROWS):
        r = r0 + i
        start = min(max(r - NA_ROWS // 2, 0), rows - NA_ROWS)
        for kk in range(NA_WIN_ROWS):
            rel = ws + kk - start
            blocks[i, kk] = ws + kk - r + (NA_ROWS - 1) if 0 <= rel < NA_ROWS else None
    return blocks


def _na_patterns(rows):
    layouts, first = [], []
    for qi in range(rows // NA_QROWS):
        blocks = _na_group_blocks(qi * NA_QROWS, rows)
        if not layouts or blocks != layouts[-1]:
            assert blocks not in layouts
            layouts.append(blocks)
            first.append(qi)
    return layouts, first


def _na_bias_kernel(rb_ref, o_ref, *, rows):
    h = pl.program_id(0)
    cq = lax.broadcasted_iota(jnp.int32, (GRID_W, GRID_W), 0)
    ck = lax.broadcasted_iota(jnp.int32, (GRID_W, GRID_W), 1)
    col_start = jnp.clip(cq - NA_COLS // 2, 0, GRID_W - NA_COLS)
    col_ok = (ck >= col_start) & (ck < col_start + NA_COLS)
    col_off = jnp.clip(ck - cq + (NA_COLS - 1), 0, N_COL_OFF - 1)
    tiles = []
    for ro in range(N_ROW_OFF):
        t = jnp.zeros((GRID_W, GRID_W), F32)
        for co in range(N_COL_OFF):
            t = jnp.where(col_off == co, rb_ref[(h * N_ROW_OFF + ro) * N_COL_OFF + co], t)
        tiles.append(jnp.where(col_ok, t, NEG))
    masked = jnp.full((GRID_W, GRID_W), NEG, F32)
    for p, blocks in enumerate(_na_patterns(rows)[0]):
        for (i, kk), ro in blocks.items():
            o_ref[0, p, i * GRID_W:(i + 1) * GRID_W, kk * GRID_W:(kk + 1) * GRID_W] = (
                masked if ro is None else tiles[ro])


def _na_bias_call(na_bias, rows):
    shape = (NA_HEADS, len(_na_patterns(rows)[0]), NA_QROWS * GRID_W, NA_WIN_ROWS * GRID_W)
    return pl.pallas_call(
        functools.partial(_na_bias_kernel, rows=rows),
        out_shape=jax.ShapeDtypeStruct(shape, F32),
        grid=(NA_HEADS,),
        in_specs=[pl.BlockSpec(memory_space=pltpu.SMEM)],
        out_specs=pl.BlockSpec((1,) + shape[1:], lambda h: (h, 0, 0, 0)),
        compiler_params=_params("arbitrary"),
        name="na_bias_table",
    )(na_bias.reshape(-1))


def _na_attn_kernel(q_ref, k_ref, v_ref, ck_ref, cv_ref, bt_ref, o_ref, *, L):
    rows = L // GRID_W
    groups = rows // NA_QROWS
    pattern_starts = _na_patterns(rows)[1]
    lane = lax.broadcasted_iota(jnp.int32, (NA_QROWS * GRID_W, PAIR), 1)
    for hp in range(NA_HEADS * HEAD_DIM // PAIR):
        heads = (2 * hp, 2 * hp + 1)
        lanes = slice(hp * PAIR, (hp + 1) * PAIR)
        ctx_k = jnp.concatenate([ck_ref[0, 0, h].astype(BF) for h in heads], axis=0)
        ctx_v = jnp.concatenate([cv_ref[0, 0, h].astype(BF) for h in heads], axis=0)

        def body(qi, carry, heads=heads, lanes=lanes, ctx_k=ctx_k, ctx_v=ctx_v):
            r0 = qi * NA_QROWS
            ws = jnp.clip(r0 - NA_ROWS // 2, 0, rows - NA_WIN_ROWS)
            pat = sum(jnp.where(qi >= first, 1, 0) for first in pattern_starts[1:])
            rq = pl.ds(pl.multiple_of(r0 * GRID_W, NA_QROWS * GRID_W), NA_QROWS * GRID_W)
            rk = pl.ds(pl.multiple_of(ws * GRID_W, GRID_W), NA_WIN_ROWS * GRID_W)
            q2 = q_ref[0, rq, lanes] * ATTN_SCALE
            k2 = k_ref[0, rk, lanes]
            v2 = v_ref[0, rk, lanes]
            outs = []
            for g, h in enumerate(heads):
                qg = _one_head(q2, lane, g)
                s = _dot_nt(qg, k2) + bt_ref[h, pat]
                s_ctx = _dot(qg, ctx_k)
                m = jnp.maximum(jnp.max(s, axis=-1, keepdims=True), jnp.max(s_ctx, axis=-1, keepdims=True))
                p = jnp.exp(s - m)
                p_ctx = jnp.exp(s_ctx - m)
                den = jnp.sum(p, axis=-1, keepdims=True) + jnp.sum(p_ctx, axis=-1, keepdims=True)
                outs.append((_dot(p.astype(BF), v2) + _dot_nt(p_ctx.astype(BF), ctx_v)) / den)
            o_ref[0, rq, lanes] = jnp.where(lane // HEAD_DIM == 0, outs[0], outs[1]).astype(BF)
            return carry

        lax.fori_loop(0, groups, body, 0, unroll=4)


def _na_attn_call(z, cache_k, cache_v, bias_table, layer):
    B, L, _ = z.shape
    P = cache_k.shape[4]
    assert L // GRID_W >= NA_WIN_ROWS + NA_QROWS and (L // GRID_W) % NA_QROWS == 0
    cspec = pl.BlockSpec((1, 1, NA_HEADS, HEAD_DIM, P), lambda b: (b, layer, 0, 0, 0))
    return pl.pallas_call(
        functools.partial(_na_attn_kernel, L=L),
        out_shape=jax.ShapeDtypeStruct((B, L, BRANCH_W), BF),
        grid=(B,),
        in_specs=[
            _zcol_spec(L, COL_DQ), _zcol_spec(L, COL_DK), _zcol_spec(L, COL_DV), cspec, cspec,
            pl.BlockSpec(bias_table.shape, lambda b: (0, 0, 0, 0), pipeline_mode=pl.Buffered(1)),
        ],
        out_specs=pl.BlockSpec((1, L, BRANCH_W), lambda b: (b, 0, 0)),
        compiler_params=_params("arbitrary"),
        name="neighbourhood_attention",
    )(z, z, z, cache_k, cache_v, bias_table)


def _merge_kernel(x_ref, g_ref, sc_ref, sh_ref, gate_ref, ba_ref, bb_ref, bc_ref, bd_ref,
                  wg0_ref, wg1_ref, wb_ref, wo_ref, o_ref):
    x = x_ref[0]
    h = _norm_mod(x, g_ref[...], sc_ref[0], sh_ref[0]).astype(BF)
    merged = jnp.zeros(x.shape, F32)
    per_ref = MIX_COLS // D_MODEL
    for n, br_ref in enumerate((ba_ref, bb_ref, bc_ref, bd_ref)):
        wg_ref = (wg0_ref, wg1_ref)[n // per_ref]
        c0 = (n % per_ref) * D_MODEL
        gate = jax.nn.sigmoid(_dot(h, wg_ref[0, :, c0:c0 + D_MODEL]))
        merged = merged + gate * _dot(br_ref[0], wb_ref[0, n])
    o_ref[0] = x + gate_ref[0] * _dot(merged.astype(BF), wo_ref[0])


def _merge_call(x, norm_g, mod, branches, w_in_bf, w_branch, w_out, layer, tm):
    B, L, _ = x.shape
    bm = mod.shape[0]
    xspec = pl.BlockSpec((1, tm, D_MODEL), lambda b, i: (b, i, 0))
    bspec = pl.BlockSpec((1, tm, BRANCH_W), lambda b, i: (b, i, 0))
    return pl.pallas_call(
        _merge_kernel,
        out_shape=jax.ShapeDtypeStruct((B, L, D_MODEL), F32),
        grid=(B, L // tm),
        in_specs=[
            xspec,
            pl.BlockSpec((1, D_MODEL), lambda b, i: (0, 0)),
            _mod_spec(bm, 1, 2), _mod_spec(bm, 0, 2), _mod_spec(bm, 2, 2),
            bspec, bspec, bspec, bspec,
            pl.BlockSpec((1, D_MODEL, MIX_COLS), lambda b, i: (layer, 0, 1)),
            pl.BlockSpec((1, D_MODEL, MIX_COLS), lambda b, i: (layer, 0, 2)),
            pl.BlockSpec((1, N_BRANCH, BRANCH_W, D_MODEL), lambda b, i: (layer, 0, 0, 0)),
            pl.BlockSpec((1, D_MODEL, D_MODEL), lambda b, i: (layer, 0, 0)),
        ],
        out_specs=xspec,
        compiler_params=_params("arbitrary", "arbitrary"),
        name="merge_out_proj",
    )(x, norm_g, mod, mod, mod, *branches, w_in_bf, w_in_bf, w_branch, w_out)


ROUTE_CHUNK = 256
GATHER_ROWS = 512
SIGN_BIT = 31
SEARCH_BITS = 4
SEARCH_TOP_SHIFT = ((SIGN_BIT - 1) // SEARCH_BITS) * SEARCH_BITS


def _prefix_excl(mask_ref, out_ref, tri, L):
    totals = [jnp.sum(mask_ref[:, c0:c0 + LANES], axis=-1, keepdims=True) for c0 in range(0, L, LANES)]
    run = jnp.zeros((N_EXPERTS, 1), F32)
    for i, c0 in enumerate(range(0, L, LANES)):
        out_ref[:, c0:c0 + LANES] = _dot(mask_ref[:, c0:c0 + LANES].astype(BF), tri) + run
        run = run + totals[i]


def _route_kernel(x_ref, g_ref, sc_ref, sh_ref, wh_ref, wl_ref, tri_ref,
                  h_ref, gate_ref, slot_ref, slot_t_ref, off_ref, lt_ref, msk_ref, pre_ref, *, L, cap):
    g = g_ref[...]
    sc = sc_ref[0]
    sh = sh_ref[0]
    wh = wh_ref[...]
    wl = wl_ref[...]
    rc = min(L, ROUTE_CHUNK)
    for r0 in range(0, L, rc):
        h = _norm_mod(x_ref[0, r0:r0 + rc, :], g, sc, sh)
        h_hi = h.astype(BF)
        h_ref[0, r0:r0 + rc, :] = h_hi
        h_lo = (h - h_hi.astype(F32)).astype(BF)
        lt_ref[:, r0:r0 + rc] = _dot_nt(wh, h_hi) + (_dot_nt(wh, h_lo) + _dot_nt(wl, h_hi))
    lt = lt_ref[...]
    e = jnp.exp(lt - jnp.max(lt, axis=0, keepdims=True))
    aff = e / jnp.sum(e, axis=0, keepdims=True)
    thr = jnp.zeros((N_EXPERTS, 1), jnp.int32)
    for shift in range(SEARCH_TOP_SHIFT, -1, -SEARCH_BITS):
        n_digits = min(1 << SEARCH_BITS, 1 << (SIGN_BIT - shift))
        digit = jnp.zeros((N_EXPERTS, 1), jnp.int32)
        for d in range(1, n_digits):
            cand = pltpu.bitcast(thr | (d << shift), F32)
            cnt = jnp.sum((aff >= cand).astype(F32), axis=-1, keepdims=True)
            digit = digit + (cnt >= cap).astype(jnp.int32)
        thr = thr | (digit << shift)
    thr_f = pltpu.bitcast(thr, F32)
    gt = aff > thr_f
    eq = aff == thr_f
    need = cap - jnp.sum(gt.astype(F32), axis=-1, keepdims=True)
    tri = tri_ref[...]
    msk_ref[...] = eq.astype(F32)
    _prefix_excl(msk_ref, pre_ref, tri, L)
    sel = gt | (eq & (pre_ref[...] < need))
    msk_ref[...] = sel.astype(F32)
    _prefix_excl(msk_ref, pre_ref, tri, L)
    off_ref[0] = jnp.concatenate([pre_ref[:, c:c + 1] for c in range(0, L, min(L, COMBINE_TL))], axis=1)
    slot = jnp.where(sel, pre_ref[...], -1.0)
    slot_ref[0] = slot
    slot_t_ref[0] = slot.T
    gate_ref[0] = jnp.where(sel, aff, 0.0)


def _route_call(x, norm_g, mod, wr_hi, wr_lo):
    B, L, _ = x.shape
    bm = mod.shape[0]
    cap = EC_CAPACITY * L // N_EXPERTS
    n_tiles = L // min(L, COMBINE_TL)
    tri = jnp.asarray(np.triu(np.ones((LANES, LANES)), 1).astype(ml_dtypes.bfloat16))
    wspec = pl.BlockSpec((N_EXPERTS, D_MODEL), lambda b: (0, 0))
    return pl.pallas_call(
        functools.partial(_route_kernel, L=L, cap=cap),
        out_shape=[
            jax.ShapeDtypeStruct((B, L, D_MODEL), BF),
            jax.ShapeDtypeStruct((B, N_EXPERTS, L), F32),
            jax.ShapeDtypeStruct((B, N_EXPERTS, L), F32),
            jax.ShapeDtypeStruct((B, L, N_EXPERTS), F32),
            jax.ShapeDtypeStruct((B, N_EXPERTS, n_tiles), F32),
        ],
        grid=(B,),
        in_specs=[
            pl.BlockSpec((1, L, D_MODEL), lambda b: (b, 0, 0)),
            pl.BlockSpec((1, D_MODEL), lambda b: (0, 0)),
            _mod_spec(bm, 4, 1), _mod_spec(bm, 3, 1),
            wspec, wspec,
            pl.BlockSpec((LANES, LANES), lambda b: (0, 0)),
        ],
        out_specs=[
            pl.BlockSpec((1, L, D_MODEL), lambda b: (b, 0, 0)),
            pl.BlockSpec((1, N_EXPERTS, L), lambda b: (b, 0, 0)),
            pl.BlockSpec((1, N_EXPERTS, L), lambda b: (b, 0, 0)),
            pl.BlockSpec((1, L, N_EXPERTS), lambda b: (b, 0, 0)),
            pl.BlockSpec((1, N_EXPERTS, n_tiles), lambda b: (b, 0, 0)),
        ],
        scratch_shapes=[
            pltpu.VMEM((N_EXPERTS, L), F32),
            pltpu.VMEM((N_EXPERTS, L), F32),
            pltpu.VMEM((N_EXPERTS, L), F32),
        ],
        compiler_params=_params("arbitrary"),
        name="ec_route",
    )(x, norm_g, mod, mod, wr_hi, wr_lo, tri)


COMBINE_TL = 256
COMBINE_WIN = 64
BF_TILE = 16


def _list_windows(off_ref, cap):
    n_tiles = pl.num_programs(1)
    base = (pl.program_id(0) * n_tiles + pl.program_id(1)) * N_EXPERTS
    starts = []
    fits = None
    for ex in range(N_EXPERTS):
        first = off_ref[base + ex]
        end = jnp.where(pl.program_id(1) == n_tiles - 1, cap, off_ref[base + N_EXPERTS + ex])
        start = jnp.minimum((first // BF_TILE) * BF_TILE, cap - COMBINE_WIN)
        ok = end <= start + COMBINE_WIN
        fits = ok if fits is None else (fits & ok)
        starts.append(start)
    return starts, fits


def _gather_kernel(off_ref, slot_ref, gate_ref, h_ref, xs_ref, gs_ref, *, cap):
    h = h_ref[0]
    tl = h.shape[0]

    @pl.when(pl.program_id(1) == 0)
    def _():
        xs_ref[...] = jnp.zeros(xs_ref.shape, BF)
        gs_ref[...] = jnp.zeros(gs_ref.shape, F32)

    def select(ex, first, rows):
        pos = lax.broadcasted_iota(jnp.int32, (rows, tl), 0).astype(F32)
        hit = slot_ref[0, ex:ex + 1, :] - first == pos
        gate = jnp.sum(jnp.where(hit, gate_ref[0, ex:ex + 1, :], 0.0), axis=-1, keepdims=True)
        return jnp.where(hit, 1.0, 0.0).astype(BF), jnp.broadcast_to(gate, (rows, LANES))

    def gather_all():
        ge = max(1, min(N_EXPERTS, GATHER_ROWS // cap))
        for e0 in range(0, N_EXPERTS, ge):
            picked = [select(ex, 0.0, cap) for ex in range(e0, e0 + ge)]
            rows = _dot(jnp.concatenate([p[0] for p in picked], axis=0), h).astype(BF)
            for j in range(ge):
                xs_ref[e0 + j, 0] += rows[j * cap:(j + 1) * cap]
                gs_ref[e0 + j, 0] += picked[j][1]

    if cap <= COMBINE_WIN:
        gather_all()
        return

    starts, fits = _list_windows(off_ref, cap)

    @pl.when(fits)
    def _():
        picked = [select(ex, starts[ex].astype(F32), COMBINE_WIN) for ex in range(N_EXPERTS)]
        rows = _dot(jnp.concatenate([p[0] for p in picked], axis=0), h).astype(BF)
        for ex in range(N_EXPERTS):
            win = pl.ds(pl.multiple_of(starts[ex], BF_TILE), COMBINE_WIN)
            xs_ref[ex, 0, win, :] += rows[ex * COMBINE_WIN:(ex + 1) * COMBINE_WIN]
            gs_ref[ex, 0, win, :] += picked[ex][1]

    @pl.when(jnp.logical_not(fits))
    def _():
        gather_all()


def _gather_call(offsets, slot, gate_rows, h2):
    B, L, _ = h2.shape
    cap = EC_CAPACITY * L // N_EXPERTS
    tl = min(L, COMBINE_TL)
    assert cap <= COMBINE_WIN or (cap % BF_TILE == 0 and COMBINE_WIN % BF_TILE == 0)
    return pl.pallas_call(
        functools.partial(_gather_kernel, cap=cap),
        out_shape=[
            jax.ShapeDtypeStruct((N_EXPERTS, B, cap, D_MODEL), BF),
            jax.ShapeDtypeStruct((N_EXPERTS, B, cap, LANES), F32),
        ],
        grid_spec=pltpu.PrefetchScalarGridSpec(
            num_scalar_prefetch=1,
            grid=(B, L // tl),
            in_specs=[
                pl.BlockSpec((1, N_EXPERTS, tl), lambda b, i, off: (b, 0, i)),
                pl.BlockSpec((1, N_EXPERTS, tl), lambda b, i, off: (b, 0, i)),
                pl.BlockSpec((1, tl, D_MODEL), lambda b, i, off: (b, i, 0)),
            ],
            out_specs=[
                pl.BlockSpec((N_EXPERTS, 1, cap, D_MODEL), lambda b, i, off: (0, b, 0, 0)),
                pl.BlockSpec((N_EXPERTS, 1, cap, LANES), lambda b, i, off: (0, b, 0, 0)),
            ],
        ),
        compiler_params=_params("arbitrary", "arbitrary"),
        name="ec_gather",
    )(offsets, slot, gate_rows, h2)


EXPERT_TR = 512


def _expert_kernel(xa_ref, ga_ref, xb_ref, gb_ref, wg_ref, wu_ref, wd_ref, ya_ref, yb_ref,
                   wgb_ref, wub_ref, wdb_ref, *, tiles_a):
    j = pl.program_id(1)

    @pl.when(j == 0)
    def _():
        wgb_ref[...] = wg_ref[0, 0].astype(BF)
        wub_ref[...] = wu_ref[0, 0].astype(BF)
        wdb_ref[...] = wd_ref[0, 0].astype(BF)

    def ffn(x_ref, g_ref, y_ref):
        x = x_ref[0]
        a = _dot(x, wgb_ref[...])
        u = _dot(x, wub_ref[...])
        y = _dot((_silu(a) * u).astype(BF), wdb_ref[...])
        y_ref[0] = (y * g_ref[0, :, 0:1]).astype(BF)

    @pl.when(j < tiles_a)
    def _():
        ffn(xa_ref, ga_ref, ya_ref)

    @pl.when(j >= tiles_a)
    def _():
        ffn(xb_ref, gb_ref, yb_ref)


def _expert_call(xs_a, gs_a, xs_b, gs_b, w_gate, w_up, w_down, layer):
    E, ra, _ = xs_a.shape
    rb = xs_b.shape[1]
    tr = math.gcd(math.gcd(ra, rb), EXPERT_TR)
    tiles_a, tiles_b = ra // tr, rb // tr
    amap = lambda e, j: (e, jnp.minimum(j, tiles_a - 1), 0)
    bmap = lambda e, j: (e, jnp.maximum(j - tiles_a, 0), 0)
    wspec_in = pl.BlockSpec((1, 1, D_MODEL, EXPERT_FF), lambda e, j: (layer, e, 0, 0))
    wspec_out = pl.BlockSpec((1, 1, EXPERT_FF, D_MODEL), lambda e, j: (layer, e, 0, 0))
    return pl.pallas_call(
        functools.partial(_expert_kernel, tiles_a=tiles_a),
        out_shape=[jax.ShapeDtypeStruct((E, ra, D_MODEL), BF), jax.ShapeDtypeStruct((E, rb, D_MODEL), BF)],
        grid=(E, tiles_a + tiles_b),
        in_specs=[
            pl.BlockSpec((1, tr, D_MODEL), amap), pl.BlockSpec((1, tr, LANES), amap),
            pl.BlockSpec((1, tr, D_MODEL), bmap), pl.BlockSpec((1, tr, LANES), bmap),
            wspec_in, wspec_in, wspec_out,
        ],
        out_specs=[pl.BlockSpec((1, tr, D_MODEL), amap), pl.BlockSpec((1, tr, D_MODEL), bmap)],
        scratch_shapes=[
            pltpu.VMEM((D_MODEL, EXPERT_FF), BF),
            pltpu.VMEM((D_MODEL, EXPERT_FF), BF),
            pltpu.VMEM((EXPERT_FF, D_MODEL), BF),
        ],
        compiler_params=_params("arbitrary", "arbitrary"),
        name="ec_experts",
    )(xs_a, gs_a, xs_b, gs_b, w_gate, w_up, w_down)


def _combine_kernel(off_ref, st_ref, y_ref, x_ref, gate_ref, fg_ref, o_ref, *, cap, final):
    st = st_ref[0]
    tl = st.shape[0]

    def finish(scattered):
        v = x_ref[0] + gate_ref[0] * scattered
        if final:
            v = v * lax.rsqrt(jnp.mean(v * v, axis=-1, keepdims=True) + EPS) * fg_ref[...]
        o_ref[0] = v

    def onehot(width, starts):
        n = N_EXPERTS * width
        owner = lax.broadcasted_iota(jnp.int32, (N_EXPERTS, n), 1) // width
        spread = jnp.where(owner == lax.broadcasted_iota(jnp.int32, (N_EXPERTS, n), 0), 1.0, 0.0).astype(BF)
        slot = _dot(st.astype(BF), spread)
        lane = lax.broadcasted_iota(jnp.int32, (1, n), 1)
        want = (lane % width).astype(F32)
        if starts is not None:
            for ex in range(N_EXPERTS):
                want = want + jnp.where(lane // width == ex, starts[ex].astype(F32), 0.0)
        return jnp.where(slot == want, 1.0, 0.0).astype(BF)

    def scatter_all():
        finish(_dot(onehot(cap, None), y_ref[:, 0].reshape(N_EXPERTS * cap, D_MODEL)))

    if cap <= COMBINE_WIN:
        scatter_all()
        return

    starts, fits = _list_windows(off_ref, cap)

    @pl.when(fits)
    def _():
        rows = [y_ref[ex, 0, pl.ds(pl.multiple_of(starts[ex], BF_TILE), COMBINE_WIN), :]
                for ex in range(N_EXPERTS)]
        finish(_dot(onehot(COMBINE_WIN, starts), jnp.concatenate(rows, axis=0)))

    @pl.when(jnp.logical_not(fits))
    def _():
        scatter_all()


def _combine_call(slot_t, offsets, y, x, mod, final_g, final):
    B, L, _ = x.shape
    bm = mod.shape[0]
    cap = y.shape[2]
    tl = min(L, COMBINE_TL)
    assert cap <= COMBINE_WIN or (cap % BF_TILE == 0 and COMBINE_WIN % BF_TILE == 0)
    assert cap <= 256
    gate_map =(lambda b, i, off: (b, 0, 5)) if bm > 1 else (lambda b, i, off: (0, 0, 5))
    return pl.pallas_call(
        functools.partial(_combine_kernel, cap=cap, final=final),
        out_shape=jax.ShapeDtypeStruct((B, L, D_MODEL), F32),
        grid_spec=pltpu.PrefetchScalarGridSpec(
            num_scalar_prefetch=1,
            grid=(B, L // tl),
            in_specs=[
                pl.BlockSpec((1, tl, N_EXPERTS), lambda b, i, off: (b, i, 0)),
                pl.BlockSpec((N_EXPERTS, 1, cap, D_MODEL), lambda b, i, off: (0, b, 0, 0)),
                pl.BlockSpec((1, tl, D_MODEL), lambda b, i, off: (b, i, 0)),
                pl.BlockSpec((1, 1, D_MODEL), gate_map),
                pl.BlockSpec((1, D_MODEL), lambda b, i, off: (0, 0)),
            ],
            out_specs=pl.BlockSpec((1, tl, D_MODEL), lambda b, i, off: (b, i, 0)),
        ),
        compiler_params=_params("arbitrary", "arbitrary"),
        name="ec_combine",
    )(offsets, slot_t, y, x, mod, final_g)


def _mixer_half(x, mod, lw, ctx, tm, kv_prev=None):
    latent = ctx is not None
    layer = lw["layer"]
    res = _inproj_call(x, lw["norm1_g"], mod, lw["w_in"], layer, tm, rope=latent, want_kv=not latent,
                       kv_prev=kv_prev)
    z = res[0]
    br_a = _conv_call(z, lw["conv_w"], lw["conv_b"], lw["conv_ln_g"], lw["conv_ln_b"])
    br_b = _fourier_call(z, *lw["dft"][z.shape[1]])
    if latent:
        br_c = _win_attn_call(z, ctx["win_k"], ctx["win_v"], lw["win_sink"], layer)
        br_d = _na_attn_call(z, ctx["nat_k"], ctx["nat_v"], lw["bias_table"], layer)
    else:
        br_c, br_d = _ctx_attn_call(z, lw["win_sink"])
    x = _merge_call(x, lw["norm1_g"], mod, (br_a, br_b, br_c, br_d),
                    lw["w_in"], lw["w_branch"], lw["w_out"], layer, tm)
    return x, (tuple(res[1:]) if not latent else None)


def kernel(x_prompt, x_sample, cache_win_k, cache_win_v, cache_nat_k, cache_nat_v, c, c_ctx,
           norm1_g, norm2_g, w_ada, b_ada, w_in, conv_w, conv_b, conv_ln_g, conv_ln_b,
           win_sink, na_bias, w_branch, w_out, w_router, w_e_gate, w_e_up, w_e_down, final_norm_g):
    n_dec = c.shape[0]
    cond = jnp.zeros((ADA_ROWS, D_MODEL), F32).at[0].set(c_ctx).at[1:1 + n_dec].set(c)
    mods = _ada_call(cond, w_ada, b_ada)
    final_g = final_norm_g.reshape(1, D_MODEL)
    w_in_bf = _cast_call(w_in)
    w_branch_bf = _cast_call(w_branch)
    w_out_bf = _cast_call(w_out)
    dft = {L: _dft_tables(L) for L in (x_prompt.shape[1], x_sample.shape[1])}
    dim_major = lambda a: jnp.swapaxes(a, -1, -2)
    ctx = {"win_k": dim_major(cache_win_k), "win_v": dim_major(cache_win_v),
           "nat_k": dim_major(cache_nat_k), "nat_v": dim_major(cache_nat_v)}

    xp, xs = x_prompt, x_sample
    caches = None
    for l in range(DEPTH):
        wr_t = w_router[l].T
        wr_hi = wr_t.astype(BF)
        wr_lo = (wr_t - wr_hi.astype(F32)).astype(BF)
        lw = {
            "layer": l,
            "norm1_g": norm1_g[l].reshape(1, D_MODEL),
            "w_in": w_in_bf, "w_branch": w_branch_bf, "w_out": w_out_bf, "dft": dft,
            "conv_w": conv_w[l], "conv_b": conv_b[l],
            "conv_ln_g": conv_ln_g[l], "conv_ln_b": conv_ln_b[l],
            "win_sink": win_sink[l],
            "bias_table": _na_bias_call(na_bias[l], x_sample.shape[1] // GRID_W),
        }
        norm2 = norm2_g[l].reshape(1, D_MODEL)
        final = l == DEPTH - 1
        mod_p = mods[l, 0:1].reshape(1, 1, 6 * D_MODEL)
        mod_s = mods[l, 1:1 + n_dec].reshape(n_dec, 1, 6 * D_MODEL)
        xp, caches = _mixer_half(xp, mod_p, lw, None, tm=256, kv_prev=caches)
        xs, _ = _mixer_half(xs, mod_s, lw, ctx, tm=512)
        smem = lambda off: jnp.pad(off.transpose(0, 2, 1).astype(jnp.int32).reshape(-1), (0, N_EXPERTS))
        h_p, g_p, slot_p, slot_t_p, off_p = _route_call(xp, norm2, mod_p, wr_hi, wr_lo)
        h_s, g_s, slot_s, slot_t_s, off_s = _route_call(xs, norm2, mod_s, wr_hi, wr_lo)
        off_p, off_s = smem(off_p), smem(off_s)
        rows_p, gates_p = _gather_call(off_p, slot_p, g_p, h_p)
        rows_s, gates_s = _gather_call(off_s, slot_s, g_s, h_s)
        flat = lambda a: a.reshape(N_EXPERTS, a.shape[1] * a.shape[2], a.shape[3])
        y_p, y_s = _expert_call(flat(rows_p), flat(gates_p), flat(rows_s), flat(gates_s),
                                w_e_gate, w_e_up, w_e_down, l)
        xp = _combine_call(slot_t_p, off_p, y_p.reshape(rows_p.shape), xp, mod_p, final_g, final)
        xs = _combine_call(slot_t_s, off_s, y_s.reshape(rows_s.shape), xs, mod_s, final_g, final)

    return (xp, xs) + tuple(dim_major(c) for c in caches)
```

```python
import functools
import math

import ml_dtypes
import numpy as np
import jax
import jax.numpy as jnp
from jax import lax
from jax.experimental import pallas as pl
from jax.experimental.pallas import tpu as pltpu

D_MODEL = 1024
DEPTH = 2
GRID_W = 64
BRANCH_W = 256
HEAD_DIM = 64
N_BRANCH = 4
CONV_K = 31
FNET_GROUPS = 4
FNET_GDIM = BRANCH_W // FNET_GROUPS
WIN_Q_HEADS = 4
WIN_KV_HEADS = 2
WIN_BLOCK = 128
NA_HEADS = 4
NA_ROWS = 8
NA_COLS = 16
ROPE_THETA = 10000.0
ATTN_SCALE = HEAD_DIM ** -0.5
N_EXPERTS = 16
EXPERT_FF = 1024
EC_CAPACITY = 2
EPS = 1e-6
NEG = -1e30

MIX_COLS = 2048
GATE_COLS = N_BRANCH * D_MODEL
IN_COLS = MIX_COLS + GATE_COLS
COL_A, COL_F, COL_CQ, COL_CKV, COL_DQ, COL_DK, COL_DV = 0, 512, 768, 1024, 1280, 1536, 1792
ROPE_LO, ROPE_HI = COL_CQ, COL_CKV + WIN_KV_HEADS * HEAD_DIM
KV_COLS = (COL_CKV, COL_CKV + WIN_KV_HEADS * HEAD_DIM, COL_DK, COL_DV)
KV_HEADS = (WIN_KV_HEADS, WIN_KV_HEADS, NA_HEADS, NA_HEADS)

LANES = 128
SUBLANES = 8
VMEM_LIMIT = 56 * 1024 * 1024

BF = jnp.bfloat16
F32 = jnp.float32


def _params(*sem):
    return pltpu.CompilerParams(dimension_semantics=sem, vmem_limit_bytes=VMEM_LIMIT)


def _dot(a, b):
    return jnp.dot(a, b, preferred_element_type=F32)


def _dot_nt(a, b):
    return lax.dot_general(a, b, (((1,), (1,)), ((), ())), preferred_element_type=F32)


def _norm_mod(x, g, sc, sh):
    y = x * lax.rsqrt(jnp.mean(x * x, axis=-1, keepdims=True) + EPS)
    return (y * g) * (1.0 + sc) + sh


def _silu(x):
    return x * jax.nn.sigmoid(x)


CAST_BYTES = 4 * 1024 * 1024


def _cast_kernel(x_ref, o_ref):
    o_ref[...] = x_ref[...].astype(BF)


def _cast_call(x):
    shape = x.shape
    x2 = x.reshape(-1, shape[-1])
    rows, cols = x2.shape
    tr = min(rows, 1 << (max(16, CAST_BYTES // (4 * cols)).bit_length() - 1))
    assert rows % tr == 0
    out = pl.pallas_call(
        _cast_kernel,
        out_shape=jax.ShapeDtypeStruct((rows, cols), BF),
        grid=(rows // tr,),
        in_specs=[pl.BlockSpec((tr, cols), lambda i: (i, 0))],
        out_specs=pl.BlockSpec((tr, cols), lambda i: (i, 0)),
        compiler_params=_params("arbitrary"),
        name="cast_bf16",
    )(x2)
    return out.reshape(shape)


ADA_ROWS = 16
ADA_TN = 1536


def _ada_kernel(c_ref, w_ref, b_ref, o_ref):
    s = _silu(c_ref[...]).astype(BF)
    o_ref[0] = _dot(s, w_ref[0].astype(BF)) + b_ref[0]


def _ada_call(cond, w_ada, b_ada):
    n = 6 * D_MODEL
    return pl.pallas_call(
        _ada_kernel,
        out_shape=jax.ShapeDtypeStruct((DEPTH, ADA_ROWS, n), F32),
        grid=(DEPTH, n // ADA_TN),
        in_specs=[
            pl.BlockSpec((ADA_ROWS, D_MODEL), lambda l, j: (0, 0)),
            pl.BlockSpec((1, D_MODEL, ADA_TN), lambda l, j: (l, 0, j)),
            pl.BlockSpec((1, 1, ADA_TN), lambda l, j: (l, 0, j)),
        ],
        out_specs=pl.BlockSpec((1, ADA_ROWS, ADA_TN), lambda l, j: (l, 0, j)),
        compiler_params=_params("arbitrary", "arbitrary"),
        name="ada_mod",
    )(cond, w_ada, b_ada.reshape(DEPTH, 1, n))


def _mod_spec(bm, chunk, ngrid):
    if ngrid == 1:
        imap = (lambda b: (b, 0, chunk)) if bm > 1 else (lambda b: (0, 0, chunk))
    else:
        imap = (lambda b, i: (b, 0, chunk)) if bm > 1 else (lambda b, i: (0, 0, chunk))
    return pl.BlockSpec((1, 1, D_MODEL), imap)


def _inproj_kernel(*refs, rope, kv_layer, kv_aliased):
    x_ref, g_ref, sc_ref, sh_ref, w_ref = refs[:5]
    pos = 5
    if rope:
        cos_ref, sin_ref = refs[pos:pos + 2]
        pos += 2
    if kv_aliased:
        pos += len(KV_HEADS)
    z_ref = refs[pos]
    h = _norm_mod(x_ref[0], g_ref[...], sc_ref[0], sh_ref[0]).astype(BF)
    z = _dot(h, w_ref[0])
    if kv_layer is not None:
        for kv_ref, c0, heads in zip(refs[pos + 1:pos + 1 + len(KV_HEADS)], KV_COLS, KV_HEADS):
            here = 0 if kv_aliased else kv_layer
            for hd in range(heads):
                kv_ref[0, here, hd] = z[:, c0 + hd * HEAD_DIM:c0 + (hd + 1) * HEAD_DIM].T
            for other in range(kv_ref.shape[1]):
                if other != here:
                    kv_ref[0, other] = jnp.zeros(kv_ref.shape[2:], F32)
    if rope:
        z_ref[0, :, 0:ROPE_LO] = z[:, 0:ROPE_LO].astype(BF)
        lane = lax.broadcasted_iota(jnp.int32, (z.shape[0], LANES), 1)
        first_half = (lane % 32) < 16
        cos = cos_ref[...]
        sin = sin_ref[...]
        for c0 in range(ROPE_LO, ROPE_HI, LANES):
            xc = z[:, c0:c0 + LANES]
            partner = jnp.where(first_half, pltpu.roll(xc, LANES - 16, 1), pltpu.roll(xc, 16, 1))
            z_ref[0, :, c0:c0 + LANES] = (xc * cos + partner * sin).astype(BF)
        z_ref[0, :, ROPE_HI:MIX_COLS] = z[:, ROPE_HI:MIX_COLS].astype(BF)
    else:
        z_ref[0] = z.astype(BF)


def _rope_tables(L):
    t = np.arange(L)
    half = HEAD_DIM // 4
    freqs = 1.0 / (ROPE_THETA ** (np.arange(half, dtype=np.float64) / half))
    ang_r = (t // GRID_W).astype(np.float64)[:, None] * freqs[None, :]
    ang_c = (t % GRID_W).astype(np.float64)[:, None] * freqs[None, :]
    cos_h = np.concatenate([np.cos(ang_r), np.cos(ang_r), np.cos(ang_c), np.cos(ang_c)], axis=1)
    sin_h = np.concatenate([-np.sin(ang_r), np.sin(ang_r), -np.sin(ang_c), np.sin(ang_c)], axis=1)
    reps = LANES // HEAD_DIM
    return (jnp.asarray(np.tile(cos_h, (1, reps)), F32), jnp.asarray(np.tile(sin_h, (1, reps)), F32))


def _inproj_call(x, norm_g, mod, w_in_bf, layer, tm, rope, want_kv, kv_prev=None):
    B, L, _ = x.shape
    bm = mod.shape[0]
    in_specs = [
        pl.BlockSpec((1, tm, D_MODEL), lambda b, i: (b, i, 0)),
        pl.BlockSpec((1, D_MODEL), lambda b, i: (0, 0)),
        _mod_spec(bm, 1, 2),
        _mod_spec(bm, 0, 2),
        pl.BlockSpec((1, D_MODEL, MIX_COLS), lambda b, i: (layer, 0, 0)),
    ]
    args = [x, norm_g, mod, mod, w_in_bf]
    if rope:
        cos, sin = _rope_tables(L)
        in_specs += [pl.BlockSpec((tm, LANES), lambda b, i: (i, 0))] * 2
        args += [cos, sin]
    out_shape = [jax.ShapeDtypeStruct((B, L, MIX_COLS), BF)]
    out_specs = [pl.BlockSpec((1, tm, MIX_COLS), lambda b, i: (b, i, 0))]
    aliases = {}
    if want_kv:
        for n, heads in enumerate(KV_HEADS):
            out_shape.append(jax.ShapeDtypeStruct((B, DEPTH, heads, HEAD_DIM, L), F32))
            if kv_prev is None:
                out_specs.append(pl.BlockSpec((1, DEPTH, heads, HEAD_DIM, tm), lambda b, i: (b, 0, 0, 0, i)))
            else:
                out_specs.append(pl.BlockSpec((1, 1, heads, HEAD_DIM, tm), lambda b, i: (b, layer, 0, 0, i)))
                aliases[len(args)] = 1 + n
                in_specs.append(pl.BlockSpec(memory_space=pl.ANY))
                args.append(kv_prev[n])
    return pl.pallas_call(
        functools.partial(_inproj_kernel, rope=rope, kv_layer=layer if want_kv else None,
                          kv_aliased=kv_prev is not None),
        out_shape=out_shape,
        grid=(B, L // tm),
        in_specs=in_specs,
        out_specs=out_specs,
        input_output_aliases=aliases,
        compiler_params=_params("arbitrary", "arbitrary"),
        name="in_proj",
    )(*args)


CONV_PAD = 16
CONV_TR = 64
CONV_CHUNK = 256


def _conv_kernel(a_ref, w_ref, cb_ref, lg_ref, lb_ref, o_ref, p_ref, u_ref, wb_ref, *, L):
    for k in range(CONV_K):
        wb_ref[k] = jnp.broadcast_to(w_ref[k:k + 1, :], (SUBLANES, BRANCH_W))
    zeros = jnp.zeros((CONV_PAD, BRANCH_W), F32)
    p_ref[0:CONV_PAD, :] = zeros
    p_ref[L + CONV_PAD:L + 2 * CONV_PAD, :] = zeros
    for r0 in range(0, L, CONV_CHUNK):
        a = a_ref[0, r0:r0 + CONV_CHUNK, :].astype(F32)
        p_ref[CONV_PAD + r0:CONV_PAD + r0 + CONV_CHUNK, :] = a[:, :BRANCH_W] * jax.nn.sigmoid(a[:, BRANCH_W:])
    n_u = L + 2 * CONV_PAD - SUBLANES
    for s in range(SUBLANES):
        u_ref[s] = p_ref[s:s + n_u, :]
    cb = cb_ref[...]

    def body(i, carry):
        base = pl.multiple_of(i * CONV_TR, CONV_TR)
        groups = (CONV_TR // SUBLANES, SUBLANES, BRANCH_W)
        acc = jnp.zeros(groups, F32) + cb
        for k in range(CONV_K):
            q, s = divmod(k + CONV_PAD - CONV_K // 2, SUBLANES)
            u = u_ref[s, pl.ds(base + SUBLANES * q, CONV_TR), :]
            acc = acc + wb_ref[k] * u.reshape(groups)
        p_ref[pl.ds(base, CONV_TR), :] = acc.reshape(CONV_TR, BRANCH_W)
        return carry

    lax.fori_loop(0, L // CONV_TR, body, 0)
    lg = lg_ref[...]
    lb = lb_ref[...]
    for r0 in range(0, L, CONV_CHUNK):
        acc = p_ref[r0:r0 + CONV_CHUNK, :]
        mu = jnp.mean(acc, axis=-1, keepdims=True)
        d = acc - mu
        var = jnp.mean(d * d, axis=-1, keepdims=True)
        y = d * lax.rsqrt(var + EPS) * lg + lb
        o_ref[0, r0:r0 + CONV_CHUNK, :] = _silu(y).astype(BF)


def _conv_call(z, conv_w, conv_b, ln_g, ln_b):
    B, L, _ = z.shape
    n_u = L + 2 * CONV_PAD - SUBLANES
    vec = lambda b: (0, 0)
    return pl.pallas_call(
        functools.partial(_conv_kernel, L=L),
        out_shape=jax.ShapeDtypeStruct((B, L, BRANCH_W), BF),
        grid=(B,),
        in_specs=[
            pl.BlockSpec((1, L, 2 * BRANCH_W), lambda b: (b, 0, COL_A // (2 * BRANCH_W))),
            pl.BlockSpec((CONV_K, BRANCH_W), vec),
            pl.BlockSpec((1, BRANCH_W), vec),
            pl.BlockSpec((1, BRANCH_W), vec),
            pl.BlockSpec((1, BRANCH_W), vec),
        ],
        out_specs=pl.BlockSpec((1, L, BRANCH_W), lambda b: (b, 0, 0)),
        scratch_shapes=[
            pltpu.VMEM((L + 2 * CONV_PAD, BRANCH_W), F32),
            pltpu.VMEM((SUBLANES, n_u, BRANCH_W), F32),
            pltpu.VMEM((CONV_K, SUBLANES, BRANCH_W), F32),
        ],
        compiler_params=_params("arbitrary"),
        name="conv_module",
    )(z, conv_w, conv_b.reshape(1, -1), ln_g.reshape(1, -1), ln_b.reshape(1, -1))


def _dft_tables(L):
    c = np.arange(FNET_GDIM)
    ang_c = 2.0 * np.pi * ((c[:, None] * c[None, :]) % FNET_GDIM) / FNET_GDIM
    cc = np.zeros((BRANCH_W, BRANCH_W))
    sc = np.zeros((BRANCH_W, BRANCH_W))
    for g in range(FNET_GROUPS):
        sl = slice(g * FNET_GDIM, (g + 1) * FNET_GDIM)
        cc[sl, sl] = np.cos(ang_c)
        sc[sl, sl] = np.sin(ang_c)
    ccs = np.concatenate([cc, sc], axis=1)
    f = np.arange(L)
    t = np.arange(L // 2)
    ang_l = 2.0 * np.pi * ((f[:, None] * t[None, :]) % L) / L
    csl = np.concatenate([np.cos(ang_l), -np.sin(ang_l)], axis=1)
    return _cast_call(jnp.asarray(ccs, F32)), _cast_call(jnp.asarray(csl, F32))


FOURIER_TR = 512
FOURIER_PAD = 8


def _fourier_kernel(f_ref, ccs_ref, csl_ref, o_ref, pq_ref, fold_ref, *, L):
    scale = float(1.0 / np.sqrt(L * FNET_GDIM))
    half = L // 2
    tr = min(L, FOURIER_TR)
    for r0 in range(0, L, tr):
        pq_ref[r0:r0 + tr, :] = _dot(f_ref[0, r0:r0 + tr, :], ccs_ref[...])
    pq_ref[L:L + FOURIER_PAD, :] = jnp.zeros((FOURIER_PAD, 2 * BRANCH_W), F32)
    blk = min(half, LANES)
    ri = lax.broadcasted_iota(jnp.int32, (blk, blk), 0)
    ci = lax.broadcasted_iota(jnp.int32, (blk, blk), 1)
    exchange = jnp.where(ri + ci == blk - 1, 1.0, 0.0).astype(BF)
    for i in range(half // blk):
        lo = L - blk * (i + 1) + 1
        partner = _dot(exchange, pq_ref[lo:lo + blk, :].astype(BF))
        own = pq_ref[blk * i:blk * (i + 1), :]
        fold_ref[blk * i:blk * (i + 1), :] = (own[:, :BRANCH_W] + partner[:, :BRANCH_W]).astype(BF)
        fold_ref[half + blk * i:half + blk * (i + 1), :] = (own[:, BRANCH_W:] - partner[:, BRANCH_W:]).astype(BF)
    mid = pq_ref[half:half + 1, 0:BRANCH_W]
    for r0 in range(0, L, tr):
        o = _dot(csl_ref[r0:r0 + tr, :], fold_ref[...])
        row = lax.broadcasted_iota(jnp.int32, (tr, 1), 0) + r0
        sign = (1 - 2 * (row % 2)).astype(F32)
        o_ref[0, r0:r0 + tr, :] = ((o + sign * mid) * scale).astype(BF)


def _fourier_call(z, ccs, csl):
    B, L, _ = z.shape
    return pl.pallas_call(
        functools.partial(_fourier_kernel, L=L),
        out_shape=jax.ShapeDtypeStruct((B, L, BRANCH_W), BF),
        grid=(B,),
        in_specs=[
            pl.BlockSpec((1, L, BRANCH_W), lambda b: (b, 0, COL_F // BRANCH_W)),
            pl.BlockSpec((BRANCH_W, 2 * BRANCH_W), lambda b: (0, 0)),
            pl.BlockSpec((L, L), lambda b: (0, 0), pipeline_mode=pl.Buffered(1)),
        ],
        out_specs=pl.BlockSpec((1, L, BRANCH_W), lambda b: (b, 0, 0)),
        scratch_shapes=[
            pltpu.VMEM((L + FOURIER_PAD, 2 * BRANCH_W), F32),
            pltpu.VMEM((L, BRANCH_W), BF),
        ],
        compiler_params=_params("arbitrary"),
        name="fourier_mix",
    )(z, ccs, csl)


def _head(ref, rows, h):
    return ref[0, rows, h * HEAD_DIM:(h + 1) * HEAD_DIM]


def _all_heads_attention(q, k, v, sinks):
    L, width = q.shape
    heads = width // HEAD_DIM
    lane = lax.broadcasted_iota(jnp.int32, (L, width), 1)
    q = q * ATTN_SCALE
    stacked = jnp.concatenate([jnp.where(lane // HEAD_DIM == h, q, jnp.zeros_like(q)) for h in range(heads)],
                              axis=0)
    s = _dot_nt(stacked, k)
    m = jnp.max(s, axis=-1, keepdims=True)
    if sinks is not None:
        row = lax.broadcasted_iota(jnp.int32, (heads * L, 1), 0)
        sink = jnp.zeros((heads * L, 1), F32)
        for h in range(heads):
            sink = jnp.where(row // L == h, sinks[h], sink)
        m = jnp.maximum(m, sink)
    p = jnp.exp(s - m)
    den = jnp.sum(p, axis=-1, keepdims=True)
    if sinks is not None:
        den = den + jnp.exp(sink - m)
    o = _dot(p.astype(BF), v) / den
    out = jnp.zeros((L, width), F32)
    for h in range(heads):
        out = jnp.where(lane // HEAD_DIM == h, o[h * L:(h + 1) * L], out)
    return out.astype(BF)


def _ctx_attn_kernel(sink_ref, qc_ref, kvc_ref, qd_ref, kd_ref, vd_ref, oc_ref, od_ref):
    rows = slice(None)
    group = WIN_Q_HEADS // WIN_KV_HEADS
    k_c = jnp.concatenate([_head(kvc_ref, rows, h // group) for h in range(WIN_Q_HEADS)], axis=1)
    v_c = jnp.concatenate([_head(kvc_ref, rows, WIN_KV_HEADS + h // group) for h in range(WIN_Q_HEADS)], axis=1)
    oc_ref[0] = _all_heads_attention(qc_ref[0], k_c, v_c, [sink_ref[h] for h in range(WIN_Q_HEADS)])
    od_ref[0] = _all_heads_attention(qd_ref[0], kd_ref[0], vd_ref[0], None)


def _zcol_spec(L, col):
    return pl.BlockSpec((1, L, BRANCH_W), lambda b: (b, 0, col // BRANCH_W))


def _ctx_attn_call(z, sink):
    B, L, _ = z.shape
    out = jax.ShapeDtypeStruct((B, L, BRANCH_W), BF)
    ospec = pl.BlockSpec((1, L, BRANCH_W), lambda b: (b, 0, 0))
    return pl.pallas_call(
        _ctx_attn_kernel,
        out_shape=[out, out],
        grid=(B,),
        in_specs=[
            pl.BlockSpec(memory_space=pltpu.SMEM),
            _zcol_spec(L, COL_CQ), _zcol_spec(L, COL_CKV),
            _zcol_spec(L, COL_DQ), _zcol_spec(L, COL_DK), _zcol_spec(L, COL_DV),
        ],
        out_specs=[ospec, ospec],
        compiler_params=_params("arbitrary"),
        name="ctx_attention",
    )(sink, z, z, z, z, z)


PAIR = 2 * HEAD_DIM


def _one_head(x, lane, g):
    return jnp.where(lane // HEAD_DIM == g, x, jnp.zeros_like(x))


def _win_attn_kernel(sink_ref, q_ref, kv_ref, ck_ref, cv_ref, o_ref, *, L):
    nb = L // WIN_BLOCK
    group = WIN_Q_HEADS // WIN_KV_HEADS
    m_rows = group * WIN_BLOCK
    assert WIN_BLOCK == LANES
    row = lax.broadcasted_iota(jnp.int32, (m_rows, WIN_BLOCK), 0)
    a = row % WIN_BLOCK
    j = lax.broadcasted_iota(jnp.int32, (m_rows, WIN_BLOCK), 1)
    row1 = lax.broadcasted_iota(jnp.int32, (m_rows, 1), 0)
    for kvh in range(WIN_KV_HEADS):
        ctx_k = ck_ref[0, 0, kvh].astype(BF)
        ctx_v = cv_ref[0, 0, kvh].astype(BF)
        sink = jnp.zeros((m_rows, 1), F32)
        for g in range(group):
            sink = jnp.where(row1 // WIN_BLOCK == g, sink_ref[kvh * group + g], sink)

        def body(n, carry, kvh=kvh, ctx_k=ctx_k, ctx_v=ctx_v, sink=sink):
            r0 = pl.multiple_of(n * WIN_BLOCK, WIN_BLOCK)
            q = jnp.concatenate(
                [_head(q_ref, pl.ds(r0, WIN_BLOCK), kvh * group + g) for g in range(group)], axis=0)
            q = q * ATTN_SCALE
            s_ctx = _dot(q, ctx_k)
            m_el = jnp.maximum(s_ctx[:, :LANES], s_ctx[:, LANES:])
            scores = []
            vals = []
            for dj in (-1, 0, 1):
                nk = jnp.clip(n + dj, 0, nb - 1)
                rk = pl.multiple_of(nk * WIN_BLOCK, WIN_BLOCK)
                s = _dot_nt(q, _head(kv_ref, pl.ds(rk, WIN_BLOCK), kvh))
                if dj == -1:
                    s = jnp.where((j >= a) & (n >= 1), s, NEG)
                elif dj == 1:
                    s = jnp.where((j <= a) & (n <= nb - 2), s, NEG)
                m_el = jnp.maximum(m_el, s)
                scores.append(s)
                vals.append(_head(kv_ref, pl.ds(rk, WIN_BLOCK), WIN_KV_HEADS + kvh))
            m = jnp.maximum(jnp.max(m_el, axis=-1, keepdims=True), sink)
            p_ctx = jnp.exp(s_ctx - m)
            den_el = p_ctx[:, :LANES] + p_ctx[:, LANES:]
            o = _dot_nt(p_ctx.astype(BF), ctx_v)
            for s, v in zip(scores, vals):
                p = jnp.exp(s - m)
                den_el = den_el + p
                o = o + _dot(p.astype(BF), v)
            den = jnp.sum(den_el, axis=-1, keepdims=True) + jnp.exp(sink - m)
            o = (o / den).astype(BF)
            for g in range(group):
                h = kvh * group + g
                o_ref[0, pl.ds(r0, WIN_BLOCK), h * HEAD_DIM:(h + 1) * HEAD_DIM] = (
                    o[g * WIN_BLOCK:(g + 1) * WIN_BLOCK])
            return carry

        lax.fori_loop(0, nb, body, 0, unroll=8)


def _win_attn_call(z, cache_k, cache_v, sink, layer):
    B, L, _ = z.shape
    P = cache_k.shape[4]
    assert P == 2 * LANES
    cspec = pl.BlockSpec((1, 1, WIN_KV_HEADS, HEAD_DIM, P), lambda b: (b, layer, 0, 0, 0))
    return pl.pallas_call(
        functools.partial(_win_attn_kernel, L=L),
        out_shape=jax.ShapeDtypeStruct((B, L, BRANCH_W), BF),
        grid=(B,),
        in_specs=[
            pl.BlockSpec(memory_space=pltpu.SMEM),
            _zcol_spec(L, COL_CQ), _zcol_spec(L, COL_CKV), cspec, cspec,
        ],
        out_specs=pl.BlockSpec((1, L, BRANCH_W), lambda b: (b, 0, 0)),
        compiler_params=_params("arbitrary"),
        name="window_attention",
    )(sink, z, z, cache_k, cache_v)


N_ROW_OFF = 2 * NA_ROWS - 1
N_COL_OFF = 2 * NA_COLS - 1


NA_QROWS = 4
NA_WIN_ROWS = NA_ROWS + NA_QROWS


def _na_window_start(r0, rows):
    return min(max(r0 - NA_ROWS // 2, 0), rows - NA_WIN_ROWS)


def _na_group_blocks(r0, rows):
    ws = _na_window_start(r0, rows)
    blocks = {}
    for i in range(NA_QROWS):
        r = r0 + i
        start = min(max(r - NA_ROWS // 2, 0), rows - NA_ROWS)
        for kk in range(NA_WIN_ROWS):
            rel = ws + kk - start
            blocks[i, kk] = ws + kk - r + (NA_ROWS - 1) if 0 <= rel < NA_ROWS else None
    return blocks


def _na_patterns(rows):
    layouts, first = [], []
    for qi in range(rows // NA_QROWS):
        blocks = _na_group_blocks(qi * NA_QROWS, rows)
        if not layouts or blocks != layouts[-1]:
            assert blocks not in layouts
            layouts.append(blocks)
            first.append(qi)
    return layouts, first


def _na_bias_kernel(rb_ref, o_ref, *, rows):
    h = pl.program_id(0)
    cq = lax.broadcasted_iota(jnp.int32, (GRID_W, GRID_W), 0)
    ck = lax.broadcasted_iota(jnp.int32, (GRID_W, GRID_W), 1)
    col_start = jnp.clip(cq - NA_COLS // 2, 0, GRID_W - NA_COLS)
    col_ok = (ck >= col_start) & (ck < col_start + NA_COLS)
    col_off = jnp.clip(ck - cq + (NA_COLS - 1), 0, N_COL_OFF - 1)
    tiles = []
    for ro in range(N_ROW_OFF):
        t = jnp.zeros((GRID_W, GRID_W), F32)
        for co in range(N_COL_OFF):
            t = jnp.where(col_off == co, rb_ref[(h * N_ROW_OFF + ro) * N_COL_OFF + co], t)
        tiles.append(jnp.where(col_ok, t, NEG))
    masked = jnp.full((GRID_W, GRID_W), NEG, F32)
    for p, blocks in enumerate(_na_patterns(rows)[0]):
        for (i, kk), ro in blocks.items():
            o_ref[0, p, i * GRID_W:(i + 1) * GRID_W, kk * GRID_W:(kk + 1) * GRID_W] = (
                masked if ro is None else tiles[ro])


def _na_bias_call(na_bias, rows):
    shape = (NA_HEADS, len(_na_patterns(rows)[0]), NA_QROWS * GRID_W, NA_WIN_ROWS * GRID_W)
    return pl.pallas_call(
        functools.partial(_na_bias_kernel, rows=rows),
        out_shape=jax.ShapeDtypeStruct(shape, F32),
        grid=(NA_HEADS,),
        in_specs=[pl.BlockSpec(memory_space=pltpu.SMEM)],
        out_specs=pl.BlockSpec((1,) + shape[1:], lambda h: (h, 0, 0, 0)),
        compiler_params=_params("arbitrary"),
        name="na_bias_table",
    )(na_bias.reshape(-1))


def _na_attn_kernel(q_ref, k_ref, v_ref, ck_ref, cv_ref, bt_ref, o_ref, *, L):
    rows = L // GRID_W
    groups = rows // NA_QROWS
    pattern_starts = _na_patterns(rows)[1]
    lane = lax.broadcasted_iota(jnp.int32, (NA_QROWS * GRID_W, PAIR), 1)
    for hp in range(NA_HEADS * HEAD_DIM // PAIR):
        heads = (2 * hp, 2 * hp + 1)
        lanes = slice(hp * PAIR, (hp + 1) * PAIR)
        ctx_k = jnp.concatenate([ck_ref[0, 0, h].astype(BF) for h in heads], axis=0)
        ctx_v = jnp.concatenate([cv_ref[0, 0, h].astype(BF) for h in heads], axis=0)

        def body(qi, carry, heads=heads, lanes=lanes, ctx_k=ctx_k, ctx_v=ctx_v):
            r0 = qi * NA_QROWS
            ws = jnp.clip(r0 - NA_ROWS // 2, 0, rows - NA_WIN_ROWS)
            pat = sum(jnp.where(qi >= first, 1, 0) for first in pattern_starts[1:])
            rq = pl.ds(pl.multiple_of(r0 * GRID_W, NA_QROWS * GRID_W), NA_QROWS * GRID_W)
            rk = pl.ds(pl.multiple_of(ws * GRID_W, GRID_W), NA_WIN_ROWS * GRID_W)
            q2 = q_ref[0, rq, lanes] * ATTN_SCALE
            k2 = k_ref[0, rk, lanes]
            v2 = v_ref[0, rk, lanes]
            outs = []
            for g, h in enumerate(heads):
                qg = _one_head(q2, lane, g)
                s = _dot_nt(qg, k2) + bt_ref[h, pat]
                s_ctx = _dot(qg, ctx_k)
                m = jnp.maximum(jnp.max(s, axis=-1, keepdims=True), jnp.max(s_ctx, axis=-1, keepdims=True))
                p = jnp.exp(s - m)
                p_ctx = jnp.exp(s_ctx - m)
                den = jnp.sum(p, axis=-1, keepdims=True) + jnp.sum(p_ctx, axis=-1, keepdims=True)
                outs.append((_dot(p.astype(BF), v2) + _dot_nt(p_ctx.astype(BF), ctx_v)) / den)
            o_ref[0, rq, lanes] = jnp.where(lane // HEAD_DIM == 0, outs[0], outs[1]).astype(BF)
            return carry

        lax.fori_loop(0, groups, body, 0, unroll=4)


def _na_attn_call(z, cache_k, cache_v, bias_table, layer):
    B, L, _ = z.shape
    P = cache_k.shape[4]
    assert L // GRID_W >= NA_WIN_ROWS + NA_QROWS and (L // GRID_W) % NA_QROWS == 0
    cspec = pl.BlockSpec((1, 1, NA_HEADS, HEAD_DIM, P), lambda b: (b, layer, 0, 0, 0))
    return pl.pallas_call(
        functools.partial(_na_attn_kernel, L=L),
        out_shape=jax.ShapeDtypeStruct((B, L, BRANCH_W), BF),
        grid=(B,),
        in_specs=[
            _zcol_spec(L, COL_DQ), _zcol_spec(L, COL_DK), _zcol_spec(L, COL_DV), cspec, cspec,
            pl.BlockSpec(bias_table.shape, lambda b: (0, 0, 0, 0), pipeline_mode=pl.Buffered(1)),
        ],
        out_specs=pl.BlockSpec((1, L, BRANCH_W), lambda b: (b, 0, 0)),
        compiler_params=_params("arbitrary"),
        name="neighbourhood_attention",
    )(z, z, z, cache_k, cache_v, bias_table)


def _merge_kernel(x_ref, g_ref, sc_ref, sh_ref, gate_ref, ba_ref, bb_ref, bc_ref, bd_ref,
                  wg0_ref, wg1_ref, wb_ref, wo_ref, o_ref):
    x = x_ref[0]
    h = _norm_mod(x, g_ref[...], sc_ref[0], sh_ref[0]).astype(BF)
    merged = jnp.zeros(x.shape, F32)
    per_ref = MIX_COLS // D_MODEL
    for n, br_ref in enumerate((ba_ref, bb_ref, bc_ref, bd_ref)):
        wg_ref = (wg0_ref, wg1_ref)[n // per_ref]
        c0 = (n % per_ref) * D_MODEL
        gate = jax.nn.sigmoid(_dot(h, wg_ref[0, :, c0:c0 + D_MODEL]))
        merged = merged + gate * _dot(br_ref[0], wb_ref[0, n])
    o_ref[0] = x + gate_ref[0] * _dot(merged.astype(BF), wo_ref[0])


def _merge_call(x, norm_g, mod, branches, w_in_bf, w_branch, w_out, layer, tm):
    B, L, _ = x.shape
    bm = mod.shape[0]
    xspec = pl.BlockSpec((1, tm, D_MODEL), lambda b, i: (b, i, 0))
    bspec = pl.BlockSpec((1, tm, BRANCH_W), lambda b, i: (b, i, 0))
    return pl.pallas_call(
        _merge_kernel,
        out_shape=jax.ShapeDtypeStruct((B, L, D_MODEL), F32),
        grid=(B, L // tm),
        in_specs=[
            xspec,
            pl.BlockSpec((1, D_MODEL), lambda b, i: (0, 0)),
            _mod_spec(bm, 1, 2), _mod_spec(bm, 0, 2), _mod_spec(bm, 2, 2),
            bspec, bspec, bspec, bspec,
            pl.BlockSpec((1, D_MODEL, MIX_COLS), lambda b, i: (layer, 0, 1)),
            pl.BlockSpec((1, D_MODEL, MIX_COLS), lambda b, i: (layer, 0, 2)),
            pl.BlockSpec((1, N_BRANCH, BRANCH_W, D_MODEL), lambda b, i: (layer, 0, 0, 0)),
            pl.BlockSpec((1, D_MODEL, D_MODEL), lambda b, i: (layer, 0, 0)),
        ],
        out_specs=xspec,
        compiler_params=_params("arbitrary", "arbitrary"),
        name="merge_out_proj",
    )(x, norm_g, mod, mod, mod, *branches, w_in_bf, w_in_bf, w_branch, w_out)


ROUTE_CHUNK = 256
GATHER_ROWS = 512
SIGN_BIT = 31
SEARCH_BITS = 4
SEARCH_TOP_SHIFT = ((SIGN_BIT - 1) // SEARCH_BITS) * SEARCH_BITS


def _prefix_excl(mask_ref, out_ref, tri, L):
    totals = [jnp.sum(mask_ref[:, c0:c0 + LANES], axis=-1, keepdims=True) for c0 in range(0, L, LANES)]
    run = jnp.zeros((N_EXPERTS, 1), F32)
    for i, c0 in enumerate(range(0, L, LANES)):
        out_ref[:, c0:c0 + LANES] = _dot(mask_ref[:, c0:c0 + LANES].astype(BF), tri) + run
        run = run + totals[i]


def _route_kernel(x_ref, g_ref, sc_ref, sh_ref, wh_ref, wl_ref, tri_ref,
                  h_ref, gate_ref, slot_ref, slot_t_ref, off_ref, lt_ref, msk_ref, pre_ref, *, L, cap):
    g = g_ref[...]
    sc = sc_ref[0]
    sh = sh_ref[0]
    wh = wh_ref[...]
    wl = wl_ref[...]
    rc = min(L, ROUTE_CHUNK)
    for r0 in range(0, L, rc):
        h = _norm_mod(x_ref[0, r0:r0 + rc, :], g, sc, sh)
        h_hi = h.astype(BF)
        h_ref[0, r0:r0 + rc, :] = h_hi
        h_lo = (h - h_hi.astype(F32)).astype(BF)
        lt_ref[:, r0:r0 + rc] = _dot_nt(wh, h_hi) + (_dot_nt(wh, h_lo) + _dot_nt(wl, h_hi))
    lt = lt_ref[...]
    e = jnp.exp(lt - jnp.max(lt, axis=0, keepdims=True))
    aff = e / jnp.sum(e, axis=0, keepdims=True)
    thr = jnp.zeros((N_EXPERTS, 1), jnp.int32)
    for shift in range(SEARCH_TOP_SHIFT, -1, -SEARCH_BITS):
        n_digits = min(1 << SEARCH_BITS, 1 << (SIGN_BIT - shift))
        digit = jnp.zeros((N_EXPERTS, 1), jnp.int32)
        for d in range(1, n_digits):
            cand = pltpu.bitcast(thr | (d << shift), F32)
            cnt = jnp.sum((aff >= cand).astype(F32), axis=-1, keepdims=True)
            digit = digit + (cnt >= cap).astype(jnp.int32)
        thr = thr | (digit << shift)
    thr_f = pltpu.bitcast(thr, F32)
    gt = aff > thr_f
    eq = aff == thr_f
    need = cap - jnp.sum(gt.astype(F32), axis=-1, keepdims=True)
    tri = tri_ref[...]
    msk_ref[...] = eq.astype(F32)
    _prefix_excl(msk_ref, pre_ref, tri, L)
    sel = gt | (eq & (pre_ref[...] < need))
    msk_ref[...] = sel.astype(F32)
    _prefix_excl(msk_ref, pre_ref, tri, L)
    off_ref[0] = jnp.concatenate([pre_ref[:, c:c + 1] for c in range(0, L, min(L, COMBINE_TL))], axis=1)
    slot = jnp.where(sel, pre_ref[...], -1.0)
    slot_ref[0] = slot
    slot_t_ref[0] = slot.T
    gate_ref[0] = jnp.where(sel, aff, 0.0)


def _route_call(x, norm_g, mod, wr_hi, wr_lo):
    B, L, _ = x.shape
    bm = mod.shape[0]
    cap = EC_CAPACITY * L // N_EXPERTS
    n_tiles = L // min(L, COMBINE_TL)
    tri = jnp.asarray(np.triu(np.ones((LANES, LANES)), 1).astype(ml_dtypes.bfloat16))
    wspec = pl.BlockSpec((N_EXPERTS, D_MODEL), lambda b: (0, 0))
    return pl.pallas_call(
        functools.partial(_route_kernel, L=L, cap=cap),
        out_shape=[
            jax.ShapeDtypeStruct((B, L, D_MODEL), BF),
            jax.ShapeDtypeStruct((B, N_EXPERTS, L), F32),
            jax.ShapeDtypeStruct((B, N_EXPERTS, L), F32),
            jax.ShapeDtypeStruct((B, L, N_EXPERTS), F32),
            jax.ShapeDtypeStruct((B, N_EXPERTS, n_tiles), F32),
        ],
        grid=(B,),
        in_specs=[
            pl.BlockSpec((1, L, D_MODEL), lambda b: (b, 0, 0)),
            pl.BlockSpec((1, D_MODEL), lambda b: (0, 0)),
            _mod_spec(bm, 4, 1), _mod_spec(bm, 3, 1),
            wspec, wspec,
            pl.BlockSpec((LANES, LANES), lambda b: (0, 0)),
        ],
        out_specs=[
            pl.BlockSpec((1, L, D_MODEL), lambda b: (b, 0, 0)),
            pl.BlockSpec((1, N_EXPERTS, L), lambda b: (b, 0, 0)),
            pl.BlockSpec((1, N_EXPERTS, L), lambda b: (b, 0, 0)),
            pl.BlockSpec((1, L, N_EXPERTS), lambda b: (b, 0, 0)),
            pl.BlockSpec((1, N_EXPERTS, n_tiles), lambda b: (b, 0, 0)),
        ],
        scratch_shapes=[
            pltpu.VMEM((N_EXPERTS, L), F32),
            pltpu.VMEM((N_EXPERTS, L), F32),
            pltpu.VMEM((N_EXPERTS, L), F32),
        ],
        compiler_params=_params("arbitrary"),
        name="ec_route",
    )(x, norm_g, mod, mod, wr_hi, wr_lo, tri)


COMBINE_TL = 256
COMBINE_WIN = 64
BF_TILE = 16


def _list_windows(off_ref, cap):
    n_tiles = pl.num_programs(1)
    base = (pl.program_id(0) * n_tiles + pl.program_id(1)) * N_EXPERTS
    starts = []
    fits = None
    for ex in range(N_EXPERTS):
        first = off_ref[base + ex]
        end = jnp.where(pl.program_id(1) == n_tiles - 1, cap, off_ref[base + N_EXPERTS + ex])
        start = jnp.minimum((first // BF_TILE) * BF_TILE, cap - COMBINE_WIN)
        ok = end <= start + COMBINE_WIN
        fits = ok if fits is None else (fits & ok)
        starts.append(start)
    return starts, fits


def _gather_kernel(off_ref, slot_ref, gate_ref, h_ref, xs_ref, gs_ref, *, cap):
    h = h_ref[0]
    tl = h.shape[0]

    @pl.when(pl.program_id(1) == 0)
    def _():
        xs_ref[...] = jnp.zeros(xs_ref.shape, BF)
        gs_ref[...] = jnp.zeros(gs_ref.shape, F32)

    def select(ex, first, rows):
        pos = lax.broadcasted_iota(jnp.int32, (rows, tl), 0).astype(F32)
        hit = slot_ref[0, ex:ex + 1, :] - first == pos
        gate = jnp.sum(jnp.where(hit, gate_ref[0, ex:ex + 1, :], 0.0), axis=-1, keepdims=True)
        return jnp.where(hit, 1.0, 0.0).astype(BF), jnp.broadcast_to(gate, (rows, LANES))

    def gather_all():
        ge = max(1, min(N_EXPERTS, GATHER_ROWS // cap))
        for e0 in range(0, N_EXPERTS, ge):
            picked = [select(ex, 0.0, cap) for ex in range(e0, e0 + ge)]
            rows = _dot(jnp.concatenate([p[0] for p in picked], axis=0), h).astype(BF)
            for j in range(ge):
                xs_ref[e0 + j, 0] += rows[j * cap:(j + 1) * cap]
                gs_ref[e0 + j, 0] += picked[j][1]

    if cap <= COMBINE_WIN:
        gather_all()
        return

    starts, fits = _list_windows(off_ref, cap)

    @pl.when(fits)
    def _():
        picked = [select(ex, starts[ex].astype(F32), COMBINE_WIN) for ex in range(N_EXPERTS)]
        rows = _dot(jnp.concatenate([p[0] for p in picked], axis=0), h).astype(BF)
        for ex in range(N_EXPERTS):
            win = pl.ds(pl.multiple_of(starts[ex], BF_TILE), COMBINE_WIN)
            xs_ref[ex, 0, win, :] += rows[ex * COMBINE_WIN:(ex + 1) * COMBINE_WIN]
            gs_ref[ex, 0, win, :] += picked[ex][1]

    @pl.when(jnp.logical_not(fits))
    def _():
        gather_all()


def _gather_call(offsets, slot, gate_rows, h2):
    B, L, _ = h2.shape
    cap = EC_CAPACITY * L // N_EXPERTS
    tl = min(L, COMBINE_TL)
    assert cap <= COMBINE_WIN or (cap % BF_TILE == 0 and COMBINE_WIN % BF_TILE == 0)
    return pl.pallas_call(
        functools.partial(_gather_kernel, cap=cap),
        out_shape=[
            jax.ShapeDtypeStruct((N_EXPERTS, B, cap, D_MODEL), BF),
            jax.ShapeDtypeStruct((N_EXPERTS, B, cap, LANES), F32),
        ],
        grid_spec=pltpu.PrefetchScalarGridSpec(
            num_scalar_prefetch=1,
            grid=(B, L // tl),
            in_specs=[
                pl.BlockSpec((1, N_EXPERTS, tl), lambda b, i, off: (b, 0, i)),
                pl.BlockSpec((1, N_EXPERTS, tl), lambda b, i, off: (b, 0, i)),
                pl.BlockSpec((1, tl, D_MODEL), lambda b, i, off: (b, i, 0)),
            ],
            out_specs=[
                pl.BlockSpec((N_EXPERTS, 1, cap, D_MODEL), lambda b, i, off: (0, b, 0, 0)),
                pl.BlockSpec((N_EXPERTS, 1, cap, LANES), lambda b, i, off: (0, b, 0, 0)),
            ],
        ),
        compiler_params=_params("arbitrary", "arbitrary"),
        name="ec_gather",
    )(offsets, slot, gate_rows, h2)


EXPERT_TR = 512
EXPERT_FC = 512


def _expert_kernel(xa_ref, ga_ref, xb_ref, gb_ref, wg_ref, wu_ref, wd_ref, ya_ref, yb_ref,
                   wgb_ref, wub_ref, wdb_ref, *, tiles_a):
    j = pl.program_id(1)

    @pl.when(j == 0)
    def _():
        wgb_ref[...] = wg_ref[0, 0].astype(BF)
        wub_ref[...] = wu_ref[0, 0].astype(BF)
        wdb_ref[...] = wd_ref[0, 0].astype(BF)

    def ffn(x_ref, g_ref, y_ref):
        x = x_ref[0]
        y = None
        for f0 in range(0, EXPERT_FF, EXPERT_FC):
            a = _dot(x, wgb_ref[:, f0:f0 + EXPERT_FC])
            u = _dot(x, wub_ref[:, f0:f0 + EXPERT_FC])
            part = _dot((_silu(a) * u).astype(BF), wdb_ref[f0:f0 + EXPERT_FC, :])
            y = part if y is None else y + part
        y_ref[0] = (y * g_ref[0, :, 0:1]).astype(BF)

    @pl.when(j < tiles_a)
    def _():
        ffn(xa_ref, ga_ref, ya_ref)

    @pl.when(j >= tiles_a)
    def _():
        ffn(xb_ref, gb_ref, yb_ref)


def _expert_call(xs_a, gs_a, xs_b, gs_b, w_gate, w_up, w_down, layer):
    E, ra, _ = xs_a.shape
    rb = xs_b.shape[1]
    tr = math.gcd(math.gcd(ra, rb), EXPERT_TR)
    tiles_a, tiles_b = ra // tr, rb // tr
    amap = lambda e, j: (e, jnp.minimum(j, tiles_a - 1), 0)
    bmap = lambda e, j: (e, jnp.maximum(j - tiles_a, 0), 0)
    wspec_in = pl.BlockSpec((1, 1, D_MODEL, EXPERT_FF), lambda e, j: (layer, e, 0, 0))
    wspec_out = pl.BlockSpec((1, 1, EXPERT_FF, D_MODEL), lambda e, j: (layer, e, 0, 0))
    return pl.pallas_call(
        functools.partial(_expert_kernel, tiles_a=tiles_a),
        out_shape=[jax.ShapeDtypeStruct((E, ra, D_MODEL), BF), jax.ShapeDtypeStruct((E, rb, D_MODEL), BF)],
        grid=(E, tiles_a + tiles_b),
        in_specs=[
            pl.BlockSpec((1, tr, D_MODEL), amap), pl.BlockSpec((1, tr, LANES), amap),
            pl.BlockSpec((1, tr, D_MODEL), bmap), pl.BlockSpec((1, tr, LANES), bmap),
            wspec_in, wspec_in, wspec_out,
        ],
        out_specs=[pl.BlockSpec((1, tr, D_MODEL), amap), pl.BlockSpec((1, tr, D_MODEL), bmap)],
        scratch_shapes=[
            pltpu.VMEM((D_MODEL, EXPERT_FF), BF),
            pltpu.VMEM((D_MODEL, EXPERT_FF), BF),
            pltpu.VMEM((EXPERT_FF, D_MODEL), BF),
        ],
        compiler_params=_params("arbitrary", "arbitrary"),
        name="ec_experts",
    )(xs_a, gs_a, xs_b, gs_b, w_gate, w_up, w_down)


def _combine_kernel(off_ref, st_ref, y_ref, x_ref, gate_ref, fg_ref, o_ref, *, cap, final):
    st = st_ref[0]
    tl = st.shape[0]

    def finish(scattered):
        v = x_ref[0] + gate_ref[0] * scattered
        if final:
            v = v * lax.rsqrt(jnp.mean(v * v, axis=-1, keepdims=True) + EPS) * fg_ref[...]
        o_ref[0] = v

    def onehot(width, starts):
        n = N_EXPERTS * width
        owner = lax.broadcasted_iota(jnp.int32, (N_EXPERTS, n), 1) // width
        spread = jnp.where(owner == lax.broadcasted_iota(jnp.int32, (N_EXPERTS, n), 0), 1.0, 0.0).astype(BF)
        slot = _dot(st.astype(BF), spread)
        lane = lax.broadcasted_iota(jnp.int32, (1, n), 1)
        want = (lane % width).astype(F32)
        if starts is not None:
            for ex in range(N_EXPERTS):
                want = want + jnp.where(lane // width == ex, starts[ex].astype(F32), 0.0)
        return jnp.where(slot == want, 1.0, 0.0).astype(BF)

    def scatter_all():
        finish(_dot(onehot(cap, None), y_ref[:, 0].reshape(N_EXPERTS * cap, D_MODEL)))

    if cap <= COMBINE_WIN:
        scatter_all()
        return

    starts, fits = _list_windows(off_ref, cap)

    @pl.when(fits)
    def _():
        rows = [y_ref[ex, 0, pl.ds(pl.multiple_of(starts[ex], BF_TILE), COMBINE_WIN), :]
                for ex in range(N_EXPERTS)]
        finish(_dot(onehot(COMBINE_WIN, starts), jnp.concatenate(rows, axis=0)))

    @pl.when(jnp.logical_not(fits))
    def _():
        scatter_all()


def _combine_call(slot_t, offsets, y, x, mod, final_g, final):
    B, L, _ = x.shape
    bm = mod.shape[0]
    cap = y.shape[2]
    tl = min(L, COMBINE_TL)
    assert cap <= COMBINE_WIN or (cap % BF_TILE == 0 and COMBINE_WIN % BF_TILE == 0)
    assert cap <= 256
    gate_map =(lambda b, i, off: (b, 0, 5)) if bm > 1 else (lambda b, i, off: (0, 0, 5))
    return pl.pallas_call(
        functools.partial(_combine_kernel, cap=cap, final=final),
        out_shape=jax.ShapeDtypeStruct((B, L, D_MODEL), F32),
        grid_spec=pltpu.PrefetchScalarGridSpec(
            num_scalar_prefetch=1,
            grid=(B, L // tl),
            in_specs=[
                pl.BlockSpec((1, tl, N_EXPERTS), lambda b, i, off: (b, i, 0)),
                pl.BlockSpec((N_EXPERTS, 1, cap, D_MODEL), lambda b, i, off: (0, b, 0, 0)),
                pl.BlockSpec((1, tl, D_MODEL), lambda b, i, off: (b, i, 0)),
                pl.BlockSpec((1, 1, D_MODEL), gate_map),
                pl.BlockSpec((1, D_MODEL), lambda b, i, off: (0, 0)),
            ],
            out_specs=pl.BlockSpec((1, tl, D_MODEL), lambda b, i, off: (b, i, 0)),
        ),
        compiler_params=_params("arbitrary", "arbitrary"),
        name="ec_combine",
    )(offsets, slot_t, y, x, mod, final_g)


def _mixer_half(x, mod, lw, ctx, tm, kv_prev=None):
    latent = ctx is not None
    layer = lw["layer"]
    res = _inproj_call(x, lw["norm1_g"], mod, lw["w_in"], layer, tm, rope=latent, want_kv=not latent,
                       kv_prev=kv_prev)
    z = res[0]
    br_a = _conv_call(z, lw["conv_w"], lw["conv_b"], lw["conv_ln_g"], lw["conv_ln_b"])
    br_b = _fourier_call(z, *lw["dft"][z.shape[1]])
    if latent:
        br_c = _win_attn_call(z, ctx["win_k"], ctx["win_v"], lw["win_sink"], layer)
        br_d = _na_attn_call(z, ctx["nat_k"], ctx["nat_v"], lw["bias_table"], layer)
    else:
        br_c, br_d = _ctx_attn_call(z, lw["win_sink"])
    x = _merge_call(x, lw["norm1_g"], mod, (br_a, br_b, br_c, br_d),
                    lw["w_in"], lw["w_branch"], lw["w_out"], layer, tm)
    return x, (tuple(res[1:]) if not latent else None)


def kernel(x_prompt, x_sample, cache_win_k, cache_win_v, cache_nat_k, cache_nat_v, c, c_ctx,
           norm1_g, norm2_g, w_ada, b_ada, w_in, conv_w, conv_b, conv_ln_g, conv_ln_b,
           win_sink, na_bias, w_branch, w_out, w_router, w_e_gate, w_e_up, w_e_down, final_norm_g):
    n_dec = c.shape[0]
    cond = jnp.zeros((ADA_ROWS, D_MODEL), F32).at[0].set(c_ctx).at[1:1 + n_dec].set(c)
    mods = _ada_call(cond, w_ada, b_ada)
    final_g = final_norm_g.reshape(1, D_MODEL)
    w_in_bf = _cast_call(w_in)
    w_branch_bf = _cast_call(w_branch)
    w_out_bf = _cast_call(w_out)
    dft = {L: _dft_tables(L) for L in (x_prompt.shape[1], x_sample.shape[1])}
    dim_major = lambda a: jnp.swapaxes(a, -1, -2)
    ctx = {"win_k": dim_major(cache_win_k), "win_v": dim_major(cache_win_v),
           "nat_k": dim_major(cache_nat_k), "nat_v": dim_major(cache_nat_v)}

    xp, xs = x_prompt, x_sample
    caches = None
    for l in range(DEPTH):
        wr_t = w_router[l].T
        wr_hi = wr_t.astype(BF)
        wr_lo = (wr_t - wr_hi.astype(F32)).astype(BF)
        lw = {
            "layer": l,
            "norm1_g": norm1_g[l].reshape(1, D_MODEL),
            "w_in": w_in_bf, "w_branch": w_branch_bf, "w_out": w_out_bf, "dft": dft,
            "conv_w": conv_w[l], "conv_b": conv_b[l],
            "conv_ln_g": conv_ln_g[l], "conv_ln_b": conv_ln_b[l],
            "win_sink": win_sink[l],
            "bias_table": _na_bias_call(na_bias[l], x_sample.shape[1] // GRID_W),
        }
        norm2 = norm2_g[l].reshape(1, D_MODEL)
        final = l == DEPTH - 1
        mod_p = mods[l, 0:1].reshape(1, 1, 6 * D_MODEL)
        mod_s = mods[l, 1:1 + n_dec].reshape(n_dec, 1, 6 * D_MODEL)
        xp, caches = _mixer_half(xp, mod_p, lw, None, tm=256, kv_prev=caches)
        xs, _ = _mixer_half(xs, mod_s, lw, ctx, tm=512)
        smem = lambda off: jnp.pad(off.transpose(0, 2, 1).astype(jnp.int32).reshape(-1), (0, N_EXPERTS))
        h_p, g_p, slot_p, slot_t_p, off_p = _route_call(xp, norm2, mod_p, wr_hi, wr_lo)
        h_s, g_s, slot_s, slot_t_s, off_s = _route_call(xs, norm2, mod_s, wr_hi, wr_lo)
        off_p, off_s = smem(off_p), smem(off_s)
        rows_p, gates_p = _gather_call(off_p, slot_p, g_p, h_p)
        rows_s, gates_s = _gather_call(off_s, slot_s, g_s, h_s)
        flat = lambda a: a.reshape(N_EXPERTS, a.shape[1] * a.shape[2], a.shape[3])
        y_p, y_s = _expert_call(flat(rows_p), flat(gates_p), flat(rows_s), flat(gates_s),
                                w_e_gate, w_e_up, w_e_down, l)
        xp = _combine_call(slot_t_p, off_p, y_p.reshape(rows_p.shape), xp, mod_p, final_g, final)
        xs = _combine_call(slot_t_s, off_s, y_s.reshape(rows_s.shape), xs, mod_s, final_g, final)

    return (xp, xs) + tuple(dim_major(c) for c in caches)
```
